```python
import jax, jax.numpy as jnp
from jax import lax
import numpy as np

D_MODEL = 2048
BATCH = 2
SEQ = 4096
DEPTH = 1

RWKV_HEAD_DIM = 64
RWKV_HEADS = (D_MODEL // 2) // RWKV_HEAD_DIM
RWKV_DIM = RWKV_HEADS * RWKV_HEAD_DIM
DECAY_LORA = 64
ICLR_LORA = 64
GATE_LORA = 160
ATTN_HEAD_DIM = 64
ATTN_Q_HEADS = (D_MODEL // 2) // ATTN_HEAD_DIM
ATTN_KV_HEADS = max(1, ATTN_Q_HEADS // 8)
ATTN_GROUP = ATTN_Q_HEADS // ATTN_KV_HEADS
WINDOW = 128
FFN_DIM = 4 * D_MODEL
NORM_EPS = 1e-6
RWKV_GN_EPS = 64e-5
N_BRANCHES = 2

RWKV_COLS = 3 * RWKV_DIM + DECAY_LORA + ICLR_LORA + GATE_LORA
ATTN_Q_COLS = ATTN_Q_HEADS * ATTN_HEAD_DIM
ATTN_KV_COLS = ATTN_KV_HEADS * ATTN_HEAD_DIM
ATTN_COLS = ATTN_Q_COLS + 2 * ATTN_KV_COLS
GATE_COLS = N_BRANCHES * D_MODEL
IN_COLS = RWKV_COLS + ATTN_COLS + GATE_COLS

kernel_name = "hybrid_rwkv7_swa_sink_alibi_block"


def rms_norm(x, gain):
    x32 = x.astype(jnp.float32)
    y = x32 * lax.rsqrt(jnp.mean(x32 * x32, axis=-1, keepdims=True) + NORM_EPS)
    return (y * gain.astype(jnp.float32)).astype(x.dtype)


def token_shift(p):
    return jnp.pad(p, ((0, 0), (1, 0), (0, 0)))[:, :-1]


def rwkv7_scan(r, decay, k, v, a, b):
    B, T, H, N = r.shape

    def step(S, inp):
        r_t, w_t, k_t, v_t, a_t, b_t = inp
        sa = jnp.einsum('bhij,bhj->bhi', S, a_t)
        S = S * w_t[:, :, None, :] + sa[..., None] * b_t[:, :, None, :] + v_t[..., None] * k_t[:, :, None, :]
        y = jnp.einsum('bhij,bhj->bhi', S, r_t)
        return S, y

    xs = tuple(jnp.moveaxis(t, 1, 0) for t in (r, decay, k, v, a, b))
    S0 = jnp.zeros((B, H, N, N), jnp.float32)
    _, y = lax.scan(step, S0, xs)
    return jnp.moveaxis(y, 0, 1)


def rwkv7_branch(p, mix, w0, w2, a0, a2, g2, k_k, k_a, r_k, ln_w, ln_b):
    B, T, _ = p.shape
    H, N, C = RWKV_HEADS, RWKV_HEAD_DIM, RWKV_DIM
    p = p + (token_shift(p) - p) * mix
    r = p[..., :C]
    k = p[..., C:2 * C]
    v = p[..., 2 * C:3 * C]
    o = 3 * C
    wd = p[..., o:o + DECAY_LORA]
    o += DECAY_LORA
    ad = p[..., o:o + ICLR_LORA]
    o += ICLR_LORA
    gd = p[..., o:o + GATE_LORA]

    w = -jax.nn.softplus(-(w0 + jnp.tanh(wd) @ w2)) - 0.5
    decay = jnp.exp(-jnp.exp(w.astype(jnp.float32)))
    a = jax.nn.sigmoid(a0 + ad @ a2)
    g = jax.nn.sigmoid(gd) @ g2

    hs = lambda t: t.reshape(B, T, H, N).astype(jnp.float32)
    kk = hs(k * k_k)
    kk = kk / jnp.maximum(jnp.sqrt(jnp.sum(kk * kk, axis=-1, keepdims=True)), 1e-12)
    k_mod = k * (1.0 + (a - 1.0) * k_a)
    r_h, k_h, v_h, a_h = hs(r), hs(k_mod), hs(v), hs(a)

    y = rwkv7_scan(r_h, hs(decay), k_h, v_h, -kk, kk * a_h)

    mu = jnp.mean(y, axis=-1, keepdims=True)
    var = jnp.mean(jnp.square(y - mu), axis=-1, keepdims=True)
    yn = ((y - mu) * lax.rsqrt(var + RWKV_GN_EPS)).reshape(B, T, C)
    yn = yn * ln_w.astype(jnp.float32) + ln_b.astype(jnp.float32)
    bonus = jnp.sum(r_h * k_h * r_k.astype(jnp.float32), axis=-1, keepdims=True) * v_h
    out = (yn + bonus.reshape(B, T, C)) * g.astype(jnp.float32)
    return out.astype(p.dtype)


def sliding_window_sink_attention(q, k, v, sinks):
    B, T = q.shape[:2]
    W, HKV, G, hd = WINDOW, ATTN_KV_HEADS, ATTN_GROUP, ATTN_HEAD_DIM
    nb = T // W
    qb = q.reshape(B, nb, W, HKV, G, hd)
    pad = ((0, 0), (W, 0), (0, 0), (0, 0))
    kp = jnp.pad(k, pad).reshape(B, nb + 1, W, HKV, hd)
    vp = jnp.pad(v, pad).reshape(B, nb + 1, W, HKV, hd)
    kw = jnp.concatenate([kp[:, :-1], kp[:, 1:]], axis=2)
    vw = jnp.concatenate([vp[:, :-1], vp[:, 1:]], axis=2)

    s = jnp.einsum('bnqhgd,bnkhd->bnhgqk', qb, kw).astype(jnp.float32) * (hd ** -0.5)
    qi = jnp.arange(W)[:, None]
    kj = jnp.arange(2 * W)[None, :]
    dist = qi + W - kj
    kpos = jnp.arange(nb)[:, None] * W - W + jnp.arange(2 * W)[None, :]
    valid = ((dist >= 0) & (dist < W))[None] & (kpos >= 0)[:, None, :]

    slopes = 2.0 ** (-8.0 * jnp.arange(1, ATTN_Q_HEADS + 1, dtype=jnp.float32) / ATTN_Q_HEADS)
    alibi = -slopes.reshape(HKV, G)[:, :, None, None] * dist.astype(jnp.float32)
    s = jnp.where(valid[None, :, None, None], s + alibi, -jnp.inf)
    sink = jnp.broadcast_to(sinks.astype(jnp.float32).reshape(HKV, G)[None, None, :, :, None, None],
                            s.shape[:-1] + (1,))
    prob = jax.nn.softmax(jnp.concatenate([s, sink], axis=-1), axis=-1)[..., :2 * W]
    o = jnp.einsum('bnhgqk,bnkhd->bnqhgd', prob.astype(v.dtype), vw)
    return o.reshape(B, T, ATTN_Q_HEADS * hd)


def setup_inputs(seed: int = 0) -> dict:
    key = jax.random.key(seed)
    ks = jax.random.split(key, 32)
    L, D = DEPTH, D_MODEL
    nrm = lambda k, shape, s: jax.random.normal(k, shape, jnp.float32) * s
    return {
        "x": nrm(ks[0], (BATCH, SEQ, D), 1.0),
        "c": nrm(ks[1], (BATCH, D), 1.0),
        "w_ada": nrm(ks[2], (L, D, 6 * D), 0.5 * D ** -0.5),
        "b_ada": nrm(ks[3], (L, 6 * D), 0.01),
        "norm1_gain": 1.0 + nrm(ks[4], (L, D), 0.02),
        "w_in": nrm(ks[5], (L, D, IN_COLS), D ** -0.5),
        "b_in": nrm(ks[6], (L, IN_COLS), 0.01),
        "rwkv_mix": jax.random.uniform(ks[7], (L, RWKV_COLS), jnp.float32, 0.0, 1.0),
        "rwkv_w0": jax.random.uniform(ks[8], (L, RWKV_DIM), jnp.float32, -6.0, -1.0),
        "rwkv_w2": nrm(ks[9], (L, DECAY_LORA, RWKV_DIM), 0.5 * DECAY_LORA ** -0.5),
        "rwkv_a0": nrm(ks[10], (L, RWKV_DIM), 0.1),
        "rwkv_a2": nrm(ks[11], (L, ICLR_LORA, RWKV_DIM), 0.5 * ICLR_LORA ** -0.5),
        "rwkv_g2": nrm(ks[12], (L, GATE_LORA, RWKV_DIM), GATE_LORA ** -0.5),
        "rwkv_k_k": 0.85 + nrm(ks[13], (L, RWKV_DIM), 0.05),
        "rwkv_k_a": 1.0 + nrm(ks[14], (L, RWKV_DIM), 0.05),
        "rwkv_r_k": -0.04 + nrm(ks[15], (L, RWKV_HEADS, RWKV_HEAD_DIM), 0.02),
        "rwkv_ln_w": 1.0 + nrm(ks[16], (L, RWKV_DIM), 0.02),
        "rwkv_ln_b": nrm(ks[17], (L, RWKV_DIM), 0.01),
        "attn_sinks": nrm(ks[18], (L, ATTN_Q_HEADS), 1.0),
        "w_branch_rwkv": nrm(ks[19], (L, RWKV_DIM, D), RWKV_DIM ** -0.5),
        "w_branch_attn": nrm(ks[20], (L, ATTN_Q_COLS, D), ATTN_Q_COLS ** -0.5),
        "w_out": nrm(ks[21], (L, D, D), D ** -0.5),
        "norm2_gain": 1.0 + nrm(ks[22], (L, D), 0.02),
        "w_up": nrm(ks[23], (L, D, FFN_DIM), D ** -0.5),
        "w_down": nrm(ks[24], (L, FFN_DIM, D), FFN_DIM ** -0.5),
        "final_gain": 1.0 + nrm(ks[25], (D,), 0.02),
    }


def reference(x, c, w_ada, b_ada, norm1_gain, w_in, b_in, rwkv_mix, rwkv_w0, rwkv_w2,
              rwkv_a0, rwkv_a2, rwkv_g2, rwkv_k_k, rwkv_k_a, rwkv_r_k, rwkv_ln_w, rwkv_ln_b,
              attn_sinks, w_branch_rwkv, w_branch_attn, w_out, norm2_gain, w_up, w_down,
              final_gain):
    B, T, D = x.shape
    c_act = jax.nn.silu(c)
    for l in range(DEPTH):
        mod = c_act @ w_ada[l] + b_ada[l]
        sh1, sc1, gt1, sh2, sc2, gt2 = [m[:, None, :] for m in jnp.split(mod, 6, axis=-1)]

        h = rms_norm(x, norm1_gain[l]) * (1.0 + sc1) + sh1
        p = h @ w_in[l] + b_in[l]
        p_rwkv = p[..., :RWKV_COLS]
        p_attn = p[..., RWKV_COLS:RWKV_COLS + ATTN_COLS]
        p_gate = p[..., RWKV_COLS + ATTN_COLS:]

        y_rwkv = rwkv7_branch(p_rwkv, rwkv_mix[l], rwkv_w0[l], rwkv_w2[l], rwkv_a0[l],
                              rwkv_a2[l], rwkv_g2[l], rwkv_k_k[l], rwkv_k_a[l], rwkv_r_k[l],
                              rwkv_ln_w[l], rwkv_ln_b[l])
        q = p_attn[..., :ATTN_Q_COLS].reshape(B, T, ATTN_Q_HEADS, ATTN_HEAD_DIM)
        k = p_attn[..., ATTN_Q_COLS:ATTN_Q_COLS + ATTN_KV_COLS].reshape(B, T, ATTN_KV_HEADS, ATTN_HEAD_DIM)
        v = p_attn[..., ATTN_Q_COLS + ATTN_KV_COLS:].reshape(B, T, ATTN_KV_HEADS, ATTN_HEAD_DIM)
        y_attn = sliding_window_sink_attention(q, k, v, attn_sinks[l])

        gates = jax.nn.sigmoid(p_gate)
        g_rwkv, g_attn = gates[..., :D], gates[..., D:]
        merged = g_rwkv * (y_rwkv @ w_branch_rwkv[l]) + g_attn * (y_attn @ w_branch_attn[l])
        x = x + gt1 * (merged @ w_out[l])

        h = rms_norm(x, norm2_gain[l]) * (1.0 + sc2) + sh2
        x = x + gt2 * (jnp.square(jax.nn.relu(h @ w_up[l])) @ w_down[l])
    return rms_norm(x, final_gain)
```

```python
import functools

import jax
import jax.numpy as jnp
from jax import lax
from jax.experimental import pallas as pl
from jax.experimental.pallas import tpu as pltpu

F32 = jnp.float32
BF16 = jnp.bfloat16
HIGHEST = lax.Precision.HIGHEST

LANES = 128
HEAD = 64
CHUNK = 64
WINDOW = 128
ATTN_GROUP = 8
NORM_EPS = 1e-6
GN_EPS = 64e-5
DECAY_LORA = 64
ICLR_LORA = 64
GATE_LORA = 160
LORA_PAD = 512
VMEM_LIMIT = 56 * 1024 * 1024


def _cparams(sem):
    return pltpu.CompilerParams(dimension_semantics=sem, vmem_limit_bytes=VMEM_LIMIT)


def _dot(a, b, precision=None):
    return jnp.dot(a, b, preferred_element_type=F32, precision=precision)


def _dot_nt(a, b, precision=None):
    return lax.dot_general(a, b, (((1,), (1,)), ((), ())), preferred_element_type=F32,
                           precision=precision)


def _dot_tn(a, b, precision=None):
    return lax.dot_general(a, b, (((0,), (0,)), ((), ())), preferred_element_type=F32,
                           precision=precision)


def _rms_norm(x, gain):
    ms = jnp.mean(x * x, axis=-1, keepdims=True)
    return x * lax.rsqrt(ms + NORM_EPS) * gain


def _ada_kernel(ct_ref, w_ref, b_ref, o_ref):
    ct = ct_ref[...]
    act = ct * jax.nn.sigmoid(ct)
    for m in range(o_ref.shape[0]):
        o_ref[m:m + 1, :] = jnp.sum(w_ref[...] * act[:, m:m + 1], axis=0, keepdims=True) + b_ref[...]


def _ada(c, w, b, tn=1024):
    nb, d = c.shape
    n = w.shape[1]
    return pl.pallas_call(
        _ada_kernel,
        grid=(n // tn,),
        in_specs=[pl.BlockSpec((d, nb), lambda j: (0, 0)),
                  pl.BlockSpec((d, tn), lambda j: (0, j)),
                  pl.BlockSpec((1, tn), lambda j: (0, j))],
        out_specs=pl.BlockSpec((nb, tn), lambda j: (0, j)),
        out_shape=jax.ShapeDtypeStruct((nb, n), F32),
        compiler_params=_cparams(("arbitrary",)),
        name="adaln",
    )(c.T, w, b.reshape(1, n))


def _inproj_kernel(x_ref, mod_ref, g_ref, w_ref, b_ref, o_ref, h_ref):
    @pl.when(pl.program_id(1) == 0)
    def _():
        h = _rms_norm(x_ref[...], g_ref[...]) * (1.0 + mod_ref[0, 1:2, :]) + mod_ref[0, 0:1, :]
        h_ref[...] = h.astype(BF16)

    o_ref[...] = _dot(h_ref[...], w_ref[...]) + b_ref[...]


def _inproj(x2, mod, gain, w, b, rows_per_batch, tm, tn):
    m, d = x2.shape
    n = w.shape[1]
    tpb = rows_per_batch // tm
    return pl.pallas_call(
        _inproj_kernel,
        grid=(m // tm, n // tn),
        in_specs=[pl.BlockSpec((tm, d), lambda i, j: (i, 0)),
                  pl.BlockSpec((1, 6, d), lambda i, j: (i // tpb, 0, 0)),
                  pl.BlockSpec((1, d), lambda i, j: (0, 0)),
                  pl.BlockSpec((d, tn), lambda i, j: (0, j)),
                  pl.BlockSpec((1, tn), lambda i, j: (0, j))],
        out_specs=pl.BlockSpec((tm, tn), lambda i, j: (i, j)),
        out_shape=jax.ShapeDtypeStruct((m, n), F32),
        scratch_shapes=[pltpu.VMEM((tm, d), BF16)],
        compiler_params=_cparams(("arbitrary", "arbitrary")),
        name="inproj",
    )(x2, mod, gain, w, b)


def _head_sum(x):
    lo = lax.broadcasted_iota(jnp.int32, x.shape, 1) < HEAD
    s0 = jnp.sum(jnp.where(lo, x, 0.0), axis=-1, keepdims=True)
    s1 = jnp.sum(jnp.where(lo, 0.0, x), axis=-1, keepdims=True)
    return jnp.where(lo, s0, s1)


def _token_shift(x, carry_row, mix):
    rolled = pltpu.roll(x, 1, axis=0)
    first = lax.broadcasted_iota(jnp.int32, x.shape, 0) == 0
    prev = jnp.where(first, carry_row, rolled)
    return x + (prev - x) * mix


def _rwkv_kernel(r_ref, k_ref, v_ref, lora_ref, chan_ref, mixl_ref, w2_ref, a2_ref, g2_ref,
                 o_ref,
                 crkv_ref, clora_ref, z_ref, r_s, lw_s, k_s, v_s, a_s, b_s, y_s, g_s):
    tb = r_ref.shape[0]
    n_chunks = tb // CHUNK

    @pl.when(pl.program_id(2) == 0)
    def _():
        crkv_ref[...] = jnp.zeros_like(crkv_ref)
        clora_ref[...] = jnp.zeros_like(clora_ref)
        z_ref[...] = jnp.zeros_like(z_ref)

    chan = chan_ref[...]
    mix_r, mix_k, mix_v = chan[0:1], chan[1:2], chan[2:3]
    w0, a0, k_k, k_a, r_k = chan[3:4], chan[4:5], chan[5:6], chan[6:7], chan[7:8]

    r_in, k_in, v_in, lora_in = r_ref[...], k_ref[...], v_ref[...], lora_ref[...]
    r = _token_shift(r_in, crkv_ref[0:1, :], mix_r)
    k = _token_shift(k_in, crkv_ref[1:2, :], mix_k)
    v = _token_shift(v_in, crkv_ref[2:3, :], mix_v)
    lora = _token_shift(lora_in, clora_ref[...], mixl_ref[...])
    crkv_ref[0:1, :] = r_in[tb - 1:tb, :]
    crkv_ref[1:2, :] = k_in[tb - 1:tb, :]
    crkv_ref[2:3, :] = v_in[tb - 1:tb, :]
    clora_ref[...] = lora_in[tb - 1:tb, :]

    wd = lora[:, 0:DECAY_LORA]
    ad = lora[:, DECAY_LORA:DECAY_LORA + ICLR_LORA]
    gd = lora[:, DECAY_LORA + ICLR_LORA:]
    dw = w0 + _dot(jnp.tanh(wd), w2_ref[...], HIGHEST)
    z = -dw
    softplus = jnp.maximum(z, 0.0) + jnp.log(1.0 + jnp.exp(-jnp.abs(z)))
    logw = -jnp.exp(-softplus - 0.5)
    alr = jax.nn.sigmoid(a0 + _dot(ad, a2_ref[...], HIGHEST))
    g_s[...] = _dot(jax.nn.sigmoid(gd), g2_ref[...], HIGHEST)

    kk = k * k_k
    kk = kk / jnp.maximum(jnp.sqrt(_head_sum(kk * kk)), 1e-12)
    k_mod = k * (1.0 + (alr - 1.0) * k_a)

    r_s[...] = r
    lw_s[...] = logw
    k_s[...] = k_mod
    v_s[...] = v
    a_s[...] = -kk
    b_s[...] = kk * alr

    row = lax.broadcasted_iota(jnp.int32, (CHUNK, CHUNK), 0)
    col = lax.broadcasted_iota(jnp.int32, (CHUNK, CHUNK), 1)
    incl = row >= col
    strict = row > col
    eye = row == col
    tri = jnp.where(incl, 1.0, 0.0).astype(F32)
    level_masks = []
    n = 1
    while n < CHUNK:
        level_masks.append((row // (2 * n) == col // (2 * n)) & ((row // n) % 2 == 1) & ((col // n) % 2 == 0))
        n *= 2

    def chunk_body(c, carry):
        sl = pl.ds(pl.multiple_of(c * CHUNK, CHUNK), CHUNK)
        rc, lwc, kc, vc, ac, bc = r_s[sl, :], lw_s[sl, :], k_s[sl, :], v_s[sl, :], a_s[sl, :], b_s[sl, :]
        lw = _dot(tri, lwc, HIGHEST)
        lw_end = lw[CHUNK - 1:CHUNK, :]
        e_in = jnp.exp(lw)
        e_ex = jnp.exp(lw - lwc)
        e_inv = jnp.exp(-lw)
        e_end = jnp.exp(lw_end - lw)
        g_end = jnp.exp(lw_end)
        ra_f, aa_f = rc * e_in, ac * e_ex
        bi_f, ki_f = bc * e_inv, kc * e_inv
        bh_f, kh_f = bc * e_end, kc * e_end
        ys = []
        for h in range(LANES // HEAD):
            hs = slice(h * HEAD, (h + 1) * HEAD)
            ra, aa, bi, ki, bh, kh, vh = ra_f[:, hs], aa_f[:, hs], bi_f[:, hs], ki_f[:, hs], bh_f[:, hs], kh_f[:, hs], vc[:, hs]
            m_ab = jnp.where(strict, _dot_nt(aa, bi, HIGHEST), 0.0)
            m_ak = jnp.where(strict, _dot_nt(aa, ki, HIGHEST), 0.0)
            m_rb = jnp.where(incl, _dot_nt(ra, bi, HIGHEST), 0.0)
            m_rk = jnp.where(incl, _dot_nt(ra, ki, HIGHEST), 0.0)
            t_inv = jnp.where(eye, 1.0, 0.0) + jnp.where(level_masks[0], m_ab, 0.0)
            for mask in level_masks[1:]:
                off = jnp.where(mask, m_ab, 0.0)
                t_inv = t_inv + _dot(_dot(t_inv, off, HIGHEST), t_inv, HIGHEST)
            a_p = _dot(t_inv, aa, HIGHEST)
            u0 = _dot(t_inv, _dot(m_ak, vh, HIGHEST), HIGHEST)
            r_p = ra + _dot(m_rb, a_p, HIGHEST)
            y0 = _dot(m_rb, u0, HIGHEST) + _dot(m_rk, vh, HIGHEST)
            p_mat = jnp.where(eye, g_end[:, hs], 0.0) + _dot_tn(bh, a_p, HIGHEST)
            q_mat = _dot_tn(bh, u0, HIGHEST) + _dot_tn(kh, vh, HIGHEST)
            zh = z_ref[h]
            ys.append(_dot(r_p, zh, HIGHEST) + y0)
            z_ref[h] = _dot(p_mat, zh, HIGHEST) + q_mat
        y_s[sl, :] = jnp.concatenate(ys, axis=1)
        return carry

    lax.fori_loop(0, n_chunks, chunk_body, 0)

    ln_w, ln_b = chan[8:9], chan[9:10]
    y = y_s[...]
    mu = _head_sum(y) * (1.0 / HEAD)
    yc = y - mu
    var = _head_sum(yc * yc) * (1.0 / HEAD)
    yn = yc * lax.rsqrt(var + GN_EPS) * ln_w + ln_b
    r, k_mod, v = r_s[...], k_s[...], v_s[...]
    bonus = _head_sum(r * k_mod * r_k) * v
    o_ref[...] = ((yn + bonus) * g_s[...]).astype(o_ref.dtype)


def _rwkv(p, chan, mix_lora, w2, a2, g2, batch, seq, offs, tb):
    m = p.shape[0]
    c = w2.shape[1]
    n_pairs = c // LANES
    tpb = seq // tb
    row = lambda b, hp, t: b * tpb + t
    tile = lambda: pltpu.VMEM((tb, LANES), F32)
    return pl.pallas_call(
        _rwkv_kernel,
        grid=(batch, n_pairs, tpb),
        in_specs=[pl.BlockSpec((tb, LANES), lambda b, hp, t: (row(b, hp, t), offs["r"] // LANES + hp)),
                  pl.BlockSpec((tb, LANES), lambda b, hp, t: (row(b, hp, t), offs["k"] // LANES + hp)),
                  pl.BlockSpec((tb, LANES), lambda b, hp, t: (row(b, hp, t), offs["v"] // LANES + hp)),
                  pl.BlockSpec((tb, LORA_PAD), lambda b, hp, t: (row(b, hp, t), offs["lora"] // LORA_PAD)),
                  pl.BlockSpec((16, LANES), lambda b, hp, t: (0, hp)),
                  pl.BlockSpec((1, LORA_PAD), lambda b, hp, t: (0, 0)),
                  pl.BlockSpec((DECAY_LORA, LANES), lambda b, hp, t: (0, hp)),
                  pl.BlockSpec((ICLR_LORA, LANES), lambda b, hp, t: (0, hp)),
                  pl.BlockSpec((LORA_PAD - DECAY_LORA - ICLR_LORA, LANES), lambda b, hp, t: (0, hp))],
        out_specs=pl.BlockSpec((tb, LANES), lambda b, hp, t: (row(b, hp, t), hp)),
        out_shape=jax.ShapeDtypeStruct((m, c), BF16),
        scratch_shapes=[pltpu.VMEM((8, LANES), F32), pltpu.VMEM((1, LORA_PAD), F32),
                        pltpu.VMEM((LANES // HEAD, HEAD, HEAD), F32)] + [tile() for _ in range(8)],
        compiler_params=_cparams(("arbitrary", "arbitrary", "arbitrary")),
        name="rwkv7",
    )(p, p, p, p, chan, mix_lora, w2, a2, g2)


def _attn_kernel(sink_ref, q_ref, kvc_ref, kvp_ref, o_ref, *, blocks_per_seq, n_heads):
    nb = pl.program_id(0) % blocks_per_seq
    n_kv = n_heads // ATTN_GROUP
    kv = jnp.concatenate([kvp_ref[...], kvc_ref[...]], axis=0).astype(BF16)
    q = q_ref[...].astype(BF16)
    qi = lax.broadcasted_iota(jnp.int32, (WINDOW, 2 * WINDOW), 0)
    kj = lax.broadcasted_iota(jnp.int32, (WINDOW, 2 * WINDOW), 1)
    dist = qi + WINDOW - kj
    valid = (dist >= 0) & (dist < WINDOW) & ((kj >= WINDOW) | (nb > 0))
    dist_f = dist.astype(F32)
    for h in range(n_heads):
        hk = h // ATTN_GROUP
        slope = 2.0 ** (-8.0 * (h + 1) / n_heads)
        sink = sink_ref[h]
        qh = q[:, h * HEAD:(h + 1) * HEAD]
        kh = kv[:, hk * HEAD:(hk + 1) * HEAD]
        vh = kv[:, (n_kv + hk) * HEAD:(n_kv + hk + 1) * HEAD]
        s = _dot_nt(qh, kh) * (HEAD ** -0.5)
        s = jnp.where(valid, s - slope * dist_f, -jnp.inf)
        mx = jnp.maximum(jnp.max(s, axis=-1, keepdims=True), sink)
        e = jnp.exp(s - mx)
        denom = jnp.sum(e, axis=-1, keepdims=True) + jnp.exp(sink - mx)
        prob = e / denom
        o_ref[:, h * HEAD:(h + 1) * HEAD] = _dot(prob.astype(BF16), vh).astype(o_ref.dtype)


def _attn(p, sinks, seq, offs):
    m = p.shape[0]
    n_heads = sinks.shape[0]
    qc = n_heads * HEAD
    kvc = 2 * (n_heads // ATTN_GROUP) * HEAD
    bps = seq // WINDOW
    return pl.pallas_call(
        functools.partial(_attn_kernel, blocks_per_seq=bps, n_heads=n_heads),
        grid=(m // WINDOW,),
        in_specs=[pl.BlockSpec(memory_space=pltpu.SMEM),
                  pl.BlockSpec((WINDOW, qc), lambda n: (n, offs["q"] // qc)),
                  pl.BlockSpec((WINDOW, kvc), lambda n: (n, offs["kv"] // kvc)),
                  pl.BlockSpec((WINDOW, kvc), lambda n: (jnp.maximum(n - 1, 0), offs["kv"] // kvc))],
        out_specs=pl.BlockSpec((WINDOW, qc), lambda n: (n, 0)),
        out_shape=jax.ShapeDtypeStruct((m, qc), BF16),
        compiler_params=_cparams(("arbitrary",)),
        name="swa",
    )(sinks, p, p, p)


def _merge_kernel(yr_ref, ya_ref, gate_ref, x_ref, mod_ref, wbr_ref, wba_ref, wout_ref, g2_ref,
                  x1_ref, h2_ref):
    d = x_ref.shape[1]
    gates = jax.nn.sigmoid(gate_ref[...])
    merged = gates[:, :d] * _dot(yr_ref[...], wbr_ref[...]) + gates[:, d:] * _dot(ya_ref[...], wba_ref[...])
    x1 = x_ref[...] + mod_ref[0, 2:3, :] * _dot(merged.astype(BF16), wout_ref[...])
    x1_ref[...] = x1
    h2 = _rms_norm(x1, g2_ref[...]) * (1.0 + mod_ref[0, 4:5, :]) + mod_ref[0, 3:4, :]
    h2_ref[...] = h2.astype(BF16)


def _merge(yr, ya, p, x2, mod, wbr, wba, wout, gain2, rows_per_batch, tm):
    m, d = x2.shape
    c = yr.shape[1]
    tpb = rows_per_batch // tm
    const = lambda shape: pl.BlockSpec(shape, lambda i: (0, 0), pipeline_mode=pl.Buffered(1))
    return pl.pallas_call(
        _merge_kernel,
        grid=(m // tm,),
        in_specs=[pl.BlockSpec((tm, c), lambda i: (i, 0)),
                  pl.BlockSpec((tm, c), lambda i: (i, 0)),
                  pl.BlockSpec((tm, 2 * d), lambda i: (i, 0)),
                  pl.BlockSpec((tm, d), lambda i: (i, 0)),
                  pl.BlockSpec((1, 6, d), lambda i: (i // tpb, 0, 0)),
                  const((c, d)), const((c, d)), const((d, d)),
                  pl.BlockSpec((1, d), lambda i: (0, 0))],
        out_specs=[pl.BlockSpec((tm, d), lambda i: (i, 0)),
                   pl.BlockSpec((tm, d), lambda i: (i, 0))],
        out_shape=[jax.ShapeDtypeStruct((m, d), F32), jax.ShapeDtypeStruct((m, d), BF16)],
        compiler_params=_cparams(("arbitrary",)),
        name="merge",
    )(yr, ya, p, x2, mod, wbr, wba, wout, gain2)


def _mlp_kernel(h_ref, wu_ref, wd_ref, x1_ref, mod_ref, fg_ref, o_ref, acc_ref):
    f = pl.program_id(1)

    @pl.when(f == 0)
    def _():
        acc_ref[...] = jnp.zeros_like(acc_ref)

    u = _dot(h_ref[...], wu_ref[...])
    act = jnp.square(jnp.maximum(u, 0.0)).astype(BF16)
    acc_ref[...] += _dot(act, wd_ref[...])

    @pl.when(f == pl.num_programs(1) - 1)
    def _():
        x2 = x1_ref[...] + mod_ref[0, 5:6, :] * acc_ref[...]
        o_ref[...] = _rms_norm(x2, fg_ref[...])


def _mlp(h2, wu, wd, x1, mod, final_gain, rows_per_batch, tm, tf):
    m, d = h2.shape
    f = wu.shape[1]
    tpb = rows_per_batch // tm
    return pl.pallas_call(
        _mlp_kernel,
        grid=(m // tm, f // tf),
        in_specs=[pl.BlockSpec((tm, d), lambda i, j: (i, 0)),
                  pl.BlockSpec((d, tf), lambda i, j: (0, j)),
                  pl.BlockSpec((tf, d), lambda i, j: (j, 0)),
                  pl.BlockSpec((tm, d), lambda i, j: (i, 0)),
                  pl.BlockSpec((1, 6, d), lambda i, j: (i // tpb, 0, 0)),
                  pl.BlockSpec((1, d), lambda i, j: (0, 0))],
        out_specs=pl.BlockSpec((tm, d), lambda i, j: (i, 0)),
        out_shape=jax.ShapeDtypeStruct((m, d), F32),
        scratch_shapes=[pltpu.VMEM((tm, d), F32)],
        compiler_params=_cparams(("arbitrary", "arbitrary")),
        name="mlp",
    )(h2, wu, wd, x1, mod, final_gain)


def _forward(x, c, w_ada, b_ada, norm1_gain, w_in, b_in, rwkv_mix, rwkv_w0, rwkv_w2, rwkv_a0,
             rwkv_a2, rwkv_g2, rwkv_k_k, rwkv_k_a, rwkv_r_k, rwkv_ln_w, rwkv_ln_b, attn_sinks,
             w_branch_rwkv, w_branch_attn, w_out, norm2_gain, w_up, w_down, final_gain):
    batch, seq, d = x.shape
    depth = w_ada.shape[0]
    c_rwkv = rwkv_w0.shape[1]
    n_heads = attn_sinks.shape[1]
    qc = n_heads * HEAD
    kvc = 2 * (n_heads // ATTN_GROUP) * HEAD
    lora_w = DECAY_LORA + ICLR_LORA + GATE_LORA
    rwkv_cols = 3 * c_rwkv + lora_w
    attn_cols = qc + kvc
    offs = {"gate": 0, "r": 2 * d, "k": 2 * d + c_rwkv, "v": 2 * d + 2 * c_rwkv, "q": 2 * d + 3 * c_rwkv}
    offs["lora"] = offs["q"] + qc
    offs["kv"] = offs["lora"] + LORA_PAD
    n_pack = offs["kv"] + kvc

    def pack_cols(t):
        pad = jnp.zeros(t.shape[:-1] + (LORA_PAD - lora_w,), t.dtype)
        return jnp.concatenate([t[..., rwkv_cols + attn_cols:], t[..., :3 * c_rwkv],
                                t[..., rwkv_cols:rwkv_cols + qc], t[..., 3 * c_rwkv:rwkv_cols], pad,
                                t[..., rwkv_cols + qc:rwkv_cols + attn_cols]], axis=-1)

    tm_in = min(1024, seq)
    tn_in = 1280 if n_pack % 1280 == 0 else LANES
    tb = min(512, seq)
    tm_merge = min(256, seq)
    tm_mlp = min(512, seq)
    tf = min(1024, w_up.shape[2])

    assert depth == 1, "single-layer block"
    x2 = x.reshape(batch * seq, d)
    for l in range(depth):
        mod = _ada(c, w_ada[l], b_ada[l]).reshape(batch, 6, d)
        w_pack = pack_cols(w_in[l]).astype(BF16)
        b_pack = pack_cols(b_in[l]).reshape(1, n_pack)
        p = _inproj(x2, mod, norm1_gain[l].reshape(1, d), w_pack, b_pack, seq, tm_in, tn_in)

        mix = rwkv_mix[l]
        chan = jnp.stack([mix[:c_rwkv], mix[c_rwkv:2 * c_rwkv], mix[2 * c_rwkv:3 * c_rwkv],
                          rwkv_w0[l], rwkv_a0[l], rwkv_k_k[l], rwkv_k_a[l], rwkv_r_k[l].reshape(c_rwkv),
                          rwkv_ln_w[l], rwkv_ln_b[l]])
        chan = jnp.pad(chan, ((0, 16 - chan.shape[0]), (0, 0)))
        mix_lora = jnp.pad(mix[3 * c_rwkv:], (0, LORA_PAD - lora_w)).reshape(1, LORA_PAD)
        g2_pad = jnp.pad(rwkv_g2[l], ((0, LORA_PAD - lora_w), (0, 0)))
        y_rwkv = _rwkv(p, chan, mix_lora, rwkv_w2[l], rwkv_a2[l], g2_pad, batch, seq, offs, tb)
        y_attn = _attn(p, attn_sinks[l], seq, offs)

        x2, h2 = _merge(y_rwkv, y_attn, p, x2, mod, w_branch_rwkv[l].astype(BF16),
                        w_branch_attn[l].astype(BF16), w_out[l].astype(BF16),
                        norm2_gain[l].reshape(1, d), seq, tm_merge)
        x2 = _mlp(h2, w_up[l].astype(BF16), w_down[l].astype(BF16), x2, mod, final_gain.reshape(1, d),
                  seq, tm_mlp, tf)
    return x2.reshape(batch, seq, d)


def kernel(x, c, w_ada, b_ada, norm1_gain, w_in, b_in, rwkv_mix, rwkv_w0, rwkv_w2, rwkv_a0, rwkv_a2, rwkv_g2, rwkv_k_k, rwkv_k_a, rwkv_r_k, rwkv_ln_w, rwkv_ln_b, attn_sinks, w_branch_rwkv, w_branch_attn, w_out, norm2_gain, w_up, w_down, final_gain):
    return _forward(x, c, w_ada, b_ada, norm1_gain, w_in, b_in, rwkv_mix, rwkv_w0, rwkv_w2, rwkv_a0,
                    rwkv_a2, rwkv_g2, rwkv_k_k, rwkv_k_a, rwkv_r_k, rwkv_ln_w, rwkv_ln_b, attn_sinks,
                    w_branch_rwkv, w_branch_attn, w_out, norm2_gain, w_up, w_down, final_gain)
```

```python
import functools

import jax
import jax.numpy as jnp
from jax import lax
from jax.experimental import pallas as pl
from jax.experimental.pallas import tpu as pltpu

F32 = jnp.float32
BF16 = jnp.bfloat16
HIGHEST = lax.Precision.HIGHEST

LANES = 128
HEAD = 64
CHUNK = 128
WINDOW = 128
ATTN_GROUP = 8
NORM_EPS = 1e-6
GN_EPS = 64e-5
DECAY_LORA = 64
ICLR_LORA = 64
GATE_LORA = 160
LORA_PAD = 512
VMEM_LIMIT = 56 * 1024 * 1024


def _cparams(sem):
    return pltpu.CompilerParams(dimension_semantics=sem, vmem_limit_bytes=VMEM_LIMIT)


def _dot(a, b, precision=None):
    return jnp.dot(a, b, preferred_element_type=F32, precision=precision)


def _dot_nt(a, b, precision=None):
    return lax.dot_general(a, b, (((1,), (1,)), ((), ())), preferred_element_type=F32,
                           precision=precision)


def _dot_tn(a, b, precision=None):
    return lax.dot_general(a, b, (((0,), (0,)), ((), ())), preferred_element_type=F32,
                           precision=precision)


def _rms_norm(x, gain):
    ms = jnp.mean(x * x, axis=-1, keepdims=True)
    return x * lax.rsqrt(ms + NORM_EPS) * gain


def _ada_kernel(ct_ref, w_ref, b_ref, o_ref):
    ct = ct_ref[...]
    act = ct * jax.nn.sigmoid(ct)
    for m in range(o_ref.shape[0]):
        o_ref[m:m + 1, :] = jnp.sum(w_ref[...] * act[:, m:m + 1], axis=0, keepdims=True) + b_ref[...]


def _ada(c, w, b, tn=1024):
    nb, d = c.shape
    n = w.shape[1]
    return pl.pallas_call(
        _ada_kernel,
        grid=(n // tn,),
        in_specs=[pl.BlockSpec((d, nb), lambda j: (0, 0)),
                  pl.BlockSpec((d, tn), lambda j: (0, j)),
                  pl.BlockSpec((1, tn), lambda j: (0, j))],
        out_specs=pl.BlockSpec((nb, tn), lambda j: (0, j)),
        out_shape=jax.ShapeDtypeStruct((nb, n), F32),
        compiler_params=_cparams(("arbitrary",)),
        name="adaln",
    )(c.T, w, b.reshape(1, n))


def _inproj_kernel(x_ref, mod_ref, g_ref, w_ref, b_ref, o_ref, h_ref):
    @pl.when(pl.program_id(1) == 0)
    def _():
        h = _rms_norm(x_ref[...], g_ref[...]) * (1.0 + mod_ref[0, 1:2, :]) + mod_ref[0, 0:1, :]
        h_ref[...] = h.astype(BF16)

    o_ref[...] = _dot(h_ref[...], w_ref[...]) + b_ref[...]


def _inproj(x2, mod, gain, w, b, rows_per_batch, tm, tn):
    m, d = x2.shape
    n = w.shape[1]
    tpb = rows_per_batch // tm
    return pl.pallas_call(
        _inproj_kernel,
        grid=(m // tm, n // tn),
        in_specs=[pl.BlockSpec((tm, d), lambda i, j: (i, 0)),
                  pl.BlockSpec((1, 6, d), lambda i, j: (i // tpb, 0, 0)),
                  pl.BlockSpec((1, d), lambda i, j: (0, 0)),
                  pl.BlockSpec((d, tn), lambda i, j: (0, j)),
                  pl.BlockSpec((1, tn), lambda i, j: (0, j))],
        out_specs=pl.BlockSpec((tm, tn), lambda i, j: (i, j)),
        out_shape=jax.ShapeDtypeStruct((m, n), F32),
        scratch_shapes=[pltpu.VMEM((tm, d), BF16)],
        compiler_params=_cparams(("arbitrary", "arbitrary")),
        name="inproj",
    )(x2, mod, gain, w, b)


def _head_sum(x):
    lo = lax.broadcasted_iota(jnp.int32, x.shape, 1) < HEAD
    s0 = jnp.sum(jnp.where(lo, x, 0.0), axis=-1, keepdims=True)
    s1 = jnp.sum(jnp.where(lo, 0.0, x), axis=-1, keepdims=True)
    return jnp.where(lo, s0, s1)


def _token_shift(x, carry_row, mix):
    rolled = pltpu.roll(x, 1, axis=0)
    first = lax.broadcasted_iota(jnp.int32, x.shape, 0) == 0
    prev = jnp.where(first, carry_row, rolled)
    return x + (prev - x) * mix


class _Operands:
    def __init__(self):
        self._parts = {}

    def parts(self, x, n):
        ent = self._parts.setdefault(id(x), [x])
        while len(ent) - 1 < n:
            rest = x
            for piece in ent[1:]:
                rest = rest - piece.astype(F32)
            ent.append(rest.astype(BF16))
        return ent[1:n + 1]

    def mm(self, a, b, passes, dims=(((1,), (0,)), ((), ()))):
        dot = lambda u, w: lax.dot_general(u, w, dims, preferred_element_type=F32)
        if passes == 1:
            return dot(self.parts(a, 1)[0], self.parts(b, 1)[0])
        a_hi, a_lo = self.parts(a, 2)
        b_hi, b_lo = self.parts(b, 2)
        return dot(a_hi, b_hi) + (dot(a_hi, b_lo) + dot(a_lo, b_hi))


_NT = (((1,), (1,)), ((), ()))
_TN = (((0,), (0,)), ((), ()))
SCAN_PASSES = {"gram": 1, "inverse": 1, "apply": 1, "state": 1}


def _scan_tile(r_s, lw_s, k_s, vsw_s, a_s, b_s, y_s, z_ref, n_chunks):
    ops = _Operands()
    row = lax.broadcasted_iota(jnp.int32, (CHUNK, CHUNK), 0)
    col = lax.broadcasted_iota(jnp.int32, (CHUNK, CHUNK), 1)
    incl = row >= col
    strict = row > col
    eye = row == col
    lo_lane = col < HEAD
    same_half = lo_lane == (row < HEAD)
    lo_lane2 = lax.broadcasted_iota(jnp.int32, (2 * CHUNK, LANES), 1) < HEAD
    tri = jnp.where(incl, 1.0, 0.0).astype(BF16)
    level_masks = []
    n = 1
    while n < CHUNK:
        level_masks.append((row // (2 * n) == col // (2 * n)) & ((row // n) % 2 == 1) & ((col // n) % 2 == 0))
        n *= 2

    chunks = []
    for c in range(n_chunks):
        sl = slice(c * CHUNK, (c + 1) * CHUNK)
        rc, lwc, kc, vsw, ac, bc = r_s[sl, :], lw_s[sl, :], k_s[sl, :], vsw_s[sl, :], a_s[sl, :], b_s[sl, :]
        lw = sum(jnp.dot(tri, piece, preferred_element_type=F32) for piece in ops.parts(lwc, 3))
        lw_end = lw[CHUNK - 1:CHUNK, :]
        e_in = jnp.exp(lw)
        e_ex = jnp.exp(lw - lwc)
        e_inv = jnp.exp(-lw)
        e_end = jnp.exp(lw_end - lw)
        chunks.append(dict(sl=sl, vsw=vsw, g_end=jnp.exp(lw_end), ra=rc * e_in, aa=ac * e_ex,
                           bi=bc * e_inv, ki=kc * e_inv, bh=bc * e_end, kh=kc * e_end))

    probs = []
    for ch in chunks:
        lhs_all = jnp.concatenate([ch["aa"], ch["ra"]], axis=0)
        rhs_all = jnp.concatenate([ch["bi"], ch["ki"]], axis=0)
        for h in range(LANES // HEAD):
            own = lo_lane if h == 0 else ~lo_lane
            own2 = lo_lane2 if h == 0 else ~lo_lane2
            gram = ops.mm(jnp.where(own2, lhs_all, 0.0), rhs_all, SCAN_PASSES["gram"], _NT)
            probs.append(dict(ch=ch, own=own,
                              m_ab=jnp.where(strict, gram[:CHUNK, :CHUNK], 0.0),
                              m_ak=jnp.where(strict, gram[:CHUNK, CHUNK:], 0.0),
                              m_rb=jnp.where(incl, gram[CHUNK:, :CHUNK], 0.0),
                              m_rk=jnp.where(incl, gram[CHUNK:, CHUNK:], 0.0)))

    for p in probs:
        p["t"] = jnp.where(eye, 1.0, 0.0) + jnp.where(level_masks[0], p["m_ab"], 0.0)
    for mask in level_masks[1:]:
        for p in probs:
            p["to"] = ops.mm(p["t"], jnp.where(mask, p["m_ab"], 0.0), SCAN_PASSES["inverse"])
        for p in probs:
            p["t"] = p["t"] + ops.mm(p["to"], p["t"], SCAN_PASSES["inverse"])

    for p in probs:
        p["vm"] = jnp.where(p["own"], 0.0, p["ch"]["vsw"])
        p["x"] = jnp.where(p["own"], p["ch"]["aa"], 0.0) + ops.mm(p["m_ak"], p["vm"], SCAN_PASSES["apply"])
    for p in probs:
        p["tx"] = ops.mm(p["t"], p["x"], SCAN_PASSES["apply"])
    for p in probs:
        p["y"] = (jnp.where(p["own"], p["ch"]["ra"], 0.0) + ops.mm(p["m_rb"], p["tx"], SCAN_PASSES["apply"])
                  + ops.mm(p["m_rk"], p["vm"], SCAN_PASSES["apply"]))

    for i, ch in enumerate(chunks):
        p0, p1 = probs[2 * i], probs[2 * i + 1]
        pq = ops.mm(ch["bh"], jnp.concatenate([p0["tx"], p1["tx"]], axis=1), SCAN_PASSES["apply"], _TN)
        pq = jnp.where(row < HEAD, pq[:, :LANES], pq[:, LANES:])
        ch["p"] = jnp.where(same_half, pq, 0.0) + jnp.where(eye, ch["g_end"], 0.0)
        ch["q"] = jnp.where(same_half, 0.0, pq + ops.mm(ch["kh"], ch["vsw"], SCAN_PASSES["apply"], _TN))
        ch["r"] = jnp.where(lo_lane, p0["y"], p1["y"])
        ch["y0"] = jnp.where(lo_lane, p1["y"], p0["y"])

    z = z_ref[...]
    for ch in chunks:
        y_s[ch["sl"], :] = ops.mm(ch["r"], z, SCAN_PASSES["state"]) + ch["y0"]
        z = ops.mm(ch["p"], z, SCAN_PASSES["state"]) + ch["q"]
    z_ref[...] = z


def _rwkv_kernel(r_ref, k_ref, v_ref, lora_ref, chan_ref, mixl_ref, w2_ref, a2_ref, g2_ref,
                 o_ref,
                 crkv_ref, clora_ref, z_ref, r_s, lw_s, k_s, v_s, vsw_s, a_s, b_s, y_s, g_s):
    tb = r_ref.shape[0]
    n_chunks = tb // CHUNK

    @pl.when(pl.program_id(2) == 0)
    def _():
        crkv_ref[...] = jnp.zeros_like(crkv_ref)
        clora_ref[...] = jnp.zeros_like(clora_ref)
        z_ref[...] = jnp.zeros_like(z_ref)

    chan = chan_ref[...]
    mix_r, mix_k, mix_v = chan[0:1], chan[1:2], chan[2:3]
    w0, a0, k_k, k_a, r_k = chan[3:4], chan[4:5], chan[5:6], chan[6:7], chan[7:8]

    r_in, k_in, v_in, lora_in = r_ref[...], k_ref[...], v_ref[...], lora_ref[...]
    r = _token_shift(r_in, crkv_ref[0:1, :], mix_r)
    k = _token_shift(k_in, crkv_ref[1:2, :], mix_k)
    v = _token_shift(v_in, crkv_ref[2:3, :], mix_v)
    lora = _token_shift(lora_in, clora_ref[...], mixl_ref[...])
    crkv_ref[0:1, :] = r_in[tb - 1:tb, :]
    crkv_ref[1:2, :] = k_in[tb - 1:tb, :]
    crkv_ref[2:3, :] = v_in[tb - 1:tb, :]
    clora_ref[...] = lora_in[tb - 1:tb, :]

    wd = lora[:, 0:DECAY_LORA]
    ad = lora[:, DECAY_LORA:DECAY_LORA + ICLR_LORA]
    gd = lora[:, DECAY_LORA + ICLR_LORA:]
    dw = w0 + _dot(jnp.tanh(wd), w2_ref[...], HIGHEST)
    z = -dw
    softplus = jnp.maximum(z, 0.0) + jnp.log(1.0 + jnp.exp(-jnp.abs(z)))
    logw = -jnp.exp(-softplus - 0.5)
    alr = jax.nn.sigmoid(a0 + _dot(ad, a2_ref[...], HIGHEST))
    g_s[...] = _dot(jax.nn.sigmoid(gd), g2_ref[...], HIGHEST)

    kk = k * k_k
    kk = kk / jnp.maximum(jnp.sqrt(_head_sum(kk * kk)), 1e-12)
    k_mod = k * (1.0 + (alr - 1.0) * k_a)

    r_s[...] = r
    lw_s[...] = logw
    k_s[...] = k_mod
    v_s[...] = v
    vsw_s[...] = pltpu.roll(v, HEAD, axis=1)
    a_s[...] = -kk
    b_s[...] = kk * alr

    _scan_tile(r_s, lw_s, k_s, vsw_s, a_s, b_s, y_s, z_ref, n_chunks)

    ln_w, ln_b = chan[8:9], chan[9:10]
    y = pltpu.roll(y_s[...], HEAD, axis=1)
    mu = _head_sum(y) * (1.0 / HEAD)
    yc = y - mu
    var = _head_sum(yc * yc) * (1.0 / HEAD)
    yn = yc * lax.rsqrt(var + GN_EPS) * ln_w + ln_b
    r, k_mod, v = r_s[...], k_s[...], v_s[...]
    bonus = _head_sum(r * k_mod * r_k) * v
    o_ref[...] = ((yn + bonus) * g_s[...]).astype(o_ref.dtype)


def _rwkv(p, chan, mix_lora, w2, a2, g2, batch, seq, offs, tb):
    m = p.shape[0]
    c = w2.shape[1]
    n_pairs = c // LANES
    tpb = seq // tb
    row = lambda b, hp, t: b * tpb + t
    tile = lambda: pltpu.VMEM((tb, LANES), F32)
    return pl.pallas_call(
        _rwkv_kernel,
        grid=(batch, n_pairs, tpb),
        in_specs=[pl.BlockSpec((tb, LANES), lambda b, hp, t: (row(b, hp, t), offs["r"] // LANES + hp)),
                  pl.BlockSpec((tb, LANES), lambda b, hp, t: (row(b, hp, t), offs["k"] // LANES + hp)),
                  pl.BlockSpec((tb, LANES), lambda b, hp, t: (row(b, hp, t), offs["v"] // LANES + hp)),
                  pl.BlockSpec((tb, LORA_PAD), lambda b, hp, t: (row(b, hp, t), offs["lora"] // LORA_PAD)),
                  pl.BlockSpec((16, LANES), lambda b, hp, t: (0, hp)),
                  pl.BlockSpec((1, LORA_PAD), lambda b, hp, t: (0, 0)),
                  pl.BlockSpec((DECAY_LORA, LANES), lambda b, hp, t: (0, hp)),
                  pl.BlockSpec((ICLR_LORA, LANES), lambda b, hp, t: (0, hp)),
                  pl.BlockSpec((LORA_PAD - DECAY_LORA - ICLR_LORA, LANES), lambda b, hp, t: (0, hp))],
        out_specs=pl.BlockSpec((tb, LANES), lambda b, hp, t: (row(b, hp, t), hp)),
        out_shape=jax.ShapeDtypeStruct((m, c), BF16),
        scratch_shapes=[pltpu.VMEM((8, LANES), F32), pltpu.VMEM((1, LORA_PAD), F32),
                        pltpu.VMEM((LANES, LANES), F32)] + [tile() for _ in range(9)],
        compiler_params=_cparams(("arbitrary", "arbitrary", "arbitrary")),
        name="rwkv7",
    )(p, p, p, p, chan, mix_lora, w2, a2, g2)


def _attn_kernel(sink_ref, q_ref, kvc_ref, kvp_ref, o_ref, *, blocks_per_seq, n_heads):
    nb = pl.program_id(0) % blocks_per_seq
    n_kv = n_heads // ATTN_GROUP
    kv = jnp.concatenate([kvp_ref[...], kvc_ref[...]], axis=0).astype(BF16)
    q = q_ref[...].astype(BF16)
    qi = lax.broadcasted_iota(jnp.int32, (WINDOW, 2 * WINDOW), 0)
    kj = lax.broadcasted_iota(jnp.int32, (WINDOW, 2 * WINDOW), 1)
    dist = qi + WINDOW - kj
    valid = (dist >= 0) & (dist < WINDOW) & ((kj >= WINDOW) | (nb > 0))
    dist_f = dist.astype(F32)
    for h in range(n_heads):
        hk = h // ATTN_GROUP
        slope = 2.0 ** (-8.0 * (h + 1) / n_heads)
        sink = sink_ref[h]
        qh = q[:, h * HEAD:(h + 1) * HEAD]
        kh = kv[:, hk * HEAD:(hk + 1) * HEAD]
        vh = kv[:, (n_kv + hk) * HEAD:(n_kv + hk + 1) * HEAD]
        s = _dot_nt(qh, kh) * (HEAD ** -0.5)
        s = jnp.where(valid, s - slope * dist_f, -jnp.inf)
        mx = jnp.maximum(jnp.max(s, axis=-1, keepdims=True), sink)
        e = jnp.exp(s - mx)
        denom = jnp.sum(e, axis=-1, keepdims=True) + jnp.exp(sink - mx)
        prob = e / denom
        o_ref[:, h * HEAD:(h + 1) * HEAD] = _dot(prob.astype(BF16), vh).astype(o_ref.dtype)


def _attn(p, sinks, seq, offs):
    m = p.shape[0]
    n_heads = sinks.shape[0]
    qc = n_heads * HEAD
    kvc = 2 * (n_heads // ATTN_GROUP) * HEAD
    bps = seq // WINDOW
    return pl.pallas_call(
        functools.partial(_attn_kernel, blocks_per_seq=bps, n_heads=n_heads),
        grid=(m // WINDOW,),
        in_specs=[pl.BlockSpec(memory_space=pltpu.SMEM),
                  pl.BlockSpec((WINDOW, qc), lambda n: (n, offs["q"] // qc)),
                  pl.BlockSpec((WINDOW, kvc), lambda n: (n, offs["kv"] // kvc)),
                  pl.BlockSpec((WINDOW, kvc), lambda n: (jnp.maximum(n - 1, 0), offs["kv"] // kvc))],
        out_specs=pl.BlockSpec((WINDOW, qc), lambda n: (n, 0)),
        out_shape=jax.ShapeDtypeStruct((m, qc), BF16),
        compiler_params=_cparams(("arbitrary",)),
        name="swa",
    )(sinks, p, p, p)


def _merge_kernel(yr_ref, ya_ref, gate_ref, x_ref, mod_ref, wbr_ref, wba_ref, wout_ref, g2_ref,
                  x1_ref, h2_ref):
    d = x_ref.shape[1]
    gates = jax.nn.sigmoid(gate_ref[...])
    merged = gates[:, :d] * _dot(yr_ref[...], wbr_ref[...]) + gates[:, d:] * _dot(ya_ref[...], wba_ref[...])
    x1 = x_ref[...] + mod_ref[0, 2:3, :] * _dot(merged.astype(BF16), wout_ref[...])
    x1_ref[...] = x1
    h2 = _rms_norm(x1, g2_ref[...]) * (1.0 + mod_ref[0, 4:5, :]) + mod_ref[0, 3:4, :]
    h2_ref[...] = h2.astype(BF16)


def _merge(yr, ya, p, x2, mod, wbr, wba, wout, gain2, rows_per_batch, tm):
    m, d = x2.shape
    c = yr.shape[1]
    tpb = rows_per_batch // tm
    const = lambda shape: pl.BlockSpec(shape, lambda i: (0, 0), pipeline_mode=pl.Buffered(1))
    return pl.pallas_call(
        _merge_kernel,
        grid=(m // tm,),
        in_specs=[pl.BlockSpec((tm, c), lambda i: (i, 0)),
                  pl.BlockSpec((tm, c), lambda i: (i, 0)),
                  pl.BlockSpec((tm, 2 * d), lambda i: (i, 0)),
                  pl.BlockSpec((tm, d), lambda i: (i, 0)),
                  pl.BlockSpec((1, 6, d), lambda i: (i // tpb, 0, 0)),
                  const((c, d)), const((c, d)), const((d, d)),
                  pl.BlockSpec((1, d), lambda i: (0, 0))],
        out_specs=[pl.BlockSpec((tm, d), lambda i: (i, 0)),
                   pl.BlockSpec((tm, d), lambda i: (i, 0))],
        out_shape=[jax.ShapeDtypeStruct((m, d), F32), jax.ShapeDtypeStruct((m, d), BF16)],
        compiler_params=_cparams(("arbitrary",)),
        name="merge",
    )(yr, ya, p, x2, mod, wbr, wba, wout, gain2)


def _mlp_kernel(h_ref, wu_ref, wd_ref, x1_ref, mod_ref, fg_ref, o_ref, acc_ref):
    f = pl.program_id(1)

    @pl.when(f == 0)
    def _():
        acc_ref[...] = jnp.zeros_like(acc_ref)

    u = _dot(h_ref[...], wu_ref[...])
    act = jnp.square(jnp.maximum(u, 0.0)).astype(BF16)
    acc_ref[...] += _dot(act, wd_ref[...])

    @pl.when(f == pl.num_programs(1) - 1)
    def _():
        x2 = x1_ref[...] + mod_ref[0, 5:6, :] * acc_ref[...]
        o_ref[...] = _rms_norm(x2, fg_ref[...])


def _mlp(h2, wu, wd, x1, mod, final_gain, rows_per_batch, tm, tf):
    m, d = h2.shape
    f = wu.shape[1]
    tpb = rows_per_batch // tm
    return pl.pallas_call(
        _mlp_kernel,
        grid=(m // tm, f // tf),
        in_specs=[pl.BlockSpec((tm, d), lambda i, j: (i, 0)),
                  pl.BlockSpec((d, tf), lambda i, j: (0, j)),
                  pl.BlockSpec((tf, d), lambda i, j: (j, 0)),
                  pl.BlockSpec((tm, d), lambda i, j: (i, 0)),
                  pl.BlockSpec((1, 6, d), lambda i, j: (i // tpb, 0, 0)),
                  pl.BlockSpec((1, d), lambda i, j: (0, 0))],
        out_specs=pl.BlockSpec((tm, d), lambda i, j: (i, 0)),
        out_shape=jax.ShapeDtypeStruct((m, d), F32),
        scratch_shapes=[pltpu.VMEM((tm, d), F32)],
        compiler_params=_cparams(("arbitrary", "arbitrary")),
        name="mlp",
    )(h2, wu, wd, x1, mod, final_gain)


def _forward(x, c, w_ada, b_ada, norm1_gain, w_in, b_in, rwkv_mix, rwkv_w0, rwkv_w2, rwkv_a0,
             rwkv_a2, rwkv_g2, rwkv_k_k, rwkv_k_a, rwkv_r_k, rwkv_ln_w, rwkv_ln_b, attn_sinks,
             w_branch_rwkv, w_branch_attn, w_out, norm2_gain, w_up, w_down, final_gain):
    batch, seq, d = x.shape
    depth = w_ada.shape[0]
    c_rwkv = rwkv_w0.shape[1]
    n_heads = attn_sinks.shape[1]
    qc = n_heads * HEAD
    kvc = 2 * (n_heads // ATTN_GROUP) * HEAD
    lora_w = DECAY_LORA + ICLR_LORA + GATE_LORA
    rwkv_cols = 3 * c_rwkv + lora_w
    attn_cols = qc + kvc
    offs = {"gate": 0, "r": 2 * d, "k": 2 * d + c_rwkv, "v": 2 * d + 2 * c_rwkv, "q": 2 * d + 3 * c_rwkv}
    offs["lora"] = offs["q"] + qc
    offs["kv"] = offs["lora"] + LORA_PAD
    n_pack = offs["kv"] + kvc

    def pack_cols(t):
        pad = jnp.zeros(t.shape[:-1] + (LORA_PAD - lora_w,), t.dtype)
        return jnp.concatenate([t[..., rwkv_cols + attn_cols:], t[..., :3 * c_rwkv],
                                t[..., rwkv_cols:rwkv_cols + qc], t[..., 3 * c_rwkv:rwkv_cols], pad,
                                t[..., rwkv_cols + qc:rwkv_cols + attn_cols]], axis=-1)

    tm_in = min(1024, seq)
    tn_in = 1280 if n_pack % 1280 == 0 else LANES
    tb = min(512, seq)
    tm_merge = min(256, seq)
    tm_mlp = min(512, seq)
    tf = min(1024, w_up.shape[2])

    assert depth == 1, "single-layer block"
    x2 = x.reshape(batch * seq, d)
    for l in range(depth):
        mod = _ada(c, w_ada[l], b_ada[l]).reshape(batch, 6, d)
        w_pack = pack_cols(w_in[l]).astype(BF16)
        b_pack = pack_cols(b_in[l]).reshape(1, n_pack)
        p = _inproj(x2, mod, norm1_gain[l].reshape(1, d), w_pack, b_pack, seq, tm_in, tn_in)

        mix = rwkv_mix[l]
        chan = jnp.stack([mix[:c_rwkv], mix[c_rwkv:2 * c_rwkv], mix[2 * c_rwkv:3 * c_rwkv],
                          rwkv_w0[l], rwkv_a0[l], rwkv_k_k[l], rwkv_k_a[l], rwkv_r_k[l].reshape(c_rwkv),
                          rwkv_ln_w[l], rwkv_ln_b[l]])
        chan = jnp.pad(chan, ((0, 16 - chan.shape[0]), (0, 0)))
        mix_lora = jnp.pad(mix[3 * c_rwkv:], (0, LORA_PAD - lora_w)).reshape(1, LORA_PAD)
        g2_pad = jnp.pad(rwkv_g2[l], ((0, LORA_PAD - lora_w), (0, 0)))
        y_rwkv = _rwkv(p, chan, mix_lora, rwkv_w2[l], rwkv_a2[l], g2_pad, batch, seq, offs, tb)
        y_attn = _attn(p, attn_sinks[l], seq, offs)

        x2, h2 = _merge(y_rwkv, y_attn, p, x2, mod, w_branch_rwkv[l].astype(BF16),
                        w_branch_attn[l].astype(BF16), w_out[l].astype(BF16),
                        norm2_gain[l].reshape(1, d), seq, tm_merge)
        x2 = _mlp(h2, w_up[l].astype(BF16), w_down[l].astype(BF16), x2, mod, final_gain.reshape(1, d),
                  seq, tm_mlp, tf)
    return x2.reshape(batch, seq, d)


def kernel(x, c, w_ada, b_ada, norm1_gain, w_in, b_in, rwkv_mix, rwkv_w0, rwkv_w2, rwkv_a0, rwkv_a2, rwkv_g2, rwkv_k_k, rwkv_k_a, rwkv_r_k, rwkv_ln_w, rwkv_ln_b, attn_sinks, w_branch_rwkv, w_branch_attn, w_out, norm2_gain, w_up, w_down, final_gain):
    return _forward(x, c, w_ada, b_ada, norm1_gain, w_in, b_in, rwkv_mix, rwkv_w0, rwkv_w2, rwkv_a0,
                    rwkv_a2, rwkv_g2, rwkv_k_k, rwkv_k_a, rwkv_r_k, rwkv_ln_w, rwkv_ln_b, attn_sinks,
                    w_branch_rwkv, w_branch_attn, w_out, norm2_gain, w_up, w_down, final_gain)
```

```python
import functools

import jax
import jax.numpy as jnp
from jax import lax
from jax.experimental import pallas as pl
from jax.experimental.pallas import tpu as pltpu

F32 = jnp.float32
BF16 = jnp.bfloat16
HIGHEST = lax.Precision.HIGHEST

LANES = 128
HEAD = 64
CHUNK = 128
WINDOW = 128
ATTN_GROUP = 8
NORM_EPS = 1e-6
GN_EPS = 64e-5
DECAY_LORA = 64
ICLR_LORA = 64
GATE_LORA = 160
LORA_PAD = 512
VMEM_LIMIT = 56 * 1024 * 1024


def _cparams(sem):
    return pltpu.CompilerParams(dimension_semantics=sem, vmem_limit_bytes=VMEM_LIMIT)


def _dot(a, b, precision=None):
    return jnp.dot(a, b, preferred_element_type=F32, precision=precision)


def _dot_nt(a, b, precision=None):
    return lax.dot_general(a, b, (((1,), (1,)), ((), ())), preferred_element_type=F32,
                           precision=precision)


def _dot_tn(a, b, precision=None):
    return lax.dot_general(a, b, (((0,), (0,)), ((), ())), preferred_element_type=F32,
                           precision=precision)


def _rms_norm(x, gain):
    ms = jnp.mean(x * x, axis=-1, keepdims=True)
    return x * lax.rsqrt(ms + NORM_EPS) * gain


def _ada_kernel(ct_ref, w_ref, b_ref, o_ref):
    ct = ct_ref[...]
    act = ct * jax.nn.sigmoid(ct)
    for m in range(o_ref.shape[0]):
        o_ref[m:m + 1, :] = jnp.sum(w_ref[...] * act[:, m:m + 1], axis=0, keepdims=True) + b_ref[...]


def _ada(c, w, b, tn=1024):
    nb, d = c.shape
    n = w.shape[1]
    return pl.pallas_call(
        _ada_kernel,
        grid=(n // tn,),
        in_specs=[pl.BlockSpec((d, nb), lambda j: (0, 0)),
                  pl.BlockSpec((d, tn), lambda j: (0, j)),
                  pl.BlockSpec((1, tn), lambda j: (0, j))],
        out_specs=pl.BlockSpec((nb, tn), lambda j: (0, j)),
        out_shape=jax.ShapeDtypeStruct((nb, n), F32),
        compiler_params=_cparams(("arbitrary",)),
        name="adaln",
    )(c.T, w, b.reshape(1, n))


def _pack_kernel(w_ref, o_ref, *, segments, zero_cols):
    for src, width, dst in segments:
        o_ref[:, dst:dst + width] = w_ref[:, src:src + width].astype(o_ref.dtype)
    lo, hi = zero_cols
    o_ref[:, lo:hi] = jnp.zeros((o_ref.shape[0], hi - lo), o_ref.dtype)


def _pack_weight(w, segments, zero_cols, n_pack, tk=256):
    k, n = w.shape
    return pl.pallas_call(
        functools.partial(_pack_kernel, segments=segments, zero_cols=zero_cols),
        grid=(k // tk,),
        in_specs=[pl.BlockSpec((tk, n), lambda i: (i, 0))],
        out_specs=pl.BlockSpec((tk, n_pack), lambda i: (i, 0)),
        out_shape=jax.ShapeDtypeStruct((k, n_pack), BF16),
        compiler_params=_cparams(("arbitrary",)),
        name="pack_w_in",
    )(w)


def _inproj_kernel(x_ref, mod_ref, g_ref, w_ref, b_ref, o_ref, h_ref):
    @pl.when(pl.program_id(1) == 0)
    def _():
        h = _rms_norm(x_ref[...], g_ref[...]) * (1.0 + mod_ref[0, 1:2, :]) + mod_ref[0, 0:1, :]
        h_ref[...] = h.astype(BF16)

    o_ref[...] = _dot(h_ref[...], w_ref[...]) + b_ref[...]


def _inproj(x2, mod, gain, w, b, rows_per_batch, tm, tn):
    m, d = x2.shape
    n = w.shape[1]
    tpb = rows_per_batch // tm
    return pl.pallas_call(
        _inproj_kernel,
        grid=(m // tm, n // tn),
        in_specs=[pl.BlockSpec((tm, d), lambda i, j: (i, 0)),
                  pl.BlockSpec((1, 6, d), lambda i, j: (i // tpb, 0, 0)),
                  pl.BlockSpec((1, d), lambda i, j: (0, 0)),
                  pl.BlockSpec((d, tn), lambda i, j: (0, j)),
                  pl.BlockSpec((1, tn), lambda i, j: (0, j))],
        out_specs=pl.BlockSpec((tm, tn), lambda i, j: (i, j)),
        out_shape=jax.ShapeDtypeStruct((m, n), F32),
        scratch_shapes=[pltpu.VMEM((tm, d), BF16)],
        compiler_params=_cparams(("arbitrary", "arbitrary")),
        name="inproj",
    )(x2, mod, gain, w, b)


def _head_sum(x):
    lo = lax.broadcasted_iota(jnp.int32, x.shape, 1) < HEAD
    s0 = jnp.sum(jnp.where(lo, x, 0.0), axis=-1, keepdims=True)
    s1 = jnp.sum(jnp.where(lo, 0.0, x), axis=-1, keepdims=True)
    return jnp.where(lo, s0, s1)


def _token_shift(x, carry_row, mix):
    rolled = pltpu.roll(x, 1, axis=0)
    first = lax.broadcasted_iota(jnp.int32, x.shape, 0) == 0
    prev = jnp.where(first, carry_row, rolled)
    return x + (prev - x) * mix


class _Operands:
    def __init__(self):
        self._parts = {}

    def parts(self, x, n):
        ent = self._parts.setdefault(id(x), [x])
        while len(ent) - 1 < n:
            rest = x
            for piece in ent[1:]:
                rest = rest - piece.astype(F32)
            ent.append(rest.astype(BF16))
        return ent[1:n + 1]

    def mm(self, a, b, passes, dims=(((1,), (0,)), ((), ()))):
        dot = lambda u, w: lax.dot_general(u, w, dims, preferred_element_type=F32)
        if passes == 1:
            return dot(self.parts(a, 1)[0], self.parts(b, 1)[0])
        a_hi, a_lo = self.parts(a, 2)
        b_hi, b_lo = self.parts(b, 2)
        return dot(a_hi, b_hi) + (dot(a_hi, b_lo) + dot(a_lo, b_hi))


_NT = (((1,), (1,)), ((), ()))
_TN = (((0,), (0,)), ((), ()))
SCAN_PASSES = {"gram": 1, "inverse": 1, "apply": 1, "state": 1}


def _scan_tile(r_s, lw_s, k_s, vsw_s, a_s, b_s, y_s, z_ref, n_chunks):
    ops = _Operands()
    row = lax.broadcasted_iota(jnp.int32, (CHUNK, CHUNK), 0)
    col = lax.broadcasted_iota(jnp.int32, (CHUNK, CHUNK), 1)
    incl = row >= col
    strict = row > col
    eye = row == col
    lo_lane = col < HEAD
    same_half = lo_lane == (row < HEAD)
    lo_lane2 = lax.broadcasted_iota(jnp.int32, (2 * CHUNK, LANES), 1) < HEAD
    tri = jnp.where(incl, 1.0, 0.0).astype(BF16)
    level_masks = []
    n = 1
    while n < CHUNK:
        level_masks.append((row // (2 * n) == col // (2 * n)) & ((row // n) % 2 == 1) & ((col // n) % 2 == 0))
        n *= 2

    chunks = []
    for c in range(n_chunks):
        sl = slice(c * CHUNK, (c + 1) * CHUNK)
        rc, lwc, kc, vsw, ac, bc = r_s[sl, :], lw_s[sl, :], k_s[sl, :], vsw_s[sl, :], a_s[sl, :], b_s[sl, :]
        lw = sum(jnp.dot(tri, piece, preferred_element_type=F32) for piece in ops.parts(lwc, 3))
        lw_end = lw[CHUNK - 1:CHUNK, :]
        e_in = jnp.exp(lw)
        e_ex = jnp.exp(lw - lwc)
        e_inv = jnp.exp(-lw)
        e_end = jnp.exp(lw_end - lw)
        chunks.append(dict(sl=sl, vsw=vsw, g_end=jnp.exp(lw_end), ra=rc * e_in, aa=ac * e_ex,
                           bi=bc * e_inv, ki=kc * e_inv, bh=bc * e_end, kh=kc * e_end))

    probs = []
    for ch in chunks:
        lhs_all = jnp.concatenate([ch["aa"], ch["ra"]], axis=0)
        rhs_all = jnp.concatenate([ch["bi"], ch["ki"]], axis=0)
        for h in range(LANES // HEAD):
            own = lo_lane if h == 0 else ~lo_lane
            own2 = lo_lane2 if h == 0 else ~lo_lane2
            gram = ops.mm(jnp.where(own2, lhs_all, 0.0), rhs_all, SCAN_PASSES["gram"], _NT)
            probs.append(dict(ch=ch, own=own,
                              m_ab=jnp.where(strict, gram[:CHUNK, :CHUNK], 0.0),
                              m_ak=jnp.where(strict, gram[:CHUNK, CHUNK:], 0.0),
                              m_rb=jnp.where(incl, gram[CHUNK:, :CHUNK], 0.0),
                              m_rk=jnp.where(incl, gram[CHUNK:, CHUNK:], 0.0)))

    for p in probs:
        p["t"] = jnp.where(eye, 1.0, 0.0) + jnp.where(level_masks[0], p["m_ab"], 0.0)
    for mask in level_masks[1:]:
        for p in probs:
            p["to"] = ops.mm(p["t"], jnp.where(mask, p["m_ab"], 0.0), SCAN_PASSES["inverse"])
        for p in probs:
            p["t"] = p["t"] + ops.mm(p["to"], p["t"], SCAN_PASSES["inverse"])

    for p in probs:
        p["vm"] = jnp.where(p["own"], 0.0, p["ch"]["vsw"])
        p["x"] = jnp.where(p["own"], p["ch"]["aa"], 0.0) + ops.mm(p["m_ak"], p["vm"], SCAN_PASSES["apply"])
    for p in probs:
        p["tx"] = ops.mm(p["t"], p["x"], SCAN_PASSES["apply"])
    for p in probs:
        p["y"] = (jnp.where(p["own"], p["ch"]["ra"], 0.0) + ops.mm(p["m_rb"], p["tx"], SCAN_PASSES["apply"])
                  + ops.mm(p["m_rk"], p["vm"], SCAN_PASSES["apply"]))

    for i, ch in enumerate(chunks):
        p0, p1 = probs[2 * i], probs[2 * i + 1]
        pq = ops.mm(ch["bh"], jnp.concatenate([p0["tx"], p1["tx"]], axis=1), SCAN_PASSES["apply"], _TN)
        pq = jnp.where(row < HEAD, pq[:, :LANES], pq[:, LANES:])
        ch["p"] = jnp.where(same_half, pq, 0.0) + jnp.where(eye, ch["g_end"], 0.0)
        ch["q"] = jnp.where(same_half, 0.0, pq + ops.mm(ch["kh"], ch["vsw"], SCAN_PASSES["apply"], _TN))
        ch["r"] = jnp.where(lo_lane, p0["y"], p1["y"])
        ch["y0"] = jnp.where(lo_lane, p1["y"], p0["y"])

    z = z_ref[...]
    for ch in chunks:
        y_s[ch["sl"], :] = ops.mm(ch["r"], z, SCAN_PASSES["state"]) + ch["y0"]
        z = ops.mm(ch["p"], z, SCAN_PASSES["state"]) + ch["q"]
    z_ref[...] = z


def _rwkv_kernel(r_ref, k_ref, v_ref, lora_ref, chan_ref, mixl_ref, w2_ref, a2_ref, g2_ref,
                 o_ref,
                 crkv_ref, clora_ref, z_ref, r_s, lw_s, k_s, v_s, vsw_s, a_s, b_s, y_s, g_s):
    tb = r_ref.shape[0]
    n_chunks = tb // CHUNK

    @pl.when(pl.program_id(2) == 0)
    def _():
        crkv_ref[...] = jnp.zeros_like(crkv_ref)
        clora_ref[...] = jnp.zeros_like(clora_ref)
        z_ref[...] = jnp.zeros_like(z_ref)

    chan = chan_ref[...]
    mix_r, mix_k, mix_v = chan[0:1], chan[1:2], chan[2:3]
    w0, a0, k_k, k_a, r_k = chan[3:4], chan[4:5], chan[5:6], chan[6:7], chan[7:8]

    r_in, k_in, v_in, lora_in = r_ref[...], k_ref[...], v_ref[...], lora_ref[...]
    r = _token_shift(r_in, crkv_ref[0:1, :], mix_r)
    k = _token_shift(k_in, crkv_ref[1:2, :], mix_k)
    v = _token_shift(v_in, crkv_ref[2:3, :], mix_v)
    lora = _token_shift(lora_in, clora_ref[...], mixl_ref[...])
    crkv_ref[0:1, :] = r_in[tb - 1:tb, :]
    crkv_ref[1:2, :] = k_in[tb - 1:tb, :]
    crkv_ref[2:3, :] = v_in[tb - 1:tb, :]
    clora_ref[...] = lora_in[tb - 1:tb, :]

    wd = lora[:, 0:DECAY_LORA]
    ad = lora[:, DECAY_LORA:DECAY_LORA + ICLR_LORA]
    gd = lora[:, DECAY_LORA + ICLR_LORA:]
    lora_ops = _Operands()
    dw = w0 + lora_ops.mm(jnp.tanh(wd), w2_ref[...], 3)
    z = -dw
    softplus = jnp.maximum(z, 0.0) + jnp.log(1.0 + jnp.exp(-jnp.abs(z)))
    logw = -jnp.exp(-softplus - 0.5)
    alr = jax.nn.sigmoid(a0 + lora_ops.mm(ad, a2_ref[...], 1))
    g_s[...] = lora_ops.mm(jax.nn.sigmoid(gd), g2_ref[...], 1)

    kk = k * k_k
    kk = kk / jnp.maximum(jnp.sqrt(_head_sum(kk * kk)), 1e-12)
    k_mod = k * (1.0 + (alr - 1.0) * k_a)

    r_s[...] = r
    lw_s[...] = logw
    k_s[...] = k_mod
    v_s[...] = v
    vsw_s[...] = pltpu.roll(v, HEAD, axis=1)
    a_s[...] = -kk
    b_s[...] = kk * alr

    _scan_tile(r_s, lw_s, k_s, vsw_s, a_s, b_s, y_s, z_ref, n_chunks)

    ln_w, ln_b = chan[8:9], chan[9:10]
    y = pltpu.roll(y_s[...], HEAD, axis=1)
    mu = _head_sum(y) * (1.0 / HEAD)
    yc = y - mu
    var = _head_sum(yc * yc) * (1.0 / HEAD)
    yn = yc * lax.rsqrt(var + GN_EPS) * ln_w + ln_b
    r, k_mod, v = r_s[...], k_s[...], v_s[...]
    bonus = _head_sum(r * k_mod * r_k) * v
    o_ref[...] = ((yn + bonus) * g_s[...]).astype(o_ref.dtype)


def _rwkv(p, chan, mix_lora, w2, a2, g2, batch, seq, offs, tb):
    m = p.shape[0]
    c = w2.shape[1]
    n_pairs = c // LANES
    tpb = seq // tb
    row = lambda b, hp, t: b * tpb + t
    tile = lambda: pltpu.VMEM((tb, LANES), F32)
    return pl.pallas_call(
        _rwkv_kernel,
        grid=(batch, n_pairs, tpb),
        in_specs=[pl.BlockSpec((tb, LANES), lambda b, hp, t: (row(b, hp, t), offs["r"] // LANES + hp)),
                  pl.BlockSpec((tb, LANES), lambda b, hp, t: (row(b, hp, t), offs["k"] // LANES + hp)),
                  pl.BlockSpec((tb, LANES), lambda b, hp, t: (row(b, hp, t), offs["v"] // LANES + hp)),
                  pl.BlockSpec((tb, LORA_PAD), lambda b, hp, t: (row(b, hp, t), offs["lora"] // LORA_PAD)),
                  pl.BlockSpec((16, LANES), lambda b, hp, t: (0, hp)),
                  pl.BlockSpec((1, LORA_PAD), lambda b, hp, t: (0, 0)),
                  pl.BlockSpec((DECAY_LORA, LANES), lambda b, hp, t: (0, hp)),
                  pl.BlockSpec((ICLR_LORA, LANES), lambda b, hp, t: (0, hp)),
                  pl.BlockSpec((LORA_PAD - DECAY_LORA - ICLR_LORA, LANES), lambda b, hp, t: (0, hp))],
        out_specs=pl.BlockSpec((tb, LANES), lambda b, hp, t: (row(b, hp, t), hp)),
        out_shape=jax.ShapeDtypeStruct((m, c), BF16),
        scratch_shapes=[pltpu.VMEM((8, LANES), F32), pltpu.VMEM((1, LORA_PAD), F32),
                        pltpu.VMEM((LANES, LANES), F32)] + [tile() for _ in range(9)],
        compiler_params=_cparams(("arbitrary", "arbitrary", "arbitrary")),
        name="rwkv7",
    )(p, p, p, p, chan, mix_lora, w2, a2, g2)


def _attn_kernel(sink_ref, q_ref, kvc_ref, kvp_ref, o_ref, *, blocks_per_seq, n_heads):
    nb = pl.program_id(0) % blocks_per_seq
    n_kv = n_heads // ATTN_GROUP
    kv = jnp.concatenate([kvp_ref[...], kvc_ref[...]], axis=0).astype(BF16)
    q = q_ref[...].astype(BF16)
    qi = lax.broadcasted_iota(jnp.int32, (WINDOW, 2 * WINDOW), 0)
    kj = lax.broadcasted_iota(jnp.int32, (WINDOW, 2 * WINDOW), 1)
    dist = qi + WINDOW - kj
    valid = (dist >= 0) & (dist < WINDOW) & ((kj >= WINDOW) | (nb > 0))
    dist_f = dist.astype(F32)
    for h in range(n_heads):
        hk = h // ATTN_GROUP
        slope = 2.0 ** (-8.0 * (h + 1) / n_heads)
        sink = sink_ref[h]
        qh = q[:, h * HEAD:(h + 1) * HEAD]
        kh = kv[:, hk * HEAD:(hk + 1) * HEAD]
        vh = kv[:, (n_kv + hk) * HEAD:(n_kv + hk + 1) * HEAD]
        s = _dot_nt(qh, kh) * (HEAD ** -0.5)
        s = jnp.where(valid, s - slope * dist_f, -jnp.inf)
        mx = jnp.maximum(jnp.max(s, axis=-1, keepdims=True), sink)
        e = jnp.exp(s - mx)
        denom = jnp.sum(e, axis=-1, keepdims=True) + jnp.exp(sink - mx)
        prob = e / denom
        o_ref[:, h * HEAD:(h + 1) * HEAD] = _dot(prob.astype(BF16), vh).astype(o_ref.dtype)


def _attn(p, sinks, seq, offs):
    m = p.shape[0]
    n_heads = sinks.shape[0]
    qc = n_heads * HEAD
    kvc = 2 * (n_heads // ATTN_GROUP) * HEAD
    bps = seq // WINDOW
    return pl.pallas_call(
        functools.partial(_attn_kernel, blocks_per_seq=bps, n_heads=n_heads),
        grid=(m // WINDOW,),
        in_specs=[pl.BlockSpec(memory_space=pltpu.SMEM),
                  pl.BlockSpec((WINDOW, qc), lambda n: (n, offs["q"] // qc)),
                  pl.BlockSpec((WINDOW, kvc), lambda n: (n, offs["kv"] // kvc)),
                  pl.BlockSpec((WINDOW, kvc), lambda n: (jnp.maximum(n - 1, 0), offs["kv"] // kvc))],
        out_specs=pl.BlockSpec((WINDOW, qc), lambda n: (n, 0)),
        out_shape=jax.ShapeDtypeStruct((m, qc), BF16),
        compiler_params=_cparams(("arbitrary",)),
        name="swa",
    )(sinks, p, p, p)


def _merge_kernel(yr_ref, ya_ref, gate_ref, x_ref, mod_ref, wbr_ref, wba_ref, wout_ref, g2_ref,
                  x1_ref, h2_ref):
    d = x_ref.shape[1]
    gates = jax.nn.sigmoid(gate_ref[...])
    merged = gates[:, :d] * _dot(yr_ref[...], wbr_ref[...]) + gates[:, d:] * _dot(ya_ref[...], wba_ref[...])
    x1 = x_ref[...] + mod_ref[0, 2:3, :] * _dot(merged.astype(BF16), wout_ref[...])
    x1_ref[...] = x1
    h2 = _rms_norm(x1, g2_ref[...]) * (1.0 + mod_ref[0, 4:5, :]) + mod_ref[0, 3:4, :]
    h2_ref[...] = h2.astype(BF16)


def _merge(yr, ya, p, x2, mod, wbr, wba, wout, gain2, rows_per_batch, tm):
    m, d = x2.shape
    c = yr.shape[1]
    tpb = rows_per_batch // tm
    const = lambda shape: pl.BlockSpec(shape, lambda i: (0, 0), pipeline_mode=pl.Buffered(1))
    return pl.pallas_call(
        _merge_kernel,
        grid=(m // tm,),
        in_specs=[pl.BlockSpec((tm, c), lambda i: (i, 0)),
                  pl.BlockSpec((tm, c), lambda i: (i, 0)),
                  pl.BlockSpec((tm, 2 * d), lambda i: (i, 0)),
                  pl.BlockSpec((tm, d), lambda i: (i, 0)),
                  pl.BlockSpec((1, 6, d), lambda i: (i // tpb, 0, 0)),
                  const((c, d)), const((c, d)), const((d, d)),
                  pl.BlockSpec((1, d), lambda i: (0, 0))],
        out_specs=[pl.BlockSpec((tm, d), lambda i: (i, 0)),
                   pl.BlockSpec((tm, d), lambda i: (i, 0))],
        out_shape=[jax.ShapeDtypeStruct((m, d), F32), jax.ShapeDtypeStruct((m, d), BF16)],
        compiler_params=_cparams(("arbitrary",)),
        name="merge",
    )(yr, ya, p, x2, mod, wbr, wba, wout, gain2)


def _mlp_kernel(h_ref, wu_ref, wd_ref, x1_ref, mod_ref, fg_ref, o_ref, acc_ref):
    f = pl.program_id(1)

    @pl.when(f == 0)
    def _():
        acc_ref[...] = jnp.zeros_like(acc_ref)

    u = _dot(h_ref[...], wu_ref[...])
    act = jnp.square(jnp.maximum(u, 0.0)).astype(BF16)
    acc_ref[...] += _dot(act, wd_ref[...])

    @pl.when(f == pl.num_programs(1) - 1)
    def _():
        x2 = x1_ref[...] + mod_ref[0, 5:6, :] * acc_ref[...]
        o_ref[...] = _rms_norm(x2, fg_ref[...])


def _mlp(h2, wu, wd, x1, mod, final_gain, rows_per_batch, tm, tf):
    m, d = h2.shape
    f = wu.shape[1]
    tpb = rows_per_batch // tm
    return pl.pallas_call(
        _mlp_kernel,
        grid=(m // tm, f // tf),
        in_specs=[pl.BlockSpec((tm, d), lambda i, j: (i, 0)),
                  pl.BlockSpec((d, tf), lambda i, j: (0, j)),
                  pl.BlockSpec((tf, d), lambda i, j: (j, 0)),
                  pl.BlockSpec((tm, d), lambda i, j: (i, 0)),
                  pl.BlockSpec((1, 6, d), lambda i, j: (i // tpb, 0, 0)),
                  pl.BlockSpec((1, d), lambda i, j: (0, 0))],
        out_specs=pl.BlockSpec((tm, d), lambda i, j: (i, 0)),
        out_shape=jax.ShapeDtypeStruct((m, d), F32),
        scratch_shapes=[pltpu.VMEM((tm, d), F32)],
        compiler_params=_cparams(("arbitrary", "arbitrary")),
        name="mlp",
    )(h2, wu, wd, x1, mod, final_gain)


def _forward(x, c, w_ada, b_ada, norm1_gain, w_in, b_in, rwkv_mix, rwkv_w0, rwkv_w2, rwkv_a0,
             rwkv_a2, rwkv_g2, rwkv_k_k, rwkv_k_a, rwkv_r_k, rwkv_ln_w, rwkv_ln_b, attn_sinks,
             w_branch_rwkv, w_branch_attn, w_out, norm2_gain, w_up, w_down, final_gain):
    batch, seq, d = x.shape
    depth = w_ada.shape[0]
    c_rwkv = rwkv_w0.shape[1]
    n_heads = attn_sinks.shape[1]
    qc = n_heads * HEAD
    kvc = 2 * (n_heads // ATTN_GROUP) * HEAD
    lora_w = DECAY_LORA + ICLR_LORA + GATE_LORA
    rwkv_cols = 3 * c_rwkv + lora_w
    attn_cols = qc + kvc
    offs = {"gate": 0, "r": 2 * d, "k": 2 * d + c_rwkv, "v": 2 * d + 2 * c_rwkv, "q": 2 * d + 3 * c_rwkv}
    offs["lora"] = offs["q"] + qc
    offs["kv"] = offs["lora"] + LORA_PAD
    n_pack = offs["kv"] + kvc

    segments = ((rwkv_cols + attn_cols, 2 * d, offs["gate"]), (0, 3 * c_rwkv, offs["r"]),
                (rwkv_cols, qc, offs["q"]), (3 * c_rwkv, lora_w, offs["lora"]),
                (rwkv_cols + qc, kvc, offs["kv"]))
    zero_cols = (offs["lora"] + lora_w, offs["kv"])

    def pack_cols(t):
        out = jnp.zeros(t.shape[:-1] + (n_pack,), t.dtype)
        for src, width, dst in segments:
            out = out.at[..., dst:dst + width].set(t[..., src:src + width])
        return out

    tm_in = min(1024, seq)
    tn_in = 1280 if n_pack % 1280 == 0 else LANES
    tb = min(512, seq)
    tm_merge = min(256, seq)
    tm_mlp = min(512, seq)
    tf = min(1024, w_up.shape[2])

    assert depth == 1, "single-layer block"
    layer = lambda t: t.reshape(t.shape[1:])
    x2 = x.reshape(batch * seq, d)
    for l in range(depth):
        mod = _ada(c, layer(w_ada), b_ada[l]).reshape(batch, 6, d)
        w_pack = _pack_weight(layer(w_in), segments, zero_cols, n_pack)
        b_pack = pack_cols(b_in[l]).reshape(1, n_pack)
        p = _inproj(x2, mod, norm1_gain[l].reshape(1, d), w_pack, b_pack, seq, tm_in, tn_in)

        mix = rwkv_mix[l]
        chan = jnp.stack([mix[:c_rwkv], mix[c_rwkv:2 * c_rwkv], mix[2 * c_rwkv:3 * c_rwkv],
                          rwkv_w0[l], rwkv_a0[l], rwkv_k_k[l], rwkv_k_a[l], rwkv_r_k[l].reshape(c_rwkv),
                          rwkv_ln_w[l], rwkv_ln_b[l]])
        chan = jnp.pad(chan, ((0, 16 - chan.shape[0]), (0, 0)))
        mix_lora = jnp.pad(mix[3 * c_rwkv:], (0, LORA_PAD - lora_w)).reshape(1, LORA_PAD)
        g2_pad = jnp.pad(rwkv_g2[l], ((0, LORA_PAD - lora_w), (0, 0)))
        y_rwkv = _rwkv(p, chan, mix_lora, rwkv_w2[l], rwkv_a2[l], g2_pad, batch, seq, offs, tb)
        y_attn = _attn(p, attn_sinks[l], seq, offs)

        x2, h2 = _merge(y_rwkv, y_attn, p, x2, mod, layer(w_branch_rwkv).astype(BF16),
                        layer(w_branch_attn).astype(BF16), layer(w_out).astype(BF16),
                        norm2_gain[l].reshape(1, d), seq, tm_merge)
        x2 = _mlp(h2, layer(w_up).astype(BF16), layer(w_down).astype(BF16), x2, mod,
                  final_gain.reshape(1, d), seq, tm_mlp, tf)
    return x2.reshape(batch, seq, d)


def kernel(x, c, w_ada, b_ada, norm1_gain, w_in, b_in, rwkv_mix, rwkv_w0, rwkv_w2, rwkv_a0, rwkv_a2, rwkv_g2, rwkv_k_k, rwkv_k_a, rwkv_r_k, rwkv_ln_w, rwkv_ln_b, attn_sinks, w_branch_rwkv, w_branch_attn, w_out, norm2_gain, w_up, w_down, final_gain):
    return _forward(x, c, w_ada, b_ada, norm1_gain, w_in, b_in, rwkv_mix, rwkv_w0, rwkv_w2, rwkv_a0,
                    rwkv_a2, rwkv_g2, rwkv_k_k, rwkv_k_a, rwkv_r_k, rwkv_ln_w, rwkv_ln_b, attn_sinks,
                    w_branch_rwkv, w_branch_attn, w_out, norm2_gain, w_up, w_down, final_gain)
```

```python
import functools

import jax
import jax.numpy as jnp
from jax import lax
from jax.experimental import pallas as pl
from jax.experimental.pallas import tpu as pltpu

F32 = jnp.float32
BF16 = jnp.bfloat16
HIGHEST = lax.Precision.HIGHEST

LANES = 128
SUBLANES = 8
HEAD = 64
CHUNK = 128
WINDOW = 128
ATTN_GROUP = 8
NORM_EPS = 1e-6
GN_EPS = 64e-5
DECAY_LORA = 64
ICLR_LORA = 64
GATE_LORA = 160
LORA_PAD = 512
VMEM_LIMIT = 56 * 1024 * 1024


def _cparams(sem):
    return pltpu.CompilerParams(dimension_semantics=sem, vmem_limit_bytes=VMEM_LIMIT)


def _dot(a, b, precision=None):
    return jnp.dot(a, b, preferred_element_type=F32, precision=precision)


def _dot_nt(a, b, precision=None):
    return lax.dot_general(a, b, (((1,), (1,)), ((), ())), preferred_element_type=F32,
                           precision=precision)


def _dot_tn(a, b, precision=None):
    return lax.dot_general(a, b, (((0,), (0,)), ((), ())), preferred_element_type=F32,
                           precision=precision)


def _rms_norm(x, gain):
    ms = jnp.mean(x * x, axis=-1, keepdims=True)
    return x * lax.rsqrt(ms + NORM_EPS) * gain


def _ada_kernel(ct_ref, w_ref, b_ref, o_ref):
    ct = ct_ref[...]
    act = ct * jax.nn.sigmoid(ct)
    for m in range(o_ref.shape[0]):
        o_ref[m:m + 1, :] = jnp.sum(w_ref[...] * act[:, m:m + 1], axis=0, keepdims=True) + b_ref[...]


def _ada(c, w, b, tn=1024):
    nb, d = c.shape
    n = w.shape[1]
    return pl.pallas_call(
        _ada_kernel,
        grid=(n // tn,),
        in_specs=[pl.BlockSpec((d, nb), lambda j: (0, 0)),
                  pl.BlockSpec((d, tn), lambda j: (0, j)),
                  pl.BlockSpec((1, tn), lambda j: (0, j))],
        out_specs=pl.BlockSpec((nb, tn), lambda j: (0, j)),
        out_shape=jax.ShapeDtypeStruct((nb, n), F32),
        compiler_params=_cparams(("arbitrary",)),
        name="adaln",
    )(c.T, w, b.reshape(1, n))


def _pack_kernel(w_ref, o_ref, *, segments, zero_rows):
    for src, width, dst in segments:
        o_ref[dst:dst + width, :] = w_ref[src:src + width, :].astype(o_ref.dtype)
    lo, hi = zero_rows
    o_ref[lo:hi, :] = jnp.zeros((hi - lo, o_ref.shape[1]), o_ref.dtype)


def _pack_weight_t(w_t, segments, zero_rows, n_pack, tk=256):
    n, k = w_t.shape
    return pl.pallas_call(
        functools.partial(_pack_kernel, segments=segments, zero_rows=zero_rows),
        grid=(k // tk,),
        in_specs=[pl.BlockSpec((n, tk), lambda i: (0, i))],
        out_specs=pl.BlockSpec((n_pack, tk), lambda i: (0, i)),
        out_shape=jax.ShapeDtypeStruct((n_pack, k), BF16),
        compiler_params=_cparams(("arbitrary",)),
        name="pack_w_in",
    )(w_t)


def _inproj_kernel(x_ref, mod_ref, g_ref, w_ref, b_ref, o_ref, h_ref):
    @pl.when(pl.program_id(1) == 0)
    def _():
        h = _rms_norm(x_ref[...], g_ref[...]) * (1.0 + mod_ref[0, 1:2, :]) + mod_ref[0, 0:1, :]
        h_ref[...] = h.astype(BF16)

    o_ref[...] = _dot_nt(h_ref[...], w_ref[...]) + b_ref[...]


def _inproj(x2, mod, gain, w_t, b, rows_per_batch, tm, tn):
    m, d = x2.shape
    n = w_t.shape[0]
    tpb = rows_per_batch // tm
    return pl.pallas_call(
        _inproj_kernel,
        grid=(m // tm, n // tn),
        in_specs=[pl.BlockSpec((tm, d), lambda i, j: (i, 0)),
                  pl.BlockSpec((1, 6, d), lambda i, j: (i // tpb, 0, 0)),
                  pl.BlockSpec((1, d), lambda i, j: (0, 0)),
                  pl.BlockSpec((tn, d), lambda i, j: (j, 0)),
                  pl.BlockSpec((1, tn), lambda i, j: (0, j))],
        out_specs=pl.BlockSpec((tm, tn), lambda i, j: (i, j)),
        out_shape=jax.ShapeDtypeStruct((m, n), F32),
        scratch_shapes=[pltpu.VMEM((tm, d), BF16)],
        compiler_params=_cparams(("arbitrary", "arbitrary")),
        name="inproj",
    )(x2, mod, gain, w_t, b)


def _head_sum(x):
    lo = lax.broadcasted_iota(jnp.int32, x.shape, 1) < HEAD
    s0 = jnp.sum(jnp.where(lo, x, 0.0), axis=-1, keepdims=True)
    s1 = jnp.sum(jnp.where(lo, 0.0, x), axis=-1, keepdims=True)
    return jnp.where(lo, s0, s1)


def _token_shift(x, carry_row, mix):
    rolled = pltpu.roll(x, 1, axis=0)
    first = lax.broadcasted_iota(jnp.int32, x.shape, 0) == 0
    prev = jnp.where(first, carry_row, rolled)
    return x + (prev - x) * mix


class _Operands:
    def __init__(self):
        self._parts = {}

    def parts(self, x, n):
        ent = self._parts.setdefault(id(x), [x])
        while len(ent) - 1 < n:
            rest = x
            for piece in ent[1:]:
                rest = rest - piece.astype(F32)
            ent.append(rest.astype(BF16))
        return ent[1:n + 1]

    def mm(self, a, b, passes, dims=(((1,), (0,)), ((), ()))):
        dot = lambda u, w: lax.dot_general(u, w, dims, preferred_element_type=F32)
        if passes == 1:
            return dot(self.parts(a, 1)[0], self.parts(b, 1)[0])
        a_hi, a_lo = self.parts(a, 2)
        b_hi, b_lo = self.parts(b, 2)
        return dot(a_hi, b_hi) + (dot(a_hi, b_lo) + dot(a_lo, b_hi))


_NT = (((1,), (1,)), ((), ()))
_TN = (((0,), (0,)), ((), ()))
SCAN_PASSES = {"gram": 1, "inverse": 1, "apply": 1, "state": 1}


def _scan_tile(r_s, lw_s, k_s, vsw_s, a_s, b_s, y_s, z_ref, n_chunks):
    ops = _Operands()
    row = lax.broadcasted_iota(jnp.int32, (CHUNK, CHUNK), 0)
    col = lax.broadcasted_iota(jnp.int32, (CHUNK, CHUNK), 1)
    incl = row >= col
    strict = row > col
    eye = row == col
    lo_lane = col < HEAD
    same_half = lo_lane == (row < HEAD)
    lo_lane2 = lax.broadcasted_iota(jnp.int32, (2 * CHUNK, LANES), 1) < HEAD
    tri = jnp.where(incl, 1.0, 0.0).astype(BF16)
    level_masks = []
    n = 1
    while n < CHUNK:
        level_masks.append((row // (2 * n) == col // (2 * n)) & ((row // n) % 2 == 1) & ((col // n) % 2 == 0))
        n *= 2

    chunks = []
    for c in range(n_chunks):
        sl = slice(c * CHUNK, (c + 1) * CHUNK)
        rc, lwc, kc, vsw, ac, bc = r_s[sl, :], lw_s[sl, :], k_s[sl, :], vsw_s[sl, :], a_s[sl, :], b_s[sl, :]
        lw = sum(jnp.dot(tri, piece, preferred_element_type=F32) for piece in ops.parts(lwc, 2))
        lw_end = lw[CHUNK - 1:CHUNK, :]
        e_in = jnp.exp(lw)
        e_ex = jnp.exp(lw - lwc)
        e_inv = jnp.exp(-lw)
        e_end = jnp.exp(lw_end - lw)
        chunks.append(dict(sl=sl, vsw=vsw, g_end=jnp.exp(lw_end), ra=rc * e_in, aa=ac * e_ex,
                           bi=bc * e_inv, ki=kc * e_inv, bh=bc * e_end, kh=kc * e_end))

    probs = []
    for ch in chunks:
        lhs_all = jnp.concatenate([ch["aa"], ch["ra"]], axis=0)
        rhs_all = jnp.concatenate([ch["bi"], ch["ki"]], axis=0)
        for h in range(LANES // HEAD):
            own = lo_lane if h == 0 else ~lo_lane
            own2 = lo_lane2 if h == 0 else ~lo_lane2
            gram = ops.mm(jnp.where(own2, lhs_all, 0.0), rhs_all, SCAN_PASSES["gram"], _NT)
            probs.append(dict(ch=ch, own=own,
                              m_ab=jnp.where(strict, gram[:CHUNK, :CHUNK], 0.0),
                              m_ak=jnp.where(strict, gram[:CHUNK, CHUNK:], 0.0),
                              m_rb=jnp.where(incl, gram[CHUNK:, :CHUNK], 0.0),
                              m_rk=jnp.where(incl, gram[CHUNK:, CHUNK:], 0.0)))

    for p in probs:
        p["t"] = jnp.where(eye, 1.0, 0.0) + jnp.where(level_masks[0], p["m_ab"], 0.0)
    n = 2
    for mask in level_masks[1:]:
        if n < SUBLANES:
            for p in probs:
                p["to"] = ops.mm(p["t"], jnp.where(mask, p["m_ab"], 0.0), SCAN_PASSES["inverse"])
            for p in probs:
                p["t"] = p["t"] + ops.mm(p["to"], p["t"], SCAN_PASSES["inverse"])
        else:
            lower = lambda t, n=n: jnp.concatenate(
                [t[i:i + n] for i in range(n, CHUNK, 2 * n)], axis=0)
            for p in probs:
                p["tl"] = lower(p["t"])
                p["to"] = ops.mm(p["tl"], jnp.where(mask, p["m_ab"], 0.0), SCAN_PASSES["inverse"])
            for p in probs:
                tl = p["tl"] + ops.mm(p["to"], p["t"], SCAN_PASSES["inverse"])
                slabs = []
                for j, i in enumerate(range(0, CHUNK, 2 * n)):
                    slabs += [p["t"][i:i + n], tl[j * n:(j + 1) * n]]
                p["t"] = jnp.concatenate(slabs, axis=0)
        n *= 2

    for p in probs:
        p["vm"] = jnp.where(p["own"], 0.0, p["ch"]["vsw"])
        p["x"] = jnp.where(p["own"], p["ch"]["aa"], 0.0) + ops.mm(p["m_ak"], p["vm"], SCAN_PASSES["apply"])
    for p in probs:
        p["tx"] = ops.mm(p["t"], p["x"], SCAN_PASSES["apply"])
    for p in probs:
        p["y"] = jnp.where(p["own"], p["ch"]["ra"], 0.0) + ops.mm(
            jnp.concatenate([p["m_rb"], p["m_rk"]], axis=1),
            jnp.concatenate([p["tx"], p["vm"]], axis=0), SCAN_PASSES["apply"])

    for i, ch in enumerate(chunks):
        p0, p1 = probs[2 * i], probs[2 * i + 1]
        pq = ops.mm(jnp.concatenate([ch["bh"], ch["kh"]], axis=0),
                    jnp.concatenate([jnp.concatenate([p0["tx"], p1["tx"]], axis=1),
                                     jnp.concatenate([p0["vm"], p1["vm"]], axis=1)], axis=0),
                    SCAN_PASSES["apply"], _TN)
        pq = jnp.where(row < HEAD, pq[:, :LANES], pq[:, LANES:])
        ch["p"] = jnp.where(same_half, pq, 0.0) + jnp.where(eye, ch["g_end"], 0.0)
        ch["q"] = jnp.where(same_half, 0.0, pq)
        ch["r"] = jnp.where(lo_lane, p0["y"], p1["y"])
        ch["y0"] = jnp.where(lo_lane, p1["y"], p0["y"])

    z = z_ref[...]
    for ch in chunks:
        rz = ops.mm(jnp.concatenate([ch["r"], ch["p"]], axis=0), z, SCAN_PASSES["state"])
        y_s[ch["sl"], :] = rz[:CHUNK] + ch["y0"]
        z = rz[CHUNK:] + ch["q"]
    z_ref[...] = z


def _rwkv_kernel(r_ref, k_ref, v_ref, lora_ref, chan_ref, mixl_ref, w2_ref, a2_ref, g2_ref,
                 o_ref,
                 crkv_ref, clora_ref, z_ref, r_s, lw_s, k_s, v_s, vsw_s, a_s, b_s, y_s, g_s):
    tb = r_ref.shape[0]
    n_chunks = tb // CHUNK

    @pl.when(pl.program_id(2) == 0)
    def _():
        crkv_ref[...] = jnp.zeros_like(crkv_ref)
        clora_ref[...] = jnp.zeros_like(clora_ref)
        z_ref[...] = jnp.zeros_like(z_ref)

    chan = chan_ref[...]
    mix_r, mix_k, mix_v = chan[0:1], chan[1:2], chan[2:3]
    w0, a0, k_k, k_a, r_k = chan[3:4], chan[4:5], chan[5:6], chan[6:7], chan[7:8]

    r_in, k_in, v_in, lora_in = r_ref[...], k_ref[...], v_ref[...], lora_ref[...]
    r = _token_shift(r_in, crkv_ref[0:1, :], mix_r)
    k = _token_shift(k_in, crkv_ref[1:2, :], mix_k)
    v = _token_shift(v_in, crkv_ref[2:3, :], mix_v)
    lora = _token_shift(lora_in, clora_ref[...], mixl_ref[...])
    crkv_ref[0:1, :] = r_in[tb - 1:tb, :]
    crkv_ref[1:2, :] = k_in[tb - 1:tb, :]
    crkv_ref[2:3, :] = v_in[tb - 1:tb, :]
    clora_ref[...] = lora_in[tb - 1:tb, :]

    wd = lora[:, 0:DECAY_LORA]
    ad = lora[:, DECAY_LORA:DECAY_LORA + ICLR_LORA]
    gd = lora[:, DECAY_LORA + ICLR_LORA:]
    lora_ops = _Operands()
    dw = w0 + lora_ops.mm(jnp.tanh(wd), w2_ref[...], 3)
    z = -dw
    softplus = jnp.maximum(z, 0.0) + jnp.log(1.0 + jnp.exp(-jnp.abs(z)))
    logw = -jnp.exp(-softplus - 0.5)
    alr = jax.nn.sigmoid(a0 + lora_ops.mm(ad, a2_ref[...], 1))
    g_s[...] = lora_ops.mm(jax.nn.sigmoid(gd), g2_ref[...], 1)

    kk = k * k_k
    kk = kk / jnp.maximum(jnp.sqrt(_head_sum(kk * kk)), 1e-12)
    k_mod = k * (1.0 + (alr - 1.0) * k_a)

    r_s[...] = r
    lw_s[...] = logw
    k_s[...] = k_mod
    v_s[...] = v
    vsw_s[...] = pltpu.roll(v, HEAD, axis=1)
    a_s[...] = -kk
    b_s[...] = kk * alr

    _scan_tile(r_s, lw_s, k_s, vsw_s, a_s, b_s, y_s, z_ref, n_chunks)

    ln_w, ln_b = chan[8:9], chan[9:10]
    y = pltpu.roll(y_s[...], HEAD, axis=1)
    mu = _head_sum(y) * (1.0 / HEAD)
    yc = y - mu
    var = _head_sum(yc * yc) * (1.0 / HEAD)
    yn = yc * lax.rsqrt(var + GN_EPS) * ln_w + ln_b
    r, k_mod, v = r_s[...], k_s[...], v_s[...]
    bonus = _head_sum(r * k_mod * r_k) * v
    o_ref[...] = ((yn + bonus) * g_s[...]).astype(o_ref.dtype)


def _rwkv(p, chan, mix_lora, w2, a2, g2, batch, seq, offs, tb):
    m = p.shape[0]
    c = w2.shape[1]
    n_pairs = c // LANES
    tpb = seq // tb
    row = lambda b, hp, t: b * tpb + t
    tile = lambda: pltpu.VMEM((tb, LANES), F32)
    return pl.pallas_call(
        _rwkv_kernel,
        grid=(batch, n_pairs, tpb),
        in_specs=[pl.BlockSpec((tb, LANES), lambda b, hp, t: (row(b, hp, t), offs["r"] // LANES + hp)),
                  pl.BlockSpec((tb, LANES), lambda b, hp, t: (row(b, hp, t), offs["k"] // LANES + hp)),
                  pl.BlockSpec((tb, LANES), lambda b, hp, t: (row(b, hp, t), offs["v"] // LANES + hp)),
                  pl.BlockSpec((tb, LORA_PAD), lambda b, hp, t: (row(b, hp, t), offs["lora"] // LORA_PAD)),
                  pl.BlockSpec((16, LANES), lambda b, hp, t: (0, hp)),
                  pl.BlockSpec((1, LORA_PAD), lambda b, hp, t: (0, 0)),
                  pl.BlockSpec((DECAY_LORA, LANES), lambda b, hp, t: (0, hp)),
                  pl.BlockSpec((ICLR_LORA, LANES), lambda b, hp, t: (0, hp)),
                  pl.BlockSpec((LORA_PAD - DECAY_LORA - ICLR_LORA, LANES), lambda b, hp, t: (0, hp))],
        out_specs=pl.BlockSpec((tb, LANES), lambda b, hp, t: (row(b, hp, t), hp)),
        out_shape=jax.ShapeDtypeStruct((m, c), BF16),
        scratch_shapes=[pltpu.VMEM((8, LANES), F32), pltpu.VMEM((1, LORA_PAD), F32),
                        pltpu.VMEM((LANES, LANES), F32)] + [tile() for _ in range(9)],
        compiler_params=_cparams(("arbitrary", "arbitrary", "arbitrary")),
        name="rwkv7",
    )(p, p, p, p, chan, mix_lora, w2, a2, g2)


def _attn_kernel(sink_ref, q_ref, kvc_ref, kvp_ref, o_ref, *, blocks_per_seq, n_heads):
    nb = pl.program_id(0) % blocks_per_seq
    n_kv = n_heads // ATTN_GROUP
    kv = jnp.concatenate([kvp_ref[...], kvc_ref[...]], axis=0).astype(BF16)
    q = q_ref[...].astype(BF16)
    qi = lax.broadcasted_iota(jnp.int32, (WINDOW, 2 * WINDOW), 0)
    kj = lax.broadcasted_iota(jnp.int32, (WINDOW, 2 * WINDOW), 1)
    dist = qi + WINDOW - kj
    valid = (dist >= 0) & (dist < WINDOW) & ((kj >= WINDOW) | (nb > 0))
    dist_f = dist.astype(F32)
    for h in range(n_heads):
        hk = h // ATTN_GROUP
        slope = 2.0 ** (-8.0 * (h + 1) / n_heads)
        sink = sink_ref[h]
        qh = q[:, h * HEAD:(h + 1) * HEAD]
        kh = kv[:, hk * HEAD:(hk + 1) * HEAD]
        vh = kv[:, (n_kv + hk) * HEAD:(n_kv + hk + 1) * HEAD]
        s = _dot_nt(qh, kh) * (HEAD ** -0.5)
        s = jnp.where(valid, s - slope * dist_f, -jnp.inf)
        mx = jnp.maximum(jnp.max(s, axis=-1, keepdims=True), sink)
        e = jnp.exp(s - mx)
        denom = jnp.sum(e, axis=-1, keepdims=True) + jnp.exp(sink - mx)
        prob = e / denom
        o_ref[:, h * HEAD:(h + 1) * HEAD] = _dot(prob.astype(BF16), vh).astype(o_ref.dtype)


def _attn(p, sinks, seq, offs):
    m = p.shape[0]
    n_heads = sinks.shape[0]
    qc = n_heads * HEAD
    kvc = 2 * (n_heads // ATTN_GROUP) * HEAD
    bps = seq // WINDOW
    return pl.pallas_call(
        functools.partial(_attn_kernel, blocks_per_seq=bps, n_heads=n_heads),
        grid=(m // WINDOW,),
        in_specs=[pl.BlockSpec(memory_space=pltpu.SMEM),
                  pl.BlockSpec((WINDOW, qc), lambda n: (n, offs["q"] // qc)),
                  pl.BlockSpec((WINDOW, kvc), lambda n: (n, offs["kv"] // kvc)),
                  pl.BlockSpec((WINDOW, kvc), lambda n: (jnp.maximum(n - 1, 0), offs["kv"] // kvc))],
        out_specs=pl.BlockSpec((WINDOW, qc), lambda n: (n, 0)),
        out_shape=jax.ShapeDtypeStruct((m, qc), BF16),
        compiler_params=_cparams(("arbitrary",)),
        name="swa",
    )(sinks, p, p, p)


def _merge_kernel(yr_ref, ya_ref, gate_ref, x_ref, mod_ref, wbr_ref, wba_ref, wout_ref, g2_ref,
                  x1_ref, h2_ref):
    d = x_ref.shape[1]
    gates = jax.nn.sigmoid(gate_ref[...])
    merged = gates[:, :d] * _dot(yr_ref[...], wbr_ref[...]) + gates[:, d:] * _dot(ya_ref[...], wba_ref[...])
    x1 = x_ref[...] + mod_ref[0, 2:3, :] * _dot(merged.astype(BF16), wout_ref[...])
    x1_ref[...] = x1
    h2 = _rms_norm(x1, g2_ref[...]) * (1.0 + mod_ref[0, 4:5, :]) + mod_ref[0, 3:4, :]
    h2_ref[...] = h2.astype(BF16)


def _merge(yr, ya, p, x2, mod, wbr, wba, wout, gain2, rows_per_batch, tm):
    m, d = x2.shape
    c = yr.shape[1]
    tpb = rows_per_batch // tm
    const = lambda shape: pl.BlockSpec(shape, lambda i: (0, 0), pipeline_mode=pl.Buffered(1))
    return pl.pallas_call(
        _merge_kernel,
        grid=(m // tm,),
        in_specs=[pl.BlockSpec((tm, c), lambda i: (i, 0)),
                  pl.BlockSpec((tm, c), lambda i: (i, 0)),
                  pl.BlockSpec((tm, 2 * d), lambda i: (i, 0)),
                  pl.BlockSpec((tm, d), lambda i: (i, 0)),
                  pl.BlockSpec((1, 6, d), lambda i: (i // tpb, 0, 0)),
                  const((c, d)), const((c, d)), const((d, d)),
                  pl.BlockSpec((1, d), lambda i: (0, 0))],
        out_specs=[pl.BlockSpec((tm, d), lambda i: (i, 0)),
                   pl.BlockSpec((tm, d), lambda i: (i, 0))],
        out_shape=[jax.ShapeDtypeStruct((m, d), F32), jax.ShapeDtypeStruct((m, d), BF16)],
        compiler_params=_cparams(("arbitrary",)),
        name="merge",
    )(yr, ya, p, x2, mod, wbr, wba, wout, gain2)


def _mlp_kernel(h_ref, wu_ref, wd_ref, x1_ref, mod_ref, fg_ref, o_ref, acc_ref):
    f = pl.program_id(1)

    @pl.when(f == 0)
    def _():
        acc_ref[...] = jnp.zeros_like(acc_ref)

    u = _dot(h_ref[...], wu_ref[...])
    act = jnp.square(jnp.maximum(u, 0.0)).astype(BF16)
    acc_ref[...] += _dot(act, wd_ref[...])

    @pl.when(f == pl.num_programs(1) - 1)
    def _():
        x2 = x1_ref[...] + mod_ref[0, 5:6, :] * acc_ref[...]
        o_ref[...] = _rms_norm(x2, fg_ref[...])


def _mlp(h2, wu, wd, x1, mod, final_gain, rows_per_batch, tm, tf):
    m, d = h2.shape
    f = wu.shape[1]
    tpb = rows_per_batch // tm
    return pl.pallas_call(
        _mlp_kernel,
        grid=(m // tm, f // tf),
        in_specs=[pl.BlockSpec((tm, d), lambda i, j: (i, 0)),
                  pl.BlockSpec((d, tf), lambda i, j: (0, j)),
                  pl.BlockSpec((tf, d), lambda i, j: (j, 0)),
                  pl.BlockSpec((tm, d), lambda i, j: (i, 0)),
                  pl.BlockSpec((1, 6, d), lambda i, j: (i // tpb, 0, 0)),
                  pl.BlockSpec((1, d), lambda i, j: (0, 0))],
        out_specs=pl.BlockSpec((tm, d), lambda i, j: (i, 0)),
        out_shape=jax.ShapeDtypeStruct((m, d), F32),
        scratch_shapes=[pltpu.VMEM((tm, d), F32)],
        compiler_params=_cparams(("arbitrary", "arbitrary")),
        name="mlp",
    )(h2, wu, wd, x1, mod, final_gain)


def _forward(x, c, w_ada, b_ada, norm1_gain, w_in, b_in, rwkv_mix, rwkv_w0, rwkv_w2, rwkv_a0,
             rwkv_a2, rwkv_g2, rwkv_k_k, rwkv_k_a, rwkv_r_k, rwkv_ln_w, rwkv_ln_b, attn_sinks,
             w_branch_rwkv, w_branch_attn, w_out, norm2_gain, w_up, w_down, final_gain):
    batch, seq, d = x.shape
    depth = w_ada.shape[0]
    c_rwkv = rwkv_w0.shape[1]
    n_heads = attn_sinks.shape[1]
    qc = n_heads * HEAD
    kvc = 2 * (n_heads // ATTN_GROUP) * HEAD
    lora_w = DECAY_LORA + ICLR_LORA + GATE_LORA
    rwkv_cols = 3 * c_rwkv + lora_w
    attn_cols = qc + kvc
    offs = {"gate": 0, "r": 2 * d, "k": 2 * d + c_rwkv, "v": 2 * d + 2 * c_rwkv, "q": 2 * d + 3 * c_rwkv}
    offs["lora"] = offs["q"] + qc
    offs["kv"] = offs["lora"] + LORA_PAD
    n_pack = offs["kv"] + kvc

    segments = ((rwkv_cols + attn_cols, 2 * d, offs["gate"]), (0, 3 * c_rwkv, offs["r"]),
                (rwkv_cols, qc, offs["q"]), (3 * c_rwkv, lora_w, offs["lora"]),
                (rwkv_cols + qc, kvc, offs["kv"]))
    zero_rows = (offs["lora"] + lora_w, offs["kv"])

    def pack_cols(t):
        out = jnp.zeros(t.shape[:-1] + (n_pack,), t.dtype)
        for src, width, dst in segments:
            out = out.at[..., dst:dst + width].set(t[..., src:src + width])
        return out

    tm_in = min(1024, seq)
    tn_in = 1280 if n_pack % 1280 == 0 else LANES
    tb = min(512, seq)
    tm_merge = min(256, seq)
    tm_mlp = min(512, seq)
    tf = min(1024, w_up.shape[2])

    assert depth == 1, "single-layer block"
    layer = lambda t: t.reshape(t.shape[1:])
    x2 = x.reshape(batch * seq, d)
    for l in range(depth):
        mod = _ada(c, layer(w_ada), b_ada[l]).reshape(batch, 6, d)
        w_pack = _pack_weight_t(layer(w_in).T, segments, zero_rows, n_pack)
        b_pack = pack_cols(b_in[l]).reshape(1, n_pack)
        p = _inproj(x2, mod, norm1_gain[l].reshape(1, d), w_pack, b_pack, seq, tm_in, tn_in)

        mix = rwkv_mix[l]
        chan = jnp.stack([mix[:c_rwkv], mix[c_rwkv:2 * c_rwkv], mix[2 * c_rwkv:3 * c_rwkv],
                          rwkv_w0[l], rwkv_a0[l], rwkv_k_k[l], rwkv_k_a[l], rwkv_r_k[l].reshape(c_rwkv),
                          rwkv_ln_w[l], rwkv_ln_b[l]])
        chan = jnp.pad(chan, ((0, 16 - chan.shape[0]), (0, 0)))
        mix_lora = jnp.pad(mix[3 * c_rwkv:], (0, LORA_PAD - lora_w)).reshape(1, LORA_PAD)
        g2_pad = jnp.pad(rwkv_g2[l], ((0, LORA_PAD - lora_w), (0, 0)))
        y_rwkv = _rwkv(p, chan, mix_lora, rwkv_w2[l], rwkv_a2[l], g2_pad, batch, seq, offs, tb)
        y_attn = _attn(p, attn_sinks[l], seq, offs)

        x2, h2 = _merge(y_rwkv, y_attn, p, x2, mod, layer(w_branch_rwkv).astype(BF16),
                        layer(w_branch_attn).astype(BF16), layer(w_out).astype(BF16),
                        norm2_gain[l].reshape(1, d), seq, tm_merge)
        x2 = _mlp(h2, layer(w_up).astype(BF16), layer(w_down).astype(BF16), x2, mod,
                  final_gain.reshape(1, d), seq, tm_mlp, tf)
    return x2.reshape(batch, seq, d)


def kernel(x, c, w_ada, b_ada, norm1_gain, w_in, b_in, rwkv_mix, rwkv_w0, rwkv_w2, rwkv_a0, rwkv_a2, rwkv_g2, rwkv_k_k, rwkv_k_a, rwkv_r_k, rwkv_ln_w, rwkv_ln_b, attn_sinks, w_branch_rwkv, w_branch_attn, w_out, norm2_gain, w_up, w_down, final_gain):
    return _forward(x, c, w_ada, b_ada, norm1_gain, w_in, b_in, rwkv_mix, rwkv_w0, rwkv_w2, rwkv_a0,
                    rwkv_a2, rwkv_g2, rwkv_k_k, rwkv_k_a, rwkv_r_k, rwkv_ln_w, rwkv_ln_b, attn_sinks,
                    w_branch_rwkv, w_branch_attn, w_out, norm2_gain, w_up, w_down, final_gain)
```

```python
import functools

import jax
import jax.numpy as jnp
from jax import lax
from jax.experimental import pallas as pl
from jax.experimental.pallas import tpu as pltpu

F32 = jnp.float32
BF16 = jnp.bfloat16
HIGHEST = lax.Precision.HIGHEST

LANES = 128
SUBLANES = 8
HEAD = 64
CHUNK = 128
WINDOW = 128
ATTN_GROUP = 8
NORM_EPS = 1e-6
GN_EPS = 64e-5
DECAY_LORA = 64
ICLR_LORA = 64
GATE_LORA = 160
LORA_PAD = 512
VMEM_LIMIT = 56 * 1024 * 1024


def _cparams(sem):
    return pltpu.CompilerParams(dimension_semantics=sem, vmem_limit_bytes=VMEM_LIMIT)


def _dot(a, b, precision=None):
    return jnp.dot(a, b, preferred_element_type=F32, precision=precision)


def _dot_nt(a, b, precision=None):
    return lax.dot_general(a, b, (((1,), (1,)), ((), ())), preferred_element_type=F32,
                           precision=precision)


def _dot_tn(a, b, precision=None):
    return lax.dot_general(a, b, (((0,), (0,)), ((), ())), preferred_element_type=F32,
                           precision=precision)


def _rms_norm(x, gain):
    ms = jnp.mean(x * x, axis=-1, keepdims=True)
    return x * lax.rsqrt(ms + NORM_EPS) * gain


def _ada_kernel(ct_ref, w_ref, b_ref, o_ref):
    ct = ct_ref[...]
    act = ct * jax.nn.sigmoid(ct)
    for m in range(o_ref.shape[0]):
        o_ref[m:m + 1, :] = jnp.sum(w_ref[...] * act[:, m:m + 1], axis=0, keepdims=True) + b_ref[...]


def _ada(c, w, b, tn=1024):
    nb, d = c.shape
    n = w.shape[1]
    return pl.pallas_call(
        _ada_kernel,
        grid=(n // tn,),
        in_specs=[pl.BlockSpec((d, nb), lambda j: (0, 0)),
                  pl.BlockSpec((d, tn), lambda j: (0, j)),
                  pl.BlockSpec((1, tn), lambda j: (0, j))],
        out_specs=pl.BlockSpec((nb, tn), lambda j: (0, j)),
        out_shape=jax.ShapeDtypeStruct((nb, n), F32),
        compiler_params=_cparams(("arbitrary",)),
        name="adaln",
    )(c.T, w, b.reshape(1, n))


def _pack_kernel(w_ref, o_ref, *, segments, zero_rows):
    for src, width, dst in segments:
        o_ref[dst:dst + width, :] = w_ref[src:src + width, :].astype(o_ref.dtype)
    lo, hi = zero_rows
    o_ref[lo:hi, :] = jnp.zeros((hi - lo, o_ref.shape[1]), o_ref.dtype)


def _pack_weight_t(w_t, segments, zero_rows, n_pack, tk=256):
    n, k = w_t.shape
    return pl.pallas_call(
        functools.partial(_pack_kernel, segments=segments, zero_rows=zero_rows),
        grid=(k // tk,),
        in_specs=[pl.BlockSpec((n, tk), lambda i: (0, i))],
        out_specs=pl.BlockSpec((n_pack, tk), lambda i: (0, i)),
        out_shape=jax.ShapeDtypeStruct((n_pack, k), BF16),
        compiler_params=_cparams(("arbitrary",)),
        name="pack_w_in",
    )(w_t)


def _inproj_kernel(x_ref, mod_ref, g_ref, w_ref, b_ref, o_ref, h_ref):
    @pl.when(pl.program_id(1) == 0)
    def _():
        h = _rms_norm(x_ref[...], g_ref[...]) * (1.0 + mod_ref[0, 1:2, :]) + mod_ref[0, 0:1, :]
        h_ref[...] = h.astype(BF16)

    o_ref[...] = _dot_nt(h_ref[...], w_ref[...]) + b_ref[...]


def _inproj(x2, mod, gain, w_t, b, rows_per_batch, tm, tn):
    m, d = x2.shape
    n = w_t.shape[0]
    tpb = rows_per_batch // tm
    return pl.pallas_call(
        _inproj_kernel,
        grid=(m // tm, n // tn),
        in_specs=[pl.BlockSpec((tm, d), lambda i, j: (i, 0)),
                  pl.BlockSpec((1, 6, d), lambda i, j: (i // tpb, 0, 0)),
                  pl.BlockSpec((1, d), lambda i, j: (0, 0)),
                  pl.BlockSpec((tn, d), lambda i, j: (j, 0)),
                  pl.BlockSpec((1, tn), lambda i, j: (0, j))],
        out_specs=pl.BlockSpec((tm, tn), lambda i, j: (i, j)),
        out_shape=jax.ShapeDtypeStruct((m, n), F32),
        scratch_shapes=[pltpu.VMEM((tm, d), BF16)],
        compiler_params=_cparams(("arbitrary", "arbitrary")),
        name="inproj",
    )(x2, mod, gain, w_t, b)


def _head_sum(x):
    lo = lax.broadcasted_iota(jnp.int32, x.shape, 1) < HEAD
    s0 = jnp.sum(jnp.where(lo, x, 0.0), axis=-1, keepdims=True)
    s1 = jnp.sum(jnp.where(lo, 0.0, x), axis=-1, keepdims=True)
    return jnp.where(lo, s0, s1)


def _token_shift(x, carry_row, mix):
    rolled = pltpu.roll(x, 1, axis=0)
    first = lax.broadcasted_iota(jnp.int32, x.shape, 0) == 0
    prev = jnp.where(first, carry_row, rolled)
    return x + (prev - x) * mix


class _Operands:
    def __init__(self):
        self._parts = {}

    def parts(self, x, n):
        ent = self._parts.setdefault(id(x), [x])
        while len(ent) - 1 < n:
            rest = x
            for piece in ent[1:]:
                rest = rest - piece.astype(F32)
            ent.append(rest.astype(BF16))
        return ent[1:n + 1]

    def mm(self, a, b, passes, dims=(((1,), (0,)), ((), ()))):
        dot = lambda u, w: lax.dot_general(u, w, dims, preferred_element_type=F32)
        if passes == 1:
            return dot(self.parts(a, 1)[0], self.parts(b, 1)[0])
        a_hi, a_lo = self.parts(a, 2)
        b_hi, b_lo = self.parts(b, 2)
        return dot(a_hi, b_hi) + (dot(a_hi, b_lo) + dot(a_lo, b_hi))


_NT = (((1,), (1,)), ((), ()))
_TN = (((0,), (0,)), ((), ()))
SCAN_PASSES = {"gram": 1, "inverse": 1, "apply": 1, "state": 1}


def _scan_precompute(ops, load_chunk, n_chunks, tick):
    row = lax.broadcasted_iota(jnp.int32, (CHUNK, CHUNK), 0)
    col = lax.broadcasted_iota(jnp.int32, (CHUNK, CHUNK), 1)
    incl = row >= col
    strict = row > col
    eye = row == col
    lo_lane = col < HEAD
    same_half = lo_lane == (row < HEAD)
    lo_lane2 = lax.broadcasted_iota(jnp.int32, (2 * CHUNK, LANES), 1) < HEAD
    tri = jnp.where(incl, 1.0, 0.0).astype(BF16)
    level_masks = []
    n = 1
    while n < CHUNK:
        level_masks.append((row // (2 * n) == col // (2 * n)) & ((row // n) % 2 == 1) & ((col // n) % 2 == 0))
        n *= 2

    chunks = []
    for c in range(n_chunks):
        rc, lwc, kc, vsw, ac, bc = load_chunk(c)
        lw = sum(jnp.dot(tri, piece, preferred_element_type=F32) for piece in ops.parts(lwc, 2))
        lw_end = lw[CHUNK - 1:CHUNK, :]
        e_in = jnp.exp(lw)
        e_ex = jnp.exp(lw - lwc)
        e_inv = jnp.exp(-lw)
        e_end = jnp.exp(lw_end - lw)
        chunks.append(dict(vsw=vsw, g_end=jnp.exp(lw_end), ra=rc * e_in, aa=ac * e_ex,
                           bi=bc * e_inv, ki=kc * e_inv, bh=bc * e_end, kh=kc * e_end))
    tick()

    probs = []
    for ch in chunks:
        lhs_all = jnp.concatenate([ch["aa"], ch["ra"]], axis=0)
        rhs_all = jnp.concatenate([ch["bi"], ch["ki"]], axis=0)
        for h in range(LANES // HEAD):
            own = lo_lane if h == 0 else ~lo_lane
            own2 = lo_lane2 if h == 0 else ~lo_lane2
            gram = ops.mm(jnp.where(own2, lhs_all, 0.0), rhs_all, SCAN_PASSES["gram"], _NT)
            probs.append(dict(ch=ch, own=own,
                              m_ab=jnp.where(strict, gram[:CHUNK, :CHUNK], 0.0),
                              m_ak=jnp.where(strict, gram[:CHUNK, CHUNK:], 0.0),
                              m_rb=jnp.where(incl, gram[CHUNK:, :CHUNK], 0.0),
                              m_rk=jnp.where(incl, gram[CHUNK:, CHUNK:], 0.0)))
    tick()

    for p in probs:
        p["t"] = jnp.where(eye, 1.0, 0.0) + jnp.where(level_masks[0], p["m_ab"], 0.0)
    n = 2
    for mask in level_masks[1:]:
        if n < SUBLANES:
            for p in probs:
                p["to"] = ops.mm(p["t"], jnp.where(mask, p["m_ab"], 0.0), SCAN_PASSES["inverse"])
            for p in probs:
                p["t"] = p["t"] + ops.mm(p["to"], p["t"], SCAN_PASSES["inverse"])
        else:
            lower = lambda t, n=n: jnp.concatenate(
                [t[i:i + n] for i in range(n, CHUNK, 2 * n)], axis=0)
            for p in probs:
                p["tl"] = lower(p["t"])
                p["to"] = ops.mm(p["tl"], jnp.where(mask, p["m_ab"], 0.0), SCAN_PASSES["inverse"])
            for p in probs:
                tl = p["tl"] + ops.mm(p["to"], p["t"], SCAN_PASSES["inverse"])
                slabs = []
                for j, i in enumerate(range(0, CHUNK, 2 * n)):
                    slabs += [p["t"][i:i + n], tl[j * n:(j + 1) * n]]
                p["t"] = jnp.concatenate(slabs, axis=0)
        n *= 2
        tick()

    for p in probs:
        p["vm"] = jnp.where(p["own"], 0.0, p["ch"]["vsw"])
        p["x"] = jnp.where(p["own"], p["ch"]["aa"], 0.0) + ops.mm(p["m_ak"], p["vm"], SCAN_PASSES["apply"])
    for p in probs:
        p["tx"] = ops.mm(p["t"], p["x"], SCAN_PASSES["apply"])
    for p in probs:
        p["y"] = jnp.where(p["own"], p["ch"]["ra"], 0.0) + ops.mm(
            jnp.concatenate([p["m_rb"], p["m_rk"]], axis=1),
            jnp.concatenate([p["tx"], p["vm"]], axis=0), SCAN_PASSES["apply"])

    for i, ch in enumerate(chunks):
        p0, p1 = probs[2 * i], probs[2 * i + 1]
        pq = ops.mm(jnp.concatenate([ch["bh"], ch["kh"]], axis=0),
                    jnp.concatenate([jnp.concatenate([p0["tx"], p1["tx"]], axis=1),
                                     jnp.concatenate([p0["vm"], p1["vm"]], axis=1)], axis=0),
                    SCAN_PASSES["apply"], _TN)
        pq = jnp.where(row < HEAD, pq[:, :LANES], pq[:, LANES:])
        ch["p"] = jnp.where(same_half, pq, 0.0) + jnp.where(eye, ch["g_end"], 0.0)
        ch["q"] = jnp.where(same_half, 0.0, pq)
        ch["r"] = jnp.where(lo_lane, p0["y"], p1["y"])
        ch["y0"] = jnp.where(lo_lane, p1["y"], p0["y"])
    return chunks


def _rwkv_kernel(r_ref, k_ref, v_ref, lora_ref, chan_ref, chanf_ref, mixl_ref, w2_ref, a2_ref, g2_ref,
                 o_ref,
                 crkv_ref, clora_ref, z_ref, twd_s, ad_s, sgd_s, r2_s, k2_s, v2_s, g2_s,
                 lw_s, vsw_s, a_s, b_s, y_s, p_s, q_s, rr_s, y0_s, *, n_pairs, n_tblocks, n_tiles):
    tb = r_ref.shape[0]
    n_chunks = tb // CHUNK
    s = pl.program_id(0)
    sa = jnp.minimum(s, n_tiles - 1)
    sf = jnp.maximum(s - 1, 0)
    hp, t_a, slot = sa % n_pairs, (sa // n_pairs) % n_tblocks, s % 2
    hpf, t_f, slotf = sf % n_pairs, (sf // n_pairs) % n_tblocks, (s + 1) % 2

    @pl.when(s == 0)
    def _():
        for ref in (crkv_ref, clora_ref, z_ref, r2_s, k2_s, v2_s, g2_s, p_s, q_s, rr_s, y0_s):
            ref[...] = jnp.zeros_like(ref)

    @pl.when(hp == 0)
    def _():
        lora_in = lora_ref[...]
        carry = jnp.where(t_a == 0, 0.0, clora_ref[...])
        lora = _token_shift(lora_in, carry, mixl_ref[...])
        clora_ref[...] = lora_in[tb - 1:tb, :]
        twd = jnp.tanh(lora[:, 0:DECAY_LORA])
        twd_hi = twd.astype(BF16)
        twd_s[0] = twd_hi
        twd_s[1] = (twd - twd_hi.astype(F32)).astype(BF16)
        ad_s[...] = lora[:, DECAY_LORA:DECAY_LORA + ICLR_LORA].astype(BF16)
        sgd_s[...] = jax.nn.sigmoid(lora[:, DECAY_LORA + ICLR_LORA:]).astype(BF16)

    ops = _Operands()
    chanf = chanf_ref[...]
    fin = dict(z=jnp.where(t_f == 0, 0.0, z_ref[hpf]), c=0)

    def finish_piece():
        c = fin["c"]
        fin["c"] = c + 1
        if c < n_chunks:
            i = slotf * n_chunks + c
            rz = ops.mm(jnp.concatenate([rr_s[i], p_s[i]], axis=0), fin["z"], SCAN_PASSES["state"])
            y_s[c * CHUNK:(c + 1) * CHUNK, :] = rz[:CHUNK] + y0_s[i]
            fin["z"] = rz[CHUNK:] + q_s[i]
        elif c == n_chunks:
            z_ref[hpf] = fin["z"]
            r_k, ln_w, ln_b = chanf[7:8], chanf[8:9], chanf[9:10]
            y = pltpu.roll(y_s[...], HEAD, axis=1)
            mu = _head_sum(y) * (1.0 / HEAD)
            yc = y - mu
            var = _head_sum(yc * yc) * (1.0 / HEAD)
            yn = yc * lax.rsqrt(var + GN_EPS) * ln_w + ln_b
            bonus = _head_sum(r2_s[slotf] * k2_s[slotf] * r_k) * v2_s[slotf]
            o_ref[...] = ((yn + bonus) * g2_s[slotf]).astype(o_ref.dtype)

    chan = chan_ref[...]
    mix_r, mix_k, mix_v = chan[0:1], chan[1:2], chan[2:3]
    w0, a0, k_k, k_a = chan[3:4], chan[4:5], chan[5:6], chan[6:7]

    finish_piece()
    r_in, k_in, v_in = r_ref[...], k_ref[...], v_ref[...]
    carry = jnp.where(t_a == 0, 0.0, crkv_ref[hp])
    r = _token_shift(r_in, carry[0:1], mix_r)
    k = _token_shift(k_in, carry[1:2], mix_k)
    v = _token_shift(v_in, carry[2:3], mix_v)
    crkv_ref[hp, 0:1, :] = r_in[tb - 1:tb, :]
    crkv_ref[hp, 1:2, :] = k_in[tb - 1:tb, :]
    crkv_ref[hp, 2:3, :] = v_in[tb - 1:tb, :]

    finish_piece()
    w2_hi = w2_ref[...].astype(BF16)
    w2_lo = (w2_ref[...] - w2_hi.astype(F32)).astype(BF16)
    twd_hi, twd_lo = twd_s[0], twd_s[1]
    dw = w0 + (_dot(twd_hi, w2_hi) + (_dot(twd_hi, w2_lo) + _dot(twd_lo, w2_hi)))
    zneg = -dw
    softplus = jnp.maximum(zneg, 0.0) + jnp.log(1.0 + jnp.exp(-jnp.abs(zneg)))
    logw = -jnp.exp(-softplus - 0.5)
    alr = jax.nn.sigmoid(a0 + _dot(ad_s[...], a2_ref[...].astype(BF16)))
    g2_s[slot] = _dot(sgd_s[...], g2_ref[...].astype(BF16))

    kk = k * k_k
    kk = kk / jnp.maximum(jnp.sqrt(_head_sum(kk * kk)), 1e-12)
    r2_s[slot] = r
    k2_s[slot] = k * (1.0 + (alr - 1.0) * k_a)
    v2_s[slot] = v
    lw_s[...] = logw
    vsw_s[...] = pltpu.roll(v, HEAD, axis=1)
    a_s[...] = -kk
    b_s[...] = kk * alr

    def load_chunk(c):
        sl = slice(c * CHUNK, (c + 1) * CHUNK)
        return r2_s[slot, sl, :], lw_s[sl, :], k2_s[slot, sl, :], vsw_s[sl, :], a_s[sl, :], b_s[sl, :]

    chunks = _scan_precompute(ops, load_chunk, n_chunks, finish_piece)
    for c, ch in enumerate(chunks):
        i = slot * n_chunks + c
        p_s[i], q_s[i], rr_s[i], y0_s[i] = ch["p"], ch["q"], ch["r"], ch["y0"]
    while fin["c"] <= n_chunks:
        finish_piece()


def _rwkv(p, chan, mix_lora, w2, a2, g2, batch, seq, offs, tb):
    m = p.shape[0]
    c = w2.shape[1]
    n_pairs = c // LANES
    n_tblocks = seq // tb
    n_tiles = batch * n_tblocks * n_pairs
    n_chunks = tb // CHUNK
    gate_w = LORA_PAD - DECAY_LORA - ICLR_LORA
    prep = lambda s: jnp.minimum(s, n_tiles - 1)
    fin = lambda s: jnp.maximum(s - 1, 0)
    col = lambda name: lambda s: (prep(s) // n_pairs, offs[name] // LANES + prep(s) % n_pairs)
    weight = lambda s: (0, prep(s) % n_pairs)
    tile = lambda: pltpu.VMEM((tb, LANES), F32)
    tile2 = lambda: pltpu.VMEM((2, tb, LANES), F32)
    mats = lambda: pltpu.VMEM((2 * n_chunks, CHUNK, LANES), F32)
    return pl.pallas_call(
        functools.partial(_rwkv_kernel, n_pairs=n_pairs, n_tblocks=n_tblocks, n_tiles=n_tiles),
        grid=(n_tiles + 1,),
        in_specs=[pl.BlockSpec((tb, LANES), col("r")),
                  pl.BlockSpec((tb, LANES), col("k")),
                  pl.BlockSpec((tb, LANES), col("v")),
                  pl.BlockSpec((tb, LORA_PAD), lambda s: (prep(s) // n_pairs, offs["lora"] // LORA_PAD)),
                  pl.BlockSpec((16, LANES), weight),
                  pl.BlockSpec((16, LANES), lambda s: (0, fin(s) % n_pairs)),
                  pl.BlockSpec((1, LORA_PAD), lambda s: (0, 0)),
                  pl.BlockSpec((DECAY_LORA, LANES), weight),
                  pl.BlockSpec((ICLR_LORA, LANES), weight),
                  pl.BlockSpec((gate_w, LANES), weight)],
        out_specs=pl.BlockSpec((tb, LANES), lambda s: (fin(s) // n_pairs, fin(s) % n_pairs)),
        out_shape=jax.ShapeDtypeStruct((m, c), BF16),
        scratch_shapes=[pltpu.VMEM((n_pairs, SUBLANES, LANES), F32),
                        pltpu.VMEM((1, LORA_PAD), F32),
                        pltpu.VMEM((n_pairs, CHUNK, LANES), F32),
                        pltpu.VMEM((2, tb, DECAY_LORA), BF16),
                        pltpu.VMEM((tb, ICLR_LORA), BF16),
                        pltpu.VMEM((tb, gate_w), BF16),
                        tile2(), tile2(), tile2(), tile2(),
                        tile(), tile(), tile(), tile(), tile(),
                        mats(), mats(), mats(), mats()],
        compiler_params=_cparams(("arbitrary",)),
        name="rwkv7",
    )(p, p, p, p, chan, chan, mix_lora, w2, a2, g2)


def _attn_kernel(sink_ref, q_ref, kvc_ref, kvp_ref, o_ref, *, blocks_per_seq, n_heads):
    nb = pl.program_id(0) % blocks_per_seq
    n_kv = n_heads // ATTN_GROUP
    kv = jnp.concatenate([kvp_ref[...], kvc_ref[...]], axis=0).astype(BF16)
    q = q_ref[...].astype(BF16)
    qi = lax.broadcasted_iota(jnp.int32, (WINDOW, 2 * WINDOW), 0)
    kj = lax.broadcasted_iota(jnp.int32, (WINDOW, 2 * WINDOW), 1)
    dist = qi + WINDOW - kj
    valid = (dist >= 0) & (dist < WINDOW) & ((kj >= WINDOW) | (nb > 0))
    dist_f = dist.astype(F32)
    for h in range(n_heads):
        hk = h // ATTN_GROUP
        slope = 2.0 ** (-8.0 * (h + 1) / n_heads)
        sink = sink_ref[h]
        qh = q[:, h * HEAD:(h + 1) * HEAD]
        kh = kv[:, hk * HEAD:(hk + 1) * HEAD]
        vh = kv[:, (n_kv + hk) * HEAD:(n_kv + hk + 1) * HEAD]
        s = _dot_nt(qh, kh) * (HEAD ** -0.5)
        s = jnp.where(valid, s - slope * dist_f, -jnp.inf)
        mx = jnp.maximum(jnp.max(s, axis=-1, keepdims=True), sink)
        e = jnp.exp(s - mx)
        denom = jnp.sum(e, axis=-1, keepdims=True) + jnp.exp(sink - mx)
        prob = e / denom
        o_ref[:, h * HEAD:(h + 1) * HEAD] = _dot(prob.astype(BF16), vh).astype(o_ref.dtype)


def _attn(p, sinks, seq, offs):
    m = p.shape[0]
    n_heads = sinks.shape[0]
    qc = n_heads * HEAD
    kvc = 2 * (n_heads // ATTN_GROUP) * HEAD
    bps = seq // WINDOW
    return pl.pallas_call(
        functools.partial(_attn_kernel, blocks_per_seq=bps, n_heads=n_heads),
        grid=(m // WINDOW,),
        in_specs=[pl.BlockSpec(memory_space=pltpu.SMEM),
                  pl.BlockSpec((WINDOW, qc), lambda n: (n, offs["q"] // qc)),
                  pl.BlockSpec((WINDOW, kvc), lambda n: (n, offs["kv"] // kvc)),
                  pl.BlockSpec((WINDOW, kvc), lambda n: (jnp.maximum(n - 1, 0), offs["kv"] // kvc))],
        out_specs=pl.BlockSpec((WINDOW, qc), lambda n: (n, 0)),
        out_shape=jax.ShapeDtypeStruct((m, qc), BF16),
        compiler_params=_cparams(("arbitrary",)),
        name="swa",
    )(sinks, p, p, p)


def _merge_kernel(yr_ref, ya_ref, gate_ref, x_ref, mod_ref, wbr_ref, wba_ref, wout_ref, g2_ref,
                  x1_ref, h2_ref):
    d = x_ref.shape[1]
    gates = jax.nn.sigmoid(gate_ref[...])
    merged = gates[:, :d] * _dot(yr_ref[...], wbr_ref[...]) + gates[:, d:] * _dot(ya_ref[...], wba_ref[...])
    x1 = x_ref[...] + mod_ref[0, 2:3, :] * _dot(merged.astype(BF16), wout_ref[...])
    x1_ref[...] = x1
    h2 = _rms_norm(x1, g2_ref[...]) * (1.0 + mod_ref[0, 4:5, :]) + mod_ref[0, 3:4, :]
    h2_ref[...] = h2.astype(BF16)


def _merge(yr, ya, p, x2, mod, wbr, wba, wout, gain2, rows_per_batch, tm):
    m, d = x2.shape
    c = yr.shape[1]
    tpb = rows_per_batch // tm
    const = lambda shape: pl.BlockSpec(shape, lambda i: (0, 0), pipeline_mode=pl.Buffered(1))
    return pl.pallas_call(
        _merge_kernel,
        grid=(m // tm,),
        in_specs=[pl.BlockSpec((tm, c), lambda i: (i, 0)),
                  pl.BlockSpec((tm, c), lambda i: (i, 0)),
                  pl.BlockSpec((tm, 2 * d), lambda i: (i, 0)),
                  pl.BlockSpec((tm, d), lambda i: (i, 0)),
                  pl.BlockSpec((1, 6, d), lambda i: (i // tpb, 0, 0)),
                  const((c, d)), const((c, d)), const((d, d)),
                  pl.BlockSpec((1, d), lambda i: (0, 0))],
        out_specs=[pl.BlockSpec((tm, d), lambda i: (i, 0)),
                   pl.BlockSpec((tm, d), lambda i: (i, 0))],
        out_shape=[jax.ShapeDtypeStruct((m, d), F32), jax.ShapeDtypeStruct((m, d), BF16)],
        compiler_params=_cparams(("arbitrary",)),
        name="merge",
    )(yr, ya, p, x2, mod, wbr, wba, wout, gain2)


def _mlp_kernel(h_ref, wu_ref, wd_ref, x1_ref, mod_ref, fg_ref, o_ref, acc_ref):
    f = pl.program_id(1)

    @pl.when(f == 0)
    def _():
        acc_ref[...] = jnp.zeros_like(acc_ref)

    u = _dot(h_ref[...], wu_ref[...])
    act = jnp.square(jnp.maximum(u, 0.0)).astype(BF16)
    acc_ref[...] += _dot(act, wd_ref[...])

    @pl.when(f == pl.num_programs(1) - 1)
    def _():
        x2 = x1_ref[...] + mod_ref[0, 5:6, :] * acc_ref[...]
        o_ref[...] = _rms_norm(x2, fg_ref[...])


def _mlp(h2, wu, wd, x1, mod, final_gain, rows_per_batch, tm, tf):
    m, d = h2.shape
    f = wu.shape[1]
    tpb = rows_per_batch // tm
    return pl.pallas_call(
        _mlp_kernel,
        grid=(m // tm, f // tf),
        in_specs=[pl.BlockSpec((tm, d), lambda i, j: (i, 0)),
                  pl.BlockSpec((d, tf), lambda i, j: (0, j)),
                  pl.BlockSpec((tf, d), lambda i, j: (j, 0)),
                  pl.BlockSpec((tm, d), lambda i, j: (i, 0)),
                  pl.BlockSpec((1, 6, d), lambda i, j: (i // tpb, 0, 0)),
                  pl.BlockSpec((1, d), lambda i, j: (0, 0))],
        out_specs=pl.BlockSpec((tm, d), lambda i, j: (i, 0)),
        out_shape=jax.ShapeDtypeStruct((m, d), F32),
        scratch_shapes=[pltpu.VMEM((tm, d), F32)],
        compiler_params=_cparams(("arbitrary", "arbitrary")),
        name="mlp",
    )(h2, wu, wd, x1, mod, final_gain)


def _forward(x, c, w_ada, b_ada, norm1_gain, w_in, b_in, rwkv_mix, rwkv_w0, rwkv_w2, rwkv_a0,
             rwkv_a2, rwkv_g2, rwkv_k_k, rwkv_k_a, rwkv_r_k, rwkv_ln_w, rwkv_ln_b, attn_sinks,
             w_branch_rwkv, w_branch_attn, w_out, norm2_gain, w_up, w_down, final_gain):
    batch, seq, d = x.shape
    depth = w_ada.shape[0]
    c_rwkv = rwkv_w0.shape[1]
    n_heads = attn_sinks.shape[1]
    qc = n_heads * HEAD
    kvc = 2 * (n_heads // ATTN_GROUP) * HEAD
    lora_w = DECAY_LORA + ICLR_LORA + GATE_LORA
    rwkv_cols = 3 * c_rwkv + lora_w
    attn_cols = qc + kvc
    offs = {"gate": 0, "r": 2 * d, "k": 2 * d + c_rwkv, "v": 2 * d + 2 * c_rwkv, "q": 2 * d + 3 * c_rwkv}
    offs["lora"] = offs["q"] + qc
    offs["kv"] = offs["lora"] + LORA_PAD
    n_pack = offs["kv"] + kvc

    segments = ((rwkv_cols + attn_cols, 2 * d, offs["gate"]), (0, 3 * c_rwkv, offs["r"]),
                (rwkv_cols, qc, offs["q"]), (3 * c_rwkv, lora_w, offs["lora"]),
                (rwkv_cols + qc, kvc, offs["kv"]))
    zero_rows = (offs["lora"] + lora_w, offs["kv"])

    def pack_cols(t):
        out = jnp.zeros(t.shape[:-1] + (n_pack,), t.dtype)
        for src, width, dst in segments:
            out = out.at[..., dst:dst + width].set(t[..., src:src + width])
        return out

    tm_in = min(1024, seq)
    tn_in = 1280 if n_pack % 1280 == 0 else LANES
    tb = min(512, seq)
    tm_merge = min(256, seq)
    tm_mlp = min(512, seq)
    tf = min(1024, w_up.shape[2])

    assert depth == 1, "single-layer block"
    layer = lambda t: t.reshape(t.shape[1:])
    x2 = x.reshape(batch * seq, d)
    for l in range(depth):
        mod = _ada(c, layer(w_ada), b_ada[l]).reshape(batch, 6, d)
        w_pack = _pack_weight_t(layer(w_in).T, segments, zero_rows, n_pack)
        b_pack = pack_cols(b_in[l]).reshape(1, n_pack)
        p = _inproj(x2, mod, norm1_gain[l].reshape(1, d), w_pack, b_pack, seq, tm_in, tn_in)

        mix = rwkv_mix[l]
        chan = jnp.stack([mix[:c_rwkv], mix[c_rwkv:2 * c_rwkv], mix[2 * c_rwkv:3 * c_rwkv],
                          rwkv_w0[l], rwkv_a0[l], rwkv_k_k[l], rwkv_k_a[l], rwkv_r_k[l].reshape(c_rwkv),
                          rwkv_ln_w[l], rwkv_ln_b[l]])
        chan = jnp.pad(chan, ((0, 16 - chan.shape[0]), (0, 0)))
        mix_lora = jnp.pad(mix[3 * c_rwkv:], (0, LORA_PAD - lora_w)).reshape(1, LORA_PAD)
        g2_pad = jnp.pad(rwkv_g2[l], ((0, LORA_PAD - lora_w), (0, 0)))
        y_rwkv = _rwkv(p, chan, mix_lora, rwkv_w2[l], rwkv_a2[l], g2_pad, batch, seq, offs, tb)
        y_attn = _attn(p, attn_sinks[l], seq, offs)

        x2, h2 = _merge(y_rwkv, y_attn, p, x2, mod, layer(w_branch_rwkv).astype(BF16),
                        layer(w_branch_attn).astype(BF16), layer(w_out).astype(BF16),
                        norm2_gain[l].reshape(1, d), seq, tm_merge)
        x2 = _mlp(h2, layer(w_up).astype(BF16), layer(w_down).astype(BF16), x2, mod,
                  final_gain.reshape(1, d), seq, tm_mlp, tf)
    return x2.reshape(batch, seq, d)


def kernel(x, c, w_ada, b_ada, norm1_gain, w_in, b_in, rwkv_mix, rwkv_w0, rwkv_w2, rwkv_a0, rwkv_a2, rwkv_g2, rwkv_k_k, rwkv_k_a, rwkv_r_k, rwkv_ln_w, rwkv_ln_b, attn_sinks, w_branch_rwkv, w_branch_attn, w_out, norm2_gain, w_up, w_down, final_gain):
    return _forward(x, c, w_ada, b_ada, norm1_gain, w_in, b_in, rwkv_mix, rwkv_w0, rwkv_w2, rwkv_a0,
                    rwkv_a2, rwkv_g2, rwkv_k_k, rwkv_k_a, rwkv_r_k, rwkv_ln_w, rwkv_ln_b, attn_sinks,
                    w_branch_rwkv, w_branch_attn, w_out, norm2_gain, w_up, w_down, final_gain)
```

```python
import functools

import jax
import jax.numpy as jnp
from jax import lax
from jax.experimental import pallas as pl
from jax.experimental.pallas import tpu as pltpu

F32 = jnp.float32
BF16 = jnp.bfloat16
HIGHEST = lax.Precision.HIGHEST

LANES = 128
SUBLANES = 8
HEAD = 64
CHUNK = 128
WINDOW = 128
ATTN_GROUP = 8
ATTN_BATCH = 8
NORM_EPS = 1e-6
GN_EPS = 64e-5
DECAY_LORA = 64
ICLR_LORA = 64
GATE_LORA = 160
LORA_PAD = 512
VMEM_LIMIT = 56 * 1024 * 1024


def _cparams(sem):
    return pltpu.CompilerParams(dimension_semantics=sem, vmem_limit_bytes=VMEM_LIMIT)


def _dot(a, b, precision=None):
    return jnp.dot(a, b, preferred_element_type=F32, precision=precision)


def _dot_nt(a, b, precision=None):
    return lax.dot_general(a, b, (((1,), (1,)), ((), ())), preferred_element_type=F32,
                           precision=precision)


def _dot_tn(a, b, precision=None):
    return lax.dot_general(a, b, (((0,), (0,)), ((), ())), preferred_element_type=F32,
                           precision=precision)


def _rms_norm(x, gain):
    ms = jnp.mean(x * x, axis=-1, keepdims=True)
    return x * lax.rsqrt(ms + NORM_EPS) * gain


def _ada_kernel(ct_ref, w_ref, b_ref, o_ref):
    ct = ct_ref[...]
    act = ct * jax.nn.sigmoid(ct)
    for m in range(o_ref.shape[0]):
        o_ref[m:m + 1, :] = jnp.sum(w_ref[...] * act[:, m:m + 1], axis=0, keepdims=True) + b_ref[...]


def _ada(c, w, b, tn=1024):
    nb, d = c.shape
    n = w.shape[1]
    return pl.pallas_call(
        _ada_kernel,
        grid=(n // tn,),
        in_specs=[pl.BlockSpec((d, nb), lambda j: (0, 0)),
                  pl.BlockSpec((d, tn), lambda j: (0, j)),
                  pl.BlockSpec((1, tn), lambda j: (0, j))],
        out_specs=pl.BlockSpec((nb, tn), lambda j: (0, j)),
        out_shape=jax.ShapeDtypeStruct((nb, n), F32),
        compiler_params=_cparams(("arbitrary",)),
        name="adaln",
    )(c.T, w, b.reshape(1, n))


def _pack_kernel(w_ref, o_ref, *, segments, zero_rows):
    for src, width, dst in segments:
        o_ref[dst:dst + width, :] = w_ref[src:src + width, :].astype(o_ref.dtype)
    lo, hi = zero_rows
    o_ref[lo:hi, :] = jnp.zeros((hi - lo, o_ref.shape[1]), o_ref.dtype)


def _pack_weight_t(w_t, segments, zero_rows, n_pack, tk=256):
    n, k = w_t.shape
    return pl.pallas_call(
        functools.partial(_pack_kernel, segments=segments, zero_rows=zero_rows),
        grid=(k // tk,),
        in_specs=[pl.BlockSpec((n, tk), lambda i: (0, i))],
        out_specs=pl.BlockSpec((n_pack, tk), lambda i: (0, i)),
        out_shape=jax.ShapeDtypeStruct((n_pack, k), BF16),
        compiler_params=_cparams(("arbitrary",)),
        name="pack_w_in",
    )(w_t)


def _inproj_kernel(x_ref, mod_ref, g_ref, w_ref, b_ref, o_ref, h_ref):
    @pl.when(pl.program_id(1) == 0)
    def _():
        h = _rms_norm(x_ref[...], g_ref[...]) * (1.0 + mod_ref[0, 1:2, :]) + mod_ref[0, 0:1, :]
        h_ref[...] = h.astype(BF16)

    o_ref[...] = _dot_nt(h_ref[...], w_ref[...]) + b_ref[...]


def _inproj(x2, mod, gain, w_t, b, rows_per_batch, tm, tn):
    m, d = x2.shape
    n = w_t.shape[0]
    tpb = rows_per_batch // tm
    return pl.pallas_call(
        _inproj_kernel,
        grid=(m // tm, n // tn),
        in_specs=[pl.BlockSpec((tm, d), lambda i, j: (i, 0)),
                  pl.BlockSpec((1, 6, d), lambda i, j: (i // tpb, 0, 0)),
                  pl.BlockSpec((1, d), lambda i, j: (0, 0)),
                  pl.BlockSpec((tn, d), lambda i, j: (j, 0)),
                  pl.BlockSpec((1, tn), lambda i, j: (0, j))],
        out_specs=pl.BlockSpec((tm, tn), lambda i, j: (i, j)),
        out_shape=jax.ShapeDtypeStruct((m, n), F32),
        scratch_shapes=[pltpu.VMEM((tm, d), BF16)],
        compiler_params=_cparams(("arbitrary", "arbitrary")),
        name="inproj",
    )(x2, mod, gain, w_t, b)


def _head_sum(x):
    lo = lax.broadcasted_iota(jnp.int32, x.shape, 1) < HEAD
    s0 = jnp.sum(jnp.where(lo, x, 0.0), axis=-1, keepdims=True)
    s1 = jnp.sum(jnp.where(lo, 0.0, x), axis=-1, keepdims=True)
    return jnp.where(lo, s0, s1)


def _token_shift(x, carry_row, mix):
    rolled = pltpu.roll(x, 1, axis=0)
    first = lax.broadcasted_iota(jnp.int32, x.shape, 0) == 0
    prev = jnp.where(first, carry_row, rolled)
    return x + (prev - x) * mix


class _Operands:
    def __init__(self):
        self._parts = {}

    def parts(self, x, n):
        ent = self._parts.setdefault(id(x), [x])
        while len(ent) - 1 < n:
            rest = x
            for piece in ent[1:]:
                rest = rest - piece.astype(F32)
            ent.append(rest.astype(BF16))
        return ent[1:n + 1]

    def mm(self, a, b, passes, dims=(((1,), (0,)), ((), ()))):
        dot = lambda u, w: lax.dot_general(u, w, dims, preferred_element_type=F32)
        if passes == 1:
            return dot(self.parts(a, 1)[0], self.parts(b, 1)[0])
        a_hi, a_lo = self.parts(a, 2)
        b_hi, b_lo = self.parts(b, 2)
        return dot(a_hi, b_hi) + (dot(a_hi, b_lo) + dot(a_lo, b_hi))


_NT = (((1,), (1,)), ((), ()))
_TN = (((0,), (0,)), ((), ()))
SCAN_PASSES = {"gram": 1, "inverse": 1, "apply": 1, "state": 1}


CHUNK_FIELDS = ("ra", "aa", "bi", "ki", "bh", "kh", "vsw")


def _decay_chunk(ops, rc, lwc, kc, vsw, ac, bc):
    incl = (lax.broadcasted_iota(jnp.int32, (CHUNK, CHUNK), 0)
            >= lax.broadcasted_iota(jnp.int32, (CHUNK, CHUNK), 1))
    tri = jnp.where(incl, 1.0, 0.0).astype(BF16)
    lw = sum(jnp.dot(tri, piece, preferred_element_type=F32) for piece in ops.parts(lwc, 2))
    lw_end = lw[CHUNK - 1:CHUNK, :]
    e_in = jnp.exp(lw)
    e_ex = jnp.exp(lw - lwc)
    e_inv = jnp.exp(-lw)
    e_end = jnp.exp(lw_end - lw)
    return dict(vsw=vsw, g_end=jnp.exp(lw_end), ra=rc * e_in, aa=ac * e_ex,
                bi=bc * e_inv, ki=kc * e_inv, bh=bc * e_end, kh=kc * e_end)


def _scan_matrices(ops, chunks):
    row = lax.broadcasted_iota(jnp.int32, (CHUNK, CHUNK), 0)
    col = lax.broadcasted_iota(jnp.int32, (CHUNK, CHUNK), 1)
    incl = row >= col
    strict = row > col
    eye = row == col
    lo_lane = col < HEAD
    same_half = lo_lane == (row < HEAD)
    lo_lane2 = lax.broadcasted_iota(jnp.int32, (2 * CHUNK, LANES), 1) < HEAD
    level_masks = []
    n = 1
    while n < CHUNK:
        level_masks.append((row // (2 * n) == col // (2 * n)) & ((row // n) % 2 == 1) & ((col // n) % 2 == 0))
        n *= 2

    probs = []
    for ch in chunks:
        lhs_all = jnp.concatenate([ch["aa"], ch["ra"]], axis=0)
        rhs_all = jnp.concatenate([ch["bi"], ch["ki"]], axis=0)
        for h in range(LANES // HEAD):
            own = lo_lane if h == 0 else ~lo_lane
            own2 = lo_lane2 if h == 0 else ~lo_lane2
            gram = ops.mm(jnp.where(own2, lhs_all, 0.0), rhs_all, SCAN_PASSES["gram"], _NT)
            probs.append(dict(ch=ch, own=own,
                              m_ab=jnp.where(strict, gram[:CHUNK, :CHUNK], 0.0),
                              m_ak=jnp.where(strict, gram[:CHUNK, CHUNK:], 0.0),
                              m_rb=jnp.where(incl, gram[CHUNK:, :CHUNK], 0.0),
                              m_rk=jnp.where(incl, gram[CHUNK:, CHUNK:], 0.0)))
    yield

    for p in probs:
        p["t"] = jnp.where(eye, 1.0, 0.0) + jnp.where(level_masks[0], p["m_ab"], 0.0)
    n = 2
    for mask in level_masks[1:]:
        if n < SUBLANES:
            for p in probs:
                p["to"] = ops.mm(p["t"], jnp.where(mask, p["m_ab"], 0.0), SCAN_PASSES["inverse"])
            for p in probs:
                p["t"] = p["t"] + ops.mm(p["to"], p["t"], SCAN_PASSES["inverse"])
        else:
            lower = lambda t, n=n: jnp.concatenate(
                [t[i:i + n] for i in range(n, CHUNK, 2 * n)], axis=0)
            for p in probs:
                p["tl"] = lower(p["t"])
                p["to"] = ops.mm(p["tl"], jnp.where(mask, p["m_ab"], 0.0), SCAN_PASSES["inverse"])
            for p in probs:
                tl = p["tl"] + ops.mm(p["to"], p["t"], SCAN_PASSES["inverse"])
                slabs = []
                for j, i in enumerate(range(0, CHUNK, 2 * n)):
                    slabs += [p["t"][i:i + n], tl[j * n:(j + 1) * n]]
                p["t"] = jnp.concatenate(slabs, axis=0)
        n *= 2
        yield

    for p in probs:
        p["vm"] = jnp.where(p["own"], 0.0, p["ch"]["vsw"])
        p["x"] = jnp.where(p["own"], p["ch"]["aa"], 0.0) + ops.mm(p["m_ak"], p["vm"], SCAN_PASSES["apply"])
    yield
    for p in probs:
        p["tx"] = ops.mm(p["t"], p["x"], SCAN_PASSES["apply"])
    yield
    for p in probs:
        p["y"] = jnp.where(p["own"], p["ch"]["ra"], 0.0) + ops.mm(
            jnp.concatenate([p["m_rb"], p["m_rk"]], axis=1),
            jnp.concatenate([p["tx"], p["vm"]], axis=0), SCAN_PASSES["apply"])
    yield

    for i, ch in enumerate(chunks):
        p0, p1 = probs[2 * i], probs[2 * i + 1]
        pq = ops.mm(jnp.concatenate([ch["bh"], ch["kh"]], axis=0),
                    jnp.concatenate([jnp.concatenate([p0["tx"], p1["tx"]], axis=1),
                                     jnp.concatenate([p0["vm"], p1["vm"]], axis=1)], axis=0),
                    SCAN_PASSES["apply"], _TN)
        pq = jnp.where(row < HEAD, pq[:, :LANES], pq[:, LANES:])
        ch["p"] = jnp.where(same_half, pq, 0.0) + jnp.where(eye, ch["g_end"], 0.0)
        ch["q"] = jnp.where(same_half, 0.0, pq)
        ch["r"] = jnp.where(lo_lane, p0["y"], p1["y"])
        ch["y0"] = jnp.where(lo_lane, p1["y"], p0["y"])


def _interleave(main, *side):
    for _ in main:
        for gen in side:
            next(gen, None)
    for gen in side:
        for _ in gen:
            pass


def _rwkv_kernel(r_ref, k_ref, v_ref, lora_ref, chan_ref, chanf_ref, mixl_ref, w2_ref, a2_ref, g2_ref,
                 o_ref,
                 crkv_ref, clora_ref, z_ref, twd_s, ad_s, sgd_s, r3_s, k3_s, v3_s, g3_s,
                 y_s, gend_s, p_s, q_s, rr_s, y0_s, *decay_s, n_pairs, n_tblocks, n_tiles):
    tb = r_ref.shape[0]
    n_chunks = tb // CHUNK
    s = pl.program_id(0)
    sa = jnp.minimum(s, n_tiles - 1)
    sf = jnp.maximum(s - 2, 0)
    hp, t_a = sa % n_pairs, (sa // n_pairs) % n_tblocks
    hpf, t_f = sf % n_pairs, (sf // n_pairs) % n_tblocks
    slot_w, slot_r = s % 2, (s + 1) % 2
    keep_w, keep_r = s % 3, (s + 1) % 3

    @pl.when(s == 0)
    def _():
        for ref in (crkv_ref, clora_ref, z_ref, r3_s, k3_s, v3_s, g3_s, gend_s, p_s, q_s, rr_s, y0_s,
                    *decay_s):
            ref[...] = jnp.zeros_like(ref)

    @pl.when(hp == 0)
    def _():
        lora_in = lora_ref[...]
        carry = jnp.where(t_a == 0, 0.0, clora_ref[...])
        lora = _token_shift(lora_in, carry, mixl_ref[...])
        clora_ref[...] = lora_in[tb - 1:tb, :]
        twd = jnp.tanh(lora[:, 0:DECAY_LORA])
        twd_hi = twd.astype(BF16)
        twd_s[0] = twd_hi
        twd_s[1] = (twd - twd_hi.astype(F32)).astype(BF16)
        ad_s[...] = lora[:, DECAY_LORA:DECAY_LORA + ICLR_LORA].astype(BF16)
        sgd_s[...] = jax.nn.sigmoid(lora[:, DECAY_LORA + ICLR_LORA:]).astype(BF16)

    ops = _Operands()

    def finish():
        chanf = chanf_ref[...]
        z = jnp.where(t_f == 0, 0.0, z_ref[hpf])
        for c in range(n_chunks):
            i = slot_r * n_chunks + c
            rz = ops.mm(jnp.concatenate([rr_s[i], p_s[i]], axis=0), z, SCAN_PASSES["state"])
            y_s[c * CHUNK:(c + 1) * CHUNK, :] = rz[:CHUNK] + y0_s[i]
            z = rz[CHUNK:] + q_s[i]
            yield
        z_ref[hpf] = z
        r_k, ln_w, ln_b = chanf[7:8], chanf[8:9], chanf[9:10]
        y = pltpu.roll(y_s[...], HEAD, axis=1)
        mu = _head_sum(y) * (1.0 / HEAD)
        yc = y - mu
        var = _head_sum(yc * yc) * (1.0 / HEAD)
        yn = yc * lax.rsqrt(var + GN_EPS) * ln_w + ln_b
        bonus = _head_sum(r3_s[keep_r] * k3_s[keep_r] * r_k) * v3_s[keep_r]
        o_ref[...] = ((yn + bonus) * g3_s[keep_r]).astype(o_ref.dtype)
        yield

    def matrices():
        chunks = []
        for c in range(n_chunks):
            i = slot_r * n_chunks + c
            ch = {name: ref[i] for name, ref in zip(CHUNK_FIELDS, decay_s)}
            ch["g_end"] = gend_s[i, 0:1, :]
            chunks.append(ch)
        yield from _scan_matrices(ops, chunks)
        for c, ch in enumerate(chunks):
            i = slot_w * n_chunks + c
            p_s[i], q_s[i], rr_s[i], y0_s[i] = ch["p"], ch["q"], ch["r"], ch["y0"]
        yield

    def prepare():
        chan = chan_ref[...]
        mix_r, mix_k, mix_v = chan[0:1], chan[1:2], chan[2:3]
        w0, a0, k_k, k_a = chan[3:4], chan[4:5], chan[5:6], chan[6:7]
        r_in, k_in, v_in = r_ref[...], k_ref[...], v_ref[...]
        carry = jnp.where(t_a == 0, 0.0, crkv_ref[hp])
        r = _token_shift(r_in, carry[0:1], mix_r)
        k = _token_shift(k_in, carry[1:2], mix_k)
        v = _token_shift(v_in, carry[2:3], mix_v)
        crkv_ref[hp, 0:1, :] = r_in[tb - 1:tb, :]
        crkv_ref[hp, 1:2, :] = k_in[tb - 1:tb, :]
        crkv_ref[hp, 2:3, :] = v_in[tb - 1:tb, :]
        yield
        w2_hi = w2_ref[...].astype(BF16)
        w2_lo = (w2_ref[...] - w2_hi.astype(F32)).astype(BF16)
        twd_hi, twd_lo = twd_s[0], twd_s[1]
        dw = w0 + (_dot(twd_hi, w2_hi) + (_dot(twd_hi, w2_lo) + _dot(twd_lo, w2_hi)))
        zneg = -dw
        softplus = jnp.maximum(zneg, 0.0) + jnp.log(1.0 + jnp.exp(-jnp.abs(zneg)))
        logw = -jnp.exp(-softplus - 0.5)
        alr = jax.nn.sigmoid(a0 + _dot(ad_s[...], a2_ref[...].astype(BF16)))
        g3_s[keep_w] = _dot(sgd_s[...], g2_ref[...].astype(BF16))
        yield
        kk = k * k_k
        kk = kk / jnp.maximum(jnp.sqrt(_head_sum(kk * kk)), 1e-12)
        k_mod = k * (1.0 + (alr - 1.0) * k_a)
        vsw = pltpu.roll(v, HEAD, axis=1)
        a, b = -kk, kk * alr
        r3_s[keep_w], k3_s[keep_w], v3_s[keep_w] = r, k_mod, v
        yield
        for c in range(n_chunks):
            sl = slice(c * CHUNK, (c + 1) * CHUNK)
            ch = _decay_chunk(ops, r[sl], logw[sl], k_mod[sl], vsw[sl], a[sl], b[sl])
            i = slot_w * n_chunks + c
            for name, ref in zip(CHUNK_FIELDS, decay_s):
                ref[i] = ch[name]
            gend_s[i] = jnp.broadcast_to(ch["g_end"], (SUBLANES, LANES))
            yield

    _interleave(matrices(), finish(), prepare())


def _rwkv(p, chan, mix_lora, w2, a2, g2, batch, seq, offs, tb):
    m = p.shape[0]
    c = w2.shape[1]
    n_pairs = c // LANES
    n_tblocks = seq // tb
    n_tiles = batch * n_tblocks * n_pairs
    n_chunks = tb // CHUNK
    gate_w = LORA_PAD - DECAY_LORA - ICLR_LORA
    prep = lambda s: jnp.minimum(s, n_tiles - 1)
    fin = lambda s: jnp.maximum(s - 2, 0)
    col = lambda name: lambda s: (prep(s) // n_pairs, offs[name] // LANES + prep(s) % n_pairs)
    weight = lambda s: (0, prep(s) % n_pairs)
    tile = lambda: pltpu.VMEM((tb, LANES), F32)
    tile3 = lambda: pltpu.VMEM((3, tb, LANES), F32)
    mats = lambda: pltpu.VMEM((2 * n_chunks, CHUNK, LANES), F32)
    return pl.pallas_call(
        functools.partial(_rwkv_kernel, n_pairs=n_pairs, n_tblocks=n_tblocks, n_tiles=n_tiles),
        grid=(n_tiles + 2,),
        in_specs=[pl.BlockSpec((tb, LANES), col("r")),
                  pl.BlockSpec((tb, LANES), col("k")),
                  pl.BlockSpec((tb, LANES), col("v")),
                  pl.BlockSpec((tb, LORA_PAD), lambda s: (prep(s) // n_pairs, offs["lora"] // LORA_PAD)),
                  pl.BlockSpec((16, LANES), weight),
                  pl.BlockSpec((16, LANES), lambda s: (0, fin(s) % n_pairs)),
                  pl.BlockSpec((1, LORA_PAD), lambda s: (0, 0)),
                  pl.BlockSpec((DECAY_LORA, LANES), weight),
                  pl.BlockSpec((ICLR_LORA, LANES), weight),
                  pl.BlockSpec((gate_w, LANES), weight)],
        out_specs=pl.BlockSpec((tb, LANES), lambda s: (fin(s) // n_pairs, fin(s) % n_pairs)),
        out_shape=jax.ShapeDtypeStruct((m, c), BF16),
        scratch_shapes=[pltpu.VMEM((n_pairs, SUBLANES, LANES), F32),
                        pltpu.VMEM((1, LORA_PAD), F32),
                        pltpu.VMEM((n_pairs, CHUNK, LANES), F32),
                        pltpu.VMEM((2, tb, DECAY_LORA), BF16),
                        pltpu.VMEM((tb, ICLR_LORA), BF16),
                        pltpu.VMEM((tb, gate_w), BF16),
                        tile3(), tile3(), tile3(), tile3(),
                        tile(),
                        pltpu.VMEM((2 * n_chunks, SUBLANES, LANES), F32),
                        mats(), mats(), mats(), mats()]
                       + [mats() for _ in CHUNK_FIELDS],
        compiler_params=_cparams(("arbitrary",)),
        name="rwkv7",
    )(p, p, p, p, chan, chan, mix_lora, w2, a2, g2)


def _attn_kernel(sink_ref, q_ref, kvc_ref, kvp_ref, o_ref, bias_s, *, blocks_per_seq, n_heads):
    nb = pl.program_id(0) % blocks_per_seq
    n_kv = n_heads // ATTN_GROUP

    @pl.when(nb <= 1)
    def _():
        qi = lax.broadcasted_iota(jnp.int32, (WINDOW, 2 * WINDOW), 0)
        kj = lax.broadcasted_iota(jnp.int32, (WINDOW, 2 * WINDOW), 1)
        dist = qi + WINDOW - kj
        valid = (dist >= 0) & (dist < WINDOW) & ((kj >= WINDOW) | (nb > 0))
        neg_dist = jnp.where(valid, -dist.astype(F32), -jnp.inf)
        for h in range(n_heads):
            bias_s[h] = (2.0 ** (-8.0 * (h + 1) / n_heads)) * neg_dist

    kv = jnp.concatenate([kvp_ref[...], kvc_ref[...]], axis=0).astype(BF16)
    q = (q_ref[...] * (HEAD ** -0.5)).astype(BF16)
    lo_lane = lax.broadcasted_iota(jnp.int32, (WINDOW, LANES), 1) < HEAD
    for g0 in range(0, n_heads, ATTN_BATCH):
        heads = range(g0, g0 + ATTN_BATCH)
        hk = g0 // ATTN_GROUP
        kh = kv[:, hk * HEAD:(hk + 1) * HEAD]
        vh = kv[:, (n_kv + hk) * HEAD:(n_kv + hk + 1) * HEAD]
        s = {h: _dot_nt(q[:, h * HEAD:(h + 1) * HEAD], kh) + bias_s[h] for h in heads}
        mx = {h: jnp.maximum(jnp.max(s[h], axis=-1, keepdims=True), sink_ref[h]) for h in heads}
        e = {h: jnp.exp(s[h] - mx[h]) for h in heads}
        denom = {h: jnp.sum(e[h], axis=-1, keepdims=True) + jnp.exp(sink_ref[h] - mx[h]) for h in heads}
        pv = {h: _dot(e[h].astype(BF16), vh) for h in heads}
        for h0 in range(g0, g0 + ATTN_BATCH, 2):
            pair = (jnp.concatenate([pv[h0], pv[h0 + 1]], axis=1)
                    * jnp.where(lo_lane, 1.0 / denom[h0], 1.0 / denom[h0 + 1]))
            o_ref[:, h0 * HEAD:(h0 + 2) * HEAD] = pair.astype(o_ref.dtype)


def _attn(p, sinks, seq, offs):
    m = p.shape[0]
    n_heads = sinks.shape[0]
    qc = n_heads * HEAD
    kvc = 2 * (n_heads // ATTN_GROUP) * HEAD
    bps = seq // WINDOW
    return pl.pallas_call(
        functools.partial(_attn_kernel, blocks_per_seq=bps, n_heads=n_heads),
        grid=(m // WINDOW,),
        in_specs=[pl.BlockSpec(memory_space=pltpu.SMEM),
                  pl.BlockSpec((WINDOW, qc), lambda n: (n, offs["q"] // qc)),
                  pl.BlockSpec((WINDOW, kvc), lambda n: (n, offs["kv"] // kvc)),
                  pl.BlockSpec((WINDOW, kvc), lambda n: (jnp.maximum(n - 1, 0), offs["kv"] // kvc))],
        out_specs=pl.BlockSpec((WINDOW, qc), lambda n: (n, 0)),
        out_shape=jax.ShapeDtypeStruct((m, qc), BF16),
        scratch_shapes=[pltpu.VMEM((n_heads, WINDOW, 2 * WINDOW), F32)],
        compiler_params=_cparams(("arbitrary",)),
        name="swa",
    )(sinks, p, p, p)


def _merge_kernel(yr_ref, ya_ref, gate_ref, x_ref, mod_ref, wbr_ref, wba_ref, wout_ref, g2_ref,
                  x1_ref, h2_ref):
    d = x_ref.shape[1]
    gates = jax.nn.sigmoid(gate_ref[...])
    merged = gates[:, :d] * _dot(yr_ref[...], wbr_ref[...]) + gates[:, d:] * _dot(ya_ref[...], wba_ref[...])
    x1 = x_ref[...] + mod_ref[0, 2:3, :] * _dot(merged.astype(BF16), wout_ref[...])
    x1_ref[...] = x1
    h2 = _rms_norm(x1, g2_ref[...]) * (1.0 + mod_ref[0, 4:5, :]) + mod_ref[0, 3:4, :]
    h2_ref[...] = h2.astype(BF16)


def _merge(yr, ya, p, x2, mod, wbr, wba, wout, gain2, rows_per_batch, tm):
    m, d = x2.shape
    c = yr.shape[1]
    tpb = rows_per_batch // tm
    const = lambda shape: pl.BlockSpec(shape, lambda i: (0, 0), pipeline_mode=pl.Buffered(1))
    return pl.pallas_call(
        _merge_kernel,
        grid=(m // tm,),
        in_specs=[pl.BlockSpec((tm, c), lambda i: (i, 0)),
                  pl.BlockSpec((tm, c), lambda i: (i, 0)),
                  pl.BlockSpec((tm, 2 * d), lambda i: (i, 0)),
                  pl.BlockSpec((tm, d), lambda i: (i, 0)),
                  pl.BlockSpec((1, 6, d), lambda i: (i // tpb, 0, 0)),
                  const((c, d)), const((c, d)), const((d, d)),
                  pl.BlockSpec((1, d), lambda i: (0, 0))],
        out_specs=[pl.BlockSpec((tm, d), lambda i: (i, 0)),
                   pl.BlockSpec((tm, d), lambda i: (i, 0))],
        out_shape=[jax.ShapeDtypeStruct((m, d), F32), jax.ShapeDtypeStruct((m, d), BF16)],
        compiler_params=_cparams(("arbitrary",)),
        name="merge",
    )(yr, ya, p, x2, mod, wbr, wba, wout, gain2)


def _mlp_kernel(h_ref, wu_ref, wd_ref, x1_ref, mod_ref, fg_ref, o_ref, acc_ref):
    f = pl.program_id(1)

    @pl.when(f == 0)
    def _():
        acc_ref[...] = jnp.zeros_like(acc_ref)

    u = _dot(h_ref[...], wu_ref[...])
    act = jnp.square(jnp.maximum(u, 0.0)).astype(BF16)
    acc_ref[...] += _dot(act, wd_ref[...])

    @pl.when(f == pl.num_programs(1) - 1)
    def _():
        x2 = x1_ref[...] + mod_ref[0, 5:6, :] * acc_ref[...]
        o_ref[...] = _rms_norm(x2, fg_ref[...])


def _mlp(h2, wu, wd, x1, mod, final_gain, rows_per_batch, tm, tf):
    m, d = h2.shape
    f = wu.shape[1]
    tpb = rows_per_batch // tm
    return pl.pallas_call(
        _mlp_kernel,
        grid=(m // tm, f // tf),
        in_specs=[pl.BlockSpec((tm, d), lambda i, j: (i, 0)),
                  pl.BlockSpec((d, tf), lambda i, j: (0, j)),
                  pl.BlockSpec((tf, d), lambda i, j: (j, 0)),
                  pl.BlockSpec((tm, d), lambda i, j: (i, 0)),
                  pl.BlockSpec((1, 6, d), lambda i, j: (i // tpb, 0, 0)),
                  pl.BlockSpec((1, d), lambda i, j: (0, 0))],
        out_specs=pl.BlockSpec((tm, d), lambda i, j: (i, 0)),
        out_shape=jax.ShapeDtypeStruct((m, d), F32),
        scratch_shapes=[pltpu.VMEM((tm, d), F32)],
        compiler_params=_cparams(("arbitrary", "arbitrary")),
        name="mlp",
    )(h2, wu, wd, x1, mod, final_gain)


def _forward(x, c, w_ada, b_ada, norm1_gain, w_in, b_in, rwkv_mix, rwkv_w0, rwkv_w2, rwkv_a0,
             rwkv_a2, rwkv_g2, rwkv_k_k, rwkv_k_a, rwkv_r_k, rwkv_ln_w, rwkv_ln_b, attn_sinks,
             w_branch_rwkv, w_branch_attn, w_out, norm2_gain, w_up, w_down, final_gain):
    batch, seq, d = x.shape
    depth = w_ada.shape[0]
    c_rwkv = rwkv_w0.shape[1]
    n_heads = attn_sinks.shape[1]
    qc = n_heads * HEAD
    kvc = 2 * (n_heads // ATTN_GROUP) * HEAD
    lora_w = DECAY_LORA + ICLR_LORA + GATE_LORA
    rwkv_cols = 3 * c_rwkv + lora_w
    attn_cols = qc + kvc
    offs = {"gate": 0, "r": 2 * d, "k": 2 * d + c_rwkv, "v": 2 * d + 2 * c_rwkv, "q": 2 * d + 3 * c_rwkv}
    offs["lora"] = offs["q"] + qc
    offs["kv"] = offs["lora"] + LORA_PAD
    n_pack = offs["kv"] + kvc

    segments = ((rwkv_cols + attn_cols, 2 * d, offs["gate"]), (0, 3 * c_rwkv, offs["r"]),
                (rwkv_cols, qc, offs["q"]), (3 * c_rwkv, lora_w, offs["lora"]),
                (rwkv_cols + qc, kvc, offs["kv"]))
    zero_rows = (offs["lora"] + lora_w, offs["kv"])

    def pack_cols(t):
        out = jnp.zeros(t.shape[:-1] + (n_pack,), t.dtype)
        for src, width, dst in segments:
            out = out.at[..., dst:dst + width].set(t[..., src:src + width])
        return out

    tm_in = min(1024, seq)
    tn_in = 1280 if n_pack % 1280 == 0 else LANES
    tb = min(512, seq)
    tm_merge = min(256, seq)
    tm_mlp = min(512, seq)
    tf = min(1024, w_up.shape[2])

    assert depth == 1, "single-layer block"
    layer = lambda t: t.reshape(t.shape[1:])
    x2 = x.reshape(batch * seq, d)
    for l in range(depth):
        mod = _ada(c, layer(w_ada), b_ada[l]).reshape(batch, 6, d)
        w_pack = _pack_weight_t(layer(w_in).T, segments, zero_rows, n_pack)
        b_pack = pack_cols(b_in[l]).reshape(1, n_pack)
        p = _inproj(x2, mod, norm1_gain[l].reshape(1, d), w_pack, b_pack, seq, tm_in, tn_in)

        mix = rwkv_mix[l]
        chan = jnp.stack([mix[:c_rwkv], mix[c_rwkv:2 * c_rwkv], mix[2 * c_rwkv:3 * c_rwkv],
                          rwkv_w0[l], rwkv_a0[l], rwkv_k_k[l], rwkv_k_a[l], rwkv_r_k[l].reshape(c_rwkv),
                          rwkv_ln_w[l], rwkv_ln_b[l]])
        chan = jnp.pad(chan, ((0, 16 - chan.shape[0]), (0, 0)))
        mix_lora = jnp.pad(mix[3 * c_rwkv:], (0, LORA_PAD - lora_w)).reshape(1, LORA_PAD)
        g2_pad = jnp.pad(rwkv_g2[l], ((0, LORA_PAD - lora_w), (0, 0)))
        y_rwkv = _rwkv(p, chan, mix_lora, rwkv_w2[l], rwkv_a2[l], g2_pad, batch, seq, offs, tb)
        y_attn = _attn(p, attn_sinks[l], seq, offs)

        x2, h2 = _merge(y_rwkv, y_attn, p, x2, mod, layer(w_branch_rwkv).astype(BF16),
                        layer(w_branch_attn).astype(BF16), layer(w_out).astype(BF16),
                        norm2_gain[l].reshape(1, d), seq, tm_merge)
        x2 = _mlp(h2, layer(w_up).astype(BF16), layer(w_down).astype(BF16), x2, mod,
                  final_gain.reshape(1, d), seq, tm_mlp, tf)
    return x2.reshape(batch, seq, d)


def kernel(x, c, w_ada, b_ada, norm1_gain, w_in, b_in, rwkv_mix, rwkv_w0, rwkv_w2, rwkv_a0, rwkv_a2, rwkv_g2, rwkv_k_k, rwkv_k_a, rwkv_r_k, rwkv_ln_w, rwkv_ln_b, attn_sinks, w_branch_rwkv, w_branch_attn, w_out, norm2_gain, w_up, w_down, final_gain):
    return _forward(x, c, w_ada, b_ada, norm1_gain, w_in, b_in, rwkv_mix, rwkv_w0, rwkv_w2, rwkv_a0,
                    rwkv_a2, rwkv_g2, rwkv_k_k, rwkv_k_a, rwkv_r_k, rwkv_ln_w, rwkv_ln_b, attn_sinks,
                    w_branch_rwkv, w_branch_attn, w_out, norm2_gain, w_up, w_down, final_gain)
```

```python
import functools

import jax
import jax.numpy as jnp
from jax import lax
from jax.experimental import pallas as pl
from jax.experimental.pallas import tpu as pltpu

F32 = jnp.float32
BF16 = jnp.bfloat16
HIGHEST = lax.Precision.HIGHEST

LANES = 128
SUBLANES = 8
BF16_ROWS = 16
HEAD = 64
CHUNK = 128
WINDOW = 128
ATTN_GROUP = 8
ATTN_BATCH = 8
NORM_EPS = 1e-6
GN_EPS = 64e-5
DECAY_LORA = 64
ICLR_LORA = 64
GATE_LORA = 160
LORA_PAD = 512
VMEM_LIMIT = 56 * 1024 * 1024


def _cparams(sem):
    return pltpu.CompilerParams(dimension_semantics=sem, vmem_limit_bytes=VMEM_LIMIT)


def _dot(a, b, precision=None):
    return jnp.dot(a, b, preferred_element_type=F32, precision=precision)


def _dot_nt(a, b, precision=None):
    return lax.dot_general(a, b, (((1,), (1,)), ((), ())), preferred_element_type=F32,
                           precision=precision)


def _dot_tn(a, b, precision=None):
    return lax.dot_general(a, b, (((0,), (0,)), ((), ())), preferred_element_type=F32,
                           precision=precision)


def _rms_norm(x, gain):
    ms = jnp.mean(x * x, axis=-1, keepdims=True)
    return x * lax.rsqrt(ms + NORM_EPS) * gain


def _ada_kernel(ct_ref, w_ref, b_ref, o_ref):
    ct = ct_ref[...]
    act = ct * jax.nn.sigmoid(ct)
    for m in range(o_ref.shape[0]):
        o_ref[m:m + 1, :] = jnp.sum(w_ref[...] * act[:, m:m + 1], axis=0, keepdims=True) + b_ref[...]


def _ada(c, w, b, tn=1024):
    nb, d = c.shape
    n = w.shape[1]
    return pl.pallas_call(
        _ada_kernel,
        grid=(n // tn,),
        in_specs=[pl.BlockSpec((d, nb), lambda j: (0, 0)),
                  pl.BlockSpec((d, tn), lambda j: (0, j)),
                  pl.BlockSpec((1, tn), lambda j: (0, j))],
        out_specs=pl.BlockSpec((nb, tn), lambda j: (0, j)),
        out_shape=jax.ShapeDtypeStruct((nb, n), F32),
        compiler_params=_cparams(("arbitrary",)),
        name="adaln",
    )(c.T, w, b.reshape(1, n))


def _pack_kernel(w_ref, o_ref, *, segments, zero_rows):
    for src, width, dst in segments:
        o_ref[dst:dst + width, :] = w_ref[src:src + width, :].astype(o_ref.dtype)
    lo, hi = zero_rows
    o_ref[lo:hi, :] = jnp.zeros((hi - lo, o_ref.shape[1]), o_ref.dtype)


def _pack_weight_t(w_t, segments, zero_rows, n_pack, tk=256):
    n, k = w_t.shape
    return pl.pallas_call(
        functools.partial(_pack_kernel, segments=segments, zero_rows=zero_rows),
        grid=(k // tk,),
        in_specs=[pl.BlockSpec((n, tk), lambda i: (0, i))],
        out_specs=pl.BlockSpec((n_pack, tk), lambda i: (0, i)),
        out_shape=jax.ShapeDtypeStruct((n_pack, k), BF16),
        compiler_params=_cparams(("arbitrary",)),
        name="pack_w_in",
    )(w_t)


def _inproj_kernel(x_ref, mod_ref, g_ref, w_ref, b_ref, o_ref, h_ref):
    @pl.when(pl.program_id(1) == 0)
    def _():
        h = _rms_norm(x_ref[...], g_ref[...]) * (1.0 + mod_ref[0, 1:2, :]) + mod_ref[0, 0:1, :]
        h_ref[...] = h.astype(BF16)

    o_ref[...] = _dot_nt(h_ref[...], w_ref[...]) + b_ref[...]


def _inproj(x2, mod, gain, w_t, b, rows_per_batch, tm, tn):
    m, d = x2.shape
    n = w_t.shape[0]
    tpb = rows_per_batch // tm
    return pl.pallas_call(
        _inproj_kernel,
        grid=(m // tm, n // tn),
        in_specs=[pl.BlockSpec((tm, d), lambda i, j: (i, 0)),
                  pl.BlockSpec((1, 6, d), lambda i, j: (i // tpb, 0, 0)),
                  pl.BlockSpec((1, d), lambda i, j: (0, 0)),
                  pl.BlockSpec((tn, d), lambda i, j: (j, 0)),
                  pl.BlockSpec((1, tn), lambda i, j: (0, j))],
        out_specs=pl.BlockSpec((tm, tn), lambda i, j: (i, j)),
        out_shape=jax.ShapeDtypeStruct((m, n), F32),
        scratch_shapes=[pltpu.VMEM((tm, d), BF16)],
        compiler_params=_cparams(("arbitrary", "arbitrary")),
        name="inproj",
    )(x2, mod, gain, w_t, b)


def _head_sum(x):
    lo = lax.broadcasted_iota(jnp.int32, x.shape, 1) < HEAD
    s0 = jnp.sum(jnp.where(lo, x, 0.0), axis=-1, keepdims=True)
    s1 = jnp.sum(jnp.where(lo, 0.0, x), axis=-1, keepdims=True)
    return jnp.where(lo, s0, s1)


def _token_shift(x, carry_row, mix):
    rolled = pltpu.roll(x, 1, axis=0)
    first = lax.broadcasted_iota(jnp.int32, x.shape, 0) == 0
    prev = jnp.where(first, carry_row, rolled)
    return x + (prev - x) * mix


class _Operands:
    def __init__(self):
        self._parts = {}

    def parts(self, x, n):
        ent = self._parts.setdefault(id(x), [x])
        while len(ent) - 1 < n:
            rest = x
            for piece in ent[1:]:
                rest = rest - piece.astype(F32)
            ent.append(rest.astype(BF16))
        return ent[1:n + 1]

    def mm(self, a, b, passes, dims=(((1,), (0,)), ((), ()))):
        dot = lambda u, w: lax.dot_general(u, w, dims, preferred_element_type=F32)
        if passes == 1:
            return dot(self.parts(a, 1)[0], self.parts(b, 1)[0])
        a_hi, a_lo = self.parts(a, 2)
        b_hi, b_lo = self.parts(b, 2)
        return dot(a_hi, b_hi) + (dot(a_hi, b_lo) + dot(a_lo, b_hi))


_NT = (((1,), (1,)), ((), ()))
_TN = (((0,), (0,)), ((), ()))
SCAN_PASSES = {"gram": 1, "inverse": 1, "apply": 1, "state": 1}


CHUNK_FIELDS = ("ra", "aa", "bi", "ki", "bh", "kh", "vsw")


def _decay_chunk(ops, rc, lwc, kc, vsw, ac, bc):
    incl = (lax.broadcasted_iota(jnp.int32, (CHUNK, CHUNK), 0)
            >= lax.broadcasted_iota(jnp.int32, (CHUNK, CHUNK), 1))
    tri = jnp.where(incl, 1.0, 0.0).astype(BF16)
    lw = sum(jnp.dot(tri, piece, preferred_element_type=F32) for piece in ops.parts(lwc, 2))
    lw_end = lw[CHUNK - 1:CHUNK, :]
    e_in = jnp.exp(lw)
    e_ex = jnp.exp(lw - lwc)
    e_inv = jnp.exp(-lw)
    e_end = jnp.exp(lw_end - lw)
    return dict(vsw=vsw, g_end=jnp.exp(lw_end), ra=rc * e_in, aa=ac * e_ex,
                bi=bc * e_inv, ki=kc * e_inv, bh=bc * e_end, kh=kc * e_end)


def _scan_matrices(ops, chunks):
    row = lax.broadcasted_iota(jnp.int32, (CHUNK, CHUNK), 0)
    col = lax.broadcasted_iota(jnp.int32, (CHUNK, CHUNK), 1)
    incl = row >= col
    strict = row > col
    eye = row == col
    lo_lane = col < HEAD
    same_half = lo_lane == (row < HEAD)
    lo_lane2 = lax.broadcasted_iota(jnp.int32, (2 * CHUNK, LANES), 1) < HEAD
    level_masks = []
    n = 1
    while n < CHUNK:
        level_masks.append((row // (2 * n) == col // (2 * n)) & ((row // n) % 2 == 1) & ((col // n) % 2 == 0))
        n *= 2

    probs = []
    for ch in chunks:
        lhs_all = jnp.concatenate([ch["aa"], ch["ra"]], axis=0)
        rhs_all = jnp.concatenate([ch["bi"], ch["ki"]], axis=0)
        for h in range(LANES // HEAD):
            own = lo_lane if h == 0 else ~lo_lane
            own2 = lo_lane2 if h == 0 else ~lo_lane2
            gram = ops.mm(jnp.where(own2, lhs_all, 0.0), rhs_all, SCAN_PASSES["gram"], _NT)
            probs.append(dict(ch=ch, own=own,
                              m_ab=jnp.where(strict, gram[:CHUNK, :CHUNK], 0.0),
                              m_ak=jnp.where(strict, gram[:CHUNK, CHUNK:], 0.0),
                              m_rb=jnp.where(incl, gram[CHUNK:, :CHUNK], 0.0),
                              m_rk=jnp.where(incl, gram[CHUNK:, CHUNK:], 0.0)))
    yield

    for p in probs:
        p["t"] = jnp.where(eye, 1.0, 0.0) + jnp.where(level_masks[0], p["m_ab"], 0.0)
    n = 2
    for mask in level_masks[1:]:
        if n < SUBLANES:
            for p in probs:
                p["to"] = ops.mm(p["t"], jnp.where(mask, p["m_ab"], 0.0), SCAN_PASSES["inverse"])
            for p in probs:
                p["t"] = p["t"] + ops.mm(p["to"], p["t"], SCAN_PASSES["inverse"])
        else:
            lower = lambda t, n=n: jnp.concatenate(
                [t[i:i + n] for i in range(n, CHUNK, 2 * n)], axis=0)
            for p in probs:
                p["tl"] = lower(p["t"])
                p["to"] = ops.mm(p["tl"], jnp.where(mask, p["m_ab"], 0.0), SCAN_PASSES["inverse"])
            for p in probs:
                tl = p["tl"] + ops.mm(p["to"], p["t"], SCAN_PASSES["inverse"])
                slabs = []
                for j, i in enumerate(range(0, CHUNK, 2 * n)):
                    slabs += [p["t"][i:i + n], tl[j * n:(j + 1) * n]]
                p["t"] = jnp.concatenate(slabs, axis=0)
        n *= 2
        yield

    for p in probs:
        p["vm"] = jnp.where(p["own"], 0.0, p["ch"]["vsw"])
        p["x"] = jnp.where(p["own"], p["ch"]["aa"], 0.0) + ops.mm(p["m_ak"], p["vm"], SCAN_PASSES["apply"])
    yield
    for p in probs:
        p["tx"] = ops.mm(p["t"], p["x"], SCAN_PASSES["apply"])
    yield
    for p in probs:
        p["y"] = jnp.where(p["own"], p["ch"]["ra"], 0.0) + ops.mm(
            jnp.concatenate([p["m_rb"], p["m_rk"]], axis=1),
            jnp.concatenate([p["tx"], p["vm"]], axis=0), SCAN_PASSES["apply"])
    yield

    for i, ch in enumerate(chunks):
        p0, p1 = probs[2 * i], probs[2 * i + 1]
        pq = ops.mm(jnp.concatenate([ch["bh"], ch["kh"]], axis=0),
                    jnp.concatenate([jnp.concatenate([p0["tx"], p1["tx"]], axis=1),
                                     jnp.concatenate([p0["vm"], p1["vm"]], axis=1)], axis=0),
                    SCAN_PASSES["apply"], _TN)
        pq = jnp.where(row < HEAD, pq[:, :LANES], pq[:, LANES:])
        ch["p"] = jnp.where(same_half, pq, 0.0) + jnp.where(eye, ch["g_end"], 0.0)
        ch["q"] = jnp.where(same_half, 0.0, pq)
        ch["r"] = jnp.where(lo_lane, p0["y"], p1["y"])
        ch["y0"] = jnp.where(lo_lane, p1["y"], p0["y"])


def _interleave(main, *side):
    for _ in main:
        for gen in side:
            next(gen, None)
    for gen in side:
        for _ in gen:
            pass


def _rwkv_kernel(*refs, n_pairs, n_tblocks, n_tiles, cast_steps):
    n_cast = len(cast_steps)
    (r_ref, k_ref, v_ref, lora_ref, chan_ref, chanf_ref, mixl_ref, w2_ref, a2_ref, g2_ref) = refs[:10]
    cast_in, o_ref, cast_out = refs[10:10 + n_cast], refs[10 + n_cast], refs[11 + n_cast:11 + 2 * n_cast]
    (crkv_ref, clora_ref, z_ref, twd_s, ad_s, sgd_s, r3_s, k3_s, v3_s, g3_s,
     y_s, gend_s, p_s, q_s, rr_s, y0_s, *decay_s) = refs[11 + 2 * n_cast:]
    for w_ref, wb_ref, steps in zip(cast_in, cast_out, cast_steps):
        @pl.when(pl.program_id(0) < steps)
        def _(w_ref=w_ref, wb_ref=wb_ref):
            wb_ref[...] = w_ref[...].astype(wb_ref.dtype)

    tb = r_ref.shape[0]
    n_chunks = tb // CHUNK
    s = pl.program_id(0)
    sa = jnp.minimum(s, n_tiles - 1)
    sf = jnp.maximum(s - 2, 0)
    hp, t_a = sa % n_pairs, (sa // n_pairs) % n_tblocks
    hpf, t_f = sf % n_pairs, (sf // n_pairs) % n_tblocks
    slot_w, slot_r = s % 2, (s + 1) % 2
    keep_w, keep_r = s % 3, (s + 1) % 3

    @pl.when(s == 0)
    def _():
        for ref in (crkv_ref, clora_ref, z_ref, r3_s, k3_s, v3_s, g3_s, gend_s, p_s, q_s, rr_s, y0_s,
                    *decay_s):
            ref[...] = jnp.zeros_like(ref)

    @pl.when(hp == 0)
    def _():
        lora_in = lora_ref[...]
        carry = jnp.where(t_a == 0, 0.0, clora_ref[...])
        lora = _token_shift(lora_in, carry, mixl_ref[...])
        clora_ref[...] = lora_in[tb - 1:tb, :]
        twd = jnp.tanh(lora[:, 0:DECAY_LORA])
        twd_hi = twd.astype(BF16)
        twd_s[0] = twd_hi
        twd_s[1] = (twd - twd_hi.astype(F32)).astype(BF16)
        ad_s[...] = lora[:, DECAY_LORA:DECAY_LORA + ICLR_LORA].astype(BF16)
        sgd_s[...] = jax.nn.sigmoid(lora[:, DECAY_LORA + ICLR_LORA:]).astype(BF16)

    ops = _Operands()

    def finish():
        chanf = chanf_ref[...]
        z = jnp.where(t_f == 0, 0.0, z_ref[hpf])
        for c in range(n_chunks):
            i = slot_r * n_chunks + c
            rz = ops.mm(jnp.concatenate([rr_s[i], p_s[i]], axis=0), z, SCAN_PASSES["state"])
            y_s[c * CHUNK:(c + 1) * CHUNK, :] = rz[:CHUNK] + y0_s[i]
            z = rz[CHUNK:] + q_s[i]
            yield
        z_ref[hpf] = z
        r_k, ln_w, ln_b = chanf[7:8], chanf[8:9], chanf[9:10]
        y = pltpu.roll(y_s[...], HEAD, axis=1)
        mu = _head_sum(y) * (1.0 / HEAD)
        yc = y - mu
        var = _head_sum(yc * yc) * (1.0 / HEAD)
        yn = yc * lax.rsqrt(var + GN_EPS) * ln_w + ln_b
        bonus = _head_sum(r3_s[keep_r] * k3_s[keep_r] * r_k) * v3_s[keep_r]
        o_ref[...] = ((yn + bonus) * g3_s[keep_r]).astype(o_ref.dtype)
        yield

    def matrices():
        chunks = []
        for c in range(n_chunks):
            i = slot_r * n_chunks + c
            ch = {name: ref[i] for name, ref in zip(CHUNK_FIELDS, decay_s)}
            ch["g_end"] = gend_s[i, 0:1, :]
            chunks.append(ch)
        yield from _scan_matrices(ops, chunks)
        for c, ch in enumerate(chunks):
            i = slot_w * n_chunks + c
            p_s[i], q_s[i], rr_s[i], y0_s[i] = ch["p"], ch["q"], ch["r"], ch["y0"]
        yield

    def prepare():
        chan = chan_ref[...]
        mix_r, mix_k, mix_v = chan[0:1], chan[1:2], chan[2:3]
        w0, a0, k_k, k_a = chan[3:4], chan[4:5], chan[5:6], chan[6:7]
        r_in, k_in, v_in = r_ref[...], k_ref[...], v_ref[...]
        carry = jnp.where(t_a == 0, 0.0, crkv_ref[hp])
        r = _token_shift(r_in, carry[0:1], mix_r)
        k = _token_shift(k_in, carry[1:2], mix_k)
        v = _token_shift(v_in, carry[2:3], mix_v)
        crkv_ref[hp, 0:1, :] = r_in[tb - 1:tb, :]
        crkv_ref[hp, 1:2, :] = k_in[tb - 1:tb, :]
        crkv_ref[hp, 2:3, :] = v_in[tb - 1:tb, :]
        yield
        w2_hi = w2_ref[...].astype(BF16)
        w2_lo = (w2_ref[...] - w2_hi.astype(F32)).astype(BF16)
        twd_hi, twd_lo = twd_s[0], twd_s[1]
        dw = w0 + (_dot(twd_hi, w2_hi) + (_dot(twd_hi, w2_lo) + _dot(twd_lo, w2_hi)))
        zneg = -dw
        softplus = jnp.maximum(zneg, 0.0) + jnp.log(1.0 + jnp.exp(-jnp.abs(zneg)))
        logw = -jnp.exp(-softplus - 0.5)
        alr = jax.nn.sigmoid(a0 + _dot(ad_s[...], a2_ref[...].astype(BF16)))
        g3_s[keep_w] = _dot(sgd_s[...], g2_ref[...].astype(BF16))
        yield
        kk = k * k_k
        kk = kk / jnp.maximum(jnp.sqrt(_head_sum(kk * kk)), 1e-12)
        k_mod = k * (1.0 + (alr - 1.0) * k_a)
        vsw = pltpu.roll(v, HEAD, axis=1)
        a, b = -kk, kk * alr
        r3_s[keep_w], k3_s[keep_w], v3_s[keep_w] = r, k_mod, v
        yield
        for c in range(n_chunks):
            sl = slice(c * CHUNK, (c + 1) * CHUNK)
            ch = _decay_chunk(ops, r[sl], logw[sl], k_mod[sl], vsw[sl], a[sl], b[sl])
            i = slot_w * n_chunks + c
            for name, ref in zip(CHUNK_FIELDS, decay_s):
                ref[i] = ch[name]
            gend_s[i] = jnp.broadcast_to(ch["g_end"], (SUBLANES, LANES))
            yield

    _interleave(matrices(), finish(), prepare())


def _rwkv(p, chan, mix_lora, w2, a2, g2, batch, seq, offs, tb, cast):
    m = p.shape[0]
    c = w2.shape[1]
    n_pairs = c // LANES
    n_tblocks = seq // tb
    n_tiles = batch * n_tblocks * n_pairs
    n_chunks = tb // CHUNK
    gate_w = LORA_PAD - DECAY_LORA - ICLR_LORA
    prep = lambda s: jnp.minimum(s, n_tiles - 1)
    fin = lambda s: jnp.maximum(s - 2, 0)
    col = lambda name: lambda s: (prep(s) // n_pairs, offs[name] // LANES + prep(s) % n_pairs)
    weight = lambda s: (0, prep(s) % n_pairs)
    tile = lambda: pltpu.VMEM((tb, LANES), F32)
    tile3 = lambda: pltpu.VMEM((3, tb, LANES), F32)
    mats = lambda: pltpu.VMEM((2 * n_chunks, CHUNK, LANES), F32)
    cast_steps, cast_specs = [], []
    for w in cast:
        steps = min(n_tiles, w.shape[0] // BF16_ROWS)
        assert w.shape[0] % (steps * BF16_ROWS) == 0, w.shape
        cast_steps.append(steps)
        cast_specs.append(pl.BlockSpec((w.shape[0] // steps, w.shape[1]),
                                       lambda s, steps=steps: (jnp.minimum(s, steps - 1), 0)))
    outs = pl.pallas_call(
        functools.partial(_rwkv_kernel, n_pairs=n_pairs, n_tblocks=n_tblocks, n_tiles=n_tiles,
                          cast_steps=tuple(cast_steps)),
        grid=(n_tiles + 2,),
        in_specs=[pl.BlockSpec((tb, LANES), col("r")),
                  pl.BlockSpec((tb, LANES), col("k")),
                  pl.BlockSpec((tb, LANES), col("v")),
                  pl.BlockSpec((tb, LORA_PAD), lambda s: (prep(s) // n_pairs, offs["lora"] // LORA_PAD)),
                  pl.BlockSpec((16, LANES), weight),
                  pl.BlockSpec((16, LANES), lambda s: (0, fin(s) % n_pairs)),
                  pl.BlockSpec((1, LORA_PAD), lambda s: (0, 0)),
                  pl.BlockSpec((DECAY_LORA, LANES), weight),
                  pl.BlockSpec((ICLR_LORA, LANES), weight),
                  pl.BlockSpec((gate_w, LANES), weight)] + cast_specs,
        out_specs=[pl.BlockSpec((tb, LANES), lambda s: (fin(s) // n_pairs, fin(s) % n_pairs))] + cast_specs,
        out_shape=[jax.ShapeDtypeStruct((m, c), BF16)]
                  + [jax.ShapeDtypeStruct(w.shape, BF16) for w in cast],
        scratch_shapes=[pltpu.VMEM((n_pairs, SUBLANES, LANES), F32),
                        pltpu.VMEM((1, LORA_PAD), F32),
                        pltpu.VMEM((n_pairs, CHUNK, LANES), F32),
                        pltpu.VMEM((2, tb, DECAY_LORA), BF16),
                        pltpu.VMEM((tb, ICLR_LORA), BF16),
                        pltpu.VMEM((tb, gate_w), BF16),
                        tile3(), tile3(), tile3(), tile3(),
                        tile(),
                        pltpu.VMEM((2 * n_chunks, SUBLANES, LANES), F32),
                        mats(), mats(), mats(), mats()]
                       + [mats() for _ in CHUNK_FIELDS],
        compiler_params=_cparams(("arbitrary",)),
        name="rwkv7",
    )(p, p, p, p, chan, chan, mix_lora, w2, a2, g2, *cast)
    return outs[0], outs[1:]


def _attn_kernel(sink_ref, q_ref, kvc_ref, kvp_ref, o_ref, bias_s, *, blocks_per_seq, n_heads):
    nb = pl.program_id(0) % blocks_per_seq
    n_kv = n_heads // ATTN_GROUP

    @pl.when(nb <= 1)
    def _():
        qi = lax.broadcasted_iota(jnp.int32, (WINDOW, 2 * WINDOW), 0)
        kj = lax.broadcasted_iota(jnp.int32, (WINDOW, 2 * WINDOW), 1)
        dist = qi + WINDOW - kj
        valid = (dist >= 0) & (dist < WINDOW) & ((kj >= WINDOW) | (nb > 0))
        neg_dist = jnp.where(valid, -dist.astype(F32), -jnp.inf)
        for h in range(n_heads):
            bias_s[h] = (2.0 ** (-8.0 * (h + 1) / n_heads)) * neg_dist

    kv = jnp.concatenate([kvp_ref[...], kvc_ref[...]], axis=0).astype(BF16)
    q = (q_ref[...] * (HEAD ** -0.5)).astype(BF16)
    lo_lane = lax.broadcasted_iota(jnp.int32, (WINDOW, LANES), 1) < HEAD
    for g0 in range(0, n_heads, ATTN_BATCH):
        heads = range(g0, g0 + ATTN_BATCH)
        hk = g0 // ATTN_GROUP
        kh = kv[:, hk * HEAD:(hk + 1) * HEAD]
        vh = kv[:, (n_kv + hk) * HEAD:(n_kv + hk + 1) * HEAD]
        s = {h: _dot_nt(q[:, h * HEAD:(h + 1) * HEAD], kh) + bias_s[h] for h in heads}
        mx = {h: jnp.maximum(jnp.max(s[h], axis=-1, keepdims=True), sink_ref[h]) for h in heads}
        e = {h: jnp.exp(s[h] - mx[h]) for h in heads}
        denom = {h: jnp.sum(e[h], axis=-1, keepdims=True) + jnp.exp(sink_ref[h] - mx[h]) for h in heads}
        pv = {h: _dot(e[h].astype(BF16), vh) for h in heads}
        for h0 in range(g0, g0 + ATTN_BATCH, 2):
            pair = (jnp.concatenate([pv[h0], pv[h0 + 1]], axis=1)
                    * jnp.where(lo_lane, 1.0 / denom[h0], 1.0 / denom[h0 + 1]))
            o_ref[:, h0 * HEAD:(h0 + 2) * HEAD] = pair.astype(o_ref.dtype)


def _attn(p, sinks, seq, offs):
    m = p.shape[0]
    n_heads = sinks.shape[0]
    qc = n_heads * HEAD
    kvc = 2 * (n_heads // ATTN_GROUP) * HEAD
    bps = seq // WINDOW
    return pl.pallas_call(
        functools.partial(_attn_kernel, blocks_per_seq=bps, n_heads=n_heads),
        grid=(m // WINDOW,),
        in_specs=[pl.BlockSpec(memory_space=pltpu.SMEM),
                  pl.BlockSpec((WINDOW, qc), lambda n: (n, offs["q"] // qc)),
                  pl.BlockSpec((WINDOW, kvc), lambda n: (n, offs["kv"] // kvc)),
                  pl.BlockSpec((WINDOW, kvc), lambda n: (jnp.maximum(n - 1, 0), offs["kv"] // kvc))],
        out_specs=pl.BlockSpec((WINDOW, qc), lambda n: (n, 0)),
        out_shape=jax.ShapeDtypeStruct((m, qc), BF16),
        scratch_shapes=[pltpu.VMEM((n_heads, WINDOW, 2 * WINDOW), F32)],
        compiler_params=_cparams(("arbitrary",)),
        name="swa",
    )(sinks, p, p, p)


def _merge_kernel(yr_ref, ya_ref, gate_ref, x_ref, mod_ref, wbr_ref, wba_ref, wout_ref, g2_ref,
                  x1_ref, h2_ref):
    d = x_ref.shape[1]
    gates = jax.nn.sigmoid(gate_ref[...])
    merged = gates[:, :d] * _dot(yr_ref[...], wbr_ref[...]) + gates[:, d:] * _dot(ya_ref[...], wba_ref[...])
    x1 = x_ref[...] + mod_ref[0, 2:3, :] * _dot(merged.astype(BF16), wout_ref[...])
    x1_ref[...] = x1
    h2 = _rms_norm(x1, g2_ref[...]) * (1.0 + mod_ref[0, 4:5, :]) + mod_ref[0, 3:4, :]
    h2_ref[...] = h2.astype(BF16)


def _merge(yr, ya, p, x2, mod, wbr, wba, wout, gain2, rows_per_batch, tm):
    m, d = x2.shape
    c = yr.shape[1]
    tpb = rows_per_batch // tm
    const = lambda shape: pl.BlockSpec(shape, lambda i: (0, 0), pipeline_mode=pl.Buffered(1))
    return pl.pallas_call(
        _merge_kernel,
        grid=(m // tm,),
        in_specs=[pl.BlockSpec((tm, c), lambda i: (i, 0)),
                  pl.BlockSpec((tm, c), lambda i: (i, 0)),
                  pl.BlockSpec((tm, 2 * d), lambda i: (i, 0)),
                  pl.BlockSpec((tm, d), lambda i: (i, 0)),
                  pl.BlockSpec((1, 6, d), lambda i: (i // tpb, 0, 0)),
                  const((c, d)), const((c, d)), const((d, d)),
                  pl.BlockSpec((1, d), lambda i: (0, 0))],
        out_specs=[pl.BlockSpec((tm, d), lambda i: (i, 0)),
                   pl.BlockSpec((tm, d), lambda i: (i, 0))],
        out_shape=[jax.ShapeDtypeStruct((m, d), F32), jax.ShapeDtypeStruct((m, d), BF16)],
        compiler_params=_cparams(("arbitrary",)),
        name="merge",
    )(yr, ya, p, x2, mod, wbr, wba, wout, gain2)


def _mlp_kernel(h_ref, wu_ref, wd_ref, x1_ref, mod_ref, fg_ref, o_ref, acc_ref):
    f = pl.program_id(1)

    @pl.when(f == 0)
    def _():
        acc_ref[...] = jnp.zeros_like(acc_ref)

    u = _dot(h_ref[...], wu_ref[...])
    act = jnp.square(jnp.maximum(u, 0.0)).astype(BF16)
    acc_ref[...] += _dot(act, wd_ref[...])

    @pl.when(f == pl.num_programs(1) - 1)
    def _():
        x2 = x1_ref[...] + mod_ref[0, 5:6, :] * acc_ref[...]
        o_ref[...] = _rms_norm(x2, fg_ref[...])


def _mlp(h2, wu, wd, x1, mod, final_gain, rows_per_batch, tm, tf):
    m, d = h2.shape
    f = wu.shape[1]
    tpb = rows_per_batch // tm
    return pl.pallas_call(
        _mlp_kernel,
        grid=(m // tm, f // tf),
        in_specs=[pl.BlockSpec((tm, d), lambda i, j: (i, 0)),
                  pl.BlockSpec((d, tf), lambda i, j: (0, j)),
                  pl.BlockSpec((tf, d), lambda i, j: (j, 0)),
                  pl.BlockSpec((tm, d), lambda i, j: (i, 0)),
                  pl.BlockSpec((1, 6, d), lambda i, j: (i // tpb, 0, 0)),
                  pl.BlockSpec((1, d), lambda i, j: (0, 0))],
        out_specs=pl.BlockSpec((tm, d), lambda i, j: (i, 0)),
        out_shape=jax.ShapeDtypeStruct((m, d), F32),
        scratch_shapes=[pltpu.VMEM((tm, d), F32)],
        compiler_params=_cparams(("arbitrary", "arbitrary")),
        name="mlp",
    )(h2, wu, wd, x1, mod, final_gain)


def _forward(x, c, w_ada, b_ada, norm1_gain, w_in, b_in, rwkv_mix, rwkv_w0, rwkv_w2, rwkv_a0,
             rwkv_a2, rwkv_g2, rwkv_k_k, rwkv_k_a, rwkv_r_k, rwkv_ln_w, rwkv_ln_b, attn_sinks,
             w_branch_rwkv, w_branch_attn, w_out, norm2_gain, w_up, w_down, final_gain):
    batch, seq, d = x.shape
    depth = w_ada.shape[0]
    c_rwkv = rwkv_w0.shape[1]
    n_heads = attn_sinks.shape[1]
    qc = n_heads * HEAD
    kvc = 2 * (n_heads // ATTN_GROUP) * HEAD
    lora_w = DECAY_LORA + ICLR_LORA + GATE_LORA
    rwkv_cols = 3 * c_rwkv + lora_w
    attn_cols = qc + kvc
    offs = {"gate": 0, "r": 2 * d, "k": 2 * d + c_rwkv, "v": 2 * d + 2 * c_rwkv, "q": 2 * d + 3 * c_rwkv}
    offs["lora"] = offs["q"] + qc
    offs["kv"] = offs["lora"] + LORA_PAD
    n_pack = offs["kv"] + kvc

    segments = ((rwkv_cols + attn_cols, 2 * d, offs["gate"]), (0, 3 * c_rwkv, offs["r"]),
                (rwkv_cols, qc, offs["q"]), (3 * c_rwkv, lora_w, offs["lora"]),
                (rwkv_cols + qc, kvc, offs["kv"]))
    zero_rows = (offs["lora"] + lora_w, offs["kv"])

    def pack_cols(t):
        out = jnp.zeros(t.shape[:-1] + (n_pack,), t.dtype)
        for src, width, dst in segments:
            out = out.at[..., dst:dst + width].set(t[..., src:src + width])
        return out

    tm_in = min(1024, seq)
    tn_in = 1280 if n_pack % 1280 == 0 else LANES
    tb = min(512, seq)
    tm_merge = min(256, seq)
    tm_mlp = min(512, seq)
    tf = min(1024, w_up.shape[2])

    assert depth == 1, "single-layer block"
    layer = lambda t: t.reshape(t.shape[1:])
    x2 = x.reshape(batch * seq, d)
    for l in range(depth):
        mod = _ada(c, layer(w_ada), b_ada[l]).reshape(batch, 6, d)
        w_pack = _pack_weight_t(layer(w_in).T, segments, zero_rows, n_pack)
        b_pack = pack_cols(b_in[l]).reshape(1, n_pack)
        p = _inproj(x2, mod, norm1_gain[l].reshape(1, d), w_pack, b_pack, seq, tm_in, tn_in)

        mix = rwkv_mix[l]
        chan = jnp.stack([mix[:c_rwkv], mix[c_rwkv:2 * c_rwkv], mix[2 * c_rwkv:3 * c_rwkv],
                          rwkv_w0[l], rwkv_a0[l], rwkv_k_k[l], rwkv_k_a[l], rwkv_r_k[l].reshape(c_rwkv),
                          rwkv_ln_w[l], rwkv_ln_b[l]])
        chan = jnp.pad(chan, ((0, 16 - chan.shape[0]), (0, 0)))
        mix_lora = jnp.pad(mix[3 * c_rwkv:], (0, LORA_PAD - lora_w)).reshape(1, LORA_PAD)
        g2_pad = jnp.pad(rwkv_g2[l], ((0, LORA_PAD - lora_w), (0, 0)))
        y_rwkv, (wbr, wba, wout, wup, wdown) = _rwkv(
            p, chan, mix_lora, rwkv_w2[l], rwkv_a2[l], g2_pad, batch, seq, offs, tb,
            cast=[layer(w_branch_rwkv), layer(w_branch_attn), layer(w_out), layer(w_up), layer(w_down)])
        y_attn = _attn(p, attn_sinks[l], seq, offs)

        x2, h2 = _merge(y_rwkv, y_attn, p, x2, mod, wbr, wba, wout,
                        norm2_gain[l].reshape(1, d), seq, tm_merge)
        x2 = _mlp(h2, wup, wdown, x2, mod, final_gain.reshape(1, d), seq, tm_mlp, tf)
    return x2.reshape(batch, seq, d)


def kernel(x, c, w_ada, b_ada, norm1_gain, w_in, b_in, rwkv_mix, rwkv_w0, rwkv_w2, rwkv_a0, rwkv_a2, rwkv_g2, rwkv_k_k, rwkv_k_a, rwkv_r_k, rwkv_ln_w, rwkv_ln_b, attn_sinks, w_branch_rwkv, w_branch_attn, w_out, norm2_gain, w_up, w_down, final_gain):
    return _forward(x, c, w_ada, b_ada, norm1_gain, w_in, b_in, rwkv_mix, rwkv_w0, rwkv_w2, rwkv_a0,
                    rwkv_a2, rwkv_g2, rwkv_k_k, rwkv_k_a, rwkv_r_k, rwkv_ln_w, rwkv_ln_b, attn_sinks,
                    w_branch_rwkv, w_branch_attn, w_out, norm2_gain, w_up, w_down, final_gain)
```

```python
import functools

import jax
import jax.numpy as jnp
from jax import lax
from jax.experimental import pallas as pl
from jax.experimental.pallas import tpu as pltpu

F32 = jnp.float32
BF16 = jnp.bfloat16
HIGHEST = lax.Precision.HIGHEST

LANES = 128
SUBLANES = 8
BF16_ROWS = 16
MXU_N = 256
HEAD = 64
CHUNK = 128
WINDOW = 128
ATTN_GROUP = 8
ATTN_BATCH = 8
NORM_EPS = 1e-6
GN_EPS = 64e-5
DECAY_LORA = 64
ICLR_LORA = 64
GATE_LORA = 160
LORA_PAD = 512
VMEM_LIMIT = 56 * 1024 * 1024


def _cparams(sem):
    return pltpu.CompilerParams(dimension_semantics=sem, vmem_limit_bytes=VMEM_LIMIT)


def _dot(a, b, precision=None):
    return jnp.dot(a, b, preferred_element_type=F32, precision=precision)


def _dot_nt(a, b, precision=None):
    return lax.dot_general(a, b, (((1,), (1,)), ((), ())), preferred_element_type=F32,
                           precision=precision)


def _dot_tn(a, b, precision=None):
    return lax.dot_general(a, b, (((0,), (0,)), ((), ())), preferred_element_type=F32,
                           precision=precision)


def _rms_norm(x, gain):
    ms = jnp.mean(x * x, axis=-1, keepdims=True)
    return x * lax.rsqrt(ms + NORM_EPS) * gain


def _ada_kernel(ct_ref, w_ref, b_ref, o_ref):
    ct = ct_ref[...]
    act = ct * jax.nn.sigmoid(ct)
    for m in range(o_ref.shape[0]):
        o_ref[m:m + 1, :] = jnp.sum(w_ref[...] * act[:, m:m + 1], axis=0, keepdims=True) + b_ref[...]


def _ada(c, w, b, tn=1024):
    nb, d = c.shape
    n = w.shape[1]
    return pl.pallas_call(
        _ada_kernel,
        grid=(n // tn,),
        in_specs=[pl.BlockSpec((d, nb), lambda j: (0, 0)),
                  pl.BlockSpec((d, tn), lambda j: (0, j)),
                  pl.BlockSpec((1, tn), lambda j: (0, j))],
        out_specs=pl.BlockSpec((nb, tn), lambda j: (0, j)),
        out_shape=jax.ShapeDtypeStruct((nb, n), F32),
        compiler_params=_cparams(("arbitrary",)),
        name="adaln",
    )(c.T, w, b.reshape(1, n))


def _pack_kernel(w_ref, o_ref, *, segments, zero_rows):
    for src, width, dst in segments:
        o_ref[dst:dst + width, :] = w_ref[src:src + width, :].astype(o_ref.dtype)
    lo, hi = zero_rows
    o_ref[lo:hi, :] = jnp.zeros((hi - lo, o_ref.shape[1]), o_ref.dtype)


def _pack_weight_t(w_t, segments, zero_rows, n_pack, tk=256):
    n, k = w_t.shape
    return pl.pallas_call(
        functools.partial(_pack_kernel, segments=segments, zero_rows=zero_rows),
        grid=(k // tk,),
        in_specs=[pl.BlockSpec((n, tk), lambda i: (0, i))],
        out_specs=pl.BlockSpec((n_pack, tk), lambda i: (0, i)),
        out_shape=jax.ShapeDtypeStruct((n_pack, k), BF16),
        compiler_params=_cparams(("arbitrary",)),
        name="pack_w_in",
    )(w_t)


def _inproj_kernel(x_ref, mod_ref, g_ref, w_ref, b_ref, o_ref, h_ref):
    @pl.when(pl.program_id(1) == 0)
    def _():
        h = _rms_norm(x_ref[...], g_ref[...]) * (1.0 + mod_ref[0, 1:2, :]) + mod_ref[0, 0:1, :]
        h_ref[...] = h.astype(BF16)

    o_ref[...] = _dot_nt(h_ref[...], w_ref[...]) + b_ref[...]


def _inproj(x2, mod, gain, w_t, b, rows_per_batch, tm, tn):
    m, d = x2.shape
    n = w_t.shape[0]
    tpb = rows_per_batch // tm
    return pl.pallas_call(
        _inproj_kernel,
        grid=(m // tm, n // tn),
        in_specs=[pl.BlockSpec((tm, d), lambda i, j: (i, 0)),
                  pl.BlockSpec((1, 6, d), lambda i, j: (i // tpb, 0, 0)),
                  pl.BlockSpec((1, d), lambda i, j: (0, 0)),
                  pl.BlockSpec((tn, d), lambda i, j: (j, 0)),
                  pl.BlockSpec((1, tn), lambda i, j: (0, j))],
        out_specs=pl.BlockSpec((tm, tn), lambda i, j: (i, j)),
        out_shape=jax.ShapeDtypeStruct((m, n), F32),
        scratch_shapes=[pltpu.VMEM((tm, d), BF16)],
        compiler_params=_cparams(("arbitrary", "arbitrary")),
        name="inproj",
    )(x2, mod, gain, w_t, b)


def _head_sum(x):
    lo = lax.broadcasted_iota(jnp.int32, x.shape, 1) < HEAD
    s0 = jnp.sum(jnp.where(lo, x, 0.0), axis=-1, keepdims=True)
    s1 = jnp.sum(jnp.where(lo, 0.0, x), axis=-1, keepdims=True)
    return jnp.where(lo, s0, s1)


def _token_shift(x, carry_row, mix):
    rolled = pltpu.roll(x, 1, axis=0)
    first = lax.broadcasted_iota(jnp.int32, x.shape, 0) == 0
    prev = jnp.where(first, carry_row, rolled)
    return x + (prev - x) * mix


class _Operands:
    def __init__(self):
        self._parts = {}

    def parts(self, x, n):
        ent = self._parts.setdefault(id(x), [x])
        while len(ent) - 1 < n:
            rest = x
            for piece in ent[1:]:
                rest = rest - piece.astype(F32)
            ent.append(rest.astype(BF16))
        return ent[1:n + 1]

    def mm(self, a, b, passes, dims=(((1,), (0,)), ((), ()))):
        dot = lambda u, w: lax.dot_general(u, w, dims, preferred_element_type=F32)
        if passes == 1:
            return dot(self.parts(a, 1)[0], self.parts(b, 1)[0])
        a_hi, a_lo = self.parts(a, 2)
        b_hi, b_lo = self.parts(b, 2)
        return dot(a_hi, b_hi) + (dot(a_hi, b_lo) + dot(a_lo, b_hi))


_NT = (((1,), (1,)), ((), ()))
_TN = (((0,), (0,)), ((), ()))
SCAN_PASSES = {"gram": 1, "inverse": 1, "apply": 1, "state": 1}


CHUNK_FIELDS = ("ra", "aa", "bi", "ki", "bh", "kh", "vsw")


def _decay_chunk(ops, rc, lwc, kc, vsw, ac, bc):
    incl = (lax.broadcasted_iota(jnp.int32, (CHUNK, CHUNK), 0)
            >= lax.broadcasted_iota(jnp.int32, (CHUNK, CHUNK), 1))
    tri = jnp.where(incl, 1.0, 0.0).astype(BF16)
    lw = sum(jnp.dot(tri, piece, preferred_element_type=F32) for piece in ops.parts(lwc, 2))
    lw_end = lw[CHUNK - 1:CHUNK, :]
    e_in = jnp.exp(lw)
    e_ex = jnp.exp(lw - lwc)
    e_inv = jnp.exp(-lw)
    e_end = jnp.exp(lw_end - lw)
    return dict(vsw=vsw, g_end=jnp.exp(lw_end), ra=rc * e_in, aa=ac * e_ex,
                bi=bc * e_inv, ki=kc * e_inv, bh=bc * e_end, kh=kc * e_end)


def _scan_matrices(ops, chunks):
    row = lax.broadcasted_iota(jnp.int32, (CHUNK, CHUNK), 0)
    col = lax.broadcasted_iota(jnp.int32, (CHUNK, CHUNK), 1)
    incl = row >= col
    strict = row > col
    eye = row == col
    lo_lane = col < HEAD
    same_half = lo_lane == (row < HEAD)
    lo_lane2 = lax.broadcasted_iota(jnp.int32, (2 * CHUNK, LANES), 1) < HEAD
    level_masks = []
    n = 1
    while n < CHUNK:
        level_masks.append((row // (2 * n) == col // (2 * n)) & ((row // n) % 2 == 1) & ((col // n) % 2 == 0))
        n *= 2

    probs = []
    for ch in chunks:
        lhs_all = jnp.concatenate([ch["aa"], ch["ra"]], axis=0)
        rhs_all = jnp.concatenate([ch["bi"], ch["ki"]], axis=0)
        for h in range(LANES // HEAD):
            own = lo_lane if h == 0 else ~lo_lane
            own2 = lo_lane2 if h == 0 else ~lo_lane2
            gram = ops.mm(jnp.where(own2, lhs_all, 0.0), rhs_all, SCAN_PASSES["gram"], _NT)
            probs.append(dict(ch=ch, own=own,
                              m_ab=jnp.where(strict, gram[:CHUNK, :CHUNK], 0.0),
                              m_ak=jnp.where(strict, gram[:CHUNK, CHUNK:], 0.0),
                              m_rb=jnp.where(incl, gram[CHUNK:, :CHUNK], 0.0),
                              m_rk=jnp.where(incl, gram[CHUNK:, CHUNK:], 0.0)))
    yield

    for p in probs:
        p["t"] = jnp.where(eye, 1.0, 0.0) + jnp.where(level_masks[0], p["m_ab"], 0.0)
    n = 2
    for mask in level_masks[1:]:
        if n < SUBLANES:
            for p in probs:
                p["to"] = ops.mm(p["t"], jnp.where(mask, p["m_ab"], 0.0), SCAN_PASSES["inverse"])
            for p in probs:
                p["t"] = p["t"] + ops.mm(p["to"], p["t"], SCAN_PASSES["inverse"])
        else:
            lower = lambda t, n=n: jnp.concatenate(
                [t[i:i + n] for i in range(n, CHUNK, 2 * n)], axis=0)
            for p in probs:
                p["tl"] = lower(p["t"])
                p["to"] = ops.mm(p["tl"], jnp.where(mask, p["m_ab"], 0.0), SCAN_PASSES["inverse"])
            for p in probs:
                tl = p["tl"] + ops.mm(p["to"], p["t"], SCAN_PASSES["inverse"])
                slabs = []
                for j, i in enumerate(range(0, CHUNK, 2 * n)):
                    slabs += [p["t"][i:i + n], tl[j * n:(j + 1) * n]]
                p["t"] = jnp.concatenate(slabs, axis=0)
        n *= 2
        yield

    for p in probs:
        p["vm"] = jnp.where(p["own"], 0.0, p["ch"]["vsw"])
        p["x"] = jnp.where(p["own"], p["ch"]["aa"], 0.0) + ops.mm(p["m_ak"], p["vm"], SCAN_PASSES["apply"])
    yield
    for p in probs:
        p["tx"] = ops.mm(p["t"], p["x"], SCAN_PASSES["apply"])
    yield
    for p in probs:
        p["y"] = jnp.where(p["own"], p["ch"]["ra"], 0.0) + ops.mm(
            jnp.concatenate([p["m_rb"], p["m_rk"]], axis=1),
            jnp.concatenate([p["tx"], p["vm"]], axis=0), SCAN_PASSES["apply"])
    yield

    for i, ch in enumerate(chunks):
        p0, p1 = probs[2 * i], probs[2 * i + 1]
        pq = ops.mm(jnp.concatenate([ch["bh"], ch["kh"]], axis=0),
                    jnp.concatenate([jnp.concatenate([p0["tx"], p1["tx"]], axis=1),
                                     jnp.concatenate([p0["vm"], p1["vm"]], axis=1)], axis=0),
                    SCAN_PASSES["apply"], _TN)
        pq = jnp.where(row < HEAD, pq[:, :LANES], pq[:, LANES:])
        ch["p"] = jnp.where(same_half, pq, 0.0) + jnp.where(eye, ch["g_end"], 0.0)
        ch["q"] = jnp.where(same_half, 0.0, pq)
        ch["r"] = jnp.where(lo_lane, p0["y"], p1["y"])
        ch["y0"] = jnp.where(lo_lane, p1["y"], p0["y"])


def _interleave(main, *side):
    for _ in main:
        for gen in side:
            next(gen, None)
    for gen in side:
        for _ in gen:
            pass


def _rwkv_kernel(*refs, n_pairs, n_tblocks, n_tiles, cast_steps):
    n_cast = len(cast_steps)
    (r_ref, k_ref, v_ref, lora_ref, chan_ref, chanf_ref, mixl_ref, w2_ref, a2_ref, g2_ref) = refs[:10]
    cast_in, o_ref, cast_out = refs[10:10 + n_cast], refs[10 + n_cast], refs[11 + n_cast:11 + 2 * n_cast]
    (crkv_ref, clora_ref, z_ref, twd_s, ad_s, sgd_s, r3_s, k3_s, v3_s, g3_s,
     y_s, gend_s, p_s, q_s, rr_s, y0_s, *decay_s) = refs[11 + 2 * n_cast:]
    for w_ref, wb_ref, steps in zip(cast_in, cast_out, cast_steps):
        @pl.when(pl.program_id(0) < steps)
        def _(w_ref=w_ref, wb_ref=wb_ref):
            wb_ref[...] = w_ref[...].astype(wb_ref.dtype)

    tb = r_ref.shape[0]
    n_chunks = tb // CHUNK
    s = pl.program_id(0)
    sa = jnp.minimum(s, n_tiles - 1)
    sf = jnp.maximum(s - 2, 0)
    hp, t_a = sa % n_pairs, (sa // n_pairs) % n_tblocks
    hpf, t_f = sf % n_pairs, (sf // n_pairs) % n_tblocks
    slot_w, slot_r = s % 2, (s + 1) % 2
    keep_w, keep_r = s % 3, (s + 1) % 3

    @pl.when(s == 0)
    def _():
        for ref in (crkv_ref, clora_ref, z_ref, r3_s, k3_s, v3_s, g3_s, gend_s, p_s, q_s, rr_s, y0_s,
                    *decay_s):
            ref[...] = jnp.zeros_like(ref)

    @pl.when(hp == 0)
    def _():
        lora_in = lora_ref[...]
        carry = jnp.where(t_a == 0, 0.0, clora_ref[...])
        lora = _token_shift(lora_in, carry, mixl_ref[...])
        clora_ref[...] = lora_in[tb - 1:tb, :]
        twd = jnp.tanh(lora[:, 0:DECAY_LORA])
        twd_hi = twd.astype(BF16)
        twd_s[0] = twd_hi
        twd_s[1] = (twd - twd_hi.astype(F32)).astype(BF16)
        ad_s[...] = lora[:, DECAY_LORA:DECAY_LORA + ICLR_LORA].astype(BF16)
        sgd_s[...] = jax.nn.sigmoid(lora[:, DECAY_LORA + ICLR_LORA:]).astype(BF16)

    ops = _Operands()

    def finish():
        chanf = chanf_ref[...]
        z = jnp.where(t_f == 0, 0.0, z_ref[hpf])
        for c in range(n_chunks):
            i = slot_r * n_chunks + c
            rz = ops.mm(jnp.concatenate([rr_s[i], p_s[i]], axis=0), z, SCAN_PASSES["state"])
            y_s[c * CHUNK:(c + 1) * CHUNK, :] = rz[:CHUNK] + y0_s[i]
            z = rz[CHUNK:] + q_s[i]
            yield
        z_ref[hpf] = z
        r_k, ln_w, ln_b = chanf[7:8], chanf[8:9], chanf[9:10]
        y = pltpu.roll(y_s[...], HEAD, axis=1)
        mu = _head_sum(y) * (1.0 / HEAD)
        yc = y - mu
        var = _head_sum(yc * yc) * (1.0 / HEAD)
        yn = yc * lax.rsqrt(var + GN_EPS) * ln_w + ln_b
        bonus = _head_sum(r3_s[keep_r] * k3_s[keep_r] * r_k) * v3_s[keep_r]
        o_ref[...] = ((yn + bonus) * g3_s[keep_r]).astype(o_ref.dtype)
        yield

    def matrices():
        chunks = []
        for c in range(n_chunks):
            i = slot_r * n_chunks + c
            ch = {name: ref[i] for name, ref in zip(CHUNK_FIELDS, decay_s)}
            ch["g_end"] = gend_s[i, 0:1, :]
            chunks.append(ch)
        yield from _scan_matrices(ops, chunks)
        for c, ch in enumerate(chunks):
            i = slot_w * n_chunks + c
            p_s[i], q_s[i], rr_s[i], y0_s[i] = ch["p"], ch["q"], ch["r"], ch["y0"]
        yield

    def prepare():
        chan = chan_ref[...]
        mix_r, mix_k, mix_v = chan[0:1], chan[1:2], chan[2:3]
        w0, a0, k_k, k_a = chan[3:4], chan[4:5], chan[5:6], chan[6:7]
        r_in, k_in, v_in = r_ref[...], k_ref[...], v_ref[...]
        carry = jnp.where(t_a == 0, 0.0, crkv_ref[hp])
        r = _token_shift(r_in, carry[0:1], mix_r)
        k = _token_shift(k_in, carry[1:2], mix_k)
        v = _token_shift(v_in, carry[2:3], mix_v)
        crkv_ref[hp, 0:1, :] = r_in[tb - 1:tb, :]
        crkv_ref[hp, 1:2, :] = k_in[tb - 1:tb, :]
        crkv_ref[hp, 2:3, :] = v_in[tb - 1:tb, :]
        yield
        w2_hi = w2_ref[...].astype(BF16)
        w2_lo = (w2_ref[...] - w2_hi.astype(F32)).astype(BF16)
        twd_hi, twd_lo = twd_s[0], twd_s[1]
        dw = w0 + (_dot(twd_hi, w2_hi) + (_dot(twd_hi, w2_lo) + _dot(twd_lo, w2_hi)))
        zneg = -dw
        softplus = jnp.maximum(zneg, 0.0) + jnp.log(1.0 + jnp.exp(-jnp.abs(zneg)))
        logw = -jnp.exp(-softplus - 0.5)
        alr = jax.nn.sigmoid(a0 + _dot(ad_s[...], a2_ref[...].astype(BF16)))
        g3_s[keep_w] = _dot(sgd_s[...], g2_ref[...].astype(BF16))
        yield
        kk = k * k_k
        kk = kk / jnp.maximum(jnp.sqrt(_head_sum(kk * kk)), 1e-12)
        k_mod = k * (1.0 + (alr - 1.0) * k_a)
        vsw = pltpu.roll(v, HEAD, axis=1)
        a, b = -kk, kk * alr
        r3_s[keep_w], k3_s[keep_w], v3_s[keep_w] = r, k_mod, v
        yield
        for c in range(n_chunks):
            sl = slice(c * CHUNK, (c + 1) * CHUNK)
            ch = _decay_chunk(ops, r[sl], logw[sl], k_mod[sl], vsw[sl], a[sl], b[sl])
            i = slot_w * n_chunks + c
            for name, ref in zip(CHUNK_FIELDS, decay_s):
                ref[i] = ch[name]
            gend_s[i] = jnp.broadcast_to(ch["g_end"], (SUBLANES, LANES))
            yield

    _interleave(matrices(), finish(), prepare())


def _rwkv(p, chan, mix_lora, w2, a2, g2, batch, seq, offs, tb, cast):
    m = p.shape[0]
    c = w2.shape[1]
    n_pairs = c // LANES
    n_tblocks = seq // tb
    n_tiles = batch * n_tblocks * n_pairs
    n_chunks = tb // CHUNK
    gate_w = LORA_PAD - DECAY_LORA - ICLR_LORA
    prep = lambda s: jnp.minimum(s, n_tiles - 1)
    fin = lambda s: jnp.maximum(s - 2, 0)
    col = lambda name: lambda s: (prep(s) // n_pairs, offs[name] // LANES + prep(s) % n_pairs)
    weight = lambda s: (0, prep(s) % n_pairs)
    tile = lambda: pltpu.VMEM((tb, LANES), F32)
    tile3 = lambda: pltpu.VMEM((3, tb, LANES), F32)
    mats = lambda: pltpu.VMEM((2 * n_chunks, CHUNK, LANES), F32)
    cast_steps, cast_specs = [], []
    for w in cast:
        steps = min(n_tiles, w.shape[0] // BF16_ROWS)
        assert w.shape[0] % (steps * BF16_ROWS) == 0, w.shape
        cast_steps.append(steps)
        cast_specs.append(pl.BlockSpec((w.shape[0] // steps, w.shape[1]),
                                       lambda s, steps=steps: (jnp.minimum(s, steps - 1), 0)))
    outs = pl.pallas_call(
        functools.partial(_rwkv_kernel, n_pairs=n_pairs, n_tblocks=n_tblocks, n_tiles=n_tiles,
                          cast_steps=tuple(cast_steps)),
        grid=(n_tiles + 2,),
        in_specs=[pl.BlockSpec((tb, LANES), col("r")),
                  pl.BlockSpec((tb, LANES), col("k")),
                  pl.BlockSpec((tb, LANES), col("v")),
                  pl.BlockSpec((tb, LORA_PAD), lambda s: (prep(s) // n_pairs, offs["lora"] // LORA_PAD)),
                  pl.BlockSpec((16, LANES), weight),
                  pl.BlockSpec((16, LANES), lambda s: (0, fin(s) % n_pairs)),
                  pl.BlockSpec((1, LORA_PAD), lambda s: (0, 0)),
                  pl.BlockSpec((DECAY_LORA, LANES), weight),
                  pl.BlockSpec((ICLR_LORA, LANES), weight),
                  pl.BlockSpec((gate_w, LANES), weight)] + cast_specs,
        out_specs=[pl.BlockSpec((tb, LANES), lambda s: (fin(s) // n_pairs, fin(s) % n_pairs))] + cast_specs,
        out_shape=[jax.ShapeDtypeStruct((m, c), BF16)]
                  + [jax.ShapeDtypeStruct(w.shape, BF16) for w in cast],
        scratch_shapes=[pltpu.VMEM((n_pairs, SUBLANES, LANES), F32),
                        pltpu.VMEM((1, LORA_PAD), F32),
                        pltpu.VMEM((n_pairs, CHUNK, LANES), F32),
                        pltpu.VMEM((2, tb, DECAY_LORA), BF16),
                        pltpu.VMEM((tb, ICLR_LORA), BF16),
                        pltpu.VMEM((tb, gate_w), BF16),
                        tile3(), tile3(), tile3(), tile3(),
                        tile(),
                        pltpu.VMEM((2 * n_chunks, SUBLANES, LANES), F32),
                        mats(), mats(), mats(), mats()]
                       + [mats() for _ in CHUNK_FIELDS],
        compiler_params=_cparams(("arbitrary",)),
        name="rwkv7",
    )(p, p, p, p, chan, chan, mix_lora, w2, a2, g2, *cast)
    return outs[0], outs[1:]


def _attn_stages(sink_ref, q, kv, bias, store, n_heads):
    n_kv = n_heads // ATTN_GROUP
    lo_lane = lax.broadcasted_iota(jnp.int32, (WINDOW, LANES), 1) < HEAD
    for g0 in range(0, n_heads, ATTN_BATCH):
        heads = range(g0, g0 + ATTN_BATCH)
        hk = g0 // ATTN_GROUP
        kh = kv[:, hk * HEAD:(hk + 1) * HEAD]
        vh = kv[:, (n_kv + hk) * HEAD:(n_kv + hk + 1) * HEAD]
        s = {h: _dot_nt(q[:, h * HEAD:(h + 1) * HEAD], kh) + bias(h) for h in heads}
        yield
        mx = {h: jnp.maximum(jnp.max(s[h], axis=-1, keepdims=True), sink_ref[h]) for h in heads}
        e = {h: jnp.exp(s[h] - mx[h]) for h in heads}
        yield
        denom = {h: jnp.sum(e[h], axis=-1, keepdims=True) + jnp.exp(sink_ref[h] - mx[h]) for h in heads}
        pv = {h: _dot(e[h].astype(BF16), vh) for h in heads}
        yield
        for h0 in range(g0, g0 + ATTN_BATCH, 2):
            pair = (jnp.concatenate([pv[h0], pv[h0 + 1]], axis=1)
                    * jnp.where(lo_lane, 1.0 / denom[h0], 1.0 / denom[h0 + 1]))
            store(h0, pair)
        yield


def _attn_merge_kernel(sink_ref, q_ref, kvc_ref, kvp_ref,
                       yr_ref, gate_ref, x_ref, mod_ref, wbr_ref, wba_ref, wout_ref, g2_ref,
                       x1_ref, h2_ref, bias_s, ya_s, *, blocks_per_seq, n_heads, n_tiles):
    tm, d = x_ref.shape
    s = pl.program_id(0)
    first_block = (jnp.minimum(s, n_tiles - 1) * (tm // WINDOW)) % blocks_per_seq == 0

    @pl.when(s == 0)
    def _():
        ya_s[...] = jnp.zeros_like(ya_s)
        qi = lax.broadcasted_iota(jnp.int32, (WINDOW, 2 * WINDOW), 0)
        kj = lax.broadcasted_iota(jnp.int32, (WINDOW, 2 * WINDOW), 1)
        dist = qi + WINDOW - kj
        valid = (dist >= 0) & (dist < WINDOW)
        for first in (0, 1):
            neg_dist = jnp.where(valid & (kj >= WINDOW) if first == 0 else valid, -dist.astype(F32), -jnp.inf)
            for h in range(n_heads):
                bias_s[first * n_heads + h] = (2.0 ** (-8.0 * (h + 1) / n_heads)) * neg_dist

    def attention():
        kv = jnp.concatenate([kvp_ref[...], kvc_ref[...]], axis=0).astype(BF16)
        q = (q_ref[...] * (HEAD ** -0.5)).astype(BF16)
        for blk in range(tm // WINDOW):
            rows = slice(blk * WINDOW, (blk + 1) * WINDOW)
            table = jnp.where(first_block, 0, 1) * n_heads if blk == 0 else n_heads

            def store(h0, pair, rows=rows):
                ya_s[s % 2, rows, h0 * HEAD:(h0 + 2) * HEAD] = pair.astype(ya_s.dtype)

            yield from _attn_stages(sink_ref, q[rows], kv[blk * WINDOW:(blk + 2) * WINDOW],
                                    lambda h, table=table: bias_s[table + h], store, n_heads)

    def merge():
        yr, ya = yr_ref[...], ya_s[(s + 1) % 2]
        merged = []
        for c0 in range(0, d, MXU_N):
            cols, cols_a = slice(c0, c0 + MXU_N), slice(d + c0, d + c0 + MXU_N)
            merged.append((jax.nn.sigmoid(gate_ref[:, cols]) * _dot(yr, wbr_ref[:, cols])
                           + jax.nn.sigmoid(gate_ref[:, cols_a]) * _dot(ya, wba_ref[:, cols])).astype(BF16))
            yield
        merged = jnp.concatenate(merged, axis=1)
        for c0 in range(0, d, MXU_N):
            cols = slice(c0, c0 + MXU_N)
            x1_ref[:, cols] = x_ref[:, cols] + mod_ref[0, 2:3, cols] * _dot(merged, wout_ref[:, cols])
            yield
        h2 = _rms_norm(x1_ref[...], g2_ref[...]) * (1.0 + mod_ref[0, 4:5, :]) + mod_ref[0, 3:4, :]
        h2_ref[...] = h2.astype(BF16)
        yield

    _interleave(merge(), attention())


def _attn_merge(yr, p, sinks, x2, mod, wbr, wba, wout, gain2, rows_per_batch, offs, tm):
    m, d = x2.shape
    c = yr.shape[1]
    n_heads = sinks.shape[0]
    qc = n_heads * HEAD
    kvc = 2 * (n_heads // ATTN_GROUP) * HEAD
    n_tiles = m // tm
    tpb = rows_per_batch // tm
    wpt = tm // WINDOW
    att = lambda s: jnp.minimum(s, n_tiles - 1)
    mrg = lambda s: jnp.maximum(s - 1, 0)
    const = lambda shape: pl.BlockSpec(shape, lambda s: (0, 0), pipeline_mode=pl.Buffered(1))
    return pl.pallas_call(
        functools.partial(_attn_merge_kernel, blocks_per_seq=rows_per_batch // WINDOW, n_heads=n_heads,
                          n_tiles=n_tiles),
        grid=(n_tiles + 1,),
        in_specs=[pl.BlockSpec(memory_space=pltpu.SMEM),
                  pl.BlockSpec((tm, qc), lambda s: (att(s), offs["q"] // qc)),
                  pl.BlockSpec((tm, kvc), lambda s: (att(s), offs["kv"] // kvc)),
                  pl.BlockSpec((WINDOW, kvc), lambda s: (jnp.maximum(att(s) * wpt - 1, 0), offs["kv"] // kvc)),
                  pl.BlockSpec((tm, c), lambda s: (mrg(s), 0)),
                  pl.BlockSpec((tm, 2 * d), lambda s: (mrg(s), 0)),
                  pl.BlockSpec((tm, d), lambda s: (mrg(s), 0)),
                  pl.BlockSpec((1, 6, d), lambda s: (mrg(s) // tpb, 0, 0)),
                  const((c, d)), const((c, d)), const((d, d)),
                  pl.BlockSpec((1, d), lambda s: (0, 0))],
        out_specs=[pl.BlockSpec((tm, d), lambda s: (mrg(s), 0)),
                   pl.BlockSpec((tm, d), lambda s: (mrg(s), 0))],
        out_shape=[jax.ShapeDtypeStruct((m, d), F32), jax.ShapeDtypeStruct((m, d), BF16)],
        scratch_shapes=[pltpu.VMEM((2 * n_heads, WINDOW, 2 * WINDOW), F32),
                        pltpu.VMEM((2, tm, qc), BF16)],
        compiler_params=_cparams(("arbitrary",)),
        name="attn_merge",
    )(sinks, p, p, p, yr, p, x2, mod, wbr, wba, wout, gain2)


def _mlp_kernel(h_ref, wu_ref, wd_ref, x1_ref, mod_ref, fg_ref, o_ref, acc_ref):
    f = pl.program_id(1)

    @pl.when(f == 0)
    def _():
        acc_ref[...] = jnp.zeros_like(acc_ref)

    u = _dot(h_ref[...], wu_ref[...])
    act = jnp.square(jnp.maximum(u, 0.0)).astype(BF16)
    acc_ref[...] += _dot(act, wd_ref[...])

    @pl.when(f == pl.num_programs(1) - 1)
    def _():
        x2 = x1_ref[...] + mod_ref[0, 5:6, :] * acc_ref[...]
        o_ref[...] = _rms_norm(x2, fg_ref[...])


def _mlp(h2, wu, wd, x1, mod, final_gain, rows_per_batch, tm, tf):
    m, d = h2.shape
    f = wu.shape[1]
    tpb = rows_per_batch // tm
    return pl.pallas_call(
        _mlp_kernel,
        grid=(m // tm, f // tf),
        in_specs=[pl.BlockSpec((tm, d), lambda i, j: (i, 0)),
                  pl.BlockSpec((d, tf), lambda i, j: (0, j)),
                  pl.BlockSpec((tf, d), lambda i, j: (j, 0)),
                  pl.BlockSpec((tm, d), lambda i, j: (i, 0)),
                  pl.BlockSpec((1, 6, d), lambda i, j: (i // tpb, 0, 0)),
                  pl.BlockSpec((1, d), lambda i, j: (0, 0))],
        out_specs=pl.BlockSpec((tm, d), lambda i, j: (i, 0)),
        out_shape=jax.ShapeDtypeStruct((m, d), F32),
        scratch_shapes=[pltpu.VMEM((tm, d), F32)],
        compiler_params=_cparams(("arbitrary", "arbitrary")),
        name="mlp",
    )(h2, wu, wd, x1, mod, final_gain)


def _forward(x, c, w_ada, b_ada, norm1_gain, w_in, b_in, rwkv_mix, rwkv_w0, rwkv_w2, rwkv_a0,
             rwkv_a2, rwkv_g2, rwkv_k_k, rwkv_k_a, rwkv_r_k, rwkv_ln_w, rwkv_ln_b, attn_sinks,
             w_branch_rwkv, w_branch_attn, w_out, norm2_gain, w_up, w_down, final_gain):
    batch, seq, d = x.shape
    depth = w_ada.shape[0]
    c_rwkv = rwkv_w0.shape[1]
    n_heads = attn_sinks.shape[1]
    qc = n_heads * HEAD
    kvc = 2 * (n_heads // ATTN_GROUP) * HEAD
    lora_w = DECAY_LORA + ICLR_LORA + GATE_LORA
    rwkv_cols = 3 * c_rwkv + lora_w
    attn_cols = qc + kvc
    offs = {"gate": 0, "r": 2 * d, "k": 2 * d + c_rwkv, "v": 2 * d + 2 * c_rwkv, "q": 2 * d + 3 * c_rwkv}
    offs["lora"] = offs["q"] + qc
    offs["kv"] = offs["lora"] + LORA_PAD
    n_pack = offs["kv"] + kvc

    segments = ((rwkv_cols + attn_cols, 2 * d, offs["gate"]), (0, 3 * c_rwkv, offs["r"]),
                (rwkv_cols, qc, offs["q"]), (3 * c_rwkv, lora_w, offs["lora"]),
                (rwkv_cols + qc, kvc, offs["kv"]))
    zero_rows = (offs["lora"] + lora_w, offs["kv"])

    def pack_cols(t):
        out = jnp.zeros(t.shape[:-1] + (n_pack,), t.dtype)
        for src, width, dst in segments:
            out = out.at[..., dst:dst + width].set(t[..., src:src + width])
        return out

    tm_in = min(1024, seq)
    tn_in = 1280 if n_pack % 1280 == 0 else LANES
    tb = min(512, seq)
    tm_merge = min(256, seq)
    tm_mlp = min(512, seq)
    tf = min(1024, w_up.shape[2])

    assert depth == 1, "single-layer block"
    layer = lambda t: t.reshape(t.shape[1:])
    x2 = x.reshape(batch * seq, d)
    for l in range(depth):
        mod = _ada(c, layer(w_ada), b_ada[l]).reshape(batch, 6, d)
        w_pack = _pack_weight_t(layer(w_in).T, segments, zero_rows, n_pack)
        b_pack = pack_cols(b_in[l]).reshape(1, n_pack)
        p = _inproj(x2, mod, norm1_gain[l].reshape(1, d), w_pack, b_pack, seq, tm_in, tn_in)

        mix = rwkv_mix[l]
        chan = jnp.stack([mix[:c_rwkv], mix[c_rwkv:2 * c_rwkv], mix[2 * c_rwkv:3 * c_rwkv],
                          rwkv_w0[l], rwkv_a0[l], rwkv_k_k[l], rwkv_k_a[l], rwkv_r_k[l].reshape(c_rwkv),
                          rwkv_ln_w[l], rwkv_ln_b[l]])
        chan = jnp.pad(chan, ((0, 16 - chan.shape[0]), (0, 0)))
        mix_lora = jnp.pad(mix[3 * c_rwkv:], (0, LORA_PAD - lora_w)).reshape(1, LORA_PAD)
        g2_pad = jnp.pad(rwkv_g2[l], ((0, LORA_PAD - lora_w), (0, 0)))
        y_rwkv, (wbr, wba, wout, wup, wdown) = _rwkv(
            p, chan, mix_lora, rwkv_w2[l], rwkv_a2[l], g2_pad, batch, seq, offs, tb,
            cast=[layer(w_branch_rwkv), layer(w_branch_attn), layer(w_out), layer(w_up), layer(w_down)])
        x2, h2 = _attn_merge(y_rwkv, p, attn_sinks[l], x2, mod, wbr, wba, wout,
                             norm2_gain[l].reshape(1, d), seq, offs, tm_merge)
        x2 = _mlp(h2, wup, wdown, x2, mod, final_gain.reshape(1, d), seq, tm_mlp, tf)
    return x2.reshape(batch, seq, d)


def kernel(x, c, w_ada, b_ada, norm1_gain, w_in, b_in, rwkv_mix, rwkv_w0, rwkv_w2, rwkv_a0, rwkv_a2, rwkv_g2, rwkv_k_k, rwkv_k_a, rwkv_r_k, rwkv_ln_w, rwkv_ln_b, attn_sinks, w_branch_rwkv, w_branch_attn, w_out, norm2_gain, w_up, w_down, final_gain):
    return _forward(x, c, w_ada, b_ada, norm1_gain, w_in, b_in, rwkv_mix, rwkv_w0, rwkv_w2, rwkv_a0,
                    rwkv_a2, rwkv_g2, rwkv_k_k, rwkv_k_a, rwkv_r_k, rwkv_ln_w, rwkv_ln_b, attn_sinks,
                    w_branch_rwkv, w_branch_attn, w_out, norm2_gain, w_up, w_down, final_gain)
```

```python
import functools

import jax
import jax.numpy as jnp
from jax import lax
from jax.experimental import pallas as pl
from jax.experimental.pallas import tpu as pltpu

F32 = jnp.float32
BF16 = jnp.bfloat16
HIGHEST = lax.Precision.HIGHEST

LANES = 128
SUBLANES = 8
BF16_ROWS = 16
MXU_N = 256
NORM_ROWS = 128
HEAD = 64
CHUNK = 128
WINDOW = 128
ATTN_GROUP = 8
ATTN_BATCH = 8
NORM_EPS = 1e-6
GN_EPS = 64e-5
DECAY_SCALE = 0.6065306597126334
DECAY_LORA = 64
ICLR_LORA = 64
GATE_LORA = 160
LORA_PAD = 512
VMEM_LIMIT = 56 * 1024 * 1024


def _cparams(sem):
    return pltpu.CompilerParams(dimension_semantics=sem, vmem_limit_bytes=VMEM_LIMIT)


def _dot(a, b, precision=None):
    return jnp.dot(a, b, preferred_element_type=F32, precision=precision)


def _dot_nt(a, b, precision=None):
    return lax.dot_general(a, b, (((1,), (1,)), ((), ())), preferred_element_type=F32,
                           precision=precision)


def _dot_tn(a, b, precision=None):
    return lax.dot_general(a, b, (((0,), (0,)), ((), ())), preferred_element_type=F32,
                           precision=precision)


def _rms_norm(x, gain):
    ms = jnp.mean(x * x, axis=-1, keepdims=True)
    return x * lax.rsqrt(ms + NORM_EPS) * gain


def _ada_kernel(ct_ref, w_ref, b_ref, o_ref):
    ct = ct_ref[...]
    act = ct * jax.nn.sigmoid(ct)
    for m in range(o_ref.shape[0]):
        o_ref[m:m + 1, :] = jnp.sum(w_ref[...] * act[:, m:m + 1], axis=0, keepdims=True) + b_ref[...]


def _ada(c, w, b, tn=1024):
    nb, d = c.shape
    n = w.shape[1]
    return pl.pallas_call(
        _ada_kernel,
        grid=(n // tn,),
        in_specs=[pl.BlockSpec((d, nb), lambda j: (0, 0)),
                  pl.BlockSpec((d, tn), lambda j: (0, j)),
                  pl.BlockSpec((1, tn), lambda j: (0, j))],
        out_specs=pl.BlockSpec((nb, tn), lambda j: (0, j)),
        out_shape=jax.ShapeDtypeStruct((nb, n), F32),
        compiler_params=_cparams(("arbitrary",)),
        name="adaln",
    )(c.T, w, b.reshape(1, n))


def _pack_kernel(w_ref, o_ref, *, segments, zero_rows):
    for src, width, dst in segments:
        o_ref[dst:dst + width, :] = w_ref[src:src + width, :].astype(o_ref.dtype)
    lo, hi = zero_rows
    o_ref[lo:hi, :] = jnp.zeros((hi - lo, o_ref.shape[1]), o_ref.dtype)


def _pack_weight_t(w_t, segments, zero_rows, n_pack, tk=256):
    n, k = w_t.shape
    return pl.pallas_call(
        functools.partial(_pack_kernel, segments=segments, zero_rows=zero_rows),
        grid=(k // tk,),
        in_specs=[pl.BlockSpec((n, tk), lambda i: (0, i))],
        out_specs=pl.BlockSpec((n_pack, tk), lambda i: (0, i)),
        out_shape=jax.ShapeDtypeStruct((n_pack, k), BF16),
        compiler_params=_cparams(("arbitrary",)),
        name="pack_w_in",
    )(w_t)


def _inproj_kernel(x_ref, mod_ref, g_ref, w_ref, b_ref, o_ref, h_ref):
    @pl.when(pl.program_id(1) == 0)
    def _():
        scale = g_ref[...] * (1.0 + mod_ref[0, 1:2, :])
        shift = mod_ref[0, 0:1, :]

        def slab(c, carry):
            rows = pl.ds(pl.multiple_of(c * NORM_ROWS, NORM_ROWS), NORM_ROWS)
            x = x_ref[rows, :]
            ms = jnp.mean(x * x, axis=-1, keepdims=True)
            h_ref[rows, :] = (x * lax.rsqrt(ms + NORM_EPS) * scale + shift).astype(BF16)
            return carry

        lax.fori_loop(0, x_ref.shape[0] // NORM_ROWS, slab, 0)

    o_ref[...] = _dot_nt(h_ref[...], w_ref[...]) + b_ref[...]


def _inproj(x2, mod, gain, w_t, b, rows_per_batch, tm, tn):
    m, d = x2.shape
    n = w_t.shape[0]
    tpb = rows_per_batch // tm
    return pl.pallas_call(
        _inproj_kernel,
        grid=(m // tm, n // tn),
        in_specs=[pl.BlockSpec((tm, d), lambda i, j: (i, 0)),
                  pl.BlockSpec((1, 6, d), lambda i, j: (i // tpb, 0, 0)),
                  pl.BlockSpec((1, d), lambda i, j: (0, 0)),
                  pl.BlockSpec((tn, d), lambda i, j: (j, 0)),
                  pl.BlockSpec((1, tn), lambda i, j: (0, j))],
        out_specs=pl.BlockSpec((tm, tn), lambda i, j: (i, j)),
        out_shape=jax.ShapeDtypeStruct((m, n), F32),
        scratch_shapes=[pltpu.VMEM((tm, d), BF16)],
        compiler_params=_cparams(("arbitrary", "arbitrary")),
        name="inproj",
    )(x2, mod, gain, w_t, b)


def _head_sum(x):
    lo = lax.broadcasted_iota(jnp.int32, x.shape, 1) < HEAD
    s0 = jnp.sum(jnp.where(lo, x, 0.0), axis=-1, keepdims=True)
    s1 = jnp.sum(jnp.where(lo, 0.0, x), axis=-1, keepdims=True)
    return jnp.where(lo, s0, s1)


def _token_shift(x, carry_row, mix):
    rolled = pltpu.roll(x, 1, axis=0)
    head = rolled[:SUBLANES]
    first = lax.broadcasted_iota(jnp.int32, head.shape, 0) == 0
    prev = jnp.concatenate([jnp.where(first, carry_row, head), rolled[SUBLANES:]], axis=0)
    return x + (prev - x) * mix


class _Operands:
    def __init__(self):
        self._parts = {}

    def parts(self, x, n):
        ent = self._parts.setdefault(id(x), [x])
        while len(ent) - 1 < n:
            rest = x
            for piece in ent[1:]:
                rest = rest - piece.astype(F32)
            ent.append(rest.astype(BF16))
        return ent[1:n + 1]

    def mm(self, a, b, passes, dims=(((1,), (0,)), ((), ()))):
        dot = lambda u, w: lax.dot_general(u, w, dims, preferred_element_type=F32)
        if passes == 1:
            return dot(self.parts(a, 1)[0], self.parts(b, 1)[0])
        a_hi, a_lo = self.parts(a, 2)
        b_hi, b_lo = self.parts(b, 2)
        return dot(a_hi, b_hi) + (dot(a_hi, b_lo) + dot(a_lo, b_hi))


_NT = (((1,), (1,)), ((), ()))
_TN = (((0,), (0,)), ((), ()))
SCAN_PASSES = {"gram": 1, "inverse": 1, "apply": 1, "state": 1}


CHUNK_FIELDS = ("ra", "aa", "bi", "ki", "bh", "kh", "vsw")


def _decay_chunk(ops, rc, lwc, kc, vsw, ac, bc):
    incl = (lax.broadcasted_iota(jnp.int32, (CHUNK, CHUNK), 0)
            >= lax.broadcasted_iota(jnp.int32, (CHUNK, CHUNK), 1))
    tri = jnp.where(incl, 1.0, 0.0).astype(BF16)
    lw = sum(jnp.dot(tri, piece, preferred_element_type=F32) for piece in ops.parts(lwc, 2))
    lw_end = lw[CHUNK - 1:CHUNK, :]
    e_in = jnp.exp(lw)
    e_ex = jnp.exp(lw - lwc)
    e_inv = jnp.exp(-lw)
    e_end = jnp.exp(lw_end - lw)
    return dict(vsw=vsw, g_end=jnp.exp(lw_end), ra=rc * e_in, aa=ac * e_ex,
                bi=bc * e_inv, ki=kc * e_inv, bh=bc * e_end, kh=kc * e_end)


def _scan_matrices(ops, chunks):
    row = lax.broadcasted_iota(jnp.int32, (CHUNK, CHUNK), 0)
    col = lax.broadcasted_iota(jnp.int32, (CHUNK, CHUNK), 1)
    incl = row >= col
    strict = row > col
    eye = row == col
    lo_lane = col < HEAD
    same_half = lo_lane == (row < HEAD)
    lo_lane2 = lax.broadcasted_iota(jnp.int32, (2 * CHUNK, LANES), 1) < HEAD
    level_masks = []
    n = 1
    while n < CHUNK:
        level_masks.append((row // (2 * n) == col // (2 * n)) & ((row // n) % 2 == 1) & ((col // n) % 2 == 0))
        n *= 2

    probs = []
    for ch in chunks:
        lhs_all = jnp.concatenate([ch["aa"], ch["ra"]], axis=0)
        rhs_all = jnp.concatenate([ch["bi"], ch["ki"]], axis=0)
        for h in range(LANES // HEAD):
            own = lo_lane if h == 0 else ~lo_lane
            own2 = lo_lane2 if h == 0 else ~lo_lane2
            gram = ops.mm(jnp.where(own2, lhs_all, 0.0), rhs_all, SCAN_PASSES["gram"], _NT)
            probs.append(dict(ch=ch, own=own,
                              m_ab=jnp.where(strict, gram[:CHUNK, :CHUNK], 0.0),
                              m_ak=jnp.where(strict, gram[:CHUNK, CHUNK:], 0.0),
                              m_rb=jnp.where(incl, gram[CHUNK:, :CHUNK], 0.0),
                              m_rk=jnp.where(incl, gram[CHUNK:, CHUNK:], 0.0)))
    yield

    for p in probs:
        p["t"] = jnp.where(eye, 1.0, 0.0) + jnp.where(level_masks[0], p["m_ab"], 0.0)
    n = 2
    for mask in level_masks[1:]:
        if n < SUBLANES:
            for p in probs:
                p["to"] = ops.mm(p["t"], jnp.where(mask, p["m_ab"], 0.0), SCAN_PASSES["inverse"])
            for p in probs:
                p["t"] = p["t"] + ops.mm(p["to"], p["t"], SCAN_PASSES["inverse"])
        else:
            lower = lambda t, n=n: jnp.concatenate(
                [t[i:i + n] for i in range(n, CHUNK, 2 * n)], axis=0)
            for p in probs:
                p["tl"] = lower(p["t"])
                p["to"] = ops.mm(p["tl"], jnp.where(mask, p["m_ab"], 0.0), SCAN_PASSES["inverse"])
            for p in probs:
                tl = p["tl"] + ops.mm(p["to"], p["t"], SCAN_PASSES["inverse"])
                slabs = []
                for j, i in enumerate(range(0, CHUNK, 2 * n)):
                    slabs += [p["t"][i:i + n], tl[j * n:(j + 1) * n]]
                p["t"] = jnp.concatenate(slabs, axis=0)
        n *= 2
        yield

    for p in probs:
        p["vm"] = jnp.where(p["own"], 0.0, p["ch"]["vsw"])
        p["x"] = jnp.where(p["own"], p["ch"]["aa"], 0.0) + ops.mm(p["m_ak"], p["vm"], SCAN_PASSES["apply"])
    yield
    for p in probs:
        p["tx"] = ops.mm(p["t"], p["x"], SCAN_PASSES["apply"])
    yield
    for p in probs:
        p["y"] = jnp.where(p["own"], p["ch"]["ra"], 0.0) + ops.mm(
            jnp.concatenate([p["m_rb"], p["m_rk"]], axis=1),
            jnp.concatenate([p["tx"], p["vm"]], axis=0), SCAN_PASSES["apply"])
    yield

    for i, ch in enumerate(chunks):
        p0, p1 = probs[2 * i], probs[2 * i + 1]
        pq = ops.mm(jnp.concatenate([ch["bh"], ch["kh"]], axis=0),
                    jnp.concatenate([jnp.concatenate([p0["tx"], p1["tx"]], axis=1),
                                     jnp.concatenate([p0["vm"], p1["vm"]], axis=1)], axis=0),
                    SCAN_PASSES["apply"], _TN)
        pq = jnp.where(row < HEAD, pq[:, :LANES], pq[:, LANES:])
        ch["p"] = jnp.where(same_half, pq, 0.0) + jnp.where(eye, ch["g_end"], 0.0)
        ch["q"] = jnp.where(same_half, 0.0, pq)
        ch["r"] = jnp.where(lo_lane, p0["y"], p1["y"])
        ch["y0"] = jnp.where(lo_lane, p1["y"], p0["y"])


def _interleave(main, *side):
    for _ in main:
        for gen in side:
            next(gen, None)
    for gen in side:
        for _ in gen:
            pass


def _rwkv_kernel(*refs, n_pairs, n_tblocks, n_tiles, cast_steps):
    n_cast = len(cast_steps)
    (r_ref, k_ref, v_ref, lora_ref, chan_ref, chanf_ref, mixl_ref, w2_ref, a2_ref, g2_ref) = refs[:10]
    cast_in, o_ref, cast_out = refs[10:10 + n_cast], refs[10 + n_cast], refs[11 + n_cast:11 + 2 * n_cast]
    (crkv_ref, clora_ref, z_ref, twd_s, ad_s, sgd_s, r3_s, k3_s, v3_s, g3_s,
     y_s, gend_s, p_s, q_s, rr_s, y0_s, *decay_s) = refs[11 + 2 * n_cast:]
    for w_ref, wb_ref, steps in zip(cast_in, cast_out, cast_steps):
        @pl.when(pl.program_id(0) < steps)
        def _(w_ref=w_ref, wb_ref=wb_ref):
            wb_ref[...] = w_ref[...].astype(wb_ref.dtype)

    tb = r_ref.shape[0]
    n_chunks = tb // CHUNK
    s = pl.program_id(0)
    sa = jnp.minimum(s, n_tiles - 1)
    sf = jnp.maximum(s - 2, 0)
    hp, t_a = sa % n_pairs, (sa // n_pairs) % n_tblocks
    hpf, t_f = sf % n_pairs, (sf // n_pairs) % n_tblocks
    slot_w, slot_r = s % 2, (s + 1) % 2
    keep_w, keep_r = s % 3, (s + 1) % 3

    @pl.when(s == 0)
    def _():
        for ref in (crkv_ref, clora_ref, z_ref, r3_s, k3_s, v3_s, g3_s, gend_s, p_s, q_s, rr_s, y0_s,
                    *decay_s):
            ref[...] = jnp.zeros_like(ref)

    @pl.when(hp == 0)
    def _():
        lora_in = lora_ref[...]
        carry = jnp.where(t_a == 0, 0.0, clora_ref[...])
        lora = _token_shift(lora_in, carry, mixl_ref[...])
        clora_ref[...] = lora_in[tb - 1:tb, :]
        twd = jnp.tanh(lora[:, 0:DECAY_LORA])
        twd_hi = twd.astype(BF16)
        twd_s[0] = twd_hi
        twd_s[1] = (twd - twd_hi.astype(F32)).astype(BF16)
        ad_s[...] = lora[:, DECAY_LORA:DECAY_LORA + ICLR_LORA].astype(BF16)
        sgd_s[...] = jax.nn.sigmoid(lora[:, DECAY_LORA + ICLR_LORA:]).astype(BF16)

    ops = _Operands()

    def finish():
        chanf = chanf_ref[...]
        z = jnp.where(t_f == 0, 0.0, z_ref[hpf])
        for c in range(n_chunks):
            i = slot_r * n_chunks + c
            rz = ops.mm(jnp.concatenate([rr_s[i], p_s[i]], axis=0), z, SCAN_PASSES["state"])
            y_s[c * CHUNK:(c + 1) * CHUNK, :] = rz[:CHUNK] + y0_s[i]
            z = rz[CHUNK:] + q_s[i]
            yield
        z_ref[hpf] = z
        r_k, ln_w, ln_b = chanf[7:8], chanf[8:9], chanf[9:10]
        y = pltpu.roll(y_s[...], HEAD, axis=1)
        mu = _head_sum(y) * (1.0 / HEAD)
        yc = y - mu
        var = _head_sum(yc * yc) * (1.0 / HEAD)
        yn = yc * lax.rsqrt(var + GN_EPS) * ln_w + ln_b
        bonus = _head_sum(r3_s[keep_r] * k3_s[keep_r] * r_k) * v3_s[keep_r]
        o_ref[...] = ((yn + bonus) * g3_s[keep_r]).astype(o_ref.dtype)
        yield

    def matrices():
        chunks = []
        for c in range(n_chunks):
            i = slot_r * n_chunks + c
            ch = {name: ref[i] for name, ref in zip(CHUNK_FIELDS, decay_s)}
            ch["g_end"] = gend_s[i, 0:1, :]
            chunks.append(ch)
        yield from _scan_matrices(ops, chunks)
        for c, ch in enumerate(chunks):
            i = slot_w * n_chunks + c
            p_s[i], q_s[i], rr_s[i], y0_s[i] = ch["p"], ch["q"], ch["r"], ch["y0"]
        yield

    def prepare():
        chan = chan_ref[...]
        mix_r, mix_k, mix_v = chan[0:1], chan[1:2], chan[2:3]
        w0, a0, k_k, k_a = chan[3:4], chan[4:5], chan[5:6], chan[6:7]
        r_in, k_in, v_in = r_ref[...], k_ref[...], v_ref[...]
        carry = jnp.where(t_a == 0, 0.0, crkv_ref[hp])
        r = _token_shift(r_in, carry[0:1], mix_r)
        k = _token_shift(k_in, carry[1:2], mix_k)
        v = _token_shift(v_in, carry[2:3], mix_v)
        crkv_ref[hp, 0:1, :] = r_in[tb - 1:tb, :]
        crkv_ref[hp, 1:2, :] = k_in[tb - 1:tb, :]
        crkv_ref[hp, 2:3, :] = v_in[tb - 1:tb, :]
        yield
        w2_hi = w2_ref[...].astype(BF16)
        w2_lo = (w2_ref[...] - w2_hi.astype(F32)).astype(BF16)
        twd_hi, twd_lo = twd_s[0], twd_s[1]
        dw = w0 + (_dot(twd_hi, w2_hi) + (_dot(twd_hi, w2_lo) + _dot(twd_lo, w2_hi)))
        logw = -DECAY_SCALE * jax.nn.sigmoid(dw)
        alr = jax.nn.sigmoid(a0 + _dot(ad_s[...], a2_ref[...].astype(BF16)))
        g3_s[keep_w] = _dot(sgd_s[...], g2_ref[...].astype(BF16))
        yield
        kk = k * k_k
        kk = kk * lax.rsqrt(jnp.maximum(_head_sum(kk * kk), 1e-24))
        k_mod = k * (1.0 + (alr - 1.0) * k_a)
        vsw = pltpu.roll(v, HEAD, axis=1)
        a, b = -kk, kk * alr
        r3_s[keep_w], k3_s[keep_w], v3_s[keep_w] = r, k_mod, v
        yield
        for c in range(n_chunks):
            sl = slice(c * CHUNK, (c + 1) * CHUNK)
            ch = _decay_chunk(ops, r[sl], logw[sl], k_mod[sl], vsw[sl], a[sl], b[sl])
            i = slot_w * n_chunks + c
            for name, ref in zip(CHUNK_FIELDS, decay_s):
                ref[i] = ch[name]
            gend_s[i] = jnp.broadcast_to(ch["g_end"], (SUBLANES, LANES))
            yield

    _interleave(matrices(), finish(), prepare())


def _rwkv(p, chan, mix_lora, w2, a2, g2, batch, seq, offs, tb, cast):
    m = p.shape[0]
    c = w2.shape[1]
    n_pairs = c // LANES
    n_tblocks = seq // tb
    n_tiles = batch * n_tblocks * n_pairs
    n_chunks = tb // CHUNK
    gate_w = LORA_PAD - DECAY_LORA - ICLR_LORA
    prep = lambda s: jnp.minimum(s, n_tiles - 1)
    fin = lambda s: jnp.maximum(s - 2, 0)
    col = lambda name: lambda s: (prep(s) // n_pairs, offs[name] // LANES + prep(s) % n_pairs)
    weight = lambda s: (0, prep(s) % n_pairs)
    tile = lambda: pltpu.VMEM((tb, LANES), F32)
    tile3 = lambda: pltpu.VMEM((3, tb, LANES), F32)
    mats = lambda: pltpu.VMEM((2 * n_chunks, CHUNK, LANES), F32)
    cast_steps, cast_specs = [], []
    for w in cast:
        steps = min(n_tiles, w.shape[0] // BF16_ROWS)
        assert w.shape[0] % (steps * BF16_ROWS) == 0, w.shape
        cast_steps.append(steps)
        cast_specs.append(pl.BlockSpec((w.shape[0] // steps, w.shape[1]),
                                       lambda s, steps=steps: (jnp.minimum(s, steps - 1), 0)))
    outs = pl.pallas_call(
        functools.partial(_rwkv_kernel, n_pairs=n_pairs, n_tblocks=n_tblocks, n_tiles=n_tiles,
                          cast_steps=tuple(cast_steps)),
        grid=(n_tiles + 2,),
        in_specs=[pl.BlockSpec((tb, LANES), col("r")),
                  pl.BlockSpec((tb, LANES), col("k")),
                  pl.BlockSpec((tb, LANES), col("v")),
                  pl.BlockSpec((tb, LORA_PAD), lambda s: (prep(s) // n_pairs, offs["lora"] // LORA_PAD)),
                  pl.BlockSpec((16, LANES), weight),
                  pl.BlockSpec((16, LANES), lambda s: (0, fin(s) % n_pairs)),
                  pl.BlockSpec((1, LORA_PAD), lambda s: (0, 0)),
                  pl.BlockSpec((DECAY_LORA, LANES), weight),
                  pl.BlockSpec((ICLR_LORA, LANES), weight),
                  pl.BlockSpec((gate_w, LANES), weight)] + cast_specs,
        out_specs=[pl.BlockSpec((tb, LANES), lambda s: (fin(s) // n_pairs, fin(s) % n_pairs))] + cast_specs,
        out_shape=[jax.ShapeDtypeStruct((m, c), BF16)]
                  + [jax.ShapeDtypeStruct(w.shape, BF16) for w in cast],
        scratch_shapes=[pltpu.VMEM((n_pairs, SUBLANES, LANES), F32),
                        pltpu.VMEM((1, LORA_PAD), F32),
                        pltpu.VMEM((n_pairs, CHUNK, LANES), F32),
                        pltpu.VMEM((2, tb, DECAY_LORA), BF16),
                        pltpu.VMEM((tb, ICLR_LORA), BF16),
                        pltpu.VMEM((tb, gate_w), BF16),
                        tile3(), tile3(), tile3(), tile3(),
                        tile(),
                        pltpu.VMEM((2 * n_chunks, SUBLANES, LANES), F32),
                        mats(), mats(), mats(), mats()]
                       + [mats() for _ in CHUNK_FIELDS],
        compiler_params=_cparams(("arbitrary",)),
        name="rwkv7",
    )(p, p, p, p, chan, chan, mix_lora, w2, a2, g2, *cast)
    return outs[0], outs[1:]


def _attn_stages(sink_ref, q, kv, bias, store, n_heads):
    n_kv = n_heads // ATTN_GROUP
    lo_lane = lax.broadcasted_iota(jnp.int32, (WINDOW, LANES), 1) < HEAD
    for g0 in range(0, n_heads, ATTN_BATCH):
        heads = range(g0, g0 + ATTN_BATCH)
        hk = g0 // ATTN_GROUP
        kh = kv[:, hk * HEAD:(hk + 1) * HEAD]
        vh = kv[:, (n_kv + hk) * HEAD:(n_kv + hk + 1) * HEAD]
        s = {h: _dot_nt(q[:, h * HEAD:(h + 1) * HEAD], kh) + bias(h) for h in heads}
        yield
        mx = {h: jnp.maximum(jnp.max(s[h], axis=-1, keepdims=True), sink_ref[h]) for h in heads}
        e = {h: jnp.exp(s[h] - mx[h]) for h in heads}
        yield
        denom = {h: jnp.sum(e[h], axis=-1, keepdims=True) + jnp.exp(sink_ref[h] - mx[h]) for h in heads}
        pv = {h: _dot(e[h].astype(BF16), vh) for h in heads}
        yield
        for h0 in range(g0, g0 + ATTN_BATCH, 2):
            pair = (jnp.concatenate([pv[h0], pv[h0 + 1]], axis=1)
                    * jnp.where(lo_lane, 1.0 / denom[h0], 1.0 / denom[h0 + 1]))
            store(h0, pair)
        yield


def _attn_merge_kernel(sink_ref, q_ref, kvc_ref, kvp_ref,
                       yr_ref, gate_ref, x_ref, mod_ref, wbr_ref, wba_ref, wout_ref, g2_ref,
                       x1_ref, h2_ref, bias_s, ya_s, *, blocks_per_seq, n_heads, n_tiles):
    tm, d = x_ref.shape
    s = pl.program_id(0)
    first_block = (jnp.minimum(s, n_tiles - 1) * (tm // WINDOW)) % blocks_per_seq == 0

    @pl.when(s == 0)
    def _():
        ya_s[...] = jnp.zeros_like(ya_s)
        qi = lax.broadcasted_iota(jnp.int32, (WINDOW, 2 * WINDOW), 0)
        kj = lax.broadcasted_iota(jnp.int32, (WINDOW, 2 * WINDOW), 1)
        dist = qi + WINDOW - kj
        valid = (dist >= 0) & (dist < WINDOW)
        for first in (0, 1):
            neg_dist = jnp.where(valid & (kj >= WINDOW) if first == 0 else valid, -dist.astype(F32), -jnp.inf)
            for h in range(n_heads):
                bias_s[first * n_heads + h] = (2.0 ** (-8.0 * (h + 1) / n_heads)) * neg_dist

    def attention():
        kv = jnp.concatenate([kvp_ref[...], kvc_ref[...]], axis=0).astype(BF16)
        q = (q_ref[...] * (HEAD ** -0.5)).astype(BF16)
        for blk in range(tm // WINDOW):
            rows = slice(blk * WINDOW, (blk + 1) * WINDOW)
            table = jnp.where(first_block, 0, 1) * n_heads if blk == 0 else n_heads

            def store(h0, pair, rows=rows):
                ya_s[s % 2, rows, h0 * HEAD:(h0 + 2) * HEAD] = pair.astype(ya_s.dtype)

            yield from _attn_stages(sink_ref, q[rows], kv[blk * WINDOW:(blk + 2) * WINDOW],
                                    lambda h, table=table: bias_s[table + h], store, n_heads)

    def merge():
        yr, ya = yr_ref[...], ya_s[(s + 1) % 2]
        merged = []
        for c0 in range(0, d, MXU_N):
            cols, cols_a = slice(c0, c0 + MXU_N), slice(d + c0, d + c0 + MXU_N)
            merged.append((jax.nn.sigmoid(gate_ref[:, cols]) * _dot(yr, wbr_ref[:, cols])
                           + jax.nn.sigmoid(gate_ref[:, cols_a]) * _dot(ya, wba_ref[:, cols])).astype(BF16))
            yield
        merged = jnp.concatenate(merged, axis=1)
        for c0 in range(0, d, MXU_N):
            cols = slice(c0, c0 + MXU_N)
            x1_ref[:, cols] = x_ref[:, cols] + mod_ref[0, 2:3, cols] * _dot(merged, wout_ref[:, cols])
            yield
        h2 = _rms_norm(x1_ref[...], g2_ref[...]) * (1.0 + mod_ref[0, 4:5, :]) + mod_ref[0, 3:4, :]
        h2_ref[...] = h2.astype(BF16)
        yield

    _interleave(merge(), attention())


def _attn_merge(yr, p, sinks, x2, mod, wbr, wba, wout, gain2, rows_per_batch, offs, tm):
    m, d = x2.shape
    c = yr.shape[1]
    n_heads = sinks.shape[0]
    qc = n_heads * HEAD
    kvc = 2 * (n_heads // ATTN_GROUP) * HEAD
    n_tiles = m // tm
    tpb = rows_per_batch // tm
    wpt = tm // WINDOW
    att = lambda s: jnp.minimum(s, n_tiles - 1)
    mrg = lambda s: jnp.maximum(s - 1, 0)
    const = lambda shape: pl.BlockSpec(shape, lambda s: (0, 0), pipeline_mode=pl.Buffered(1))
    return pl.pallas_call(
        functools.partial(_attn_merge_kernel, blocks_per_seq=rows_per_batch // WINDOW, n_heads=n_heads,
                          n_tiles=n_tiles),
        grid=(n_tiles + 1,),
        in_specs=[pl.BlockSpec(memory_space=pltpu.SMEM),
                  pl.BlockSpec((tm, qc), lambda s: (att(s), offs["q"] // qc)),
                  pl.BlockSpec((tm, kvc), lambda s: (att(s), offs["kv"] // kvc)),
                  pl.BlockSpec((WINDOW, kvc), lambda s: (jnp.maximum(att(s) * wpt - 1, 0), offs["kv"] // kvc)),
                  pl.BlockSpec((tm, c), lambda s: (mrg(s), 0)),
                  pl.BlockSpec((tm, 2 * d), lambda s: (mrg(s), 0)),
                  pl.BlockSpec((tm, d), lambda s: (mrg(s), 0)),
                  pl.BlockSpec((1, 6, d), lambda s: (mrg(s) // tpb, 0, 0)),
                  const((c, d)), const((c, d)), const((d, d)),
                  pl.BlockSpec((1, d), lambda s: (0, 0))],
        out_specs=[pl.BlockSpec((tm, d), lambda s: (mrg(s), 0)),
                   pl.BlockSpec((tm, d), lambda s: (mrg(s), 0))],
        out_shape=[jax.ShapeDtypeStruct((m, d), F32), jax.ShapeDtypeStruct((m, d), BF16)],
        scratch_shapes=[pltpu.VMEM((2 * n_heads, WINDOW, 2 * WINDOW), F32),
                        pltpu.VMEM((2, tm, qc), BF16)],
        compiler_params=_cparams(("arbitrary",)),
        name="attn_merge",
    )(sinks, p, p, p, yr, p, x2, mod, wbr, wba, wout, gain2)


def _mlp_kernel(h_ref, wu_ref, wd_ref, x1p_ref, modp_ref, x1l_ref, modl_ref, fg_ref, o_ref, acc_ref):
    i, f = pl.program_id(0), pl.program_id(1)
    last_i, last_f = pl.num_programs(0) - 1, pl.num_programs(1) - 1
    cur, prev = i % 2, (i + 1) % 2

    def contribution():
        u = _dot(h_ref[...], wu_ref[...])
        act = jnp.square(jnp.maximum(u, 0.0)).astype(BF16)
        return _dot(act, wd_ref[...])

    def finish(slot, x1_ref, mod_ref):
        x2 = x1_ref[...] + mod_ref[0, 5:6, :] * acc_ref[slot]
        o_ref[...] = _rms_norm(x2, fg_ref[...])

    @pl.when((i == 0) & (f == 0))
    def _():
        acc_ref[1] = jnp.zeros(acc_ref.shape[1:], acc_ref.dtype)

    @pl.when(f == 0)
    def _():
        acc_ref[cur] = contribution()
        finish(prev, x1p_ref, modp_ref)

    @pl.when((f > 0) & ((f < last_f) | (i < last_i)))
    def _():
        acc_ref[cur] += contribution()

    @pl.when((f == last_f) & (i == last_i))
    def _():
        acc_ref[cur] += contribution()
        finish(cur, x1l_ref, modl_ref)


def _mlp(h2, wu, wd, x1, mod, final_gain, rows_per_batch, tm, tf):
    m, d = h2.shape
    f = wu.shape[1]
    tpb = rows_per_batch // tm
    n_i, n_f = m // tm, f // tf
    done = lambda i, j: jnp.where((i == n_i - 1) & (j == n_f - 1), i, jnp.maximum(i - 1, 0))
    before = lambda i, j: jnp.maximum(i - 1, 0)
    once = functools.partial(pl.BlockSpec, pipeline_mode=pl.Buffered(1))
    return pl.pallas_call(
        _mlp_kernel,
        grid=(n_i, n_f),
        in_specs=[pl.BlockSpec((tm, d), lambda i, j: (i, 0)),
                  pl.BlockSpec((d, tf), lambda i, j: (0, j)),
                  pl.BlockSpec((tf, d), lambda i, j: (j, 0)),
                  pl.BlockSpec((tm, d), lambda i, j: (before(i, j), 0)),
                  pl.BlockSpec((1, 6, d), lambda i, j: (before(i, j) // tpb, 0, 0)),
                  once((tm, d), lambda i, j: (n_i - 1, 0)),
                  once((1, 6, d), lambda i, j: ((n_i - 1) // tpb, 0, 0)),
                  pl.BlockSpec((1, d), lambda i, j: (0, 0))],
        out_specs=pl.BlockSpec((tm, d), lambda i, j: (done(i, j), 0)),
        out_shape=jax.ShapeDtypeStruct((m, d), F32),
        scratch_shapes=[pltpu.VMEM((2, tm, d), F32)],
        compiler_params=_cparams(("arbitrary", "arbitrary")),
        name="mlp",
    )(h2, wu, wd, x1, mod, x1, mod, final_gain)


def _forward(x, c, w_ada, b_ada, norm1_gain, w_in, b_in, rwkv_mix, rwkv_w0, rwkv_w2, rwkv_a0,
             rwkv_a2, rwkv_g2, rwkv_k_k, rwkv_k_a, rwkv_r_k, rwkv_ln_w, rwkv_ln_b, attn_sinks,
             w_branch_rwkv, w_branch_attn, w_out, norm2_gain, w_up, w_down, final_gain):
    batch, seq, d = x.shape
    depth = w_ada.shape[0]
    c_rwkv = rwkv_w0.shape[1]
    n_heads = attn_sinks.shape[1]
    qc = n_heads * HEAD
    kvc = 2 * (n_heads // ATTN_GROUP) * HEAD
    lora_w = DECAY_LORA + ICLR_LORA + GATE_LORA
    rwkv_cols = 3 * c_rwkv + lora_w
    attn_cols = qc + kvc
    offs = {"gate": 0, "r": 2 * d, "k": 2 * d + c_rwkv, "v": 2 * d + 2 * c_rwkv, "q": 2 * d + 3 * c_rwkv}
    offs["lora"] = offs["q"] + qc
    offs["kv"] = offs["lora"] + LORA_PAD
    n_pack = offs["kv"] + kvc

    segments = ((rwkv_cols + attn_cols, 2 * d, offs["gate"]), (0, 3 * c_rwkv, offs["r"]),
                (rwkv_cols, qc, offs["q"]), (3 * c_rwkv, lora_w, offs["lora"]),
                (rwkv_cols + qc, kvc, offs["kv"]))
    zero_rows = (offs["lora"] + lora_w, offs["kv"])

    def pack_cols(t):
        out = jnp.zeros(t.shape[:-1] + (n_pack,), t.dtype)
        for src, width, dst in segments:
            out = out.at[..., dst:dst + width].set(t[..., src:src + width])
        return out

    tm_in = min(1024, seq)
    tn_in = 1280 if n_pack % 1280 == 0 else LANES
    tb = min(512, seq)
    tm_merge = min(256, seq)
    tm_mlp = min(512, seq)
    tf = min(1024, w_up.shape[2])

    assert depth == 1, "single-layer block"
    layer = lambda t: t.reshape(t.shape[1:])
    x2 = x.reshape(batch * seq, d)
    for l in range(depth):
        mod = _ada(c, layer(w_ada), b_ada[l]).reshape(batch, 6, d)
        w_pack = _pack_weight_t(layer(w_in).T, segments, zero_rows, n_pack)
        b_pack = pack_cols(b_in[l]).reshape(1, n_pack)
        p = _inproj(x2, mod, norm1_gain[l].reshape(1, d), w_pack, b_pack, seq, tm_in, tn_in)

        mix = rwkv_mix[l]
        chan = jnp.stack([mix[:c_rwkv], mix[c_rwkv:2 * c_rwkv], mix[2 * c_rwkv:3 * c_rwkv],
                          rwkv_w0[l], rwkv_a0[l], rwkv_k_k[l], rwkv_k_a[l], rwkv_r_k[l].reshape(c_rwkv),
                          rwkv_ln_w[l], rwkv_ln_b[l]])
        chan = jnp.pad(chan, ((0, 16 - chan.shape[0]), (0, 0)))
        mix_lora = jnp.pad(mix[3 * c_rwkv:], (0, LORA_PAD - lora_w)).reshape(1, LORA_PAD)
        g2_pad = jnp.pad(rwkv_g2[l], ((0, LORA_PAD - lora_w), (0, 0)))
        y_rwkv, (wbr, wba, wout, wup, wdown) = _rwkv(
            p, chan, mix_lora, rwkv_w2[l], rwkv_a2[l], g2_pad, batch, seq, offs, tb,
            cast=[layer(w_branch_rwkv), layer(w_branch_attn), layer(w_out), layer(w_up), layer(w_down)])
        x2, h2 = _attn_merge(y_rwkv, p, attn_sinks[l], x2, mod, wbr, wba, wout,
                             norm2_gain[l].reshape(1, d), seq, offs, tm_merge)
        x2 = _mlp(h2, wup, wdown, x2, mod, final_gain.reshape(1, d), seq, tm_mlp, tf)
    return x2.reshape(batch, seq, d)


def kernel(x, c, w_ada, b_ada, norm1_gain, w_in, b_in, rwkv_mix, rwkv_w0, rwkv_w2, rwkv_a0, rwkv_a2, rwkv_g2, rwkv_k_k, rwkv_k_a, rwkv_r_k, rwkv_ln_w, rwkv_ln_b, attn_sinks, w_branch_rwkv, w_branch_attn, w_out, norm2_gain, w_up, w_down, final_gain):
    return _forward(x, c, w_ada, b_ada, norm1_gain, w_in, b_in, rwkv_mix, rwkv_w0, rwkv_w2, rwkv_a0,
                    rwkv_a2, rwkv_g2, rwkv_k_k, rwkv_k_a, rwkv_r_k, rwkv_ln_w, rwkv_ln_b, attn_sinks,
                    w_branch_rwkv, w_branch_attn, w_out, norm2_gain, w_up, w_down, final_gain)
```

```python
import functools

import jax
import jax.numpy as jnp
from jax import lax
from jax.experimental import pallas as pl
from jax.experimental.pallas import tpu as pltpu

F32 = jnp.float32
BF16 = jnp.bfloat16
HIGHEST = lax.Precision.HIGHEST

LANES = 128
SUBLANES = 8
BF16_ROWS = 16
MXU_N = 256
NORM_ROWS = 128
HEAD = 64
CHUNK = 128
WINDOW = 128
ATTN_GROUP = 8
ATTN_BATCH = 8
NORM_EPS = 1e-6
GN_EPS = 64e-5
DECAY_SCALE = 0.6065306597126334
DECAY_LORA = 64
ICLR_LORA = 64
GATE_LORA = 160
LORA_PAD = 512
VMEM_LIMIT = 56 * 1024 * 1024


def _cparams(sem):
    return pltpu.CompilerParams(dimension_semantics=sem, vmem_limit_bytes=VMEM_LIMIT)


def _dot(a, b, precision=None):
    return jnp.dot(a, b, preferred_element_type=F32, precision=precision)


def _dot_nt(a, b, precision=None):
    return lax.dot_general(a, b, (((1,), (1,)), ((), ())), preferred_element_type=F32,
                           precision=precision)


def _dot_tn(a, b, precision=None):
    return lax.dot_general(a, b, (((0,), (0,)), ((), ())), preferred_element_type=F32,
                           precision=precision)


def _rms_norm(x, gain):
    ms = jnp.mean(x * x, axis=-1, keepdims=True)
    return x * lax.rsqrt(ms + NORM_EPS) * gain


def _ada_kernel(ct_ref, w_ref, b_ref, o_ref):
    ct = ct_ref[...]
    act = ct * jax.nn.sigmoid(ct)
    for m in range(o_ref.shape[0]):
        o_ref[m:m + 1, :] = jnp.sum(w_ref[...] * act[:, m:m + 1], axis=0, keepdims=True) + b_ref[...]


def _ada(c, w, b, tn=1024):
    nb, d = c.shape
    n = w.shape[1]
    return pl.pallas_call(
        _ada_kernel,
        grid=(n // tn,),
        in_specs=[pl.BlockSpec((d, nb), lambda j: (0, 0)),
                  pl.BlockSpec((d, tn), lambda j: (0, j)),
                  pl.BlockSpec((1, tn), lambda j: (0, j))],
        out_specs=pl.BlockSpec((nb, tn), lambda j: (0, j)),
        out_shape=jax.ShapeDtypeStruct((nb, n), F32),
        compiler_params=_cparams(("arbitrary",)),
        name="adaln",
    )(c.T, w, b.reshape(1, n))


def _pack_kernel(w_ref, o_ref, *, segments, zero_rows):
    for src, width, dst in segments:
        o_ref[dst:dst + width, :] = w_ref[src:src + width, :].astype(o_ref.dtype)
    lo, hi = zero_rows
    o_ref[lo:hi, :] = jnp.zeros((hi - lo, o_ref.shape[1]), o_ref.dtype)


def _pack_weight_t(w_t, segments, zero_rows, n_pack, tk=256):
    n, k = w_t.shape
    return pl.pallas_call(
        functools.partial(_pack_kernel, segments=segments, zero_rows=zero_rows),
        grid=(k // tk,),
        in_specs=[pl.BlockSpec((n, tk), lambda i: (0, i))],
        out_specs=pl.BlockSpec((n_pack, tk), lambda i: (0, i)),
        out_shape=jax.ShapeDtypeStruct((n_pack, k), BF16),
        compiler_params=_cparams(("arbitrary",)),
        name="pack_w_in",
    )(w_t)


def _inproj_kernel(x_ref, mod_ref, g_ref, w_ref, b_ref, o_ref, h_ref):
    @pl.when(pl.program_id(1) == 0)
    def _():
        scale = g_ref[...] * (1.0 + mod_ref[0, 1:2, :])
        shift = mod_ref[0, 0:1, :]

        def slab(c, carry):
            rows = pl.ds(pl.multiple_of(c * NORM_ROWS, NORM_ROWS), NORM_ROWS)
            x = x_ref[rows, :]
            ms = jnp.mean(x * x, axis=-1, keepdims=True)
            h_ref[rows, :] = (x * lax.rsqrt(ms + NORM_EPS) * scale + shift).astype(BF16)
            return carry

        lax.fori_loop(0, x_ref.shape[0] // NORM_ROWS, slab, 0)

    o_ref[...] = _dot_nt(h_ref[...], w_ref[...]) + b_ref[...]


def _inproj(x2, mod, gain, w_t, b, rows_per_batch, tm, tn):
    m, d = x2.shape
    n = w_t.shape[0]
    tpb = rows_per_batch // tm
    return pl.pallas_call(
        _inproj_kernel,
        grid=(m // tm, n // tn),
        in_specs=[pl.BlockSpec((tm, d), lambda i, j: (i, 0)),
                  pl.BlockSpec((1, 6, d), lambda i, j: (i // tpb, 0, 0)),
                  pl.BlockSpec((1, d), lambda i, j: (0, 0)),
                  pl.BlockSpec((tn, d), lambda i, j: (j, 0)),
                  pl.BlockSpec((1, tn), lambda i, j: (0, j))],
        out_specs=pl.BlockSpec((tm, tn), lambda i, j: (i, j)),
        out_shape=jax.ShapeDtypeStruct((m, n), F32),
        scratch_shapes=[pltpu.VMEM((tm, d), BF16)],
        compiler_params=_cparams(("arbitrary", "arbitrary")),
        name="inproj",
    )(x2, mod, gain, w_t, b)


def _head_sum(x):
    lo = lax.broadcasted_iota(jnp.int32, x.shape, 1) < HEAD
    s0 = jnp.sum(jnp.where(lo, x, 0.0), axis=-1, keepdims=True)
    s1 = jnp.sum(jnp.where(lo, 0.0, x), axis=-1, keepdims=True)
    return jnp.where(lo, s0, s1)


def _token_shift(x, carry_row, mix):
    rolled = pltpu.roll(x, 1, axis=0)
    head = rolled[:SUBLANES]
    first = lax.broadcasted_iota(jnp.int32, head.shape, 0) == 0
    prev = jnp.concatenate([jnp.where(first, carry_row, head), rolled[SUBLANES:]], axis=0)
    return x + (prev - x) * mix


class _Operands:
    def __init__(self):
        self._parts = {}

    def parts(self, x, n):
        ent = self._parts.setdefault(id(x), [x])
        while len(ent) - 1 < n:
            rest = x
            for piece in ent[1:]:
                rest = rest - piece.astype(F32)
            ent.append(rest.astype(BF16))
        return ent[1:n + 1]

    def mm(self, a, b, passes, dims=(((1,), (0,)), ((), ()))):
        dot = lambda u, w: lax.dot_general(u, w, dims, preferred_element_type=F32)
        if passes == 1:
            return dot(self.parts(a, 1)[0], self.parts(b, 1)[0])
        a_hi, a_lo = self.parts(a, 2)
        b_hi, b_lo = self.parts(b, 2)
        return dot(a_hi, b_hi) + (dot(a_hi, b_lo) + dot(a_lo, b_hi))


_NT = (((1,), (1,)), ((), ()))
_TN = (((0,), (0,)), ((), ()))
SCAN_PASSES = {"gram": 1, "inverse": 1, "apply": 1, "state": 1}


CHUNK_FIELDS = ("ra", "aa", "bi", "ki", "bh", "kh", "vsw")


def _decay_chunk(ops, rc, lwc, kc, vsw, ac, bc):
    incl = (lax.broadcasted_iota(jnp.int32, (CHUNK, CHUNK), 0)
            >= lax.broadcasted_iota(jnp.int32, (CHUNK, CHUNK), 1))
    tri = jnp.where(incl, 1.0, 0.0).astype(BF16)
    lw = sum(jnp.dot(tri, piece, preferred_element_type=F32) for piece in ops.parts(lwc, 2))
    lw_end = lw[CHUNK - 1:CHUNK, :]
    e_in = jnp.exp(lw)
    e_ex = jnp.exp(lw - lwc)
    e_inv = jnp.exp(-lw)
    e_end = jnp.exp(lw_end - lw)
    return dict(vsw=vsw, g_end=jnp.exp(lw_end), ra=rc * e_in, aa=ac * e_ex,
                bi=bc * e_inv, ki=kc * e_inv, bh=bc * e_end, kh=kc * e_end)


def _scan_matrices(ops, chunks):
    row = lax.broadcasted_iota(jnp.int32, (CHUNK, CHUNK), 0)
    col = lax.broadcasted_iota(jnp.int32, (CHUNK, CHUNK), 1)
    incl = row >= col
    strict = row > col
    eye = row == col
    lo_lane = col < HEAD
    same_half = lo_lane == (row < HEAD)
    lo_lane2 = lax.broadcasted_iota(jnp.int32, (2 * CHUNK, LANES), 1) < HEAD
    level_masks = []
    n = 1
    while n < CHUNK:
        level_masks.append((row // (2 * n) == col // (2 * n)) & ((row // n) % 2 == 1) & ((col // n) % 2 == 0))
        n *= 2

    probs = []
    for ch in chunks:
        lhs_all = jnp.concatenate([ch["aa"], ch["ra"]], axis=0)
        rhs_all = jnp.concatenate([ch["bi"], ch["ki"]], axis=0)
        for h in range(LANES // HEAD):
            own = lo_lane if h == 0 else ~lo_lane
            own2 = lo_lane2 if h == 0 else ~lo_lane2
            gram = ops.mm(jnp.where(own2, lhs_all, 0.0), rhs_all, SCAN_PASSES["gram"], _NT)
            probs.append(dict(ch=ch, own=own,
                              m_ab=jnp.where(strict, gram[:CHUNK, :CHUNK], 0.0),
                              m_ak=jnp.where(strict, gram[:CHUNK, CHUNK:], 0.0),
                              m_rb=jnp.where(incl, gram[CHUNK:, :CHUNK], 0.0),
                              m_rk=jnp.where(incl, gram[CHUNK:, CHUNK:], 0.0)))
    yield

    for p in probs:
        p["t"] = jnp.where(eye, 1.0, 0.0) + jnp.where(level_masks[0], p["m_ab"], 0.0)
    n = 2
    for mask in level_masks[1:]:
        if n < SUBLANES:
            for p in probs:
                p["to"] = ops.mm(p["t"], jnp.where(mask, p["m_ab"], 0.0), SCAN_PASSES["inverse"])
            for p in probs:
                p["t"] = p["t"] + ops.mm(p["to"], p["t"], SCAN_PASSES["inverse"])
        else:
            lower = lambda t, n=n: jnp.concatenate(
                [t[i:i + n] for i in range(n, CHUNK, 2 * n)], axis=0)
            for p in probs:
                p["tl"] = lower(p["t"])
                p["to"] = ops.mm(p["tl"], jnp.where(mask, p["m_ab"], 0.0), SCAN_PASSES["inverse"])
            for p in probs:
                tl = p["tl"] + ops.mm(p["to"], p["t"], SCAN_PASSES["inverse"])
                slabs = []
                for j, i in enumerate(range(0, CHUNK, 2 * n)):
                    slabs += [p["t"][i:i + n], tl[j * n:(j + 1) * n]]
                p["t"] = jnp.concatenate(slabs, axis=0)
        n *= 2
        yield

    for p in probs:
        p["vm"] = jnp.where(p["own"], 0.0, p["ch"]["vsw"])
        p["x"] = jnp.where(p["own"], p["ch"]["aa"], 0.0) + ops.mm(p["m_ak"], p["vm"], SCAN_PASSES["apply"])
    yield
    for p in probs:
        p["tx"] = ops.mm(p["t"], p["x"], SCAN_PASSES["apply"])
    yield
    for p in probs:
        p["y"] = jnp.where(p["own"], p["ch"]["ra"], 0.0) + ops.mm(
            jnp.concatenate([p["m_rb"], p["m_rk"]], axis=1),
            jnp.concatenate([p["tx"], p["vm"]], axis=0), SCAN_PASSES["apply"])
    yield

    for i, ch in enumerate(chunks):
        p0, p1 = probs[2 * i], probs[2 * i + 1]
        pq = ops.mm(jnp.concatenate([ch["bh"], ch["kh"]], axis=0),
                    jnp.concatenate([jnp.concatenate([p0["tx"], p1["tx"]], axis=1),
                                     jnp.concatenate([p0["vm"], p1["vm"]], axis=1)], axis=0),
                    SCAN_PASSES["apply"], _TN)
        pq = jnp.where(row < HEAD, pq[:, :LANES], pq[:, LANES:])
        ch["p"] = jnp.where(same_half, pq, 0.0) + jnp.where(eye, ch["g_end"], 0.0)
        ch["q"] = jnp.where(same_half, 0.0, pq)
        ch["r"] = jnp.where(lo_lane, p0["y"], p1["y"])
        ch["y0"] = jnp.where(lo_lane, p1["y"], p0["y"])


def _interleave(main, *side):
    for _ in main:
        for gen in side:
            next(gen, None)
    for gen in side:
        for _ in gen:
            pass


def _rwkv_kernel(*refs, n_pairs, n_tblocks, n_tiles, cast_steps):
    n_cast = len(cast_steps)
    (r_ref, k_ref, v_ref, lora_ref, chan_ref, chanf_ref, mixl_ref, w2_ref, a2_ref, g2_ref) = refs[:10]
    cast_in, o_ref, cast_out = refs[10:10 + n_cast], refs[10 + n_cast], refs[11 + n_cast:11 + 2 * n_cast]
    (crkv_ref, clora_ref, z_ref, twd_s, ad_s, sgd_s, r3_s, k3_s, v3_s, g3_s,
     y_s, gend_s, p_s, q_s, rr_s, y0_s, *decay_s) = refs[11 + 2 * n_cast:]
    for w_ref, wb_ref, steps in zip(cast_in, cast_out, cast_steps):
        @pl.when(pl.program_id(0) < steps)
        def _(w_ref=w_ref, wb_ref=wb_ref):
            wb_ref[...] = w_ref[...].astype(wb_ref.dtype)

    tb = r_ref.shape[0]
    n_chunks = tb // CHUNK
    s = pl.program_id(0)
    sa = jnp.minimum(s, n_tiles - 1)
    sf = jnp.maximum(s - 2, 0)
    hp, t_a = sa % n_pairs, (sa // n_pairs) % n_tblocks
    hpf, t_f = sf % n_pairs, (sf // n_pairs) % n_tblocks
    slot_w, slot_r = s % 2, (s + 1) % 2
    keep_w, keep_r = s % 3, (s + 1) % 3

    @pl.when(s == 0)
    def _():
        for ref in (crkv_ref, clora_ref, z_ref, r3_s, k3_s, v3_s, g3_s, gend_s, p_s, q_s, rr_s, y0_s,
                    *decay_s):
            ref[...] = jnp.zeros_like(ref)

    @pl.when(hp == 0)
    def _():
        lora_in = lora_ref[...]
        carry = jnp.where(t_a == 0, 0.0, clora_ref[...])
        lora = _token_shift(lora_in, carry, mixl_ref[...])
        clora_ref[...] = lora_in[tb - 1:tb, :]
        twd = jnp.tanh(lora[:, 0:DECAY_LORA])
        twd_hi = twd.astype(BF16)
        twd_s[0] = twd_hi
        twd_s[1] = (twd - twd_hi.astype(F32)).astype(BF16)
        ad_s[...] = lora[:, DECAY_LORA:DECAY_LORA + ICLR_LORA].astype(BF16)
        sgd_s[...] = jax.nn.sigmoid(lora[:, DECAY_LORA + ICLR_LORA:]).astype(BF16)

    ops = _Operands()

    def finish():
        chanf = chanf_ref[...]
        z = jnp.where(t_f == 0, 0.0, z_ref[hpf])
        for c in range(n_chunks):
            i = slot_r * n_chunks + c
            rz = ops.mm(jnp.concatenate([rr_s[i], p_s[i]], axis=0), z, SCAN_PASSES["state"])
            y_s[c * CHUNK:(c + 1) * CHUNK, :] = rz[:CHUNK] + y0_s[i]
            z = rz[CHUNK:] + q_s[i]
            yield
        z_ref[hpf] = z
        r_k, ln_w, ln_b = chanf[7:8], chanf[8:9], chanf[9:10]
        y = pltpu.roll(y_s[...], HEAD, axis=1)
        mu = _head_sum(y) * (1.0 / HEAD)
        yc = y - mu
        var = _head_sum(yc * yc) * (1.0 / HEAD)
        yn = yc * lax.rsqrt(var + GN_EPS) * ln_w + ln_b
        bonus = _head_sum(r3_s[keep_r] * k3_s[keep_r] * r_k) * v3_s[keep_r]
        o_ref[...] = ((yn + bonus) * g3_s[keep_r]).astype(o_ref.dtype)
        yield

    def matrices():
        chunks = []
        for c in range(n_chunks):
            i = slot_r * n_chunks + c
            ch = {name: ref[i] for name, ref in zip(CHUNK_FIELDS, decay_s)}
            ch["g_end"] = gend_s[i, 0:1, :]
            chunks.append(ch)
        yield from _scan_matrices(ops, chunks)
        for c, ch in enumerate(chunks):
            i = slot_w * n_chunks + c
            p_s[i], q_s[i], rr_s[i], y0_s[i] = ch["p"], ch["q"], ch["r"], ch["y0"]
        yield

    def prepare():
        chan = chan_ref[...]
        mix_r, mix_k, mix_v = chan[0:1], chan[1:2], chan[2:3]
        w0, a0, k_k, k_a = chan[3:4], chan[4:5], chan[5:6], chan[6:7]
        r_in, k_in, v_in = r_ref[...], k_ref[...], v_ref[...]
        carry = jnp.where(t_a == 0, 0.0, crkv_ref[hp])
        r = _token_shift(r_in, carry[0:1], mix_r)
        k = _token_shift(k_in, carry[1:2], mix_k)
        v = _token_shift(v_in, carry[2:3], mix_v)
        crkv_ref[hp, 0:1, :] = r_in[tb - 1:tb, :]
        crkv_ref[hp, 1:2, :] = k_in[tb - 1:tb, :]
        crkv_ref[hp, 2:3, :] = v_in[tb - 1:tb, :]
        yield
        w2_hi = w2_ref[...].astype(BF16)
        w2_lo = (w2_ref[...] - w2_hi.astype(F32)).astype(BF16)
        twd_hi, twd_lo = twd_s[0], twd_s[1]
        dw = w0 + (_dot(twd_hi, w2_hi) + (_dot(twd_hi, w2_lo) + _dot(twd_lo, w2_hi)))
        logw = -DECAY_SCALE * jax.nn.sigmoid(dw)
        alr = jax.nn.sigmoid(a0 + _dot(ad_s[...], a2_ref[...].astype(BF16)))
        g3_s[keep_w] = _dot(sgd_s[...], g2_ref[...].astype(BF16))
        yield
        kk = k * k_k
        kk = kk * lax.rsqrt(jnp.maximum(_head_sum(kk * kk), 1e-24))
        k_mod = k * (1.0 + (alr - 1.0) * k_a)
        vsw = pltpu.roll(v, HEAD, axis=1)
        a, b = -kk, kk * alr
        r3_s[keep_w], k3_s[keep_w], v3_s[keep_w] = r, k_mod, v
        yield
        for c in range(n_chunks):
            sl = slice(c * CHUNK, (c + 1) * CHUNK)
            ch = _decay_chunk(ops, r[sl], logw[sl], k_mod[sl], vsw[sl], a[sl], b[sl])
            i = slot_w * n_chunks + c
            for name, ref in zip(CHUNK_FIELDS, decay_s):
                ref[i] = ch[name]
            gend_s[i] = jnp.broadcast_to(ch["g_end"], (SUBLANES, LANES))
            yield

    _interleave(matrices(), finish(), prepare())


def _rwkv(p, chan, mix_lora, w2, a2, g2, batch, seq, offs, tb, cast):
    m = p.shape[0]
    c = w2.shape[1]
    n_pairs = c // LANES
    n_tblocks = seq // tb
    n_tiles = batch * n_tblocks * n_pairs
    n_chunks = tb // CHUNK
    gate_w = LORA_PAD - DECAY_LORA - ICLR_LORA
    prep = lambda s: jnp.minimum(s, n_tiles - 1)
    fin = lambda s: jnp.maximum(s - 2, 0)
    col = lambda name: lambda s: (prep(s) // n_pairs, offs[name] // LANES + prep(s) % n_pairs)
    weight = lambda s: (0, prep(s) % n_pairs)
    tile = lambda: pltpu.VMEM((tb, LANES), F32)
    tile3 = lambda: pltpu.VMEM((3, tb, LANES), F32)
    mats = lambda: pltpu.VMEM((2 * n_chunks, CHUNK, LANES), F32)
    cast_steps, cast_specs = [], []
    for w in cast:
        steps = min(n_tiles, w.shape[0] // BF16_ROWS)
        assert w.shape[0] % (steps * BF16_ROWS) == 0, w.shape
        cast_steps.append(steps)
        cast_specs.append(pl.BlockSpec((w.shape[0] // steps, w.shape[1]),
                                       lambda s, steps=steps: (jnp.minimum(s, steps - 1), 0)))
    outs = pl.pallas_call(
        functools.partial(_rwkv_kernel, n_pairs=n_pairs, n_tblocks=n_tblocks, n_tiles=n_tiles,
                          cast_steps=tuple(cast_steps)),
        grid=(n_tiles + 2,),
        in_specs=[pl.BlockSpec((tb, LANES), col("r")),
                  pl.BlockSpec((tb, LANES), col("k")),
                  pl.BlockSpec((tb, LANES), col("v")),
                  pl.BlockSpec((tb, LORA_PAD), lambda s: (prep(s) // n_pairs, offs["lora"] // LORA_PAD)),
                  pl.BlockSpec((16, LANES), weight),
                  pl.BlockSpec((16, LANES), lambda s: (0, fin(s) % n_pairs)),
                  pl.BlockSpec((1, LORA_PAD), lambda s: (0, 0)),
                  pl.BlockSpec((DECAY_LORA, LANES), weight),
                  pl.BlockSpec((ICLR_LORA, LANES), weight),
                  pl.BlockSpec((gate_w, LANES), weight)] + cast_specs,
        out_specs=[pl.BlockSpec((tb, LANES), lambda s: (fin(s) // n_pairs, fin(s) % n_pairs))] + cast_specs,
        out_shape=[jax.ShapeDtypeStruct((m, c), BF16)]
                  + [jax.ShapeDtypeStruct(w.shape, BF16) for w in cast],
        scratch_shapes=[pltpu.VMEM((n_pairs, SUBLANES, LANES), F32),
                        pltpu.VMEM((1, LORA_PAD), F32),
                        pltpu.VMEM((n_pairs, CHUNK, LANES), F32),
                        pltpu.VMEM((2, tb, DECAY_LORA), BF16),
                        pltpu.VMEM((tb, ICLR_LORA), BF16),
                        pltpu.VMEM((tb, gate_w), BF16),
                        tile3(), tile3(), tile3(), tile3(),
                        tile(),
                        pltpu.VMEM((2 * n_chunks, SUBLANES, LANES), F32),
                        mats(), mats(), mats(), mats()]
                       + [mats() for _ in CHUNK_FIELDS],
        compiler_params=_cparams(("arbitrary",)),
        name="rwkv7",
    )(p, p, p, p, chan, chan, mix_lora, w2, a2, g2, *cast)
    return outs[0], outs[1:]


def _attn_stages(sink_ref, q, kv, bias, store, n_heads):
    n_kv = n_heads // ATTN_GROUP
    lo_lane = lax.broadcasted_iota(jnp.int32, (WINDOW, LANES), 1) < HEAD
    for g0 in range(0, n_heads, ATTN_BATCH):
        heads = range(g0, g0 + ATTN_BATCH)
        hk = g0 // ATTN_GROUP
        kh = kv[:, hk * HEAD:(hk + 1) * HEAD]
        vh = kv[:, (n_kv + hk) * HEAD:(n_kv + hk + 1) * HEAD]
        s = {h: _dot_nt(q[:, h * HEAD:(h + 1) * HEAD], kh) + bias(h) for h in heads}
        yield
        mx = {h: jnp.maximum(jnp.max(s[h], axis=-1, keepdims=True), sink_ref[h]) for h in heads}
        e = {h: jnp.exp(s[h] - mx[h]) for h in heads}
        yield
        denom = {h: jnp.sum(e[h], axis=-1, keepdims=True) + jnp.exp(sink_ref[h] - mx[h]) for h in heads}
        pv = {h: _dot(e[h].astype(BF16), vh) for h in heads}
        yield
        for h0 in range(g0, g0 + ATTN_BATCH, 2):
            pair = (jnp.concatenate([pv[h0], pv[h0 + 1]], axis=1)
                    * jnp.where(lo_lane, 1.0 / denom[h0], 1.0 / denom[h0 + 1]))
            store(h0, pair)
        yield


def _attn_merge_kernel(sink_ref, q_ref, kvc_ref, kvp_ref,
                       yr_ref, gate_ref, x_ref, mod_ref, wbr_ref, wba_ref, wout_ref, g2_ref,
                       x1_ref, h2_ref, bias_s, ya_s, *, blocks_per_seq, n_heads, n_tiles):
    tm, d = x_ref.shape
    s = pl.program_id(0)
    first_block = (jnp.minimum(s, n_tiles - 1) * (tm // WINDOW)) % blocks_per_seq == 0

    @pl.when(s == 0)
    def _():
        ya_s[...] = jnp.zeros_like(ya_s)
        qi = lax.broadcasted_iota(jnp.int32, (WINDOW, 2 * WINDOW), 0)
        kj = lax.broadcasted_iota(jnp.int32, (WINDOW, 2 * WINDOW), 1)
        dist = qi + WINDOW - kj
        valid = (dist >= 0) & (dist < WINDOW)
        for first in (0, 1):
            neg_dist = jnp.where(valid & (kj >= WINDOW) if first == 0 else valid, -dist.astype(F32), -jnp.inf)
            for h in range(n_heads):
                bias_s[first * n_heads + h] = (2.0 ** (-8.0 * (h + 1) / n_heads)) * neg_dist

    def attention():
        kv = jnp.concatenate([kvp_ref[...], kvc_ref[...]], axis=0).astype(BF16)
        q = (q_ref[...] * (HEAD ** -0.5)).astype(BF16)
        for blk in range(tm // WINDOW):
            rows = slice(blk * WINDOW, (blk + 1) * WINDOW)
            table = jnp.where(first_block, 0, 1) * n_heads if blk == 0 else n_heads

            def store(h0, pair, rows=rows):
                ya_s[s % 2, rows, h0 * HEAD:(h0 + 2) * HEAD] = pair.astype(ya_s.dtype)

            yield from _attn_stages(sink_ref, q[rows], kv[blk * WINDOW:(blk + 2) * WINDOW],
                                    lambda h, table=table: bias_s[table + h], store, n_heads)

    def merge():
        yr, ya = yr_ref[...], ya_s[(s + 1) % 2]
        merged = []
        for c0 in range(0, d, MXU_N):
            cols, cols_a = slice(c0, c0 + MXU_N), slice(d + c0, d + c0 + MXU_N)
            merged.append((jax.nn.sigmoid(gate_ref[:, cols]) * _dot(yr, wbr_ref[:, cols])
                           + jax.nn.sigmoid(gate_ref[:, cols_a]) * _dot(ya, wba_ref[:, cols])).astype(BF16))
            yield
        merged = jnp.concatenate(merged, axis=1)
        for c0 in range(0, d, MXU_N):
            cols = slice(c0, c0 + MXU_N)
            x1_ref[:, cols] = x_ref[:, cols] + mod_ref[0, 2:3, cols] * _dot(merged, wout_ref[:, cols])
            yield
        h2 = _rms_norm(x1_ref[...], g2_ref[...]) * (1.0 + mod_ref[0, 4:5, :]) + mod_ref[0, 3:4, :]
        h2_ref[...] = h2.astype(BF16)
        yield

    _interleave(merge(), attention())


def _attn_merge(yr, p, sinks, x2, mod, wbr, wba, wout, gain2, rows_per_batch, offs, tm):
    m, d = x2.shape
    c = yr.shape[1]
    n_heads = sinks.shape[0]
    qc = n_heads * HEAD
    kvc = 2 * (n_heads // ATTN_GROUP) * HEAD
    n_tiles = m // tm
    tpb = rows_per_batch // tm
    wpt = tm // WINDOW
    att = lambda s: jnp.minimum(s, n_tiles - 1)
    mrg = lambda s: jnp.maximum(s - 1, 0)
    const = lambda shape: pl.BlockSpec(shape, lambda s: (0, 0), pipeline_mode=pl.Buffered(1))
    return pl.pallas_call(
        functools.partial(_attn_merge_kernel, blocks_per_seq=rows_per_batch // WINDOW, n_heads=n_heads,
                          n_tiles=n_tiles),
        grid=(n_tiles + 1,),
        in_specs=[pl.BlockSpec(memory_space=pltpu.SMEM),
                  pl.BlockSpec((tm, qc), lambda s: (att(s), offs["q"] // qc)),
                  pl.BlockSpec((tm, kvc), lambda s: (att(s), offs["kv"] // kvc)),
                  pl.BlockSpec((WINDOW, kvc), lambda s: (jnp.maximum(att(s) * wpt - 1, 0), offs["kv"] // kvc)),
                  pl.BlockSpec((tm, c), lambda s: (mrg(s), 0)),
                  pl.BlockSpec((tm, 2 * d), lambda s: (mrg(s), 0)),
                  pl.BlockSpec((tm, d), lambda s: (mrg(s), 0)),
                  pl.BlockSpec((1, 6, d), lambda s: (mrg(s) // tpb, 0, 0)),
                  const((c, d)), const((c, d)), const((d, d)),
                  pl.BlockSpec((1, d), lambda s: (0, 0))],
        out_specs=[pl.BlockSpec((tm, d), lambda s: (mrg(s), 0)),
                   pl.BlockSpec((tm, d), lambda s: (mrg(s), 0))],
        out_shape=[jax.ShapeDtypeStruct((m, d), F32), jax.ShapeDtypeStruct((m, d), BF16)],
        scratch_shapes=[pltpu.VMEM((2 * n_heads, WINDOW, 2 * WINDOW), F32),
                        pltpu.VMEM((2, tm, qc), BF16)],
        compiler_params=_cparams(("arbitrary",)),
        name="attn_merge",
    )(sinks, p, p, p, yr, p, x2, mod, wbr, wba, wout, gain2)


def _mlp_kernel(h_ref, wu_ref, wd_ref, x1_ref, mod_ref, fg_ref, o_ref, acc_ref):
    f = pl.program_id(1)

    @pl.when(f == 0)
    def _():
        acc_ref[...] = jnp.zeros_like(acc_ref)

    u = _dot(h_ref[...], wu_ref[...])
    act = jnp.square(jnp.maximum(u, 0.0)).astype(BF16)
    acc_ref[...] += _dot(act, wd_ref[...])

    @pl.when(f == pl.num_programs(1) - 1)
    def _():
        x2 = x1_ref[...] + mod_ref[0, 5:6, :] * acc_ref[...]
        o_ref[...] = _rms_norm(x2, fg_ref[...])


def _mlp(h2, wu, wd, x1, mod, final_gain, rows_per_batch, tm, tf):
    m, d = h2.shape
    f = wu.shape[1]
    tpb = rows_per_batch // tm
    return pl.pallas_call(
        _mlp_kernel,
        grid=(m // tm, f // tf),
        in_specs=[pl.BlockSpec((tm, d), lambda i, j: (i, 0)),
                  pl.BlockSpec((d, tf), lambda i, j: (0, j)),
                  pl.BlockSpec((tf, d), lambda i, j: (j, 0)),
                  pl.BlockSpec((tm, d), lambda i, j: (i, 0)),
                  pl.BlockSpec((1, 6, d), lambda i, j: (i // tpb, 0, 0)),
                  pl.BlockSpec((1, d), lambda i, j: (0, 0))],
        out_specs=pl.BlockSpec((tm, d), lambda i, j: (i, 0)),
        out_shape=jax.ShapeDtypeStruct((m, d), F32),
        scratch_shapes=[pltpu.VMEM((tm, d), F32)],
        compiler_params=_cparams(("arbitrary", "arbitrary")),
        name="mlp",
    )(h2, wu, wd, x1, mod, final_gain)


def _forward(x, c, w_ada, b_ada, norm1_gain, w_in, b_in, rwkv_mix, rwkv_w0, rwkv_w2, rwkv_a0,
             rwkv_a2, rwkv_g2, rwkv_k_k, rwkv_k_a, rwkv_r_k, rwkv_ln_w, rwkv_ln_b, attn_sinks,
             w_branch_rwkv, w_branch_attn, w_out, norm2_gain, w_up, w_down, final_gain):
    batch, seq, d = x.shape
    depth = w_ada.shape[0]
    c_rwkv = rwkv_w0.shape[1]
    n_heads = attn_sinks.shape[1]
    qc = n_heads * HEAD
    kvc = 2 * (n_heads // ATTN_GROUP) * HEAD
    lora_w = DECAY_LORA + ICLR_LORA + GATE_LORA
    rwkv_cols = 3 * c_rwkv + lora_w
    attn_cols = qc + kvc
    offs = {"gate": 0, "r": 2 * d, "k": 2 * d + c_rwkv, "v": 2 * d + 2 * c_rwkv, "q": 2 * d + 3 * c_rwkv}
    offs["lora"] = offs["q"] + qc
    offs["kv"] = offs["lora"] + LORA_PAD
    n_pack = offs["kv"] + kvc

    segments = ((rwkv_cols + attn_cols, 2 * d, offs["gate"]), (0, 3 * c_rwkv, offs["r"]),
                (rwkv_cols, qc, offs["q"]), (3 * c_rwkv, lora_w, offs["lora"]),
                (rwkv_cols + qc, kvc, offs["kv"]))
    zero_rows = (offs["lora"] + lora_w, offs["kv"])

    def pack_cols(t):
        out = jnp.zeros(t.shape[:-1] + (n_pack,), t.dtype)
        for src, width, dst in segments:
            out = out.at[..., dst:dst + width].set(t[..., src:src + width])
        return out

    tm_in = min(1024, seq)
    tn_in = 1280 if n_pack % 1280 == 0 else LANES
    tb = min(1024, seq)
    tm_merge = min(256, seq)
    tm_mlp = min(512, seq)
    tf = min(1024, w_up.shape[2])

    assert depth == 1, "single-layer block"
    layer = lambda t: t.reshape(t.shape[1:])
    x2 = x.reshape(batch * seq, d)
    for l in range(depth):
        mod = _ada(c, layer(w_ada), b_ada[l]).reshape(batch, 6, d)
        w_pack = _pack_weight_t(layer(w_in).T, segments, zero_rows, n_pack)
        b_pack = pack_cols(b_in[l]).reshape(1, n_pack)
        p = _inproj(x2, mod, norm1_gain[l].reshape(1, d), w_pack, b_pack, seq, tm_in, tn_in)

        mix = rwkv_mix[l]
        chan = jnp.stack([mix[:c_rwkv], mix[c_rwkv:2 * c_rwkv], mix[2 * c_rwkv:3 * c_rwkv],
                          rwkv_w0[l], rwkv_a0[l], rwkv_k_k[l], rwkv_k_a[l], rwkv_r_k[l].reshape(c_rwkv),
                          rwkv_ln_w[l], rwkv_ln_b[l]])
        chan = jnp.pad(chan, ((0, 16 - chan.shape[0]), (0, 0)))
        mix_lora = jnp.pad(mix[3 * c_rwkv:], (0, LORA_PAD - lora_w)).reshape(1, LORA_PAD)
        g2_pad = jnp.pad(rwkv_g2[l], ((0, LORA_PAD - lora_w), (0, 0)))
        y_rwkv, (wbr, wba, wout, wup, wdown) = _rwkv(
            p, chan, mix_lora, rwkv_w2[l], rwkv_a2[l], g2_pad, batch, seq, offs, tb,
            cast=[layer(w_branch_rwkv), layer(w_branch_attn), layer(w_out), layer(w_up), layer(w_down)])
        x2, h2 = _attn_merge(y_rwkv, p, attn_sinks[l], x2, mod, wbr, wba, wout,
                             norm2_gain[l].reshape(1, d), seq, offs, tm_merge)
        x2 = _mlp(h2, wup, wdown, x2, mod, final_gain.reshape(1, d), seq, tm_mlp, tf)
    return x2.reshape(batch, seq, d)


def kernel(x, c, w_ada, b_ada, norm1_gain, w_in, b_in, rwkv_mix, rwkv_w0, rwkv_w2, rwkv_a0, rwkv_a2, rwkv_g2, rwkv_k_k, rwkv_k_a, rwkv_r_k, rwkv_ln_w, rwkv_ln_b, attn_sinks, w_branch_rwkv, w_branch_attn, w_out, norm2_gain, w_up, w_down, final_gain):
    return _forward(x, c, w_ada, b_ada, norm1_gain, w_in, b_in, rwkv_mix, rwkv_w0, rwkv_w2, rwkv_a0,
                    rwkv_a2, rwkv_g2, rwkv_k_k, rwkv_k_a, rwkv_r_k, rwkv_ln_w, rwkv_ln_b, attn_sinks,
                    w_branch_rwkv, w_branch_attn, w_out, norm2_gain, w_up, w_down, final_gain)
```

```python
import functools

import jax
import jax.numpy as jnp
from jax import lax
from jax.experimental import pallas as pl
from jax.experimental.pallas import tpu as pltpu

F32 = jnp.float32
BF16 = jnp.bfloat16
HIGHEST = lax.Precision.HIGHEST

LANES = 128
SUBLANES = 8
BF16_ROWS = 16
MXU_N = 256
NORM_ROWS = 128
HEAD = 64
CHUNK = 128
WINDOW = 128
ATTN_GROUP = 8
ATTN_BATCH = 8
NORM_EPS = 1e-6
GN_EPS = 64e-5
DECAY_SCALE = 0.6065306597126334
DECAY_LORA = 64
ICLR_LORA = 64
GATE_LORA = 160
LORA_PAD = 512
VMEM_LIMIT = 56 * 1024 * 1024


def _cparams(sem):
    return pltpu.CompilerParams(dimension_semantics=sem, vmem_limit_bytes=VMEM_LIMIT)


def _dot(a, b, precision=None):
    return jnp.dot(a, b, preferred_element_type=F32, precision=precision)


def _dot_nt(a, b, precision=None):
    return lax.dot_general(a, b, (((1,), (1,)), ((), ())), preferred_element_type=F32,
                           precision=precision)


def _dot_tn(a, b, precision=None):
    return lax.dot_general(a, b, (((0,), (0,)), ((), ())), preferred_element_type=F32,
                           precision=precision)


def _rms_norm(x, gain):
    ms = jnp.mean(x * x, axis=-1, keepdims=True)
    return x * lax.rsqrt(ms + NORM_EPS) * gain


def _ada_kernel(ct_ref, w_ref, b_ref, o_ref):
    ct = ct_ref[...]
    act = ct * jax.nn.sigmoid(ct)
    for m in range(o_ref.shape[0]):
        o_ref[m:m + 1, :] = jnp.sum(w_ref[...] * act[:, m:m + 1], axis=0, keepdims=True) + b_ref[...]


def _ada(c, w, b, tn=512):
    nb, d = c.shape
    n = w.shape[1]
    return pl.pallas_call(
        _ada_kernel,
        grid=(n // tn,),
        in_specs=[pl.BlockSpec((d, nb), lambda j: (0, 0)),
                  pl.BlockSpec((d, tn), lambda j: (0, j)),
                  pl.BlockSpec((1, tn), lambda j: (0, j))],
        out_specs=pl.BlockSpec((nb, tn), lambda j: (0, j)),
        out_shape=jax.ShapeDtypeStruct((nb, n), F32),
        compiler_params=_cparams(("arbitrary",)),
        name="adaln",
    )(c.T, w, b.reshape(1, n))


def _pack_kernel(w_ref, o_ref, *, segments, zero_rows):
    for src, width, dst in segments:
        o_ref[dst:dst + width, :] = w_ref[src:src + width, :].astype(o_ref.dtype)
    lo, hi = zero_rows
    o_ref[lo:hi, :] = jnp.zeros((hi - lo, o_ref.shape[1]), o_ref.dtype)


def _pack_weight_t(w_t, segments, zero_rows, n_pack, tk=128):
    n, k = w_t.shape
    return pl.pallas_call(
        functools.partial(_pack_kernel, segments=segments, zero_rows=zero_rows),
        grid=(k // tk,),
        in_specs=[pl.BlockSpec((n, tk), lambda i: (0, i))],
        out_specs=pl.BlockSpec((n_pack, tk), lambda i: (0, i)),
        out_shape=jax.ShapeDtypeStruct((n_pack, k), BF16),
        compiler_params=_cparams(("arbitrary",)),
        name="pack_w_in",
    )(w_t)


def _inproj_kernel(x_ref, mod_ref, g_ref, w_ref, b_ref, o_ref, h_ref):
    @pl.when(pl.program_id(1) == 0)
    def _():
        scale = g_ref[...] * (1.0 + mod_ref[0, 1:2, :])
        shift = mod_ref[0, 0:1, :]

        def slab(c, carry):
            rows = pl.ds(pl.multiple_of(c * NORM_ROWS, NORM_ROWS), NORM_ROWS)
            x = x_ref[rows, :]
            ms = jnp.mean(x * x, axis=-1, keepdims=True)
            h_ref[rows, :] = (x * lax.rsqrt(ms + NORM_EPS) * scale + shift).astype(BF16)
            return carry

        lax.fori_loop(0, x_ref.shape[0] // NORM_ROWS, slab, 0)

    o_ref[...] = _dot_nt(h_ref[...], w_ref[...]) + b_ref[...]


def _inproj(x2, mod, gain, w_t, b, rows_per_batch, tm, tn):
    m, d = x2.shape
    n = w_t.shape[0]
    tpb = rows_per_batch // tm
    return pl.pallas_call(
        _inproj_kernel,
        grid=(m // tm, n // tn),
        in_specs=[pl.BlockSpec((tm, d), lambda i, j: (i, 0)),
                  pl.BlockSpec((1, 6, d), lambda i, j: (i // tpb, 0, 0)),
                  pl.BlockSpec((1, d), lambda i, j: (0, 0)),
                  pl.BlockSpec((tn, d), lambda i, j: (j, 0)),
                  pl.BlockSpec((1, tn), lambda i, j: (0, j))],
        out_specs=pl.BlockSpec((tm, tn), lambda i, j: (i, j)),
        out_shape=jax.ShapeDtypeStruct((m, n), F32),
        scratch_shapes=[pltpu.VMEM((tm, d), BF16)],
        compiler_params=_cparams(("arbitrary", "arbitrary")),
        name="inproj",
    )(x2, mod, gain, w_t, b)


def _head_sum(x):
    lo = lax.broadcasted_iota(jnp.int32, x.shape, 1) < HEAD
    s0 = jnp.sum(jnp.where(lo, x, 0.0), axis=-1, keepdims=True)
    s1 = jnp.sum(jnp.where(lo, 0.0, x), axis=-1, keepdims=True)
    return jnp.where(lo, s0, s1)


def _token_shift(x, carry_row, mix):
    rolled = pltpu.roll(x, 1, axis=0)
    head = rolled[:SUBLANES]
    first = lax.broadcasted_iota(jnp.int32, head.shape, 0) == 0
    prev = jnp.concatenate([jnp.where(first, carry_row, head), rolled[SUBLANES:]], axis=0)
    return x + (prev - x) * mix


class _Operands:
    def __init__(self):
        self._parts = {}

    def parts(self, x, n):
        ent = self._parts.setdefault(id(x), [x])
        while len(ent) - 1 < n:
            rest = x
            for piece in ent[1:]:
                rest = rest - piece.astype(F32)
            ent.append(rest.astype(BF16))
        return ent[1:n + 1]

    def mm(self, a, b, passes, dims=(((1,), (0,)), ((), ()))):
        dot = lambda u, w: lax.dot_general(u, w, dims, preferred_element_type=F32)
        if passes == 1:
            return dot(self.parts(a, 1)[0], self.parts(b, 1)[0])
        a_hi, a_lo = self.parts(a, 2)
        b_hi, b_lo = self.parts(b, 2)
        return dot(a_hi, b_hi) + (dot(a_hi, b_lo) + dot(a_lo, b_hi))


_NT = (((1,), (1,)), ((), ()))
_TN = (((0,), (0,)), ((), ()))
SCAN_PASSES = {"gram": 1, "inverse": 1, "apply": 1, "state": 1}


CHUNK_FIELDS = ("ra", "aa", "bi", "ki", "bh", "kh", "vsw")


def _decay_chunk(ops, rc, lwc, kc, vsw, ac, bc):
    incl = (lax.broadcasted_iota(jnp.int32, (CHUNK, CHUNK), 0)
            >= lax.broadcasted_iota(jnp.int32, (CHUNK, CHUNK), 1))
    tri = jnp.where(incl, 1.0, 0.0).astype(BF16)
    lw = sum(jnp.dot(tri, piece, preferred_element_type=F32) for piece in ops.parts(lwc, 2))
    lw_end = lw[CHUNK - 1:CHUNK, :]
    e_in = jnp.exp(lw)
    e_ex = jnp.exp(lw - lwc)
    e_inv = jnp.exp(-lw)
    e_end = jnp.exp(lw_end - lw)
    return dict(vsw=vsw, g_end=jnp.exp(lw_end), ra=rc * e_in, aa=ac * e_ex,
                bi=bc * e_inv, ki=kc * e_inv, bh=bc * e_end, kh=kc * e_end)


def _scan_matrices(ops, chunks):
    row = lax.broadcasted_iota(jnp.int32, (CHUNK, CHUNK), 0)
    col = lax.broadcasted_iota(jnp.int32, (CHUNK, CHUNK), 1)
    incl = row >= col
    strict = row > col
    eye = row == col
    lo_lane = col < HEAD
    same_half = lo_lane == (row < HEAD)
    lo_lane2 = lax.broadcasted_iota(jnp.int32, (2 * CHUNK, LANES), 1) < HEAD
    level_masks = []
    n = 1
    while n < CHUNK:
        level_masks.append((row // (2 * n) == col // (2 * n)) & ((row // n) % 2 == 1) & ((col // n) % 2 == 0))
        n *= 2

    probs = []
    for ch in chunks:
        lhs_all = jnp.concatenate([ch["aa"], ch["ra"]], axis=0)
        rhs_all = jnp.concatenate([ch["bi"], ch["ki"]], axis=0)
        for h in range(LANES // HEAD):
            own = lo_lane if h == 0 else ~lo_lane
            own2 = lo_lane2 if h == 0 else ~lo_lane2
            gram = ops.mm(jnp.where(own2, lhs_all, 0.0), rhs_all, SCAN_PASSES["gram"], _NT)
            probs.append(dict(ch=ch, own=own,
                              m_ab=jnp.where(strict, gram[:CHUNK, :CHUNK], 0.0),
                              m_ak=jnp.where(strict, gram[:CHUNK, CHUNK:], 0.0),
                              m_rb=jnp.where(incl, gram[CHUNK:, :CHUNK], 0.0),
                              m_rk=jnp.where(incl, gram[CHUNK:, CHUNK:], 0.0)))
    yield

    for p in probs:
        p["t"] = jnp.where(eye, 1.0, 0.0) + jnp.where(level_masks[0], p["m_ab"], 0.0)
    n = 2
    for mask in level_masks[1:]:
        if n < SUBLANES:
            for p in probs:
                p["to"] = ops.mm(p["t"], jnp.where(mask, p["m_ab"], 0.0), SCAN_PASSES["inverse"])
            for p in probs:
                p["t"] = p["t"] + ops.mm(p["to"], p["t"], SCAN_PASSES["inverse"])
        else:
            lower = lambda t, n=n: jnp.concatenate(
                [t[i:i + n] for i in range(n, CHUNK, 2 * n)], axis=0)
            for p in probs:
                p["tl"] = lower(p["t"])
                p["to"] = ops.mm(p["tl"], jnp.where(mask, p["m_ab"], 0.0), SCAN_PASSES["inverse"])
            for p in probs:
                tl = p["tl"] + ops.mm(p["to"], p["t"], SCAN_PASSES["inverse"])
                slabs = []
                for j, i in enumerate(range(0, CHUNK, 2 * n)):
                    slabs += [p["t"][i:i + n], tl[j * n:(j + 1) * n]]
                p["t"] = jnp.concatenate(slabs, axis=0)
        n *= 2
        yield

    for p in probs:
        p["vm"] = jnp.where(p["own"], 0.0, p["ch"]["vsw"])
        p["x"] = jnp.where(p["own"], p["ch"]["aa"], 0.0) + ops.mm(p["m_ak"], p["vm"], SCAN_PASSES["apply"])
    yield
    for p in probs:
        p["tx"] = ops.mm(p["t"], p["x"], SCAN_PASSES["apply"])
    yield
    for p in probs:
        p["y"] = jnp.where(p["own"], p["ch"]["ra"], 0.0) + ops.mm(
            jnp.concatenate([p["m_rb"], p["m_rk"]], axis=1),
            jnp.concatenate([p["tx"], p["vm"]], axis=0), SCAN_PASSES["apply"])
    yield

    for i, ch in enumerate(chunks):
        p0, p1 = probs[2 * i], probs[2 * i + 1]
        pq = ops.mm(jnp.concatenate([ch["bh"], ch["kh"]], axis=0),
                    jnp.concatenate([jnp.concatenate([p0["tx"], p1["tx"]], axis=1),
                                     jnp.concatenate([p0["vm"], p1["vm"]], axis=1)], axis=0),
                    SCAN_PASSES["apply"], _TN)
        pq = jnp.where(row < HEAD, pq[:, :LANES], pq[:, LANES:])
        ch["p"] = jnp.where(same_half, pq, 0.0) + jnp.where(eye, ch["g_end"], 0.0)
        ch["q"] = jnp.where(same_half, 0.0, pq)
        ch["r"] = jnp.where(lo_lane, p0["y"], p1["y"])
        ch["y0"] = jnp.where(lo_lane, p1["y"], p0["y"])


def _interleave(main, *side):
    for _ in main:
        for gen in side:
            next(gen, None)
    for gen in side:
        for _ in gen:
            pass


def _rwkv_kernel(*refs, n_pairs, n_tblocks, n_tiles, cast_steps):
    n_cast = len(cast_steps)
    (r_ref, k_ref, v_ref, lora_ref, chan_ref, chanf_ref, mixl_ref, w2_ref, a2_ref, g2_ref) = refs[:10]
    cast_in, o_ref, cast_out = refs[10:10 + n_cast], refs[10 + n_cast], refs[11 + n_cast:11 + 2 * n_cast]
    (crkv_ref, clora_ref, z_ref, twd_s, ad_s, sgd_s, r3_s, k3_s, v3_s, g3_s,
     y_s, gend_s, p_s, q_s, rr_s, y0_s, *decay_s) = refs[11 + 2 * n_cast:]

    tb = r_ref.shape[0]
    n_chunks = tb // CHUNK
    s = pl.program_id(0)
    sa = jnp.minimum(s, n_tiles - 1)
    sf = jnp.maximum(s - 2, 0)
    hp, t_a = sa % n_pairs, (sa // n_pairs) % n_tblocks
    hpf, t_f = sf % n_pairs, (sf // n_pairs) % n_tblocks
    slot_w, slot_r = s % 2, (s + 1) % 2
    keep_w, keep_r = s % 3, (s + 1) % 3

    @pl.when(s == 0)
    def _():
        for ref in (crkv_ref, clora_ref, z_ref, r3_s, k3_s, v3_s, g3_s, gend_s, p_s, q_s, rr_s, y0_s,
                    *decay_s):
            ref[...] = jnp.zeros_like(ref)

    @pl.when(hp == 0)
    def _():
        lora_in = lora_ref[...]
        carry = jnp.where(t_a == 0, 0.0, clora_ref[...])
        lora = _token_shift(lora_in, carry, mixl_ref[...])
        clora_ref[...] = lora_in[tb - 1:tb, :]
        twd = jnp.tanh(lora[:, 0:DECAY_LORA])
        twd_hi = twd.astype(BF16)
        twd_s[0] = twd_hi
        twd_s[1] = (twd - twd_hi.astype(F32)).astype(BF16)
        ad_s[...] = lora[:, DECAY_LORA:DECAY_LORA + ICLR_LORA].astype(BF16)
        sgd_s[...] = jax.nn.sigmoid(lora[:, DECAY_LORA + ICLR_LORA:]).astype(BF16)

    ops = _Operands()

    def finish():
        chanf = chanf_ref[...]
        z = jnp.where(t_f == 0, 0.0, z_ref[hpf])
        for c in range(n_chunks):
            i = slot_r * n_chunks + c
            rz = ops.mm(jnp.concatenate([rr_s[i], p_s[i]], axis=0), z, SCAN_PASSES["state"])
            y_s[c * CHUNK:(c + 1) * CHUNK, :] = rz[:CHUNK] + y0_s[i]
            z = rz[CHUNK:] + q_s[i]
            yield
        z_ref[hpf] = z
        r_k, ln_w, ln_b = chanf[7:8], chanf[8:9], chanf[9:10]
        y = pltpu.roll(y_s[...], HEAD, axis=1)
        mu = _head_sum(y) * (1.0 / HEAD)
        yc = y - mu
        var = _head_sum(yc * yc) * (1.0 / HEAD)
        yn = yc * lax.rsqrt(var + GN_EPS) * ln_w + ln_b
        bonus = _head_sum(r3_s[keep_r] * k3_s[keep_r] * r_k) * v3_s[keep_r]
        o_ref[...] = ((yn + bonus) * g3_s[keep_r]).astype(o_ref.dtype)
        yield

    def matrices():
        chunks = []
        for c in range(n_chunks):
            i = slot_r * n_chunks + c
            ch = {name: ref[i] for name, ref in zip(CHUNK_FIELDS, decay_s)}
            ch["g_end"] = gend_s[i, 0:1, :]
            chunks.append(ch)
        yield from _scan_matrices(ops, chunks)
        for c, ch in enumerate(chunks):
            i = slot_w * n_chunks + c
            p_s[i], q_s[i], rr_s[i], y0_s[i] = ch["p"], ch["q"], ch["r"], ch["y0"]
        yield

    def prepare():
        chan = chan_ref[...]
        mix_r, mix_k, mix_v = chan[0:1], chan[1:2], chan[2:3]
        w0, a0, k_k, k_a = chan[3:4], chan[4:5], chan[5:6], chan[6:7]
        r_in, k_in, v_in = r_ref[...], k_ref[...], v_ref[...]
        carry = jnp.where(t_a == 0, 0.0, crkv_ref[hp])
        r = _token_shift(r_in, carry[0:1], mix_r)
        k = _token_shift(k_in, carry[1:2], mix_k)
        v = _token_shift(v_in, carry[2:3], mix_v)
        crkv_ref[hp, 0:1, :] = r_in[tb - 1:tb, :]
        crkv_ref[hp, 1:2, :] = k_in[tb - 1:tb, :]
        crkv_ref[hp, 2:3, :] = v_in[tb - 1:tb, :]
        yield
        w2_hi = w2_ref[...].astype(BF16)
        w2_lo = (w2_ref[...] - w2_hi.astype(F32)).astype(BF16)
        twd_hi, twd_lo = twd_s[0], twd_s[1]
        dw = w0 + (_dot(twd_hi, w2_hi) + (_dot(twd_hi, w2_lo) + _dot(twd_lo, w2_hi)))
        logw = -DECAY_SCALE * jax.nn.sigmoid(dw)
        alr = jax.nn.sigmoid(a0 + _dot(ad_s[...], a2_ref[...].astype(BF16)))
        g3_s[keep_w] = _dot(sgd_s[...], g2_ref[...].astype(BF16))
        yield
        kk = k * k_k
        kk = kk * lax.rsqrt(jnp.maximum(_head_sum(kk * kk), 1e-24))
        k_mod = k * (1.0 + (alr - 1.0) * k_a)
        vsw = pltpu.roll(v, HEAD, axis=1)
        a, b = -kk, kk * alr
        r3_s[keep_w], k3_s[keep_w], v3_s[keep_w] = r, k_mod, v
        yield
        for c in range(n_chunks):
            sl = slice(c * CHUNK, (c + 1) * CHUNK)
            ch = _decay_chunk(ops, r[sl], logw[sl], k_mod[sl], vsw[sl], a[sl], b[sl])
            i = slot_w * n_chunks + c
            for name, ref in zip(CHUNK_FIELDS, decay_s):
                ref[i] = ch[name]
            gend_s[i] = jnp.broadcast_to(ch["g_end"], (SUBLANES, LANES))
            yield

    def casts():
        for w_ref, wb_ref in zip(cast_in, cast_out):
            wb_ref[...] = w_ref[...].astype(wb_ref.dtype)
            yield

    _interleave(matrices(), finish(), prepare(), casts())


def _rwkv(p, chan, mix_lora, w2, a2, g2, batch, seq, offs, tb, cast):
    m = p.shape[0]
    c = w2.shape[1]
    n_pairs = c // LANES
    n_tblocks = seq // tb
    n_tiles = batch * n_tblocks * n_pairs
    n_chunks = tb // CHUNK
    gate_w = LORA_PAD - DECAY_LORA - ICLR_LORA
    prep = lambda s: jnp.minimum(s, n_tiles - 1)
    fin = lambda s: jnp.maximum(s - 2, 0)
    col = lambda name: lambda s: (prep(s) // n_pairs, offs[name] // LANES + prep(s) % n_pairs)
    weight = lambda s: (0, prep(s) % n_pairs)
    tile = lambda: pltpu.VMEM((tb, LANES), F32)
    tile3 = lambda: pltpu.VMEM((3, tb, LANES), F32)
    mats = lambda: pltpu.VMEM((2 * n_chunks, CHUNK, LANES), F32)
    cast_steps, cast_specs = [], []
    for w in cast:
        steps = min(n_tiles, w.shape[0] // BF16_ROWS)
        assert w.shape[0] % (steps * BF16_ROWS) == 0, w.shape
        cast_steps.append(steps)
        cast_specs.append(pl.BlockSpec((w.shape[0] // steps, w.shape[1]),
                                       lambda s, steps=steps: (jnp.minimum(s, steps - 1), 0)))
    outs = pl.pallas_call(
        functools.partial(_rwkv_kernel, n_pairs=n_pairs, n_tblocks=n_tblocks, n_tiles=n_tiles,
                          cast_steps=tuple(cast_steps)),
        grid=(n_tiles + 2,),
        in_specs=[pl.BlockSpec((tb, LANES), col("r")),
                  pl.BlockSpec((tb, LANES), col("k")),
                  pl.BlockSpec((tb, LANES), col("v")),
                  pl.BlockSpec((tb, LORA_PAD), lambda s: (prep(s) // n_pairs, offs["lora"] // LORA_PAD)),
                  pl.BlockSpec((16, LANES), weight),
                  pl.BlockSpec((16, LANES), lambda s: (0, fin(s) % n_pairs)),
                  pl.BlockSpec((1, LORA_PAD), lambda s: (0, 0)),
                  pl.BlockSpec((DECAY_LORA, LANES), weight),
                  pl.BlockSpec((ICLR_LORA, LANES), weight),
                  pl.BlockSpec((gate_w, LANES), weight)] + cast_specs,
        out_specs=[pl.BlockSpec((tb, LANES), lambda s: (fin(s) // n_pairs, fin(s) % n_pairs))] + cast_specs,
        out_shape=[jax.ShapeDtypeStruct((m, c), BF16)]
                  + [jax.ShapeDtypeStruct(w.shape, BF16) for w in cast],
        scratch_shapes=[pltpu.VMEM((n_pairs, SUBLANES, LANES), F32),
                        pltpu.VMEM((1, LORA_PAD), F32),
                        pltpu.VMEM((n_pairs, CHUNK, LANES), F32),
                        pltpu.VMEM((2, tb, DECAY_LORA), BF16),
                        pltpu.VMEM((tb, ICLR_LORA), BF16),
                        pltpu.VMEM((tb, gate_w), BF16),
                        tile3(), tile3(), tile3(), tile3(),
                        tile(),
                        pltpu.VMEM((2 * n_chunks, SUBLANES, LANES), F32),
                        mats(), mats(), mats(), mats()]
                       + [mats() for _ in CHUNK_FIELDS],
        compiler_params=_cparams(("arbitrary",)),
        name="rwkv7",
    )(p, p, p, p, chan, chan, mix_lora, w2, a2, g2, *cast)
    return outs[0], outs[1:]


def _attn_stages(sink_ref, q, kv, bias, store, n_heads):
    n_kv = n_heads // ATTN_GROUP
    lo_lane = lax.broadcasted_iota(jnp.int32, (WINDOW, LANES), 1) < HEAD
    for g0 in range(0, n_heads, ATTN_BATCH):
        heads = range(g0, g0 + ATTN_BATCH)
        hk = g0 // ATTN_GROUP
        kh = kv[:, hk * HEAD:(hk + 1) * HEAD]
        vh = kv[:, (n_kv + hk) * HEAD:(n_kv + hk + 1) * HEAD]
        s = {h: _dot_nt(q[:, h * HEAD:(h + 1) * HEAD], kh) + bias(h) for h in heads}
        yield
        mx = {h: jnp.maximum(jnp.max(s[h], axis=-1, keepdims=True), sink_ref[h]) for h in heads}
        e = {h: jnp.exp(s[h] - mx[h]) for h in heads}
        yield
        denom = {h: jnp.sum(e[h], axis=-1, keepdims=True) + jnp.exp(sink_ref[h] - mx[h]) for h in heads}
        pv = {h: _dot(e[h].astype(BF16), vh) for h in heads}
        yield
        for h0 in range(g0, g0 + ATTN_BATCH, 2):
            pair = (jnp.concatenate([pv[h0], pv[h0 + 1]], axis=1)
                    * jnp.where(lo_lane, 1.0 / denom[h0], 1.0 / denom[h0 + 1]))
            store(h0, pair)
        yield


def _attn_merge_kernel(sink_ref, q_ref, kvc_ref, kvp_ref,
                       yr_ref, gate_ref, x_ref, mod_ref, wbr_ref, wba_ref, wout_ref, g2_ref,
                       x1_ref, h2_ref, bias_s, ya_s, *, blocks_per_seq, n_heads, n_tiles):
    tm, d = x_ref.shape
    s = pl.program_id(0)
    first_block = (jnp.minimum(s, n_tiles - 1) * (tm // WINDOW)) % blocks_per_seq == 0

    @pl.when(s == 0)
    def _():
        ya_s[...] = jnp.zeros_like(ya_s)
        qi = lax.broadcasted_iota(jnp.int32, (WINDOW, 2 * WINDOW), 0)
        kj = lax.broadcasted_iota(jnp.int32, (WINDOW, 2 * WINDOW), 1)
        dist = qi + WINDOW - kj
        valid = (dist >= 0) & (dist < WINDOW)
        for first in (0, 1):
            neg_dist = jnp.where(valid & (kj >= WINDOW) if first == 0 else valid, -dist.astype(F32), -jnp.inf)
            for h in range(n_heads):
                bias_s[first * n_heads + h] = (2.0 ** (-8.0 * (h + 1) / n_heads)) * neg_dist

    def attention():
        kv = jnp.concatenate([kvp_ref[...], kvc_ref[...]], axis=0).astype(BF16)
        q = (q_ref[...] * (HEAD ** -0.5)).astype(BF16)
        for blk in range(tm // WINDOW):
            rows = slice(blk * WINDOW, (blk + 1) * WINDOW)
            table = jnp.where(first_block, 0, 1) * n_heads if blk == 0 else n_heads

            def store(h0, pair, rows=rows):
                ya_s[s % 2, rows, h0 * HEAD:(h0 + 2) * HEAD] = pair.astype(ya_s.dtype)

            yield from _attn_stages(sink_ref, q[rows], kv[blk * WINDOW:(blk + 2) * WINDOW],
                                    lambda h, table=table: bias_s[table + h], store, n_heads)

    def merge():
        yr, ya = yr_ref[...], ya_s[(s + 1) % 2]
        merged = []
        for c0 in range(0, d, MXU_N):
            cols, cols_a = slice(c0, c0 + MXU_N), slice(d + c0, d + c0 + MXU_N)
            merged.append((jax.nn.sigmoid(gate_ref[:, cols]) * _dot(yr, wbr_ref[:, cols])
                           + jax.nn.sigmoid(gate_ref[:, cols_a]) * _dot(ya, wba_ref[:, cols])).astype(BF16))
            yield
        merged = jnp.concatenate(merged, axis=1)
        for c0 in range(0, d, MXU_N):
            cols = slice(c0, c0 + MXU_N)
            x1_ref[:, cols] = x_ref[:, cols] + mod_ref[0, 2:3, cols] * _dot(merged, wout_ref[:, cols])
            yield
        h2 = _rms_norm(x1_ref[...], g2_ref[...]) * (1.0 + mod_ref[0, 4:5, :]) + mod_ref[0, 3:4, :]
        h2_ref[...] = h2.astype(BF16)
        yield

    _interleave(merge(), attention())


def _attn_merge(yr, p, sinks, x2, mod, wbr, wba, wout, gain2, rows_per_batch, offs, tm):
    m, d = x2.shape
    c = yr.shape[1]
    n_heads = sinks.shape[0]
    qc = n_heads * HEAD
    kvc = 2 * (n_heads // ATTN_GROUP) * HEAD
    n_tiles = m // tm
    tpb = rows_per_batch // tm
    wpt = tm // WINDOW
    att = lambda s: jnp.minimum(s, n_tiles - 1)
    mrg = lambda s: jnp.maximum(s - 1, 0)
    const = lambda shape: pl.BlockSpec(shape, lambda s: (0, 0), pipeline_mode=pl.Buffered(1))
    return pl.pallas_call(
        functools.partial(_attn_merge_kernel, blocks_per_seq=rows_per_batch // WINDOW, n_heads=n_heads,
                          n_tiles=n_tiles),
        grid=(n_tiles + 1,),
        in_specs=[pl.BlockSpec(memory_space=pltpu.SMEM),
                  pl.BlockSpec((tm, qc), lambda s: (att(s), offs["q"] // qc)),
                  pl.BlockSpec((tm, kvc), lambda s: (att(s), offs["kv"] // kvc)),
                  pl.BlockSpec((WINDOW, kvc), lambda s: (jnp.maximum(att(s) * wpt - 1, 0), offs["kv"] // kvc)),
                  pl.BlockSpec((tm, c), lambda s: (mrg(s), 0)),
                  pl.BlockSpec((tm, 2 * d), lambda s: (mrg(s), 0)),
                  pl.BlockSpec((tm, d), lambda s: (mrg(s), 0)),
                  pl.BlockSpec((1, 6, d), lambda s: (mrg(s) // tpb, 0, 0)),
                  const((c, d)), const((c, d)), const((d, d)),
                  pl.BlockSpec((1, d), lambda s: (0, 0))],
        out_specs=[pl.BlockSpec((tm, d), lambda s: (mrg(s), 0)),
                   pl.BlockSpec((tm, d), lambda s: (mrg(s), 0))],
        out_shape=[jax.ShapeDtypeStruct((m, d), F32), jax.ShapeDtypeStruct((m, d), BF16)],
        scratch_shapes=[pltpu.VMEM((2 * n_heads, WINDOW, 2 * WINDOW), F32),
                        pltpu.VMEM((2, tm, qc), BF16)],
        compiler_params=_cparams(("arbitrary",)),
        name="attn_merge",
    )(sinks, p, p, p, yr, p, x2, mod, wbr, wba, wout, gain2)


def _mlp_kernel(h_ref, wu_ref, wd_ref, x1_ref, mod_ref, fg_ref, o_ref, acc_ref):
    f = pl.program_id(1)

    @pl.when(f == 0)
    def _():
        acc_ref[...] = jnp.zeros_like(acc_ref)

    u = _dot(h_ref[...], wu_ref[...])
    act = jnp.square(jnp.maximum(u, 0.0)).astype(BF16)
    acc_ref[...] += _dot(act, wd_ref[...])

    @pl.when(f == pl.num_programs(1) - 1)
    def _():
        x2 = x1_ref[...] + mod_ref[0, 5:6, :] * acc_ref[...]
        o_ref[...] = _rms_norm(x2, fg_ref[...])


def _mlp(h2, wu, wd, x1, mod, final_gain, rows_per_batch, tm, tf):
    m, d = h2.shape
    f = wu.shape[1]
    tpb = rows_per_batch // tm
    return pl.pallas_call(
        _mlp_kernel,
        grid=(m // tm, f // tf),
        in_specs=[pl.BlockSpec((tm, d), lambda i, j: (i, 0)),
                  pl.BlockSpec((d, tf), lambda i, j: (0, j)),
                  pl.BlockSpec((tf, d), lambda i, j: (j, 0)),
                  pl.BlockSpec((tm, d), lambda i, j: (i, 0)),
                  pl.BlockSpec((1, 6, d), lambda i, j: (i // tpb, 0, 0)),
                  pl.BlockSpec((1, d), lambda i, j: (0, 0))],
        out_specs=pl.BlockSpec((tm, d), lambda i, j: (i, 0)),
        out_shape=jax.ShapeDtypeStruct((m, d), F32),
        scratch_shapes=[pltpu.VMEM((tm, d), F32)],
        compiler_params=_cparams(("arbitrary", "arbitrary")),
        name="mlp",
    )(h2, wu, wd, x1, mod, final_gain)


def _forward(x, c, w_ada, b_ada, norm1_gain, w_in, b_in, rwkv_mix, rwkv_w0, rwkv_w2, rwkv_a0,
             rwkv_a2, rwkv_g2, rwkv_k_k, rwkv_k_a, rwkv_r_k, rwkv_ln_w, rwkv_ln_b, attn_sinks,
             w_branch_rwkv, w_branch_attn, w_out, norm2_gain, w_up, w_down, final_gain):
    batch, seq, d = x.shape
    depth = w_ada.shape[0]
    c_rwkv = rwkv_w0.shape[1]
    n_heads = attn_sinks.shape[1]
    qc = n_heads * HEAD
    kvc = 2 * (n_heads // ATTN_GROUP) * HEAD
    lora_w = DECAY_LORA + ICLR_LORA + GATE_LORA
    rwkv_cols = 3 * c_rwkv + lora_w
    attn_cols = qc + kvc
    offs = {"gate": 0, "r": 2 * d, "k": 2 * d + c_rwkv, "v": 2 * d + 2 * c_rwkv, "q": 2 * d + 3 * c_rwkv}
    offs["lora"] = offs["q"] + qc
    offs["kv"] = offs["lora"] + LORA_PAD
    n_pack = offs["kv"] + kvc

    segments = ((rwkv_cols + attn_cols, 2 * d, offs["gate"]), (0, 3 * c_rwkv, offs["r"]),
                (rwkv_cols, qc, offs["q"]), (3 * c_rwkv, lora_w, offs["lora"]),
                (rwkv_cols + qc, kvc, offs["kv"]))
    zero_rows = (offs["lora"] + lora_w, offs["kv"])

    def pack_cols(t):
        out = jnp.zeros(t.shape[:-1] + (n_pack,), t.dtype)
        for src, width, dst in segments:
            out = out.at[..., dst:dst + width].set(t[..., src:src + width])
        return out

    tm_in = min(1024, seq)
    tn_in = 1280 if n_pack % 1280 == 0 else LANES
    tb = min(1024, seq)
    tm_merge = min(256, seq)
    tm_mlp = min(512, seq)
    tf = min(1024, w_up.shape[2])

    assert depth == 1, "single-layer block"
    layer = lambda t: t.reshape(t.shape[1:])
    x2 = x.reshape(batch * seq, d)
    for l in range(depth):
        mod = _ada(c, layer(w_ada), b_ada[l]).reshape(batch, 6, d)
        w_pack = _pack_weight_t(layer(w_in).T, segments, zero_rows, n_pack)
        b_pack = pack_cols(b_in[l]).reshape(1, n_pack)
        p = _inproj(x2, mod, norm1_gain[l].reshape(1, d), w_pack, b_pack, seq, tm_in, tn_in)

        mix = rwkv_mix[l]
        chan = jnp.stack([mix[:c_rwkv], mix[c_rwkv:2 * c_rwkv], mix[2 * c_rwkv:3 * c_rwkv],
                          rwkv_w0[l], rwkv_a0[l], rwkv_k_k[l], rwkv_k_a[l], rwkv_r_k[l].reshape(c_rwkv),
                          rwkv_ln_w[l], rwkv_ln_b[l]])
        chan = jnp.pad(chan, ((0, 16 - chan.shape[0]), (0, 0)))
        mix_lora = jnp.pad(mix[3 * c_rwkv:], (0, LORA_PAD - lora_w)).reshape(1, LORA_PAD)
        g2_pad = jnp.pad(rwkv_g2[l], ((0, LORA_PAD - lora_w), (0, 0)))
        y_rwkv, (wbr, wba, wout, wup, wdown) = _rwkv(
            p, chan, mix_lora, rwkv_w2[l], rwkv_a2[l], g2_pad, batch, seq, offs, tb,
            cast=[layer(w_branch_rwkv), layer(w_branch_attn), layer(w_out), layer(w_up), layer(w_down)])
        x2, h2 = _attn_merge(y_rwkv, p, attn_sinks[l], x2, mod, wbr, wba, wout,
                             norm2_gain[l].reshape(1, d), seq, offs, tm_merge)
        x2 = _mlp(h2, wup, wdown, x2, mod, final_gain.reshape(1, d), seq, tm_mlp, tf)
    return x2.reshape(batch, seq, d)


def kernel(x, c, w_ada, b_ada, norm1_gain, w_in, b_in, rwkv_mix, rwkv_w0, rwkv_w2, rwkv_a0, rwkv_a2, rwkv_g2, rwkv_k_k, rwkv_k_a, rwkv_r_k, rwkv_ln_w, rwkv_ln_b, attn_sinks, w_branch_rwkv, w_branch_attn, w_out, norm2_gain, w_up, w_down, final_gain):
    return _forward(x, c, w_ada, b_ada, norm1_gain, w_in, b_in, rwkv_mix, rwkv_w0, rwkv_w2, rwkv_a0,
                    rwkv_a2, rwkv_g2, rwkv_k_k, rwkv_k_a, rwkv_r_k, rwkv_ln_w, rwkv_ln_b, attn_sinks,
                    w_branch_rwkv, w_branch_attn, w_out, norm2_gain, w_up, w_down, final_gain)
```

```python
import functools

import jax
import jax.numpy as jnp
from jax import lax
from jax.experimental import pallas as pl
from jax.experimental.pallas import tpu as pltpu

F32 = jnp.float32
BF16 = jnp.bfloat16

LANES = 128
SUBLANES = 8
BF16_ROWS = 16
MXU_N = 256
NORM_ROWS = 128
HEAD = 64
CHUNK = 128
WINDOW = 128
ATTN_GROUP = 8
ATTN_BATCH = 8
NORM_EPS = 1e-6
GN_EPS = 64e-5
DECAY_SCALE = 0.6065306597126334
KK_EPS = 1e-12
CHAN_ROWS = 16
DECAY_LORA = 64
ICLR_LORA = 64
GATE_LORA = 160
LORA_PAD = 512
VMEM_LIMIT = 56 * 1024 * 1024


def _cparams(sem):
    return pltpu.CompilerParams(dimension_semantics=sem, vmem_limit_bytes=VMEM_LIMIT)


_NN = (((1,), (0,)), ((), ()))
_NT = (((1,), (1,)), ((), ()))
_TN = (((0,), (0,)), ((), ()))


def _dot(a, b, dims=_NN):
    return lax.dot_general(a, b, dims, preferred_element_type=F32)


def _dot_nt(a, b):
    return _dot(a, b, _NT)


def _rms_norm(x, gain):
    ms = jnp.mean(x * x, axis=-1, keepdims=True)
    return x * lax.rsqrt(ms + NORM_EPS) * gain


def _ada_kernel(ct_ref, w_ref, b_ref, o_ref):
    ct = ct_ref[...]
    act = ct * jax.nn.sigmoid(ct)
    for m in range(o_ref.shape[0]):
        o_ref[m:m + 1, :] = jnp.sum(w_ref[...] * act[:, m:m + 1], axis=0, keepdims=True) + b_ref[...]


def _ada(c, w, b, tn=2048):
    nb, d = c.shape
    n = w.shape[1]
    return pl.pallas_call(
        _ada_kernel,
        grid=(n // tn,),
        in_specs=[pl.BlockSpec((d, nb), lambda j: (0, 0)),
                  pl.BlockSpec((d, tn), lambda j: (0, j)),
                  pl.BlockSpec((1, tn), lambda j: (0, j))],
        out_specs=pl.BlockSpec((nb, tn), lambda j: (0, j)),
        out_shape=jax.ShapeDtypeStruct((nb, n), F32),
        compiler_params=_cparams(("arbitrary",)),
        name="adaln",
    )(c.T, w, b.reshape(1, n))


def _pack_kernel(w_ref, o_ref, *, segments, zero_rows):
    for src, width, dst in segments:
        o_ref[dst:dst + width, :] = w_ref[src:src + width, :].astype(o_ref.dtype)
    lo, hi = zero_rows
    o_ref[lo:hi, :] = jnp.zeros((hi - lo, o_ref.shape[1]), o_ref.dtype)


def _pack_weight_t(w_t, segments, zero_rows, n_pack, tk=256):
    n, k = w_t.shape
    return pl.pallas_call(
        functools.partial(_pack_kernel, segments=segments, zero_rows=zero_rows),
        grid=(k // tk,),
        in_specs=[pl.BlockSpec((n, tk), lambda i: (0, i))],
        out_specs=pl.BlockSpec((n_pack, tk), lambda i: (0, i)),
        out_shape=jax.ShapeDtypeStruct((n_pack, k), BF16),
        compiler_params=_cparams(("arbitrary",)),
        name="pack_w_in",
    )(w_t)


def _inproj_kernel(x_ref, mod_ref, g_ref, w_ref, b_ref, o_ref, h_ref):
    @pl.when(pl.program_id(1) == 0)
    def _():
        scale = g_ref[...] * (1.0 + mod_ref[0, 1:2, :])
        shift = mod_ref[0, 0:1, :]

        def slab(c, carry):
            rows = pl.ds(pl.multiple_of(c * NORM_ROWS, NORM_ROWS), NORM_ROWS)
            x = x_ref[rows, :]
            ms = jnp.mean(x * x, axis=-1, keepdims=True)
            h_ref[rows, :] = (x * lax.rsqrt(ms + NORM_EPS) * scale + shift).astype(BF16)
            return carry

        lax.fori_loop(0, x_ref.shape[0] // NORM_ROWS, slab, 0)

    o_ref[...] = _dot_nt(h_ref[...], w_ref[...]) + b_ref[...]


def _inproj(x2, mod, gain, w_t, b, rows_per_batch, tm, tn):
    m, d = x2.shape
    n = w_t.shape[0]
    tpb = rows_per_batch // tm
    return pl.pallas_call(
        _inproj_kernel,
        grid=(m // tm, n // tn),
        in_specs=[pl.BlockSpec((tm, d), lambda i, j: (i, 0)),
                  pl.BlockSpec((1, 6, d), lambda i, j: (i // tpb, 0, 0)),
                  pl.BlockSpec((1, d), lambda i, j: (0, 0)),
                  pl.BlockSpec((tn, d), lambda i, j: (j, 0)),
                  pl.BlockSpec((1, tn), lambda i, j: (0, j))],
        out_specs=pl.BlockSpec((tm, tn), lambda i, j: (i, j)),
        out_shape=jax.ShapeDtypeStruct((m, n), F32),
        scratch_shapes=[pltpu.VMEM((tm, d), BF16)],
        compiler_params=_cparams(("arbitrary", "arbitrary")),
        name="inproj",
    )(x2, mod, gain, w_t, b)


def _head_sum(x):
    lo = lax.broadcasted_iota(jnp.int32, x.shape, 1) < HEAD
    s0 = jnp.sum(jnp.where(lo, x, 0.0), axis=-1, keepdims=True)
    s1 = jnp.sum(jnp.where(lo, 0.0, x), axis=-1, keepdims=True)
    return jnp.where(lo, s0, s1)


def _token_shift(x, carry_row, mix):
    rolled = pltpu.roll(x, 1, axis=0)
    head = rolled[:SUBLANES]
    first = lax.broadcasted_iota(jnp.int32, head.shape, 0) == 0
    prev = jnp.concatenate([jnp.where(first, carry_row, head), rolled[SUBLANES:]], axis=0)
    return x + (prev - x) * mix


class _Operands:
    def __init__(self):
        self._parts = {}

    def parts(self, x, n):
        ent = self._parts.setdefault(id(x), [x])
        while len(ent) - 1 < n:
            rest = x
            for piece in ent[1:]:
                rest = rest - piece.astype(F32)
            ent.append(rest.astype(BF16))
        return ent[1:n + 1]

    def mm(self, a, b, dims=_NN):
        return _dot(self.parts(a, 1)[0], self.parts(b, 1)[0], dims)


CHUNK_FIELDS = ("ra", "aa", "bi", "ki", "bh", "kh", "vsw")


def _decay_chunk(ops, rc, lwc, kc, vsw, ac, bc):
    incl = (lax.broadcasted_iota(jnp.int32, (CHUNK, CHUNK), 0)
            >= lax.broadcasted_iota(jnp.int32, (CHUNK, CHUNK), 1))
    tri = jnp.where(incl, 1.0, 0.0).astype(BF16)
    lw = sum(jnp.dot(tri, piece, preferred_element_type=F32) for piece in ops.parts(lwc, 2))
    lw_end = lw[CHUNK - 1:CHUNK, :]
    e_in = jnp.exp(lw)
    e_ex = jnp.exp(lw - lwc)
    e_inv = jnp.exp(-lw)
    e_end = jnp.exp(lw_end - lw)
    return dict(vsw=vsw, g_end=jnp.exp(lw_end), ra=rc * e_in, aa=ac * e_ex,
                bi=bc * e_inv, ki=kc * e_inv, bh=bc * e_end, kh=kc * e_end)


def _scan_matrices(ops, chunks):
    row = lax.broadcasted_iota(jnp.int32, (CHUNK, CHUNK), 0)
    col = lax.broadcasted_iota(jnp.int32, (CHUNK, CHUNK), 1)
    incl = row >= col
    strict = row > col
    eye = row == col
    lo_lane = col < HEAD
    same_half = lo_lane == (row < HEAD)
    lo_lane2 = lax.broadcasted_iota(jnp.int32, (2 * CHUNK, LANES), 1) < HEAD
    level_masks = []
    n = 1
    while n < CHUNK:
        level_masks.append((row // (2 * n) == col // (2 * n)) & ((row // n) % 2 == 1) & ((col // n) % 2 == 0))
        n *= 2

    probs = []
    for ch in chunks:
        lhs_all = jnp.concatenate([ch["aa"], ch["ra"]], axis=0)
        rhs_all = jnp.concatenate([ch["bi"], ch["ki"]], axis=0)
        for h in range(LANES // HEAD):
            own = lo_lane if h == 0 else ~lo_lane
            own2 = lo_lane2 if h == 0 else ~lo_lane2
            gram = ops.mm(jnp.where(own2, lhs_all, 0.0), rhs_all, _NT)
            probs.append(dict(ch=ch, own=own,
                              m_ab=jnp.where(strict, gram[:CHUNK, :CHUNK], 0.0),
                              m_ak=jnp.where(strict, gram[:CHUNK, CHUNK:], 0.0),
                              m_rb=jnp.where(incl, gram[CHUNK:, :CHUNK], 0.0),
                              m_rk=jnp.where(incl, gram[CHUNK:, CHUNK:], 0.0)))
    yield

    for p in probs:
        p["t"] = jnp.where(eye, 1.0, 0.0) + jnp.where(level_masks[0], p["m_ab"], 0.0)
    n = 2
    for mask in level_masks[1:]:
        if n < SUBLANES:
            for p in probs:
                p["to"] = ops.mm(p["t"], jnp.where(mask, p["m_ab"], 0.0))
            for p in probs:
                p["t"] = p["t"] + ops.mm(p["to"], p["t"])
        else:
            lower = lambda t, n=n: jnp.concatenate(
                [t[i:i + n] for i in range(n, CHUNK, 2 * n)], axis=0)
            for p in probs:
                p["tl"] = lower(p["t"])
                p["to"] = ops.mm(p["tl"], jnp.where(mask, p["m_ab"], 0.0))
            for p in probs:
                tl = p["tl"] + ops.mm(p["to"], p["t"])
                slabs = []
                for j, i in enumerate(range(0, CHUNK, 2 * n)):
                    slabs += [p["t"][i:i + n], tl[j * n:(j + 1) * n]]
                p["t"] = jnp.concatenate(slabs, axis=0)
        n *= 2
        yield

    for p in probs:
        p["vm"] = jnp.where(p["own"], 0.0, p["ch"]["vsw"])
        p["x"] = jnp.where(p["own"], p["ch"]["aa"], 0.0) + ops.mm(p["m_ak"], p["vm"])
    yield
    for p in probs:
        p["tx"] = ops.mm(p["t"], p["x"])
    yield
    for p in probs:
        p["y"] = jnp.where(p["own"], p["ch"]["ra"], 0.0) + ops.mm(
            jnp.concatenate([p["m_rb"], p["m_rk"]], axis=1),
            jnp.concatenate([p["tx"], p["vm"]], axis=0))
    yield

    for i, ch in enumerate(chunks):
        p0, p1 = probs[2 * i], probs[2 * i + 1]
        pq = ops.mm(jnp.concatenate([ch["bh"], ch["kh"]], axis=0),
                    jnp.concatenate([jnp.concatenate([p0["tx"], p1["tx"]], axis=1),
                                     jnp.concatenate([p0["vm"], p1["vm"]], axis=1)], axis=0),
                    _TN)
        pq = jnp.where(row < HEAD, pq[:, :LANES], pq[:, LANES:])
        ch["p"] = jnp.where(same_half, pq, 0.0) + jnp.where(eye, ch["g_end"], 0.0)
        ch["q"] = jnp.where(same_half, 0.0, pq)
        ch["r"] = jnp.where(lo_lane, p0["y"], p1["y"])
        ch["y0"] = jnp.where(lo_lane, p1["y"], p0["y"])


def _interleave(main, *side):
    for _ in main:
        for gen in side:
            next(gen, None)
    for gen in side:
        for _ in gen:
            pass


def _rwkv_kernel(*refs, n_pairs, n_tblocks, n_tiles, cast_steps):
    n_cast = len(cast_steps)
    (r_ref, k_ref, v_ref, lora_ref, chan_ref, chanf_ref, mixl_ref, w2_ref, a2_ref, g2_ref) = refs[:10]
    cast_in, o_ref, cast_out = refs[10:10 + n_cast], refs[10 + n_cast], refs[11 + n_cast:11 + 2 * n_cast]
    (crkv_ref, clora_ref, z_ref, twd_s, ad_s, sgd_s, r3_s, k3_s, v3_s, g3_s,
     y_s, gend_s, p_s, q_s, rr_s, y0_s, *decay_s) = refs[11 + 2 * n_cast:]

    tb = r_ref.shape[0]
    n_chunks = tb // CHUNK
    s = pl.program_id(0)
    sa = jnp.minimum(s, n_tiles - 1)
    sf = jnp.maximum(s - 2, 0)
    hp, t_a = sa % n_pairs, (sa // n_pairs) % n_tblocks
    hpf, t_f = sf % n_pairs, (sf // n_pairs) % n_tblocks
    slot_w, slot_r = s % 2, (s + 1) % 2
    keep_w, keep_r = s % 3, (s + 1) % 3

    @pl.when(s == 0)
    def _():
        for ref in (crkv_ref, clora_ref, z_ref, r3_s, k3_s, v3_s, g3_s, gend_s, p_s, q_s, rr_s, y0_s,
                    *decay_s):
            ref[...] = jnp.zeros_like(ref)

    @pl.when(hp == 0)
    def _():
        lora_in = lora_ref[...]
        carry = jnp.where(t_a == 0, 0.0, clora_ref[...])
        lora = _token_shift(lora_in, carry, mixl_ref[...])
        clora_ref[...] = lora_in[tb - 1:tb, :]
        twd = jnp.tanh(lora[:, 0:DECAY_LORA])
        twd_hi = twd.astype(BF16)
        twd_s[0] = twd_hi
        twd_s[1] = (twd - twd_hi.astype(F32)).astype(BF16)
        ad_s[...] = lora[:, DECAY_LORA:DECAY_LORA + ICLR_LORA].astype(BF16)
        sgd_s[...] = jax.nn.sigmoid(lora[:, DECAY_LORA + ICLR_LORA:]).astype(BF16)

    ops = _Operands()

    def finish():
        chanf = chanf_ref[...]
        z = jnp.where(t_f == 0, 0.0, z_ref[hpf])
        for c in range(n_chunks):
            i = slot_r * n_chunks + c
            rz = ops.mm(jnp.concatenate([rr_s[i], p_s[i]], axis=0), z)
            y_s[c * CHUNK:(c + 1) * CHUNK, :] = rz[:CHUNK] + y0_s[i]
            z = rz[CHUNK:] + q_s[i]
            yield
        z_ref[hpf] = z
        r_k, ln_w, ln_b = chanf[7:8], chanf[8:9], chanf[9:10]
        y = pltpu.roll(y_s[...], HEAD, axis=1)
        mu = _head_sum(y) * (1.0 / HEAD)
        yc = y - mu
        var = _head_sum(yc * yc) * (1.0 / HEAD)
        yn = yc * lax.rsqrt(var + GN_EPS) * ln_w + ln_b
        bonus = _head_sum(r3_s[keep_r] * k3_s[keep_r] * r_k) * v3_s[keep_r]
        o_ref[...] = ((yn + bonus) * g3_s[keep_r]).astype(o_ref.dtype)
        yield

    def matrices():
        chunks = []
        for c in range(n_chunks):
            i = slot_r * n_chunks + c
            ch = {name: ref[i] for name, ref in zip(CHUNK_FIELDS, decay_s)}
            ch["g_end"] = gend_s[i, 0:1, :]
            chunks.append(ch)
        yield from _scan_matrices(ops, chunks)
        for c, ch in enumerate(chunks):
            i = slot_w * n_chunks + c
            p_s[i], q_s[i], rr_s[i], y0_s[i] = ch["p"], ch["q"], ch["r"], ch["y0"]
        yield

    def prepare():
        chan = chan_ref[...]
        mix_r, mix_k, mix_v = chan[0:1], chan[1:2], chan[2:3]
        w0, a0, k_k, k_a = chan[3:4], chan[4:5], chan[5:6], chan[6:7]
        r_in, k_in, v_in = r_ref[...], k_ref[...], v_ref[...]
        carry = jnp.where(t_a == 0, 0.0, crkv_ref[hp])
        r = _token_shift(r_in, carry[0:1], mix_r)
        k = _token_shift(k_in, carry[1:2], mix_k)
        v = _token_shift(v_in, carry[2:3], mix_v)
        crkv_ref[hp, 0:1, :] = r_in[tb - 1:tb, :]
        crkv_ref[hp, 1:2, :] = k_in[tb - 1:tb, :]
        crkv_ref[hp, 2:3, :] = v_in[tb - 1:tb, :]
        yield
        w2_hi = w2_ref[...].astype(BF16)
        w2_lo = (w2_ref[...] - w2_hi.astype(F32)).astype(BF16)
        twd_hi, twd_lo = twd_s[0], twd_s[1]
        dw = w0 + (_dot(twd_hi, w2_hi) + (_dot(twd_hi, w2_lo) + _dot(twd_lo, w2_hi)))
        logw = -DECAY_SCALE * jax.nn.sigmoid(dw)
        alr = jax.nn.sigmoid(a0 + _dot(ad_s[...], a2_ref[...].astype(BF16)))
        g3_s[keep_w] = _dot(sgd_s[...], g2_ref[...].astype(BF16))
        yield
        kk = k * k_k
        kk = kk * lax.rsqrt(jnp.maximum(_head_sum(kk * kk), KK_EPS * KK_EPS))
        k_mod = k * (1.0 + (alr - 1.0) * k_a)
        vsw = pltpu.roll(v, HEAD, axis=1)
        a, b = -kk, kk * alr
        r3_s[keep_w], k3_s[keep_w], v3_s[keep_w] = r, k_mod, v
        yield
        for c in range(n_chunks):
            sl = slice(c * CHUNK, (c + 1) * CHUNK)
            ch = _decay_chunk(ops, r[sl], logw[sl], k_mod[sl], vsw[sl], a[sl], b[sl])
            i = slot_w * n_chunks + c
            for name, ref in zip(CHUNK_FIELDS, decay_s):
                ref[i] = ch[name]
            gend_s[i] = jnp.broadcast_to(ch["g_end"], (SUBLANES, LANES))
            yield

    def casts():
        for w_ref, wb_ref in zip(cast_in, cast_out):
            wb_ref[...] = w_ref[...].astype(wb_ref.dtype)
            yield

    _interleave(matrices(), finish(), prepare(), casts())


def _rwkv(p, chan, mix_lora, w2, a2, g2, batch, seq, offs, tb, cast):
    m = p.shape[0]
    c = w2.shape[1]
    n_pairs = c // LANES
    n_tblocks = seq // tb
    n_tiles = batch * n_tblocks * n_pairs
    n_chunks = tb // CHUNK
    gate_w = LORA_PAD - DECAY_LORA - ICLR_LORA
    prep = lambda s: jnp.minimum(s, n_tiles - 1)
    fin = lambda s: jnp.maximum(s - 2, 0)
    col = lambda name: lambda s: (prep(s) // n_pairs, offs[name] // LANES + prep(s) % n_pairs)
    weight = lambda s: (0, prep(s) % n_pairs)
    tile = lambda: pltpu.VMEM((tb, LANES), F32)
    tile3 = lambda: pltpu.VMEM((3, tb, LANES), F32)
    mats = lambda: pltpu.VMEM((2 * n_chunks, CHUNK, LANES), F32)
    cast_steps, cast_specs = [], []
    for w in cast:
        steps = min(n_tiles, w.shape[0] // BF16_ROWS)
        assert w.shape[0] % (steps * BF16_ROWS) == 0, w.shape
        cast_steps.append(steps)
        cast_specs.append(pl.BlockSpec((w.shape[0] // steps, w.shape[1]),
                                       lambda s, steps=steps: (jnp.minimum(s, steps - 1), 0)))
    outs = pl.pallas_call(
        functools.partial(_rwkv_kernel, n_pairs=n_pairs, n_tblocks=n_tblocks, n_tiles=n_tiles,
                          cast_steps=tuple(cast_steps)),
        grid=(n_tiles + 2,),
        in_specs=[pl.BlockSpec((tb, LANES), col("r")),
                  pl.BlockSpec((tb, LANES), col("k")),
                  pl.BlockSpec((tb, LANES), col("v")),
                  pl.BlockSpec((tb, LORA_PAD), lambda s: (prep(s) // n_pairs, offs["lora"] // LORA_PAD)),
                  pl.BlockSpec((CHAN_ROWS, LANES), weight),
                  pl.BlockSpec((CHAN_ROWS, LANES), lambda s: (0, fin(s) % n_pairs)),
                  pl.BlockSpec((1, LORA_PAD), lambda s: (0, 0)),
                  pl.BlockSpec((DECAY_LORA, LANES), weight),
                  pl.BlockSpec((ICLR_LORA, LANES), weight),
                  pl.BlockSpec((gate_w, LANES), weight)] + cast_specs,
        out_specs=[pl.BlockSpec((tb, LANES), lambda s: (fin(s) // n_pairs, fin(s) % n_pairs))] + cast_specs,
        out_shape=[jax.ShapeDtypeStruct((m, c), BF16)]
                  + [jax.ShapeDtypeStruct(w.shape, BF16) for w in cast],
        scratch_shapes=[pltpu.VMEM((n_pairs, SUBLANES, LANES), F32),
                        pltpu.VMEM((1, LORA_PAD), F32),
                        pltpu.VMEM((n_pairs, CHUNK, LANES), F32),
                        pltpu.VMEM((2, tb, DECAY_LORA), BF16),
                        pltpu.VMEM((tb, ICLR_LORA), BF16),
                        pltpu.VMEM((tb, gate_w), BF16),
                        tile3(), tile3(), tile3(), tile3(),
                        tile(),
                        pltpu.VMEM((2 * n_chunks, SUBLANES, LANES), F32),
                        mats(), mats(), mats(), mats()]
                       + [mats() for _ in CHUNK_FIELDS],
        compiler_params=_cparams(("arbitrary",)),
        name="rwkv7",
    )(p, p, p, p, chan, chan, mix_lora, w2, a2, g2, *cast)
    return outs[0], outs[1:]


def _attn_stages(sink_ref, q, kv, bias, store, n_heads):
    n_kv = n_heads // ATTN_GROUP
    lo_lane = lax.broadcasted_iota(jnp.int32, (WINDOW, LANES), 1) < HEAD
    keys = [kv[:, g * HEAD:(g + 1) * HEAD] for g in range(n_kv)]
    values = [kv[:, (n_kv + g) * HEAD:(n_kv + g + 1) * HEAD] for g in range(n_kv)]
    for g0 in range(0, n_heads, ATTN_BATCH):
        heads = range(g0, min(g0 + ATTN_BATCH, n_heads))
        s = {h: _dot_nt(q[:, h * HEAD:(h + 1) * HEAD], keys[h // ATTN_GROUP]) + bias(h) for h in heads}
        yield
        mx = {h: jnp.maximum(jnp.max(s[h], axis=-1, keepdims=True), sink_ref[h]) for h in heads}
        e = {h: jnp.exp(s[h] - mx[h]) for h in heads}
        yield
        denom = {h: jnp.sum(e[h], axis=-1, keepdims=True) + jnp.exp(sink_ref[h] - mx[h]) for h in heads}
        pv = {h: _dot(e[h].astype(BF16), values[h // ATTN_GROUP]) for h in heads}
        yield
        for h0 in heads[::2]:
            pair = (jnp.concatenate([pv[h0], pv[h0 + 1]], axis=1)
                    * jnp.where(lo_lane, 1.0 / denom[h0], 1.0 / denom[h0 + 1]))
            store(h0, pair)
        yield


def _attn_merge_kernel(sink_ref, q_ref, kvc_ref, kvp_ref,
                       yr_ref, gate_ref, x_ref, mod_ref, wbr_ref, wba_ref, wout_ref, g2_ref,
                       x1_ref, h2_ref, bias_s, ya_s, *, blocks_per_seq, n_heads, n_tiles):
    tm, d = x_ref.shape
    s = pl.program_id(0)
    first_block = (jnp.minimum(s, n_tiles - 1) * (tm // WINDOW)) % blocks_per_seq == 0

    @pl.when(s == 0)
    def _():
        ya_s[...] = jnp.zeros_like(ya_s)
        qi = lax.broadcasted_iota(jnp.int32, (WINDOW, 2 * WINDOW), 0)
        kj = lax.broadcasted_iota(jnp.int32, (WINDOW, 2 * WINDOW), 1)
        dist = qi + WINDOW - kj
        valid = (dist >= 0) & (dist < WINDOW)
        for first in (0, 1):
            neg_dist = jnp.where(valid & (kj >= WINDOW) if first == 0 else valid, -dist.astype(F32), -jnp.inf)
            for h in range(n_heads):
                bias_s[first * n_heads + h] = (2.0 ** (-8.0 * (h + 1) / n_heads)) * neg_dist

    def attention():
        kv = jnp.concatenate([kvp_ref[...], kvc_ref[...]], axis=0).astype(BF16)
        q = (q_ref[...] * (HEAD ** -0.5)).astype(BF16)
        for blk in range(tm // WINDOW):
            rows = slice(blk * WINDOW, (blk + 1) * WINDOW)
            table = jnp.where(first_block, 0, 1) * n_heads if blk == 0 else n_heads

            def store(h0, pair, rows=rows):
                ya_s[s % 2, rows, h0 * HEAD:(h0 + 2) * HEAD] = pair.astype(ya_s.dtype)

            yield from _attn_stages(sink_ref, q[rows], kv[blk * WINDOW:(blk + 2) * WINDOW],
                                    lambda h, table=table: bias_s[table + h], store, n_heads)

    def merge():
        yr, ya = yr_ref[...], ya_s[(s + 1) % 2]
        merged = []
        for c0 in range(0, d, MXU_N):
            cols, cols_a = slice(c0, c0 + MXU_N), slice(d + c0, d + c0 + MXU_N)
            merged.append((jax.nn.sigmoid(gate_ref[:, cols]) * _dot(yr, wbr_ref[:, cols])
                           + jax.nn.sigmoid(gate_ref[:, cols_a]) * _dot(ya, wba_ref[:, cols])).astype(BF16))
            yield
        merged = jnp.concatenate(merged, axis=1)
        for c0 in range(0, d, MXU_N):
            cols = slice(c0, c0 + MXU_N)
            x1_ref[:, cols] = x_ref[:, cols] + mod_ref[0, 2:3, cols] * _dot(merged, wout_ref[:, cols])
            yield
        h2 = _rms_norm(x1_ref[...], g2_ref[...]) * (1.0 + mod_ref[0, 4:5, :]) + mod_ref[0, 3:4, :]
        h2_ref[...] = h2.astype(BF16)
        yield

    _interleave(merge(), attention())


def _attn_merge(yr, p, sinks, x2, mod, wbr, wba, wout, gain2, rows_per_batch, offs, tm):
    m, d = x2.shape
    c = yr.shape[1]
    n_heads = sinks.shape[0]
    qc = n_heads * HEAD
    kvc = 2 * (n_heads // ATTN_GROUP) * HEAD
    n_tiles = m // tm
    tpb = rows_per_batch // tm
    wpt = tm // WINDOW
    att = lambda s: jnp.minimum(s, n_tiles - 1)
    mrg = lambda s: jnp.maximum(s - 1, 0)
    const = lambda shape: pl.BlockSpec(shape, lambda s: (0, 0), pipeline_mode=pl.Buffered(1))
    return pl.pallas_call(
        functools.partial(_attn_merge_kernel, blocks_per_seq=rows_per_batch // WINDOW, n_heads=n_heads,
                          n_tiles=n_tiles),
        grid=(n_tiles + 1,),
        in_specs=[pl.BlockSpec(memory_space=pltpu.SMEM),
                  pl.BlockSpec((tm, qc), lambda s: (att(s), offs["q"] // qc)),
                  pl.BlockSpec((tm, kvc), lambda s: (att(s), offs["kv"] // kvc)),
                  pl.BlockSpec((WINDOW, kvc), lambda s: (jnp.maximum(att(s) * wpt - 1, 0), offs["kv"] // kvc)),
                  pl.BlockSpec((tm, c), lambda s: (mrg(s), 0)),
                  pl.BlockSpec((tm, 2 * d), lambda s: (mrg(s), 0)),
                  pl.BlockSpec((tm, d), lambda s: (mrg(s), 0)),
                  pl.BlockSpec((1, 6, d), lambda s: (mrg(s) // tpb, 0, 0)),
                  const((c, d)), const((c, d)), const((d, d)),
                  pl.BlockSpec((1, d), lambda s: (0, 0))],
        out_specs=[pl.BlockSpec((tm, d), lambda s: (mrg(s), 0)),
                   pl.BlockSpec((tm, d), lambda s: (mrg(s), 0))],
        out_shape=[jax.ShapeDtypeStruct((m, d), F32), jax.ShapeDtypeStruct((m, d), BF16)],
        scratch_shapes=[pltpu.VMEM((2 * n_heads, WINDOW, 2 * WINDOW), F32),
                        pltpu.VMEM((2, tm, qc), BF16)],
        compiler_params=_cparams(("arbitrary",)),
        name="attn_merge",
    )(sinks, p, p, p, yr, p, x2, mod, wbr, wba, wout, gain2)


def _mlp_kernel(h_ref, wu_ref, wd_ref, x1_ref, mod_ref, fg_ref, o_ref, acc_ref):
    f = pl.program_id(1)

    @pl.when(f == 0)
    def _():
        acc_ref[...] = jnp.zeros_like(acc_ref)

    u = _dot(h_ref[...], wu_ref[...])
    act = jnp.square(jnp.maximum(u, 0.0)).astype(BF16)
    acc_ref[...] += _dot(act, wd_ref[...])

    @pl.when(f == pl.num_programs(1) - 1)
    def _():
        x2 = x1_ref[...] + mod_ref[0, 5:6, :] * acc_ref[...]
        o_ref[...] = _rms_norm(x2, fg_ref[...])


def _mlp(h2, wu, wd, x1, mod, final_gain, rows_per_batch, tm, tf):
    m, d = h2.shape
    f = wu.shape[1]
    tpb = rows_per_batch // tm
    return pl.pallas_call(
        _mlp_kernel,
        grid=(m // tm, f // tf),
        in_specs=[pl.BlockSpec((tm, d), lambda i, j: (i, 0)),
                  pl.BlockSpec((d, tf), lambda i, j: (0, j)),
                  pl.BlockSpec((tf, d), lambda i, j: (j, 0)),
                  pl.BlockSpec((tm, d), lambda i, j: (i, 0)),
                  pl.BlockSpec((1, 6, d), lambda i, j: (i // tpb, 0, 0)),
                  pl.BlockSpec((1, d), lambda i, j: (0, 0))],
        out_specs=pl.BlockSpec((tm, d), lambda i, j: (i, 0)),
        out_shape=jax.ShapeDtypeStruct((m, d), F32),
        scratch_shapes=[pltpu.VMEM((tm, d), F32)],
        compiler_params=_cparams(("arbitrary", "arbitrary")),
        name="mlp",
    )(h2, wu, wd, x1, mod, final_gain)


def _forward(x, c, w_ada, b_ada, norm1_gain, w_in, b_in, rwkv_mix, rwkv_w0, rwkv_w2, rwkv_a0,
             rwkv_a2, rwkv_g2, rwkv_k_k, rwkv_k_a, rwkv_r_k, rwkv_ln_w, rwkv_ln_b, attn_sinks,
             w_branch_rwkv, w_branch_attn, w_out, norm2_gain, w_up, w_down, final_gain):
    batch, seq, d = x.shape
    depth = w_ada.shape[0]
    c_rwkv = rwkv_w0.shape[1]
    n_heads = attn_sinks.shape[1]
    qc = n_heads * HEAD
    kvc = 2 * (n_heads // ATTN_GROUP) * HEAD
    lora_w = DECAY_LORA + ICLR_LORA + GATE_LORA
    rwkv_cols = 3 * c_rwkv + lora_w
    attn_cols = qc + kvc
    offs = {"gate": 0, "r": 2 * d, "k": 2 * d + c_rwkv, "v": 2 * d + 2 * c_rwkv, "q": 2 * d + 3 * c_rwkv}
    offs["lora"] = offs["q"] + qc
    offs["kv"] = offs["lora"] + LORA_PAD
    n_pack = offs["kv"] + kvc

    segments = ((rwkv_cols + attn_cols, 2 * d, offs["gate"]), (0, 3 * c_rwkv, offs["r"]),
                (rwkv_cols, qc, offs["q"]), (3 * c_rwkv, lora_w, offs["lora"]),
                (rwkv_cols + qc, kvc, offs["kv"]))
    zero_rows = (offs["lora"] + lora_w, offs["kv"])

    def pack_cols(t):
        out = jnp.zeros(t.shape[:-1] + (n_pack,), t.dtype)
        for src, width, dst in segments:
            out = out.at[..., dst:dst + width].set(t[..., src:src + width])
        return out

    tm_in = min(1024, seq)
    tn_in = 1280 if n_pack % 1280 == 0 else LANES
    tb = min(1024, seq)
    tm_merge = min(256, seq)
    tm_mlp = min(512, seq)
    tf = min(1024, w_up.shape[2])

    assert depth == 1, "single-layer block"
    layer = lambda t: t.reshape(t.shape[1:])
    x2 = x.reshape(batch * seq, d)
    for l in range(depth):
        mod = _ada(c, layer(w_ada), b_ada[l]).reshape(batch, 6, d)
        w_pack = _pack_weight_t(layer(w_in).T, segments, zero_rows, n_pack)
        b_pack = pack_cols(b_in[l]).reshape(1, n_pack)
        p = _inproj(x2, mod, norm1_gain[l].reshape(1, d), w_pack, b_pack, seq, tm_in, tn_in)

        mix = rwkv_mix[l]
        chan = jnp.stack([mix[:c_rwkv], mix[c_rwkv:2 * c_rwkv], mix[2 * c_rwkv:3 * c_rwkv],
                          rwkv_w0[l], rwkv_a0[l], rwkv_k_k[l], rwkv_k_a[l], rwkv_r_k[l].reshape(c_rwkv),
                          rwkv_ln_w[l], rwkv_ln_b[l]])
        chan = jnp.pad(chan, ((0, CHAN_ROWS - chan.shape[0]), (0, 0)))
        mix_lora = jnp.pad(mix[3 * c_rwkv:], (0, LORA_PAD - lora_w)).reshape(1, LORA_PAD)
        g2_pad = jnp.pad(rwkv_g2[l], ((0, LORA_PAD - lora_w), (0, 0)))
        y_rwkv, (wbr, wba, wout, wup, wdown) = _rwkv(
            p, chan, mix_lora, rwkv_w2[l], rwkv_a2[l], g2_pad, batch, seq, offs, tb,
            cast=[layer(w_branch_rwkv), layer(w_branch_attn), layer(w_out), layer(w_up), layer(w_down)])
        x2, h2 = _attn_merge(y_rwkv, p, attn_sinks[l], x2, mod, wbr, wba, wout,
                             norm2_gain[l].reshape(1, d), seq, offs, tm_merge)
        x2 = _mlp(h2, wup, wdown, x2, mod, final_gain.reshape(1, d), seq, tm_mlp, tf)
    return x2.reshape(batch, seq, d)


def kernel(x, c, w_ada, b_ada, norm1_gain, w_in, b_in, rwkv_mix, rwkv_w0, rwkv_w2, rwkv_a0, rwkv_a2, rwkv_g2, rwkv_k_k, rwkv_k_a, rwkv_r_k, rwkv_ln_w, rwkv_ln_b, attn_sinks, w_branch_rwkv, w_branch_attn, w_out, norm2_gain, w_up, w_down, final_gain):
    return _forward(x, c, w_ada, b_ada, norm1_gain, w_in, b_in, rwkv_mix, rwkv_w0, rwkv_w2, rwkv_a0,
                    rwkv_a2, rwkv_g2, rwkv_k_k, rwkv_k_a, rwkv_r_k, rwkv_ln_w, rwkv_ln_b, attn_sinks,
                    w_branch_rwkv, w_branch_attn, w_out, norm2_gain, w_up, w_down, final_gain)
```

```python
import functools

import jax
import jax.numpy as jnp
from jax import lax
from jax.experimental import pallas as pl
from jax.experimental.pallas import tpu as pltpu

F32 = jnp.float32
BF16 = jnp.bfloat16

LANES = 128
SUBLANES = 8
BF16_ROWS = 16
MXU_N = 256
NORM_ROWS = 128
HEAD = 64
CHUNK = 128
WINDOW = 128
ATTN_GROUP = 8
ATTN_BATCH = 8
NORM_EPS = 1e-6
GN_EPS = 64e-5
DECAY_SCALE = 0.6065306597126334
KK_EPS = 1e-12
CHAN_ROWS = 16
DECAY_LORA = 64
ICLR_LORA = 64
GATE_LORA = 160
LORA_PAD = 512
VMEM_LIMIT = 56 * 1024 * 1024


def _cparams(sem):
    return pltpu.CompilerParams(dimension_semantics=sem, vmem_limit_bytes=VMEM_LIMIT)


_NN = (((1,), (0,)), ((), ()))
_NT = (((1,), (1,)), ((), ()))
_TN = (((0,), (0,)), ((), ()))


def _dot(a, b, dims=_NN):
    return lax.dot_general(a, b, dims, preferred_element_type=F32)


def _dot_nt(a, b):
    return _dot(a, b, _NT)


def _rms_norm(x, gain):
    ms = jnp.mean(x * x, axis=-1, keepdims=True)
    return x * lax.rsqrt(ms + NORM_EPS) * gain


def _ada_kernel(ct_ref, w_ref, b_ref, o_ref):
    ct = ct_ref[...]
    act = ct * jax.nn.sigmoid(ct)
    for m in range(o_ref.shape[0]):
        o_ref[m:m + 1, :] = jnp.sum(w_ref[...] * act[:, m:m + 1], axis=0, keepdims=True) + b_ref[...]


def _ada(c, w, b, tn=1024):
    nb, d = c.shape
    n = w.shape[1]
    return pl.pallas_call(
        _ada_kernel,
        grid=(n // tn,),
        in_specs=[pl.BlockSpec((d, nb), lambda j: (0, 0)),
                  pl.BlockSpec((d, tn), lambda j: (0, j)),
                  pl.BlockSpec((1, tn), lambda j: (0, j))],
        out_specs=pl.BlockSpec((nb, tn), lambda j: (0, j)),
        out_shape=jax.ShapeDtypeStruct((nb, n), F32),
        compiler_params=_cparams(("arbitrary",)),
        name="adaln",
    )(c.T, w, b.reshape(1, n))


def _pack_kernel(w_ref, o_ref, *, segments, zero_rows):
    for src, width, dst in segments:
        o_ref[dst:dst + width, :] = w_ref[src:src + width, :].astype(o_ref.dtype)
    lo, hi = zero_rows
    o_ref[lo:hi, :] = jnp.zeros((hi - lo, o_ref.shape[1]), o_ref.dtype)


def _pack_weight_t(w_t, segments, zero_rows, n_pack, tk=256):
    n, k = w_t.shape
    return pl.pallas_call(
        functools.partial(_pack_kernel, segments=segments, zero_rows=zero_rows),
        grid=(k // tk,),
        in_specs=[pl.BlockSpec((n, tk), lambda i: (0, i))],
        out_specs=pl.BlockSpec((n_pack, tk), lambda i: (0, i)),
        out_shape=jax.ShapeDtypeStruct((n_pack, k), BF16),
        compiler_params=_cparams(("arbitrary",)),
        name="pack_w_in",
    )(w_t)


def _inproj_kernel(x_ref, mod_ref, g_ref, w_ref, b_ref, o_ref, h_ref):
    @pl.when(pl.program_id(1) == 0)
    def _():
        scale = g_ref[...] * (1.0 + mod_ref[0, 1:2, :])
        shift = mod_ref[0, 0:1, :]

        def slab(c, carry):
            rows = pl.ds(pl.multiple_of(c * NORM_ROWS, NORM_ROWS), NORM_ROWS)
            x = x_ref[rows, :]
            ms = jnp.mean(x * x, axis=-1, keepdims=True)
            h_ref[rows, :] = (x * lax.rsqrt(ms + NORM_EPS) * scale + shift).astype(BF16)
            return carry

        lax.fori_loop(0, x_ref.shape[0] // NORM_ROWS, slab, 0)

    o_ref[...] = _dot_nt(h_ref[...], w_ref[...]) + b_ref[...]


def _inproj(x2, mod, gain, w_t, b, rows_per_batch, tm, tn):
    m, d = x2.shape
    n = w_t.shape[0]
    tpb = rows_per_batch // tm
    return pl.pallas_call(
        _inproj_kernel,
        grid=(m // tm, n // tn),
        in_specs=[pl.BlockSpec((tm, d), lambda i, j: (i, 0)),
                  pl.BlockSpec((1, 6, d), lambda i, j: (i // tpb, 0, 0)),
                  pl.BlockSpec((1, d), lambda i, j: (0, 0)),
                  pl.BlockSpec((tn, d), lambda i, j: (j, 0)),
                  pl.BlockSpec((1, tn), lambda i, j: (0, j))],
        out_specs=pl.BlockSpec((tm, tn), lambda i, j: (i, j)),
        out_shape=jax.ShapeDtypeStruct((m, n), F32),
        scratch_shapes=[pltpu.VMEM((tm, d), BF16)],
        compiler_params=_cparams(("arbitrary", "arbitrary")),
        name="inproj",
    )(x2, mod, gain, w_t, b)


def _head_sum(x):
    lo = lax.broadcasted_iota(jnp.int32, x.shape, 1) < HEAD
    s0 = jnp.sum(jnp.where(lo, x, 0.0), axis=-1, keepdims=True)
    s1 = jnp.sum(jnp.where(lo, 0.0, x), axis=-1, keepdims=True)
    return jnp.where(lo, s0, s1)


def _token_shift(x, carry_row, mix):
    rolled = pltpu.roll(x, 1, axis=0)
    head = rolled[:SUBLANES]
    first = lax.broadcasted_iota(jnp.int32, head.shape, 0) == 0
    prev = jnp.concatenate([jnp.where(first, carry_row, head), rolled[SUBLANES:]], axis=0)
    return x + (prev - x) * mix


class _Operands:
    def __init__(self):
        self._parts = {}

    def parts(self, x, n):
        ent = self._parts.setdefault(id(x), [x])
        while len(ent) - 1 < n:
            rest = x
            for piece in ent[1:]:
                rest = rest - piece.astype(F32)
            ent.append(rest.astype(BF16))
        return ent[1:n + 1]

    def mm(self, a, b, dims=_NN):
        return _dot(self.parts(a, 1)[0], self.parts(b, 1)[0], dims)


CHUNK_FIELDS = ("ra", "aa", "bi", "ki", "bh", "kh", "vsw")


def _decay_chunk(ops, rc, lwc, kc, vsw, ac, bc):
    incl = (lax.broadcasted_iota(jnp.int32, (CHUNK, CHUNK), 0)
            >= lax.broadcasted_iota(jnp.int32, (CHUNK, CHUNK), 1))
    tri = jnp.where(incl, 1.0, 0.0).astype(BF16)
    lw = sum(jnp.dot(tri, piece, preferred_element_type=F32) for piece in ops.parts(lwc, 2))
    lw_end = lw[CHUNK - 1:CHUNK, :]
    lw_mid = lw[CHUNK // 2 - 1:CHUNK // 2, :]
    e_in = jnp.exp(lw)
    e_ex = jnp.exp(lw - lwc)
    e_inv = jnp.exp(lw_mid - lw)
    e_end = jnp.exp(lw_end - lw)
    return dict(vsw=vsw, g_end=jnp.exp(lw_end), e_mid=jnp.exp(-lw_mid), ra=rc * e_in, aa=ac * e_ex,
                bi=bc * e_inv, ki=kc * e_inv, bh=bc * e_end, kh=kc * e_end)


def _scan_matrices(ops, chunks):
    row = lax.broadcasted_iota(jnp.int32, (CHUNK, CHUNK), 0)
    col = lax.broadcasted_iota(jnp.int32, (CHUNK, CHUNK), 1)
    incl = row >= col
    strict = row > col
    eye = row == col
    lo_lane = col < HEAD
    same_half = lo_lane == (row < HEAD)
    lo_lane2 = lax.broadcasted_iota(jnp.int32, (2 * CHUNK, LANES), 1) < HEAD
    level_masks = []
    n = 1
    while n < CHUNK:
        level_masks.append((row // (2 * n) == col // (2 * n)) & ((row // n) % 2 == 1) & ((col // n) % 2 == 0))
        n *= 2

    probs = []
    for ch in chunks:
        lhs_all = jnp.concatenate([ch["aa"], ch["ra"]], axis=0) * ch["e_mid"]
        rhs_all = jnp.concatenate([ch["bi"], ch["ki"]], axis=0)
        for h in range(LANES // HEAD):
            own = lo_lane if h == 0 else ~lo_lane
            own2 = lo_lane2 if h == 0 else ~lo_lane2
            gram = ops.mm(jnp.where(own2, lhs_all, 0.0), rhs_all, _NT)
            probs.append(dict(ch=ch, own=own,
                              m_ab=jnp.where(strict, gram[:CHUNK, :CHUNK], 0.0),
                              m_ak=jnp.where(strict, gram[:CHUNK, CHUNK:], 0.0),
                              m_rb=jnp.where(incl, gram[CHUNK:, :CHUNK], 0.0),
                              m_rk=jnp.where(incl, gram[CHUNK:, CHUNK:], 0.0)))
    yield

    for p in probs:
        p["t"] = jnp.where(eye, 1.0, 0.0) + jnp.where(level_masks[0], p["m_ab"], 0.0)
    n = 2
    for mask in level_masks[1:]:
        if n < SUBLANES:
            for p in probs:
                p["to"] = ops.mm(p["t"], jnp.where(mask, p["m_ab"], 0.0))
            for p in probs:
                p["t"] = p["t"] + ops.mm(p["to"], p["t"])
        else:
            lower = lambda t, n=n: jnp.concatenate(
                [t[i:i + n] for i in range(n, CHUNK, 2 * n)], axis=0)
            for p in probs:
                p["tl"] = lower(p["t"])
                p["to"] = ops.mm(p["tl"], jnp.where(mask, p["m_ab"], 0.0))
            for p in probs:
                tl = p["tl"] + ops.mm(p["to"], p["t"])
                slabs = []
                for j, i in enumerate(range(0, CHUNK, 2 * n)):
                    slabs += [p["t"][i:i + n], tl[j * n:(j + 1) * n]]
                p["t"] = jnp.concatenate(slabs, axis=0)
        n *= 2
        yield

    for p in probs:
        p["vm"] = jnp.where(p["own"], 0.0, p["ch"]["vsw"])
        p["x"] = jnp.where(p["own"], p["ch"]["aa"], 0.0) + ops.mm(p["m_ak"], p["vm"])
    yield
    for p in probs:
        p["tx"] = ops.mm(p["t"], p["x"])
    yield
    for p in probs:
        p["y"] = jnp.where(p["own"], p["ch"]["ra"], 0.0) + ops.mm(
            jnp.concatenate([p["m_rb"], p["m_rk"]], axis=1),
            jnp.concatenate([p["tx"], p["vm"]], axis=0))
    yield

    for i, ch in enumerate(chunks):
        p0, p1 = probs[2 * i], probs[2 * i + 1]
        pq = ops.mm(jnp.concatenate([ch["bh"], ch["kh"]], axis=0),
                    jnp.concatenate([jnp.concatenate([p0["tx"], p1["tx"]], axis=1),
                                     jnp.concatenate([p0["vm"], p1["vm"]], axis=1)], axis=0),
                    _TN)
        pq = jnp.where(row < HEAD, pq[:, :LANES], pq[:, LANES:])
        ch["p"] = jnp.where(same_half, pq, 0.0) + jnp.where(eye, ch["g_end"], 0.0)
        ch["q"] = jnp.where(same_half, 0.0, pq)
        ch["r"] = jnp.where(lo_lane, p0["y"], p1["y"])
        ch["y0"] = jnp.where(lo_lane, p1["y"], p0["y"])


def _interleave(main, *side):
    for _ in main:
        for gen in side:
            next(gen, None)
    for gen in side:
        for _ in gen:
            pass


def _rwkv_kernel(*refs, n_pairs, n_tblocks, n_tiles, cast_steps):
    n_cast = len(cast_steps)
    (r_ref, k_ref, v_ref, lora_ref, chan_ref, chanf_ref, mixl_ref, w2_ref, a2_ref, g2_ref) = refs[:10]
    cast_in, o_ref, cast_out = refs[10:10 + n_cast], refs[10 + n_cast], refs[11 + n_cast:11 + 2 * n_cast]
    (crkv_ref, clora_ref, z_ref, twd_s, ad_s, sgd_s, r3_s, k3_s, v3_s, g3_s,
     y_s, gend_s, p_s, q_s, rr_s, y0_s, *decay_s) = refs[11 + 2 * n_cast:]

    tb = r_ref.shape[0]
    n_chunks = tb // CHUNK
    s = pl.program_id(0)
    sa = jnp.minimum(s, n_tiles - 1)
    sf = jnp.maximum(s - 2, 0)
    hp, t_a = sa % n_pairs, (sa // n_pairs) % n_tblocks
    hpf, t_f = sf % n_pairs, (sf // n_pairs) % n_tblocks
    slot_w, slot_r = s % 2, (s + 1) % 2
    keep_w, keep_r = s % 3, (s + 1) % 3

    @pl.when(s == 0)
    def _():
        for ref in (crkv_ref, clora_ref, z_ref, r3_s, k3_s, v3_s, g3_s, gend_s, p_s, q_s, rr_s, y0_s,
                    *decay_s):
            ref[...] = jnp.zeros_like(ref)

    @pl.when(hp == 0)
    def _():
        lora_in = lora_ref[...]
        carry = jnp.where(t_a == 0, 0.0, clora_ref[...])
        lora = _token_shift(lora_in, carry, mixl_ref[...])
        clora_ref[...] = lora_in[tb - 1:tb, :]
        twd = jnp.tanh(lora[:, 0:DECAY_LORA])
        twd_hi = twd.astype(BF16)
        twd_s[0] = twd_hi
        twd_s[1] = (twd - twd_hi.astype(F32)).astype(BF16)
        ad_s[...] = lora[:, DECAY_LORA:DECAY_LORA + ICLR_LORA].astype(BF16)
        sgd_s[...] = jax.nn.sigmoid(lora[:, DECAY_LORA + ICLR_LORA:]).astype(BF16)

    ops = _Operands()

    def finish():
        chanf = chanf_ref[...]
        z = jnp.where(t_f == 0, 0.0, z_ref[hpf])
        for c in range(n_chunks):
            i = slot_r * n_chunks + c
            rz = ops.mm(jnp.concatenate([rr_s[i], p_s[i]], axis=0), z)
            y_s[c * CHUNK:(c + 1) * CHUNK, :] = rz[:CHUNK] + y0_s[i]
            z = rz[CHUNK:] + q_s[i]
            yield
        z_ref[hpf] = z
        r_k, ln_w, ln_b = chanf[7:8], chanf[8:9], chanf[9:10]
        y = pltpu.roll(y_s[...], HEAD, axis=1)
        mu = _head_sum(y) * (1.0 / HEAD)
        yc = y - mu
        var = _head_sum(yc * yc) * (1.0 / HEAD)
        yn = yc * lax.rsqrt(var + GN_EPS) * ln_w + ln_b
        bonus = _head_sum(r3_s[keep_r] * k3_s[keep_r] * r_k) * v3_s[keep_r]
        o_ref[...] = ((yn + bonus) * g3_s[keep_r]).astype(o_ref.dtype)
        yield

    def matrices():
        chunks = []
        for c in range(n_chunks):
            i = slot_r * n_chunks + c
            ch = {name: ref[i] for name, ref in zip(CHUNK_FIELDS, decay_s)}
            ch["g_end"], ch["e_mid"] = gend_s[i, 0:1, :], gend_s[i, 1:2, :]
            chunks.append(ch)
        yield from _scan_matrices(ops, chunks)
        for c, ch in enumerate(chunks):
            i = slot_w * n_chunks + c
            p_s[i], q_s[i], rr_s[i], y0_s[i] = ch["p"], ch["q"], ch["r"], ch["y0"]
        yield

    def prepare():
        chan = chan_ref[...]
        mix_r, mix_k, mix_v = chan[0:1], chan[1:2], chan[2:3]
        w0, a0, k_k, k_a = chan[3:4], chan[4:5], chan[5:6], chan[6:7]
        r_in, k_in, v_in = r_ref[...], k_ref[...], v_ref[...]
        carry = jnp.where(t_a == 0, 0.0, crkv_ref[hp])
        r = _token_shift(r_in, carry[0:1], mix_r)
        k = _token_shift(k_in, carry[1:2], mix_k)
        v = _token_shift(v_in, carry[2:3], mix_v)
        crkv_ref[hp, 0:1, :] = r_in[tb - 1:tb, :]
        crkv_ref[hp, 1:2, :] = k_in[tb - 1:tb, :]
        crkv_ref[hp, 2:3, :] = v_in[tb - 1:tb, :]
        yield
        w2_hi = w2_ref[...].astype(BF16)
        w2_lo = (w2_ref[...] - w2_hi.astype(F32)).astype(BF16)
        twd_hi, twd_lo = twd_s[0], twd_s[1]
        dw = w0 + (_dot(twd_hi, w2_hi) + (_dot(twd_hi, w2_lo) + _dot(twd_lo, w2_hi)))
        logw = -DECAY_SCALE * jax.nn.sigmoid(dw)
        alr = jax.nn.sigmoid(a0 + _dot(ad_s[...], a2_ref[...].astype(BF16)))
        g3_s[keep_w] = _dot(sgd_s[...], g2_ref[...].astype(BF16))
        yield
        kk = k * k_k
        kk = kk * lax.rsqrt(jnp.maximum(_head_sum(kk * kk), KK_EPS * KK_EPS))
        k_mod = k * (1.0 + (alr - 1.0) * k_a)
        vsw = pltpu.roll(v, HEAD, axis=1)
        a, b = -kk, kk * alr
        r3_s[keep_w], k3_s[keep_w], v3_s[keep_w] = r, k_mod, v
        yield
        for c in range(n_chunks):
            sl = slice(c * CHUNK, (c + 1) * CHUNK)
            ch = _decay_chunk(ops, r[sl], logw[sl], k_mod[sl], vsw[sl], a[sl], b[sl])
            i = slot_w * n_chunks + c
            for name, ref in zip(CHUNK_FIELDS, decay_s):
                ref[i] = ch[name]
            gend_s[i, 0:1, :] = ch["g_end"]
            gend_s[i, 1:2, :] = ch["e_mid"]
            yield

    def casts():
        for w_ref, wb_ref in zip(cast_in, cast_out):
            wb_ref[...] = w_ref[...].astype(wb_ref.dtype)
            yield

    _interleave(matrices(), finish(), prepare(), casts())


def _rwkv(p, chan, mix_lora, w2, a2, g2, batch, seq, offs, tb, cast):
    m = p.shape[0]
    c = w2.shape[1]
    n_pairs = c // LANES
    n_tblocks = seq // tb
    n_tiles = batch * n_tblocks * n_pairs
    n_chunks = tb // CHUNK
    gate_w = LORA_PAD - DECAY_LORA - ICLR_LORA
    prep = lambda s: jnp.minimum(s, n_tiles - 1)
    fin = lambda s: jnp.maximum(s - 2, 0)
    col = lambda name: lambda s: (prep(s) // n_pairs, offs[name] // LANES + prep(s) % n_pairs)
    weight = lambda s: (0, prep(s) % n_pairs)
    tile = lambda: pltpu.VMEM((tb, LANES), F32)
    tile3 = lambda: pltpu.VMEM((3, tb, LANES), F32)
    mats = lambda: pltpu.VMEM((2 * n_chunks, CHUNK, LANES), F32)
    cast_steps, cast_specs = [], []
    for w in cast:
        steps = min(n_tiles, w.shape[0] // BF16_ROWS)
        assert w.shape[0] % (steps * BF16_ROWS) == 0, w.shape
        cast_steps.append(steps)
        cast_specs.append(pl.BlockSpec((w.shape[0] // steps, w.shape[1]),
                                       lambda s, steps=steps: (jnp.minimum(s, steps - 1), 0)))
    outs = pl.pallas_call(
        functools.partial(_rwkv_kernel, n_pairs=n_pairs, n_tblocks=n_tblocks, n_tiles=n_tiles,
                          cast_steps=tuple(cast_steps)),
        grid=(n_tiles + 2,),
        in_specs=[pl.BlockSpec((tb, LANES), col("r")),
                  pl.BlockSpec((tb, LANES), col("k")),
                  pl.BlockSpec((tb, LANES), col("v")),
                  pl.BlockSpec((tb, LORA_PAD), lambda s: (prep(s) // n_pairs, offs["lora"] // LORA_PAD)),
                  pl.BlockSpec((CHAN_ROWS, LANES), weight),
                  pl.BlockSpec((CHAN_ROWS, LANES), lambda s: (0, fin(s) % n_pairs)),
                  pl.BlockSpec((1, LORA_PAD), lambda s: (0, 0)),
                  pl.BlockSpec((DECAY_LORA, LANES), weight),
                  pl.BlockSpec((ICLR_LORA, LANES), weight),
                  pl.BlockSpec((gate_w, LANES), weight)] + cast_specs,
        out_specs=[pl.BlockSpec((tb, LANES), lambda s: (fin(s) // n_pairs, fin(s) % n_pairs))] + cast_specs,
        out_shape=[jax.ShapeDtypeStruct((m, c), BF16)]
                  + [jax.ShapeDtypeStruct(w.shape, BF16) for w in cast],
        scratch_shapes=[pltpu.VMEM((n_pairs, SUBLANES, LANES), F32),
                        pltpu.VMEM((1, LORA_PAD), F32),
                        pltpu.VMEM((n_pairs, CHUNK, LANES), F32),
                        pltpu.VMEM((2, tb, DECAY_LORA), BF16),
                        pltpu.VMEM((tb, ICLR_LORA), BF16),
                        pltpu.VMEM((tb, gate_w), BF16),
                        tile3(), tile3(), tile3(), tile3(),
                        tile(),
                        pltpu.VMEM((2 * n_chunks, SUBLANES, LANES), F32),
                        mats(), mats(), mats(), mats()]
                       + [mats() for _ in CHUNK_FIELDS],
        compiler_params=_cparams(("arbitrary",)),
        name="rwkv7",
    )(p, p, p, p, chan, chan, mix_lora, w2, a2, g2, *cast)
    return outs[0], outs[1:]


def _attn_stages(sink_ref, q, kv, bias, store, n_heads):
    n_kv = n_heads // ATTN_GROUP
    lo_lane = lax.broadcasted_iota(jnp.int32, (WINDOW, LANES), 1) < HEAD
    keys = [kv[:, g * HEAD:(g + 1) * HEAD] for g in range(n_kv)]
    values = [kv[:, (n_kv + g) * HEAD:(n_kv + g + 1) * HEAD] for g in range(n_kv)]
    for g0 in range(0, n_heads, ATTN_BATCH):
        heads = range(g0, min(g0 + ATTN_BATCH, n_heads))
        s = {h: _dot_nt(q[:, h * HEAD:(h + 1) * HEAD], keys[h // ATTN_GROUP]) + bias(h) for h in heads}
        yield
        mx = {h: jnp.maximum(jnp.max(s[h], axis=-1, keepdims=True), sink_ref[h]) for h in heads}
        e = {h: jnp.exp(s[h] - mx[h]) for h in heads}
        yield
        denom = {h: jnp.sum(e[h], axis=-1, keepdims=True) + jnp.exp(sink_ref[h] - mx[h]) for h in heads}
        pv = {h: _dot(e[h].astype(BF16), values[h // ATTN_GROUP]) for h in heads}
        yield
        for h0 in heads[::2]:
            pair = (jnp.concatenate([pv[h0], pv[h0 + 1]], axis=1)
                    * jnp.where(lo_lane, 1.0 / denom[h0], 1.0 / denom[h0 + 1]))
            store(h0, pair)
        yield


def _attn_merge_kernel(sink_ref, q_ref, kvc_ref, kvp_ref,
                       yr_ref, gate_ref, x_ref, mod_ref, wbr_ref, wba_ref, wout_ref, g2_ref,
                       x1_ref, h2_ref, bias_s, ya_s, *, blocks_per_seq, n_heads, n_tiles):
    tm, d = x_ref.shape
    s = pl.program_id(0)
    first_block = (jnp.minimum(s, n_tiles - 1) * (tm // WINDOW)) % blocks_per_seq == 0

    @pl.when(s == 0)
    def _():
        ya_s[...] = jnp.zeros_like(ya_s)
        qi = lax.broadcasted_iota(jnp.int32, (WINDOW, 2 * WINDOW), 0)
        kj = lax.broadcasted_iota(jnp.int32, (WINDOW, 2 * WINDOW), 1)
        dist = qi + WINDOW - kj
        valid = (dist >= 0) & (dist < WINDOW)
        for first in (0, 1):
            neg_dist = jnp.where(valid & (kj >= WINDOW) if first == 0 else valid, -dist.astype(F32), -jnp.inf)
            for h in range(n_heads):
                bias_s[first * n_heads + h] = (2.0 ** (-8.0 * (h + 1) / n_heads)) * neg_dist

    def attention():
        kv = jnp.concatenate([kvp_ref[...], kvc_ref[...]], axis=0).astype(BF16)
        q = (q_ref[...] * (HEAD ** -0.5)).astype(BF16)
        for blk in range(tm // WINDOW):
            rows = slice(blk * WINDOW, (blk + 1) * WINDOW)
            table = jnp.where(first_block, 0, 1) * n_heads if blk == 0 else n_heads

            def store(h0, pair, rows=rows):
                ya_s[s % 2, rows, h0 * HEAD:(h0 + 2) * HEAD] = pair.astype(ya_s.dtype)

            yield from _attn_stages(sink_ref, q[rows], kv[blk * WINDOW:(blk + 2) * WINDOW],
                                    lambda h, table=table: bias_s[table + h], store, n_heads)

    def merge():
        yr, ya = yr_ref[...], ya_s[(s + 1) % 2]
        merged = []
        for c0 in range(0, d, MXU_N):
            cols, cols_a = slice(c0, c0 + MXU_N), slice(d + c0, d + c0 + MXU_N)
            merged.append((jax.nn.sigmoid(gate_ref[:, cols]) * _dot(yr, wbr_ref[:, cols])
                           + jax.nn.sigmoid(gate_ref[:, cols_a]) * _dot(ya, wba_ref[:, cols])).astype(BF16))
            yield
        merged = jnp.concatenate(merged, axis=1)
        for c0 in range(0, d, MXU_N):
            cols = slice(c0, c0 + MXU_N)
            x1_ref[:, cols] = x_ref[:, cols] + mod_ref[0, 2:3, cols] * _dot(merged, wout_ref[:, cols])
            yield
        h2 = _rms_norm(x1_ref[...], g2_ref[...]) * (1.0 + mod_ref[0, 4:5, :]) + mod_ref[0, 3:4, :]
        h2_ref[...] = h2.astype(BF16)
        yield

    _interleave(merge(), attention())


def _attn_merge(yr, p, sinks, x2, mod, wbr, wba, wout, gain2, rows_per_batch, offs, tm):
    m, d = x2.shape
    c = yr.shape[1]
    n_heads = sinks.shape[0]
    qc = n_heads * HEAD
    kvc = 2 * (n_heads // ATTN_GROUP) * HEAD
    n_tiles = m // tm
    tpb = rows_per_batch // tm
    wpt = tm // WINDOW
    att = lambda s: jnp.minimum(s, n_tiles - 1)
    mrg = lambda s: jnp.maximum(s - 1, 0)
    const = lambda shape: pl.BlockSpec(shape, lambda s: (0, 0), pipeline_mode=pl.Buffered(1))
    return pl.pallas_call(
        functools.partial(_attn_merge_kernel, blocks_per_seq=rows_per_batch // WINDOW, n_heads=n_heads,
                          n_tiles=n_tiles),
        grid=(n_tiles + 1,),
        in_specs=[pl.BlockSpec(memory_space=pltpu.SMEM),
                  pl.BlockSpec((tm, qc), lambda s: (att(s), offs["q"] // qc)),
                  pl.BlockSpec((tm, kvc), lambda s: (att(s), offs["kv"] // kvc)),
                  pl.BlockSpec((WINDOW, kvc), lambda s: (jnp.maximum(att(s) * wpt - 1, 0), offs["kv"] // kvc)),
                  pl.BlockSpec((tm, c), lambda s: (mrg(s), 0)),
                  pl.BlockSpec((tm, 2 * d), lambda s: (mrg(s), 0)),
                  pl.BlockSpec((tm, d), lambda s: (mrg(s), 0)),
                  pl.BlockSpec((1, 6, d), lambda s: (mrg(s) // tpb, 0, 0)),
                  const((c, d)), const((c, d)), const((d, d)),
                  pl.BlockSpec((1, d), lambda s: (0, 0))],
        out_specs=[pl.BlockSpec((tm, d), lambda s: (mrg(s), 0)),
                   pl.BlockSpec((tm, d), lambda s: (mrg(s), 0))],
        out_shape=[jax.ShapeDtypeStruct((m, d), F32), jax.ShapeDtypeStruct((m, d), BF16)],
        scratch_shapes=[pltpu.VMEM((2 * n_heads, WINDOW, 2 * WINDOW), F32),
                        pltpu.VMEM((2, tm, qc), BF16)],
        compiler_params=_cparams(("arbitrary",)),
        name="attn_merge",
    )(sinks, p, p, p, yr, p, x2, mod, wbr, wba, wout, gain2)


def _mlp_kernel(h_ref, wu_ref, wd_ref, x1_ref, mod_ref, fg_ref, o_ref, acc_ref):
    f = pl.program_id(1)

    @pl.when(f == 0)
    def _():
        acc_ref[...] = jnp.zeros_like(acc_ref)

    u = _dot(h_ref[...], wu_ref[...])
    act = jnp.square(jnp.maximum(u, 0.0)).astype(BF16)
    acc_ref[...] += _dot(act, wd_ref[...])

    @pl.when(f == pl.num_programs(1) - 1)
    def _():
        x2 = x1_ref[...] + mod_ref[0, 5:6, :] * acc_ref[...]
        o_ref[...] = _rms_norm(x2, fg_ref[...])


def _mlp(h2, wu, wd, x1, mod, final_gain, rows_per_batch, tm, tf):
    m, d = h2.shape
    f = wu.shape[1]
    tpb = rows_per_batch // tm
    return pl.pallas_call(
        _mlp_kernel,
        grid=(m // tm, f // tf),
        in_specs=[pl.BlockSpec((tm, d), lambda i, j: (i, 0)),
                  pl.BlockSpec((d, tf), lambda i, j: (0, j)),
                  pl.BlockSpec((tf, d), lambda i, j: (j, 0)),
                  pl.BlockSpec((tm, d), lambda i, j: (i, 0)),
                  pl.BlockSpec((1, 6, d), lambda i, j: (i // tpb, 0, 0)),
                  pl.BlockSpec((1, d), lambda i, j: (0, 0))],
        out_specs=pl.BlockSpec((tm, d), lambda i, j: (i, 0)),
        out_shape=jax.ShapeDtypeStruct((m, d), F32),
        scratch_shapes=[pltpu.VMEM((tm, d), F32)],
        compiler_params=_cparams(("arbitrary", "arbitrary")),
        name="mlp",
    )(h2, wu, wd, x1, mod, final_gain)


def _forward(x, c, w_ada, b_ada, norm1_gain, w_in, b_in, rwkv_mix, rwkv_w0, rwkv_w2, rwkv_a0,
             rwkv_a2, rwkv_g2, rwkv_k_k, rwkv_k_a, rwkv_r_k, rwkv_ln_w, rwkv_ln_b, attn_sinks,
             w_branch_rwkv, w_branch_attn, w_out, norm2_gain, w_up, w_down, final_gain):
    batch, seq, d = x.shape
    depth = w_ada.shape[0]
    c_rwkv = rwkv_w0.shape[1]
    n_heads = attn_sinks.shape[1]
    qc = n_heads * HEAD
    kvc = 2 * (n_heads // ATTN_GROUP) * HEAD
    lora_w = DECAY_LORA + ICLR_LORA + GATE_LORA
    rwkv_cols = 3 * c_rwkv + lora_w
    attn_cols = qc + kvc
    offs = {"gate": 0, "r": 2 * d, "k": 2 * d + c_rwkv, "v": 2 * d + 2 * c_rwkv, "q": 2 * d + 3 * c_rwkv}
    offs["lora"] = offs["q"] + qc
    offs["kv"] = offs["lora"] + LORA_PAD
    n_pack = offs["kv"] + kvc

    segments = ((rwkv_cols + attn_cols, 2 * d, offs["gate"]), (0, 3 * c_rwkv, offs["r"]),
                (rwkv_cols, qc, offs["q"]), (3 * c_rwkv, lora_w, offs["lora"]),
                (rwkv_cols + qc, kvc, offs["kv"]))
    zero_rows = (offs["lora"] + lora_w, offs["kv"])

    def pack_cols(t):
        out = jnp.zeros(t.shape[:-1] + (n_pack,), t.dtype)
        for src, width, dst in segments:
            out = out.at[..., dst:dst + width].set(t[..., src:src + width])
        return out

    tm_in = min(1024, seq)
    tn_in = 1280 if n_pack % 1280 == 0 else LANES
    tb = min(1024, seq)
    tm_merge = min(256, seq)
    tm_mlp = min(512, seq)
    tf = min(1024, w_up.shape[2])

    assert depth == 1, "single-layer block"
    layer = lambda t: t.reshape(t.shape[1:])
    x2 = x.reshape(batch * seq, d)
    for l in range(depth):
        mod = _ada(c, layer(w_ada), b_ada[l]).reshape(batch, 6, d)
        w_pack = _pack_weight_t(layer(w_in).T, segments, zero_rows, n_pack)
        b_pack = pack_cols(b_in[l]).reshape(1, n_pack)
        p = _inproj(x2, mod, norm1_gain[l].reshape(1, d), w_pack, b_pack, seq, tm_in, tn_in)

        mix = rwkv_mix[l]
        chan = jnp.stack([mix[:c_rwkv], mix[c_rwkv:2 * c_rwkv], mix[2 * c_rwkv:3 * c_rwkv],
                          rwkv_w0[l], rwkv_a0[l], rwkv_k_k[l], rwkv_k_a[l], rwkv_r_k[l].reshape(c_rwkv),
                          rwkv_ln_w[l], rwkv_ln_b[l]])
        chan = jnp.pad(chan, ((0, CHAN_ROWS - chan.shape[0]), (0, 0)))
        mix_lora = jnp.pad(mix[3 * c_rwkv:], (0, LORA_PAD - lora_w)).reshape(1, LORA_PAD)
        g2_pad = jnp.pad(rwkv_g2[l], ((0, LORA_PAD - lora_w), (0, 0)))
        y_rwkv, (wbr, wba, wout, wup, wdown) = _rwkv(
            p, chan, mix_lora, rwkv_w2[l], rwkv_a2[l], g2_pad, batch, seq, offs, tb,
            cast=[layer(w_branch_rwkv), layer(w_branch_attn), layer(w_out), layer(w_up), layer(w_down)])
        x2, h2 = _attn_merge(y_rwkv, p, attn_sinks[l], x2, mod, wbr, wba, wout,
                             norm2_gain[l].reshape(1, d), seq, offs, tm_merge)
        x2 = _mlp(h2, wup, wdown, x2, mod, final_gain.reshape(1, d), seq, tm_mlp, tf)
    return x2.reshape(batch, seq, d)


def kernel(x, c, w_ada, b_ada, norm1_gain, w_in, b_in, rwkv_mix, rwkv_w0, rwkv_w2, rwkv_a0, rwkv_a2, rwkv_g2, rwkv_k_k, rwkv_k_a, rwkv_r_k, rwkv_ln_w, rwkv_ln_b, attn_sinks, w_branch_rwkv, w_branch_attn, w_out, norm2_gain, w_up, w_down, final_gain):
    return _forward(x, c, w_ada, b_ada, norm1_gain, w_in, b_in, rwkv_mix, rwkv_w0, rwkv_w2, rwkv_a0,
                    rwkv_a2, rwkv_g2, rwkv_k_k, rwkv_k_a, rwkv_r_k, rwkv_ln_w, rwkv_ln_b, attn_sinks,
                    w_branch_rwkv, w_branch_attn, w_out, norm2_gain, w_up, w_down, final_gain)
```

```python
import functools

import jax
import jax.numpy as jnp
from jax import lax
from jax.experimental import pallas as pl
from jax.experimental.pallas import tpu as pltpu

F32 = jnp.float32
BF16 = jnp.bfloat16

LANES = 128
SUBLANES = 8
BF16_ROWS = 16
MXU_N = 256
NORM_ROWS = 128
HEAD = 64
CHUNK = 128
WINDOW = 128
ATTN_GROUP = 8
ATTN_BATCH = 8
NORM_EPS = 1e-6
GN_EPS = 64e-5
DECAY_SCALE = 0.6065306597126334
KK_EPS = 1e-12
CHAN_ROWS = 16
DECAY_LORA = 64
ICLR_LORA = 64
GATE_LORA = 160
LORA_PAD = 512
VMEM_LIMIT = 56 * 1024 * 1024


def _cparams(sem):
    return pltpu.CompilerParams(dimension_semantics=sem, vmem_limit_bytes=VMEM_LIMIT)


_NN = (((1,), (0,)), ((), ()))
_NT = (((1,), (1,)), ((), ()))
_TN = (((0,), (0,)), ((), ()))


def _dot(a, b, dims=_NN):
    return lax.dot_general(a, b, dims, preferred_element_type=F32)


def _dot_nt(a, b):
    return _dot(a, b, _NT)


def _rms_norm(x, gain):
    ms = jnp.mean(x * x, axis=-1, keepdims=True)
    return x * lax.rsqrt(ms + NORM_EPS) * gain


def _ada_kernel(ct_ref, w_ref, b_ref, o_ref):
    ct = ct_ref[...]
    act = ct * jax.nn.sigmoid(ct)
    for m in range(o_ref.shape[0]):
        o_ref[m:m + 1, :] = jnp.sum(w_ref[...] * act[:, m:m + 1], axis=0, keepdims=True) + b_ref[...]


def _ada(c, w, b, tn=1024):
    nb, d = c.shape
    n = w.shape[1]
    return pl.pallas_call(
        _ada_kernel,
        grid=(n // tn,),
        in_specs=[pl.BlockSpec((d, nb), lambda j: (0, 0)),
                  pl.BlockSpec((d, tn), lambda j: (0, j)),
                  pl.BlockSpec((1, tn), lambda j: (0, j))],
        out_specs=pl.BlockSpec((nb, tn), lambda j: (0, j)),
        out_shape=jax.ShapeDtypeStruct((nb, n), F32),
        compiler_params=_cparams(("arbitrary",)),
        name="adaln",
    )(c.T, w, b.reshape(1, n))


def _pack_kernel(w_ref, o_ref, *, segments, zero_rows):
    for src, width, dst in segments:
        o_ref[dst:dst + width, :] = w_ref[src:src + width, :].astype(o_ref.dtype)
    lo, hi = zero_rows
    o_ref[lo:hi, :] = jnp.zeros((hi - lo, o_ref.shape[1]), o_ref.dtype)


def _pack_weight_t(w_t, segments, zero_rows, n_pack, tk=256):
    n, k = w_t.shape
    return pl.pallas_call(
        functools.partial(_pack_kernel, segments=segments, zero_rows=zero_rows),
        grid=(k // tk,),
        in_specs=[pl.BlockSpec((n, tk), lambda i: (0, i))],
        out_specs=pl.BlockSpec((n_pack, tk), lambda i: (0, i)),
        out_shape=jax.ShapeDtypeStruct((n_pack, k), BF16),
        compiler_params=_cparams(("arbitrary",)),
        name="pack_w_in",
    )(w_t)


def _inproj_kernel(x_ref, mod_ref, g_ref, w_ref, b_ref, o_ref, h_ref):
    @pl.when(pl.program_id(1) == 0)
    def _():
        scale = g_ref[...] * (1.0 + mod_ref[0, 1:2, :])
        shift = mod_ref[0, 0:1, :]

        def slab(c, carry):
            rows = pl.ds(pl.multiple_of(c * NORM_ROWS, NORM_ROWS), NORM_ROWS)
            x = x_ref[rows, :]
            ms = jnp.mean(x * x, axis=-1, keepdims=True)
            h_ref[rows, :] = (x * lax.rsqrt(ms + NORM_EPS) * scale + shift).astype(BF16)
            return carry

        lax.fori_loop(0, x_ref.shape[0] // NORM_ROWS, slab, 0)

    o_ref[...] = _dot_nt(h_ref[...], w_ref[...]) + b_ref[...]


def _inproj(x2, mod, gain, w_t, b, rows_per_batch, tm, tn):
    m, d = x2.shape
    n = w_t.shape[0]
    tpb = rows_per_batch // tm
    return pl.pallas_call(
        _inproj_kernel,
        grid=(m // tm, n // tn),
        in_specs=[pl.BlockSpec((tm, d), lambda i, j: (i, 0)),
                  pl.BlockSpec((1, 6, d), lambda i, j: (i // tpb, 0, 0)),
                  pl.BlockSpec((1, d), lambda i, j: (0, 0)),
                  pl.BlockSpec((tn, d), lambda i, j: (j, 0)),
                  pl.BlockSpec((1, tn), lambda i, j: (0, j))],
        out_specs=pl.BlockSpec((tm, tn), lambda i, j: (i, j)),
        out_shape=jax.ShapeDtypeStruct((m, n), F32),
        scratch_shapes=[pltpu.VMEM((tm, d), BF16)],
        compiler_params=_cparams(("arbitrary", "arbitrary")),
        name="inproj",
    )(x2, mod, gain, w_t, b)


def _head_sum(x):
    lo = lax.broadcasted_iota(jnp.int32, x.shape, 1) < HEAD
    s0 = jnp.sum(jnp.where(lo, x, 0.0), axis=-1, keepdims=True)
    s1 = jnp.sum(jnp.where(lo, 0.0, x), axis=-1, keepdims=True)
    return jnp.where(lo, s0, s1)


def _token_shift(x, carry_row, mix):
    rolled = pltpu.roll(x, 1, axis=0)
    head = rolled[:SUBLANES]
    first = lax.broadcasted_iota(jnp.int32, head.shape, 0) == 0
    prev = jnp.concatenate([jnp.where(first, carry_row, head), rolled[SUBLANES:]], axis=0)
    return x + (prev - x) * mix


class _Operands:
    def __init__(self):
        self._parts = {}

    def parts(self, x, n):
        ent = self._parts.setdefault(id(x), [x])
        while len(ent) - 1 < n:
            rest = x
            for piece in ent[1:]:
                rest = rest - piece.astype(F32)
            ent.append(rest.astype(BF16))
        return ent[1:n + 1]

    def mm(self, a, b, dims=_NN):
        return _dot(self.parts(a, 1)[0], self.parts(b, 1)[0], dims)


CHUNK_FIELDS = {"ra": F32, "aa": F32, "bi": BF16, "ki": BF16, "bh": BF16, "kh": BF16,
                "vm0": BF16, "vm1": BF16}


def _decay_chunk(ops, rc, lwc, kc, vsw, ac, bc):
    incl = (lax.broadcasted_iota(jnp.int32, (CHUNK, CHUNK), 0)
            >= lax.broadcasted_iota(jnp.int32, (CHUNK, CHUNK), 1))
    tri = jnp.where(incl, 1.0, 0.0).astype(BF16)
    lw = sum(jnp.dot(tri, piece, preferred_element_type=F32) for piece in ops.parts(lwc, 2))
    lw_end = lw[CHUNK - 1:CHUNK, :]
    lw_mid = lw[CHUNK // 2 - 1:CHUNK // 2, :]
    e_in = jnp.exp(lw)
    e_ex = jnp.exp(lw - lwc)
    e_inv = jnp.exp(lw_mid - lw)
    e_end = jnp.exp(lw_end - lw)
    lo_lane = lax.broadcasted_iota(jnp.int32, vsw.shape, 1) < HEAD
    return dict(vm0=jnp.where(lo_lane, 0.0, vsw), vm1=jnp.where(lo_lane, vsw, 0.0),
                g_end=jnp.exp(lw_end), e_mid=jnp.exp(-lw_mid), ra=rc * e_in, aa=ac * e_ex,
                bi=bc * e_inv, ki=kc * e_inv, bh=bc * e_end, kh=kc * e_end)


def _scan_matrices(ops, chunks):
    row = lax.broadcasted_iota(jnp.int32, (CHUNK, CHUNK), 0)
    col = lax.broadcasted_iota(jnp.int32, (CHUNK, CHUNK), 1)
    incl = row >= col
    strict = row > col
    eye = row == col
    lo_lane = col < HEAD
    same_half = lo_lane == (row < HEAD)
    lo_lane2 = lax.broadcasted_iota(jnp.int32, (2 * CHUNK, LANES), 1) < HEAD
    level_masks = []
    n = 1
    while n < CHUNK:
        level_masks.append((row // (2 * n) == col // (2 * n)) & ((row // n) % 2 == 1) & ((col // n) % 2 == 0))
        n *= 2

    probs = []
    for ch in chunks:
        lhs_all = jnp.concatenate([ch["aa"], ch["ra"]], axis=0) * ch["e_mid"]
        rhs_all = jnp.concatenate([ch["bi"], ch["ki"]], axis=0)
        for h in range(LANES // HEAD):
            own = lo_lane if h == 0 else ~lo_lane
            own2 = lo_lane2 if h == 0 else ~lo_lane2
            gram = ops.mm(jnp.where(own2, lhs_all, 0.0), rhs_all, _NT)
            m_ab = jnp.where(strict, gram[:CHUNK, :CHUNK], 0.0)
            probs.append(dict(ch=ch, own=own, vm=ch["vm0"] if h == 0 else ch["vm1"],
                              offs=[jnp.where(mask, m_ab, 0.0).astype(BF16) for mask in level_masks],
                              m_ak=jnp.where(strict, gram[:CHUNK, CHUNK:], 0.0),
                              m_rb=jnp.where(incl, gram[CHUNK:, :CHUNK], 0.0),
                              m_rk=jnp.where(incl, gram[CHUNK:, CHUNK:], 0.0)))
    yield

    unit = jnp.where(eye, 1.0, 0.0).astype(BF16)
    for p in probs:
        p["t"] = unit + p["offs"][0]
    n = 2
    for level in range(1, len(level_masks)):
        if n < BF16_ROWS:
            for p in probs:
                p["to"] = _dot(p["t"], p["offs"][level]).astype(BF16)
            for p in probs:
                p["t"] = p["t"] + _dot(p["to"], p["t"]).astype(BF16)
        else:
            lower = lambda t, n=n: jnp.concatenate(
                [t[i:i + n] for i in range(n, CHUNK, 2 * n)], axis=0)
            for p in probs:
                p["tl"] = lower(p["t"])
                p["to"] = _dot(p["tl"], p["offs"][level]).astype(BF16)
            for p in probs:
                tl = p["tl"] + _dot(p["to"], p["t"]).astype(BF16)
                slabs = []
                for j, i in enumerate(range(0, CHUNK, 2 * n)):
                    slabs += [p["t"][i:i + n], tl[j * n:(j + 1) * n]]
                p["t"] = jnp.concatenate(slabs, axis=0)
        n *= 2
        yield

    for p in probs:
        p["x"] = (jnp.where(p["own"], p["ch"]["aa"], 0.0) + ops.mm(p["m_ak"], p["vm"])).astype(BF16)
    yield
    for p in probs:
        p["tx"] = _dot(p["t"], p["x"]).astype(BF16)
    yield
    for p in probs:
        p["y"] = jnp.where(p["own"], p["ch"]["ra"], 0.0) + ops.mm(
            jnp.concatenate([p["m_rb"], p["m_rk"]], axis=1),
            jnp.concatenate([p["tx"], p["vm"]], axis=0))
    yield

    for i, ch in enumerate(chunks):
        p0, p1 = probs[2 * i], probs[2 * i + 1]
        pq = ops.mm(jnp.concatenate([ch["bh"], ch["kh"]], axis=0),
                    jnp.concatenate([jnp.concatenate([p0["tx"], p1["tx"]], axis=1),
                                     jnp.concatenate([p0["vm"], p1["vm"]], axis=1)], axis=0),
                    _TN)
        pq = jnp.where(row < HEAD, pq[:, :LANES], pq[:, LANES:])
        ch["p"] = jnp.where(same_half, pq, 0.0) + jnp.where(eye, ch["g_end"], 0.0)
        ch["q"] = jnp.where(same_half, 0.0, pq)
        ch["r"] = jnp.where(lo_lane, p0["y"], p1["y"])
        ch["y0"] = jnp.where(lo_lane, p1["y"], p0["y"])


def _interleave(main, *side):
    for _ in main:
        for gen in side:
            next(gen, None)
    for gen in side:
        for _ in gen:
            pass


def _rwkv_kernel(*refs, n_pairs, n_tblocks, n_tiles, cast_steps):
    n_cast = len(cast_steps)
    (r_ref, k_ref, v_ref, lora_ref, chan_ref, chanf_ref, mixl_ref, w2_ref, a2_ref, g2_ref) = refs[:10]
    cast_in, o_ref, cast_out = refs[10:10 + n_cast], refs[10 + n_cast], refs[11 + n_cast:11 + 2 * n_cast]
    (crkv_ref, clora_ref, z_ref, twd_s, ad_s, sgd_s, r3_s, k3_s, v3_s, g3_s,
     y_s, gend_s, p_s, q_s, rr_s, y0_s, *decay_s) = refs[11 + 2 * n_cast:]

    tb = r_ref.shape[0]
    n_chunks = tb // CHUNK
    s = pl.program_id(0)
    sa = jnp.minimum(s, n_tiles - 1)
    sf = jnp.maximum(s - 2, 0)
    hp, t_a = sa % n_pairs, (sa // n_pairs) % n_tblocks
    hpf, t_f = sf % n_pairs, (sf // n_pairs) % n_tblocks
    slot_w, slot_r = s % 2, (s + 1) % 2
    keep_w, keep_r = s % 3, (s + 1) % 3

    @pl.when(s == 0)
    def _():
        for ref in (crkv_ref, clora_ref, z_ref, r3_s, k3_s, v3_s, g3_s, gend_s, p_s, q_s, rr_s, y0_s,
                    *decay_s):
            ref[...] = jnp.zeros_like(ref)

    @pl.when(hp == 0)
    def _():
        lora_in = lora_ref[...]
        carry = jnp.where(t_a == 0, 0.0, clora_ref[...])
        lora = _token_shift(lora_in, carry, mixl_ref[...])
        clora_ref[...] = lora_in[tb - 1:tb, :]
        twd = jnp.tanh(lora[:, 0:DECAY_LORA])
        twd_hi = twd.astype(BF16)
        twd_s[0] = twd_hi
        twd_s[1] = (twd - twd_hi.astype(F32)).astype(BF16)
        ad_s[...] = lora[:, DECAY_LORA:DECAY_LORA + ICLR_LORA].astype(BF16)
        sgd_s[...] = jax.nn.sigmoid(lora[:, DECAY_LORA + ICLR_LORA:]).astype(BF16)

    ops = _Operands()

    def finish():
        chanf = chanf_ref[...]
        z = jnp.where(t_f == 0, 0.0, z_ref[hpf])
        for c in range(n_chunks):
            i = slot_r * n_chunks + c
            rz = ops.mm(jnp.concatenate([rr_s[i], p_s[i]], axis=0), z)
            y_s[c * CHUNK:(c + 1) * CHUNK, :] = rz[:CHUNK] + y0_s[i]
            z = rz[CHUNK:] + q_s[i]
            yield
        z_ref[hpf] = z
        r_k, ln_w, ln_b = chanf[7:8], chanf[8:9], chanf[9:10]
        y = pltpu.roll(y_s[...], HEAD, axis=1)
        mu = _head_sum(y) * (1.0 / HEAD)
        yc = y - mu
        var = _head_sum(yc * yc) * (1.0 / HEAD)
        yn = yc * lax.rsqrt(var + GN_EPS) * ln_w + ln_b
        bonus = _head_sum(r3_s[keep_r] * k3_s[keep_r] * r_k) * v3_s[keep_r]
        o_ref[...] = ((yn + bonus) * g3_s[keep_r]).astype(o_ref.dtype)
        yield

    def matrices():
        chunks = []
        for c in range(n_chunks):
            i = slot_r * n_chunks + c
            ch = {name: ref[i] for name, ref in zip(CHUNK_FIELDS, decay_s)}
            ch["g_end"], ch["e_mid"] = gend_s[i, 0:1, :], gend_s[i, 1:2, :]
            chunks.append(ch)
        yield from _scan_matrices(ops, chunks)
        for c, ch in enumerate(chunks):
            i = slot_w * n_chunks + c
            p_s[i], q_s[i] = ch["p"].astype(p_s.dtype), ch["q"]
            rr_s[i], y0_s[i] = ch["r"].astype(rr_s.dtype), ch["y0"]
        yield

    def prepare():
        chan = chan_ref[...]
        mix_r, mix_k, mix_v = chan[0:1], chan[1:2], chan[2:3]
        w0, a0, k_k, k_a = chan[3:4], chan[4:5], chan[5:6], chan[6:7]
        r_in, k_in, v_in = r_ref[...], k_ref[...], v_ref[...]
        carry = jnp.where(t_a == 0, 0.0, crkv_ref[hp])
        r = _token_shift(r_in, carry[0:1], mix_r)
        k = _token_shift(k_in, carry[1:2], mix_k)
        v = _token_shift(v_in, carry[2:3], mix_v)
        crkv_ref[hp, 0:1, :] = r_in[tb - 1:tb, :]
        crkv_ref[hp, 1:2, :] = k_in[tb - 1:tb, :]
        crkv_ref[hp, 2:3, :] = v_in[tb - 1:tb, :]
        yield
        w2_hi = w2_ref[...].astype(BF16)
        w2_lo = (w2_ref[...] - w2_hi.astype(F32)).astype(BF16)
        twd_hi, twd_lo = twd_s[0], twd_s[1]
        dw = w0 + (_dot(twd_hi, w2_hi) + (_dot(twd_hi, w2_lo) + _dot(twd_lo, w2_hi)))
        logw = -DECAY_SCALE * jax.nn.sigmoid(dw)
        alr = jax.nn.sigmoid(a0 + _dot(ad_s[...], a2_ref[...].astype(BF16)))
        g3_s[keep_w] = _dot(sgd_s[...], g2_ref[...].astype(BF16))
        yield
        kk = k * k_k
        kk = kk * lax.rsqrt(jnp.maximum(_head_sum(kk * kk), KK_EPS * KK_EPS))
        k_mod = k * (1.0 + (alr - 1.0) * k_a)
        vsw = pltpu.roll(v, HEAD, axis=1)
        a, b = -kk, kk * alr
        r3_s[keep_w], k3_s[keep_w], v3_s[keep_w] = r, k_mod, v
        yield
        for c in range(n_chunks):
            sl = slice(c * CHUNK, (c + 1) * CHUNK)
            ch = _decay_chunk(ops, r[sl], logw[sl], k_mod[sl], vsw[sl], a[sl], b[sl])
            i = slot_w * n_chunks + c
            for name, ref in zip(CHUNK_FIELDS, decay_s):
                ref[i] = ch[name].astype(ref.dtype)
            gend_s[i, 0:1, :] = ch["g_end"]
            gend_s[i, 1:2, :] = ch["e_mid"]
            yield

    def casts():
        for w_ref, wb_ref in zip(cast_in, cast_out):
            wb_ref[...] = w_ref[...].astype(wb_ref.dtype)
            yield

    _interleave(prepare(), matrices(), finish(), casts())


def _rwkv(p, chan, mix_lora, w2, a2, g2, batch, seq, offs, tb, cast):
    m = p.shape[0]
    c = w2.shape[1]
    n_pairs = c // LANES
    n_tblocks = seq // tb
    n_tiles = batch * n_tblocks * n_pairs
    n_chunks = tb // CHUNK
    gate_w = LORA_PAD - DECAY_LORA - ICLR_LORA
    prep = lambda s: jnp.minimum(s, n_tiles - 1)
    fin = lambda s: jnp.maximum(s - 2, 0)
    col = lambda name: lambda s: (prep(s) // n_pairs, offs[name] // LANES + prep(s) % n_pairs)
    weight = lambda s: (0, prep(s) % n_pairs)
    tile = lambda: pltpu.VMEM((tb, LANES), F32)
    tile3 = lambda: pltpu.VMEM((3, tb, LANES), F32)
    mats = lambda dtype=F32: pltpu.VMEM((2 * n_chunks, CHUNK, LANES), dtype)
    cast_steps, cast_specs = [], []
    for w in cast:
        steps = min(n_tiles, w.shape[0] // BF16_ROWS)
        assert w.shape[0] % (steps * BF16_ROWS) == 0, w.shape
        cast_steps.append(steps)
        cast_specs.append(pl.BlockSpec((w.shape[0] // steps, w.shape[1]),
                                       lambda s, steps=steps: (jnp.minimum(s, steps - 1), 0)))
    outs = pl.pallas_call(
        functools.partial(_rwkv_kernel, n_pairs=n_pairs, n_tblocks=n_tblocks, n_tiles=n_tiles,
                          cast_steps=tuple(cast_steps)),
        grid=(n_tiles + 2,),
        in_specs=[pl.BlockSpec((tb, LANES), col("r")),
                  pl.BlockSpec((tb, LANES), col("k")),
                  pl.BlockSpec((tb, LANES), col("v")),
                  pl.BlockSpec((tb, LORA_PAD), lambda s: (prep(s) // n_pairs, offs["lora"] // LORA_PAD)),
                  pl.BlockSpec((CHAN_ROWS, LANES), weight),
                  pl.BlockSpec((CHAN_ROWS, LANES), lambda s: (0, fin(s) % n_pairs)),
                  pl.BlockSpec((1, LORA_PAD), lambda s: (0, 0)),
                  pl.BlockSpec((DECAY_LORA, LANES), weight),
                  pl.BlockSpec((ICLR_LORA, LANES), weight),
                  pl.BlockSpec((gate_w, LANES), weight)] + cast_specs,
        out_specs=[pl.BlockSpec((tb, LANES), lambda s: (fin(s) // n_pairs, fin(s) % n_pairs))] + cast_specs,
        out_shape=[jax.ShapeDtypeStruct((m, c), BF16)]
                  + [jax.ShapeDtypeStruct(w.shape, BF16) for w in cast],
        scratch_shapes=[pltpu.VMEM((n_pairs, SUBLANES, LANES), F32),
                        pltpu.VMEM((1, LORA_PAD), F32),
                        pltpu.VMEM((n_pairs, CHUNK, LANES), F32),
                        pltpu.VMEM((2, tb, DECAY_LORA), BF16),
                        pltpu.VMEM((tb, ICLR_LORA), BF16),
                        pltpu.VMEM((tb, gate_w), BF16),
                        tile3(), tile3(), tile3(), tile3(),
                        tile(),
                        pltpu.VMEM((2 * n_chunks, SUBLANES, LANES), F32),
                        mats(BF16), mats(), mats(BF16), mats()]
                       + [mats(dtype) for dtype in CHUNK_FIELDS.values()],
        compiler_params=_cparams(("arbitrary",)),
        name="rwkv7",
    )(p, p, p, p, chan, chan, mix_lora, w2, a2, g2, *cast)
    return outs[0], outs[1:]


def _attn_stages(sink_ref, q, kv, bias, store, n_heads):
    n_kv = n_heads // ATTN_GROUP
    lo_lane = lax.broadcasted_iota(jnp.int32, (WINDOW, LANES), 1) < HEAD
    keys = [kv[:, g * HEAD:(g + 1) * HEAD] for g in range(n_kv)]
    values = [kv[:, (n_kv + g) * HEAD:(n_kv + g + 1) * HEAD] for g in range(n_kv)]
    for g0 in range(0, n_heads, ATTN_BATCH):
        heads = range(g0, min(g0 + ATTN_BATCH, n_heads))
        s = {h: _dot_nt(q[:, h * HEAD:(h + 1) * HEAD], keys[h // ATTN_GROUP]) + bias(h) for h in heads}
        yield
        mx = {h: jnp.maximum(jnp.max(s[h], axis=-1, keepdims=True), sink_ref[h]) for h in heads}
        e = {h: jnp.exp(s[h] - mx[h]) for h in heads}
        yield
        denom = {h: jnp.sum(e[h], axis=-1, keepdims=True) + jnp.exp(sink_ref[h] - mx[h]) for h in heads}
        pv = {h: _dot(e[h].astype(BF16), values[h // ATTN_GROUP]) for h in heads}
        yield
        for h0 in heads[::2]:
            pair = (jnp.concatenate([pv[h0], pv[h0 + 1]], axis=1)
                    * jnp.where(lo_lane, 1.0 / denom[h0], 1.0 / denom[h0 + 1]))
            store(h0, pair)
        yield


def _attn_merge_kernel(sink_ref, q_ref, kvc_ref, kvp_ref,
                       yr_ref, gate_ref, x_ref, mod_ref, wbr_ref, wba_ref, wout_ref, g2_ref,
                       x1_ref, h2_ref, bias_s, ya_s, *, blocks_per_seq, n_heads, n_tiles):
    tm, d = x_ref.shape
    s = pl.program_id(0)
    first_block = (jnp.minimum(s, n_tiles - 1) * (tm // WINDOW)) % blocks_per_seq == 0

    @pl.when(s == 0)
    def _():
        ya_s[...] = jnp.zeros_like(ya_s)
        qi = lax.broadcasted_iota(jnp.int32, (WINDOW, 2 * WINDOW), 0)
        kj = lax.broadcasted_iota(jnp.int32, (WINDOW, 2 * WINDOW), 1)
        dist = qi + WINDOW - kj
        valid = (dist >= 0) & (dist < WINDOW)
        for first in (0, 1):
            neg_dist = jnp.where(valid & (kj >= WINDOW) if first == 0 else valid, -dist.astype(F32), -jnp.inf)
            for h in range(n_heads):
                bias_s[first * n_heads + h] = (2.0 ** (-8.0 * (h + 1) / n_heads)) * neg_dist

    def attention():
        kv = jnp.concatenate([kvp_ref[...], kvc_ref[...]], axis=0).astype(BF16)
        q = (q_ref[...] * (HEAD ** -0.5)).astype(BF16)
        for blk in range(tm // WINDOW):
            rows = slice(blk * WINDOW, (blk + 1) * WINDOW)
            table = jnp.where(first_block, 0, 1) * n_heads if blk == 0 else n_heads

            def store(h0, pair, rows=rows):
                ya_s[s % 2, rows, h0 * HEAD:(h0 + 2) * HEAD] = pair.astype(ya_s.dtype)

            yield from _attn_stages(sink_ref, q[rows], kv[blk * WINDOW:(blk + 2) * WINDOW],
                                    lambda h, table=table: bias_s[table + h], store, n_heads)

    def merge():
        yr, ya = yr_ref[...], ya_s[(s + 1) % 2]
        merged = []
        for c0 in range(0, d, MXU_N):
            cols, cols_a = slice(c0, c0 + MXU_N), slice(d + c0, d + c0 + MXU_N)
            merged.append((jax.nn.sigmoid(gate_ref[:, cols]) * _dot(yr, wbr_ref[:, cols])
                           + jax.nn.sigmoid(gate_ref[:, cols_a]) * _dot(ya, wba_ref[:, cols])).astype(BF16))
            yield
        merged = jnp.concatenate(merged, axis=1)
        for c0 in range(0, d, MXU_N):
            cols = slice(c0, c0 + MXU_N)
            x1_ref[:, cols] = x_ref[:, cols] + mod_ref[0, 2:3, cols] * _dot(merged, wout_ref[:, cols])
            yield
        h2 = _rms_norm(x1_ref[...], g2_ref[...]) * (1.0 + mod_ref[0, 4:5, :]) + mod_ref[0, 3:4, :]
        h2_ref[...] = h2.astype(BF16)
        yield

    _interleave(merge(), attention())


def _attn_merge(yr, p, sinks, x2, mod, wbr, wba, wout, gain2, rows_per_batch, offs, tm):
    m, d = x2.shape
    c = yr.shape[1]
    n_heads = sinks.shape[0]
    qc = n_heads * HEAD
    kvc = 2 * (n_heads // ATTN_GROUP) * HEAD
    n_tiles = m // tm
    tpb = rows_per_batch // tm
    wpt = tm // WINDOW
    att = lambda s: jnp.minimum(s, n_tiles - 1)
    mrg = lambda s: jnp.maximum(s - 1, 0)
    const = lambda shape: pl.BlockSpec(shape, lambda s: (0, 0), pipeline_mode=pl.Buffered(1))
    return pl.pallas_call(
        functools.partial(_attn_merge_kernel, blocks_per_seq=rows_per_batch // WINDOW, n_heads=n_heads,
                          n_tiles=n_tiles),
        grid=(n_tiles + 1,),
        in_specs=[pl.BlockSpec(memory_space=pltpu.SMEM),
                  pl.BlockSpec((tm, qc), lambda s: (att(s), offs["q"] // qc)),
                  pl.BlockSpec((tm, kvc), lambda s: (att(s), offs["kv"] // kvc)),
                  pl.BlockSpec((WINDOW, kvc), lambda s: (jnp.maximum(att(s) * wpt - 1, 0), offs["kv"] // kvc)),
                  pl.BlockSpec((tm, c), lambda s: (mrg(s), 0)),
                  pl.BlockSpec((tm, 2 * d), lambda s: (mrg(s), 0)),
                  pl.BlockSpec((tm, d), lambda s: (mrg(s), 0)),
                  pl.BlockSpec((1, 6, d), lambda s: (mrg(s) // tpb, 0, 0)),
                  const((c, d)), const((c, d)), const((d, d)),
                  pl.BlockSpec((1, d), lambda s: (0, 0))],
        out_specs=[pl.BlockSpec((tm, d), lambda s: (mrg(s), 0)),
                   pl.BlockSpec((tm, d), lambda s: (mrg(s), 0))],
        out_shape=[jax.ShapeDtypeStruct((m, d), F32), jax.ShapeDtypeStruct((m, d), BF16)],
        scratch_shapes=[pltpu.VMEM((2 * n_heads, WINDOW, 2 * WINDOW), F32),
                        pltpu.VMEM((2, tm, qc), BF16)],
        compiler_params=_cparams(("arbitrary",)),
        name="attn_merge",
    )(sinks, p, p, p, yr, p, x2, mod, wbr, wba, wout, gain2)


def _mlp_kernel(h_ref, wu_ref, wd_ref, x1_ref, mod_ref, fg_ref, o_ref, acc_ref):
    f = pl.program_id(1)

    @pl.when(f == 0)
    def _():
        acc_ref[...] = jnp.zeros_like(acc_ref)

    u = _dot(h_ref[...], wu_ref[...])
    act = jnp.square(jnp.maximum(u, 0.0)).astype(BF16)
    acc_ref[...] += _dot(act, wd_ref[...])

    @pl.when(f == pl.num_programs(1) - 1)
    def _():
        x2 = x1_ref[...] + mod_ref[0, 5:6, :] * acc_ref[...]
        o_ref[...] = _rms_norm(x2, fg_ref[...])


def _mlp(h2, wu, wd, x1, mod, final_gain, rows_per_batch, tm, tf):
    m, d = h2.shape
    f = wu.shape[1]
    tpb = rows_per_batch // tm
    return pl.pallas_call(
        _mlp_kernel,
        grid=(m // tm, f // tf),
        in_specs=[pl.BlockSpec((tm, d), lambda i, j: (i, 0)),
                  pl.BlockSpec((d, tf), lambda i, j: (0, j)),
                  pl.BlockSpec((tf, d), lambda i, j: (j, 0)),
                  pl.BlockSpec((tm, d), lambda i, j: (i, 0)),
                  pl.BlockSpec((1, 6, d), lambda i, j: (i // tpb, 0, 0)),
                  pl.BlockSpec((1, d), lambda i, j: (0, 0))],
        out_specs=pl.BlockSpec((tm, d), lambda i, j: (i, 0)),
        out_shape=jax.ShapeDtypeStruct((m, d), F32),
        scratch_shapes=[pltpu.VMEM((tm, d), F32)],
        compiler_params=_cparams(("arbitrary", "arbitrary")),
        name="mlp",
    )(h2, wu, wd, x1, mod, final_gain)


def _forward(x, c, w_ada, b_ada, norm1_gain, w_in, b_in, rwkv_mix, rwkv_w0, rwkv_w2, rwkv_a0,
             rwkv_a2, rwkv_g2, rwkv_k_k, rwkv_k_a, rwkv_r_k, rwkv_ln_w, rwkv_ln_b, attn_sinks,
             w_branch_rwkv, w_branch_attn, w_out, norm2_gain, w_up, w_down, final_gain):
    batch, seq, d = x.shape
    depth = w_ada.shape[0]
    c_rwkv = rwkv_w0.shape[1]
    n_heads = attn_sinks.shape[1]
    qc = n_heads * HEAD
    kvc = 2 * (n_heads // ATTN_GROUP) * HEAD
    lora_w = DECAY_LORA + ICLR_LORA + GATE_LORA
    rwkv_cols = 3 * c_rwkv + lora_w
    attn_cols = qc + kvc
    offs = {"gate": 0, "r": 2 * d, "k": 2 * d + c_rwkv, "v": 2 * d + 2 * c_rwkv, "q": 2 * d + 3 * c_rwkv}
    offs["lora"] = offs["q"] + qc
    offs["kv"] = offs["lora"] + LORA_PAD
    n_pack = offs["kv"] + kvc

    segments = ((rwkv_cols + attn_cols, 2 * d, offs["gate"]), (0, 3 * c_rwkv, offs["r"]),
                (rwkv_cols, qc, offs["q"]), (3 * c_rwkv, lora_w, offs["lora"]),
                (rwkv_cols + qc, kvc, offs["kv"]))
    zero_rows = (offs["lora"] + lora_w, offs["kv"])

    def pack_cols(t):
        out = jnp.zeros(t.shape[:-1] + (n_pack,), t.dtype)
        for src, width, dst in segments:
            out = out.at[..., dst:dst + width].set(t[..., src:src + width])
        return out

    tm_in = min(1024, seq)
    tn_in = 1280 if n_pack % 1280 == 0 else LANES
    tb = min(1024, seq)
    tm_merge = min(256, seq)
    tm_mlp = min(512, seq)
    tf = min(1024, w_up.shape[2])

    assert depth == 1, "single-layer block"
    layer = lambda t: t.reshape(t.shape[1:])
    x2 = x.reshape(batch * seq, d)
    for l in range(depth):
        mod = _ada(c, layer(w_ada), b_ada[l]).reshape(batch, 6, d)
        w_pack = _pack_weight_t(layer(w_in).T, segments, zero_rows, n_pack)
        b_pack = pack_cols(b_in[l]).reshape(1, n_pack)
        p = _inproj(x2, mod, norm1_gain[l].reshape(1, d), w_pack, b_pack, seq, tm_in, tn_in)

        mix = rwkv_mix[l]
        chan = jnp.stack([mix[:c_rwkv], mix[c_rwkv:2 * c_rwkv], mix[2 * c_rwkv:3 * c_rwkv],
                          rwkv_w0[l], rwkv_a0[l], rwkv_k_k[l], rwkv_k_a[l], rwkv_r_k[l].reshape(c_rwkv),
                          rwkv_ln_w[l], rwkv_ln_b[l]])
        chan = jnp.pad(chan, ((0, CHAN_ROWS - chan.shape[0]), (0, 0)))
        mix_lora = jnp.pad(mix[3 * c_rwkv:], (0, LORA_PAD - lora_w)).reshape(1, LORA_PAD)
        g2_pad = jnp.pad(rwkv_g2[l], ((0, LORA_PAD - lora_w), (0, 0)))
        y_rwkv, (wbr, wba, wout, wup, wdown) = _rwkv(
            p, chan, mix_lora, rwkv_w2[l], rwkv_a2[l], g2_pad, batch, seq, offs, tb,
            cast=[layer(w_branch_rwkv), layer(w_branch_attn), layer(w_out), layer(w_up), layer(w_down)])
        x2, h2 = _attn_merge(y_rwkv, p, attn_sinks[l], x2, mod, wbr, wba, wout,
                             norm2_gain[l].reshape(1, d), seq, offs, tm_merge)
        x2 = _mlp(h2, wup, wdown, x2, mod, final_gain.reshape(1, d), seq, tm_mlp, tf)
    return x2.reshape(batch, seq, d)


def kernel(x, c, w_ada, b_ada, norm1_gain, w_in, b_in, rwkv_mix, rwkv_w0, rwkv_w2, rwkv_a0, rwkv_a2, rwkv_g2, rwkv_k_k, rwkv_k_a, rwkv_r_k, rwkv_ln_w, rwkv_ln_b, attn_sinks, w_branch_rwkv, w_branch_attn, w_out, norm2_gain, w_up, w_down, final_gain):
    return _forward(x, c, w_ada, b_ada, norm1_gain, w_in, b_in, rwkv_mix, rwkv_w0, rwkv_w2, rwkv_a0,
                    rwkv_a2, rwkv_g2, rwkv_k_k, rwkv_k_a, rwkv_r_k, rwkv_ln_w, rwkv_ln_b, attn_sinks,
                    w_branch_rwkv, w_branch_attn, w_out, norm2_gain, w_up, w_down, final_gain)
```

```python
import functools

import jax
import jax.numpy as jnp
from jax import lax
from jax.experimental import pallas as pl
from jax.experimental.pallas import tpu as pltpu

F32 = jnp.float32
BF16 = jnp.bfloat16

LANES = 128
SUBLANES = 8
BF16_ROWS = 16
MXU_N = 256
NORM_ROWS = 128
HEAD = 64
CHUNK = 128
WINDOW = 128
ATTN_GROUP = 8
ATTN_BATCH = 8
NORM_EPS = 1e-6
GN_EPS = 64e-5
DECAY_SCALE = 0.6065306597126334
KK_EPS = 1e-12
CHAN_ROWS = 16
DECAY_LORA = 64
ICLR_LORA = 64
GATE_LORA = 160
LORA_PAD = 512
VMEM_LIMIT = 56 * 1024 * 1024


def _cparams(sem):
    return pltpu.CompilerParams(dimension_semantics=sem, vmem_limit_bytes=VMEM_LIMIT)


_NN = (((1,), (0,)), ((), ()))
_NT = (((1,), (1,)), ((), ()))
_TN = (((0,), (0,)), ((), ()))


def _dot(a, b, dims=_NN):
    return lax.dot_general(a, b, dims, preferred_element_type=F32)


def _dot_nt(a, b):
    return _dot(a, b, _NT)


def _rms_norm(x, gain):
    ms = jnp.mean(x * x, axis=-1, keepdims=True)
    return x * lax.rsqrt(ms + NORM_EPS) * gain


def _ada_kernel(ct_ref, w_ref, b_ref, o_ref):
    ct = ct_ref[...]
    act = ct * jax.nn.sigmoid(ct)
    for m in range(o_ref.shape[0]):
        o_ref[m:m + 1, :] = jnp.sum(w_ref[...] * act[:, m:m + 1], axis=0, keepdims=True) + b_ref[...]


def _ada(c, w, b, tn=1024):
    nb, d = c.shape
    n = w.shape[1]
    return pl.pallas_call(
        _ada_kernel,
        grid=(n // tn,),
        in_specs=[pl.BlockSpec((d, nb), lambda j: (0, 0)),
                  pl.BlockSpec((d, tn), lambda j: (0, j)),
                  pl.BlockSpec((1, tn), lambda j: (0, j))],
        out_specs=pl.BlockSpec((nb, tn), lambda j: (0, j)),
        out_shape=jax.ShapeDtypeStruct((nb, n), F32),
        compiler_params=_cparams(("arbitrary",)),
        name="adaln",
    )(c.T, w, b.reshape(1, n))


def _pack_kernel(w_ref, o_ref, *, segments, zero_rows):
    for src, width, dst in segments:
        o_ref[dst:dst + width, :] = w_ref[src:src + width, :].astype(o_ref.dtype)
    lo, hi = zero_rows
    o_ref[lo:hi, :] = jnp.zeros((hi - lo, o_ref.shape[1]), o_ref.dtype)


def _pack_weight_t(w_t, segments, zero_rows, n_pack, tk=256):
    n, k = w_t.shape
    return pl.pallas_call(
        functools.partial(_pack_kernel, segments=segments, zero_rows=zero_rows),
        grid=(k // tk,),
        in_specs=[pl.BlockSpec((n, tk), lambda i: (0, i))],
        out_specs=pl.BlockSpec((n_pack, tk), lambda i: (0, i)),
        out_shape=jax.ShapeDtypeStruct((n_pack, k), BF16),
        compiler_params=_cparams(("arbitrary",)),
        name="pack_w_in",
    )(w_t)


def _inproj_kernel(x_ref, mod_ref, g_ref, w_ref, b_ref, o_ref, h_ref):
    @pl.when(pl.program_id(1) == 0)
    def _():
        scale = g_ref[...] * (1.0 + mod_ref[0, 1:2, :])
        shift = mod_ref[0, 0:1, :]

        def slab(c, carry):
            rows = pl.ds(pl.multiple_of(c * NORM_ROWS, NORM_ROWS), NORM_ROWS)
            x = x_ref[rows, :]
            ms = jnp.mean(x * x, axis=-1, keepdims=True)
            h_ref[rows, :] = (x * lax.rsqrt(ms + NORM_EPS) * scale + shift).astype(BF16)
            return carry

        lax.fori_loop(0, x_ref.shape[0] // NORM_ROWS, slab, 0)

    o_ref[...] = _dot_nt(h_ref[...], w_ref[...]) + b_ref[...]


def _inproj(x2, mod, gain, w_t, b, rows_per_batch, tm, tn):
    m, d = x2.shape
    n = w_t.shape[0]
    tpb = rows_per_batch // tm
    return pl.pallas_call(
        _inproj_kernel,
        grid=(m // tm, n // tn),
        in_specs=[pl.BlockSpec((tm, d), lambda i, j: (i, 0)),
                  pl.BlockSpec((1, 6, d), lambda i, j: (i // tpb, 0, 0)),
                  pl.BlockSpec((1, d), lambda i, j: (0, 0)),
                  pl.BlockSpec((tn, d), lambda i, j: (j, 0)),
                  pl.BlockSpec((1, tn), lambda i, j: (0, j))],
        out_specs=pl.BlockSpec((tm, tn), lambda i, j: (i, j)),
        out_shape=jax.ShapeDtypeStruct((m, n), F32),
        scratch_shapes=[pltpu.VMEM((tm, d), BF16)],
        compiler_params=_cparams(("arbitrary", "arbitrary")),
        name="inproj",
    )(x2, mod, gain, w_t, b)


def _head_sum(x):
    lo = lax.broadcasted_iota(jnp.int32, x.shape, 1) < HEAD
    s0 = jnp.sum(jnp.where(lo, x, 0.0), axis=-1, keepdims=True)
    s1 = jnp.sum(jnp.where(lo, 0.0, x), axis=-1, keepdims=True)
    return jnp.where(lo, s0, s1)


def _token_shift(x, carry_row, mix):
    rolled = pltpu.roll(x, 1, axis=0)
    head = rolled[:SUBLANES]
    first = lax.broadcasted_iota(jnp.int32, head.shape, 0) == 0
    prev = jnp.concatenate([jnp.where(first, carry_row, head), rolled[SUBLANES:]], axis=0)
    return x + (prev - x) * mix


class _Operands:
    def __init__(self):
        self._parts = {}

    def parts(self, x, n):
        ent = self._parts.setdefault(id(x), [x])
        while len(ent) - 1 < n:
            rest = x
            for piece in ent[1:]:
                rest = rest - piece.astype(F32)
            ent.append(rest.astype(BF16))
        return ent[1:n + 1]

    def mm(self, a, b, dims=_NN):
        return _dot(self.parts(a, 1)[0], self.parts(b, 1)[0], dims)


CHUNK_FIELDS = {"ra": F32, "aa": F32, "bi": BF16, "ki": BF16, "bh": BF16, "kh": BF16,
                "vm0": BF16, "vm1": BF16}


def _decay_chunk(ops, rc, lwc, kc, vsw, ac, bc):
    incl = (lax.broadcasted_iota(jnp.int32, (CHUNK, CHUNK), 0)
            >= lax.broadcasted_iota(jnp.int32, (CHUNK, CHUNK), 1))
    tri = jnp.where(incl, 1.0, 0.0).astype(BF16)
    lw = sum(jnp.dot(tri, piece, preferred_element_type=F32) for piece in ops.parts(lwc, 2))
    lw_end = lw[CHUNK - 1:CHUNK, :]
    lw_mid = lw[CHUNK // 2 - 1:CHUNK // 2, :]
    e_in = jnp.exp(lw)
    e_ex = jnp.exp(lw - lwc)
    e_inv = jnp.exp(lw_mid - lw)
    e_end = jnp.exp(lw_end - lw)
    lo_lane = lax.broadcasted_iota(jnp.int32, vsw.shape, 1) < HEAD
    return dict(vm0=jnp.where(lo_lane, 0.0, vsw), vm1=jnp.where(lo_lane, vsw, 0.0),
                g_end=jnp.exp(lw_end), e_mid=jnp.exp(-lw_mid), ra=rc * e_in, aa=ac * e_ex,
                bi=bc * e_inv, ki=kc * e_inv, bh=bc * e_end, kh=kc * e_end)


def _scan_matrices(ops, chunks):
    row = lax.broadcasted_iota(jnp.int32, (CHUNK, CHUNK), 0)
    col = lax.broadcasted_iota(jnp.int32, (CHUNK, CHUNK), 1)
    incl = row >= col
    strict = row > col
    eye = row == col
    lo_lane = col < HEAD
    same_half = lo_lane == (row < HEAD)
    lo_lane2 = lax.broadcasted_iota(jnp.int32, (2 * CHUNK, LANES), 1) < HEAD
    level_masks = []
    n = 1
    while n < CHUNK:
        level_masks.append((row // (2 * n) == col // (2 * n)) & ((row // n) % 2 == 1) & ((col // n) % 2 == 0))
        n *= 2

    probs = []
    for ch in chunks:
        lhs_all = jnp.concatenate([ch["aa"], ch["ra"]], axis=0) * ch["e_mid"]
        rhs_all = jnp.concatenate([ch["bi"], ch["ki"]], axis=0)
        for h in range(LANES // HEAD):
            own = lo_lane if h == 0 else ~lo_lane
            own2 = lo_lane2 if h == 0 else ~lo_lane2
            gram = ops.mm(jnp.where(own2, lhs_all, 0.0), rhs_all, _NT)
            m_ab = jnp.where(strict, gram[:CHUNK, :CHUNK], 0.0)
            probs.append(dict(ch=ch, own=own, vm=ch["vm0"] if h == 0 else ch["vm1"],
                              offs=[jnp.where(mask, m_ab, 0.0).astype(BF16) for mask in level_masks],
                              m_ak=jnp.where(strict, gram[:CHUNK, CHUNK:], 0.0),
                              m_rb=jnp.where(incl, gram[CHUNK:, :CHUNK], 0.0),
                              m_rk=jnp.where(incl, gram[CHUNK:, CHUNK:], 0.0)))
    yield

    unit = jnp.where(eye, 1.0, 0.0).astype(BF16)
    for p in probs:
        p["t"] = unit + p["offs"][0]
    n = 2
    for level in range(1, len(level_masks)):
        if n < BF16_ROWS:
            for p in probs:
                p["to"] = _dot(p["t"], p["offs"][level]).astype(BF16)
            for p in probs:
                p["t"] = p["t"] + _dot(p["to"], p["t"]).astype(BF16)
        else:
            lower = lambda t, n=n: jnp.concatenate(
                [t[i:i + n] for i in range(n, CHUNK, 2 * n)], axis=0)
            for p in probs:
                p["tl"] = lower(p["t"])
                p["to"] = _dot(p["tl"], p["offs"][level]).astype(BF16)
            for p in probs:
                tl = p["tl"] + _dot(p["to"], p["t"]).astype(BF16)
                slabs = []
                for j, i in enumerate(range(0, CHUNK, 2 * n)):
                    slabs += [p["t"][i:i + n], tl[j * n:(j + 1) * n]]
                p["t"] = jnp.concatenate(slabs, axis=0)
        n *= 2
        yield

    for p in probs:
        p["x"] = (jnp.where(p["own"], p["ch"]["aa"], 0.0) + ops.mm(p["m_ak"], p["vm"])).astype(BF16)
    yield
    for p in probs:
        p["tx"] = _dot(p["t"], p["x"]).astype(BF16)
    yield
    for p in probs:
        p["y"] = jnp.where(p["own"], p["ch"]["ra"], 0.0) + ops.mm(
            jnp.concatenate([p["m_rb"], p["m_rk"]], axis=1),
            jnp.concatenate([p["tx"], p["vm"]], axis=0))
    yield

    for i, ch in enumerate(chunks):
        p0, p1 = probs[2 * i], probs[2 * i + 1]
        pq = ops.mm(jnp.concatenate([ch["bh"], ch["kh"]], axis=0),
                    jnp.concatenate([jnp.concatenate([p0["tx"], p1["tx"]], axis=1),
                                     jnp.concatenate([p0["vm"], p1["vm"]], axis=1)], axis=0),
                    _TN)
        pq = jnp.where(row < HEAD, pq[:, :LANES], pq[:, LANES:])
        ch["p"] = jnp.where(same_half, pq, 0.0) + jnp.where(eye, ch["g_end"], 0.0)
        ch["q"] = jnp.where(same_half, 0.0, pq)
        ch["r"] = jnp.where(lo_lane, p0["y"], p1["y"])
        ch["y0"] = jnp.where(lo_lane, p1["y"], p0["y"])


def _interleave(main, *side):
    for _ in main:
        for gen in side:
            next(gen, None)
    for gen in side:
        for _ in gen:
            pass


def _rwkv_kernel(*refs, n_pairs, n_tblocks, n_tiles, cast_steps):
    n_cast = len(cast_steps)
    (r_ref, k_ref, v_ref, lora_ref, chan_ref, chanf_ref, mixl_ref, w2_ref, a2_ref, g2_ref) = refs[:10]
    cast_in, o_ref, cast_out = refs[10:10 + n_cast], refs[10 + n_cast], refs[11 + n_cast:11 + 2 * n_cast]
    (crkv_ref, clora_ref, z_ref, twd_s, ad_s, sgd_s, r3_s, k3_s, v3_s, g3_s,
     y_s, gend_s, p_s, q_s, rr_s, y0_s, *decay_s) = refs[11 + 2 * n_cast:]

    tb = r_ref.shape[0]
    n_chunks = tb // CHUNK
    s = pl.program_id(0)
    sa = jnp.minimum(s, n_tiles - 1)
    sf = jnp.maximum(s - 2, 0)
    hp, t_a = sa % n_pairs, (sa // n_pairs) % n_tblocks
    hpf, t_f = sf % n_pairs, (sf // n_pairs) % n_tblocks
    slot_w, slot_r = s % 2, (s + 1) % 2
    keep_w, keep_r = s % 3, (s + 1) % 3

    @pl.when(s == 0)
    def _():
        for ref in (crkv_ref, clora_ref, z_ref, r3_s, k3_s, v3_s, g3_s, gend_s, p_s, q_s, rr_s, y0_s,
                    *decay_s):
            ref[...] = jnp.zeros_like(ref)

    @pl.when(hp == 0)
    def _():
        lora_in = lora_ref[...]
        carry = jnp.where(t_a == 0, 0.0, clora_ref[...])
        lora = _token_shift(lora_in, carry, mixl_ref[...])
        clora_ref[...] = lora_in[tb - 1:tb, :]
        twd = jnp.tanh(lora[:, 0:DECAY_LORA])
        twd_hi = twd.astype(BF16)
        twd_s[0] = twd_hi
        twd_s[1] = (twd - twd_hi.astype(F32)).astype(BF16)
        ad_s[...] = lora[:, DECAY_LORA:DECAY_LORA + ICLR_LORA].astype(BF16)
        sgd_s[...] = jax.nn.sigmoid(lora[:, DECAY_LORA + ICLR_LORA:]).astype(BF16)

    ops = _Operands()

    def finish():
        chanf = chanf_ref[...]
        z = jnp.where(t_f == 0, 0.0, z_ref[hpf])
        for c in range(n_chunks):
            i = slot_r * n_chunks + c
            rz = ops.mm(jnp.concatenate([rr_s[i], p_s[i]], axis=0), z)
            y_s[c * CHUNK:(c + 1) * CHUNK, :] = rz[:CHUNK] + y0_s[i]
            z = rz[CHUNK:] + q_s[i]
            yield
        z_ref[hpf] = z
        r_k, ln_w, ln_b = chanf[7:8], chanf[8:9], chanf[9:10]
        y = pltpu.roll(y_s[...], HEAD, axis=1)
        mu = _head_sum(y) * (1.0 / HEAD)
        yc = y - mu
        var = _head_sum(yc * yc) * (1.0 / HEAD)
        yn = yc * lax.rsqrt(var + GN_EPS) * ln_w + ln_b
        bonus = _head_sum(r3_s[keep_r] * k3_s[keep_r] * r_k) * v3_s[keep_r]
        o_ref[...] = ((yn + bonus) * g3_s[keep_r]).astype(o_ref.dtype)
        yield

    def matrices():
        chunks = []
        for c in range(n_chunks):
            i = slot_r * n_chunks + c
            ch = {name: ref[i] for name, ref in zip(CHUNK_FIELDS, decay_s)}
            ch["g_end"], ch["e_mid"] = gend_s[i, 0:1, :], gend_s[i, 1:2, :]
            chunks.append(ch)
        yield from _scan_matrices(ops, chunks)
        for c, ch in enumerate(chunks):
            i = slot_w * n_chunks + c
            p_s[i], q_s[i] = ch["p"].astype(p_s.dtype), ch["q"]
            rr_s[i], y0_s[i] = ch["r"].astype(rr_s.dtype), ch["y0"]
        yield

    def prepare():
        chan = chan_ref[...]
        mix_r, mix_k, mix_v = chan[0:1], chan[1:2], chan[2:3]
        w0, a0, k_k, k_a = chan[3:4], chan[4:5], chan[5:6], chan[6:7]
        r_in, k_in, v_in = r_ref[...], k_ref[...], v_ref[...]
        carry = jnp.where(t_a == 0, 0.0, crkv_ref[hp])
        r = _token_shift(r_in, carry[0:1], mix_r)
        k = _token_shift(k_in, carry[1:2], mix_k)
        v = _token_shift(v_in, carry[2:3], mix_v)
        crkv_ref[hp, 0:1, :] = r_in[tb - 1:tb, :]
        crkv_ref[hp, 1:2, :] = k_in[tb - 1:tb, :]
        crkv_ref[hp, 2:3, :] = v_in[tb - 1:tb, :]
        yield
        w2_hi = w2_ref[...].astype(BF16)
        w2_lo = (w2_ref[...] - w2_hi.astype(F32)).astype(BF16)
        twd_hi, twd_lo = twd_s[0], twd_s[1]
        dw = w0 + (_dot(twd_hi, w2_hi) + (_dot(twd_hi, w2_lo) + _dot(twd_lo, w2_hi)))
        logw = -DECAY_SCALE * jax.nn.sigmoid(dw)
        alr = jax.nn.sigmoid(a0 + _dot(ad_s[...], a2_ref[...].astype(BF16)))
        g3_s[keep_w] = _dot(sgd_s[...], g2_ref[...].astype(BF16))
        yield
        kk = k * k_k
        kk = kk * lax.rsqrt(jnp.maximum(_head_sum(kk * kk), KK_EPS * KK_EPS))
        k_mod = k * (1.0 + (alr - 1.0) * k_a)
        vsw = pltpu.roll(v, HEAD, axis=1)
        a, b = -kk, kk * alr
        r3_s[keep_w], k3_s[keep_w], v3_s[keep_w] = r, k_mod, v
        yield
        for c in range(n_chunks):
            sl = slice(c * CHUNK, (c + 1) * CHUNK)
            ch = _decay_chunk(ops, r[sl], logw[sl], k_mod[sl], vsw[sl], a[sl], b[sl])
            i = slot_w * n_chunks + c
            for name, ref in zip(CHUNK_FIELDS, decay_s):
                ref[i] = ch[name].astype(ref.dtype)
            gend_s[i, 0:1, :] = ch["g_end"]
            gend_s[i, 1:2, :] = ch["e_mid"]
            yield

    def casts():
        for w_ref, wb_ref in zip(cast_in, cast_out):
            wb_ref[...] = w_ref[...].astype(wb_ref.dtype)
            yield

    _interleave(prepare(), matrices(), finish(), casts())


def _rwkv(p, chan, mix_lora, w2, a2, g2, batch, seq, offs, tb, cast):
    m = p.shape[0]
    c = w2.shape[1]
    n_pairs = c // LANES
    n_tblocks = seq // tb
    n_tiles = batch * n_tblocks * n_pairs
    n_chunks = tb // CHUNK
    gate_w = LORA_PAD - DECAY_LORA - ICLR_LORA
    prep = lambda s: jnp.minimum(s, n_tiles - 1)
    fin = lambda s: jnp.maximum(s - 2, 0)
    col = lambda name: lambda s: (prep(s) // n_pairs, offs[name] // LANES + prep(s) % n_pairs)
    weight = lambda s: (0, prep(s) % n_pairs)
    tile = lambda: pltpu.VMEM((tb, LANES), F32)
    tile3 = lambda: pltpu.VMEM((3, tb, LANES), F32)
    mats = lambda dtype=F32: pltpu.VMEM((2 * n_chunks, CHUNK, LANES), dtype)
    cast_steps, cast_specs = [], []
    for w in cast:
        steps = min(n_tiles, w.shape[0] // BF16_ROWS)
        assert w.shape[0] % (steps * BF16_ROWS) == 0, w.shape
        cast_steps.append(steps)
        cast_specs.append(pl.BlockSpec((w.shape[0] // steps, w.shape[1]),
                                       lambda s, steps=steps: (jnp.minimum(s, steps - 1), 0)))
    outs = pl.pallas_call(
        functools.partial(_rwkv_kernel, n_pairs=n_pairs, n_tblocks=n_tblocks, n_tiles=n_tiles,
                          cast_steps=tuple(cast_steps)),
        grid=(n_tiles + 2,),
        in_specs=[pl.BlockSpec((tb, LANES), col("r")),
                  pl.BlockSpec((tb, LANES), col("k")),
                  pl.BlockSpec((tb, LANES), col("v")),
                  pl.BlockSpec((tb, LORA_PAD), lambda s: (prep(s) // n_pairs, offs["lora"] // LORA_PAD)),
                  pl.BlockSpec((CHAN_ROWS, LANES), weight),
                  pl.BlockSpec((CHAN_ROWS, LANES), lambda s: (0, fin(s) % n_pairs)),
                  pl.BlockSpec((1, LORA_PAD), lambda s: (0, 0)),
                  pl.BlockSpec((DECAY_LORA, LANES), weight),
                  pl.BlockSpec((ICLR_LORA, LANES), weight),
                  pl.BlockSpec((gate_w, LANES), weight)] + cast_specs,
        out_specs=[pl.BlockSpec((tb, LANES), lambda s: (fin(s) // n_pairs, fin(s) % n_pairs))] + cast_specs,
        out_shape=[jax.ShapeDtypeStruct((m, c), BF16)]
                  + [jax.ShapeDtypeStruct(w.shape, BF16) for w in cast],
        scratch_shapes=[pltpu.VMEM((n_pairs, SUBLANES, LANES), F32),
                        pltpu.VMEM((1, LORA_PAD), F32),
                        pltpu.VMEM((n_pairs, CHUNK, LANES), F32),
                        pltpu.VMEM((2, tb, DECAY_LORA), BF16),
                        pltpu.VMEM((tb, ICLR_LORA), BF16),
                        pltpu.VMEM((tb, gate_w), BF16),
                        tile3(), tile3(), tile3(), tile3(),
                        tile(),
                        pltpu.VMEM((2 * n_chunks, SUBLANES, LANES), F32),
                        mats(BF16), mats(), mats(BF16), mats()]
                       + [mats(dtype) for dtype in CHUNK_FIELDS.values()],
        compiler_params=_cparams(("arbitrary",)),
        name="rwkv7",
    )(p, p, p, p, chan, chan, mix_lora, w2, a2, g2, *cast)
    return outs[0], outs[1:]


def _attn_stages(sink_ref, q, kv, bias, store, n_heads):
    n_kv = n_heads // ATTN_GROUP
    lo_lane = lax.broadcasted_iota(jnp.int32, (WINDOW, LANES), 1) < HEAD
    keys = [kv[:, g * HEAD:(g + 1) * HEAD] for g in range(n_kv)]
    values = [kv[:, (n_kv + g) * HEAD:(n_kv + g + 1) * HEAD] for g in range(n_kv)]
    for g0 in range(0, n_heads, ATTN_BATCH):
        heads = range(g0, min(g0 + ATTN_BATCH, n_heads))
        s = {h: _dot_nt(q[:, h * HEAD:(h + 1) * HEAD], keys[h // ATTN_GROUP]) + bias(h) for h in heads}
        yield
        mx = {h: jnp.maximum(jnp.max(s[h], axis=-1, keepdims=True), sink_ref[h]) for h in heads}
        e = {h: jnp.exp(s[h] - mx[h]) for h in heads}
        yield
        denom = {h: jnp.sum(e[h], axis=-1, keepdims=True) + jnp.exp(sink_ref[h] - mx[h]) for h in heads}
        pv = {h: _dot(e[h].astype(BF16), values[h // ATTN_GROUP]) for h in heads}
        yield
        for h0 in heads[::2]:
            pair = (jnp.concatenate([pv[h0], pv[h0 + 1]], axis=1)
                    * jnp.where(lo_lane, 1.0 / denom[h0], 1.0 / denom[h0 + 1]))
            store(h0, pair)
        yield


def _attn_merge_kernel(sink_ref, q_ref, kvc_ref, kvp_ref,
                       yr_ref, gate_ref, x_ref, mod_ref, wbr_ref, wba_ref, wout_ref, g2_ref,
                       x1_ref, h2_ref, bias_s, ya_s, *, blocks_per_seq, n_heads, n_tiles):
    tm, d = x_ref.shape
    s = pl.program_id(0)
    first_block = (jnp.minimum(s, n_tiles - 1) * (tm // WINDOW)) % blocks_per_seq == 0

    @pl.when(s == 0)
    def _():
        ya_s[...] = jnp.zeros_like(ya_s)
        qi = lax.broadcasted_iota(jnp.int32, (WINDOW, 2 * WINDOW), 0)
        kj = lax.broadcasted_iota(jnp.int32, (WINDOW, 2 * WINDOW), 1)
        dist = qi + WINDOW - kj
        valid = (dist >= 0) & (dist < WINDOW)
        for first in (0, 1):
            neg_dist = jnp.where(valid & (kj >= WINDOW) if first == 0 else valid, -dist.astype(F32), -jnp.inf)
            for h in range(n_heads):
                bias_s[first * n_heads + h] = (2.0 ** (-8.0 * (h + 1) / n_heads)) * neg_dist

    def attention():
        kv = jnp.concatenate([kvp_ref[...], kvc_ref[...]], axis=0).astype(BF16)
        q = (q_ref[...] * (HEAD ** -0.5)).astype(BF16)
        for blk in range(tm // WINDOW):
            rows = slice(blk * WINDOW, (blk + 1) * WINDOW)
            table = jnp.where(first_block, 0, 1) * n_heads if blk == 0 else n_heads

            def store(h0, pair, rows=rows):
                ya_s[s % 2, rows, h0 * HEAD:(h0 + 2) * HEAD] = pair.astype(ya_s.dtype)

            yield from _attn_stages(sink_ref, q[rows], kv[blk * WINDOW:(blk + 2) * WINDOW],
                                    lambda h, table=table: bias_s[table + h], store, n_heads)

    def merge():
        yr, ya = yr_ref[...], ya_s[(s + 1) % 2]
        merged = []
        for c0 in range(0, d, MXU_N):
            cols, cols_a = slice(c0, c0 + MXU_N), slice(d + c0, d + c0 + MXU_N)
            merged.append((jax.nn.sigmoid(gate_ref[:, cols]) * _dot(yr, wbr_ref[:, cols])
                           + jax.nn.sigmoid(gate_ref[:, cols_a]) * _dot(ya, wba_ref[:, cols])).astype(BF16))
            yield
        merged = jnp.concatenate(merged, axis=1)
        for c0 in range(0, d, MXU_N):
            cols = slice(c0, c0 + MXU_N)
            x1_ref[:, cols] = x_ref[:, cols] + mod_ref[0, 2:3, cols] * _dot(merged, wout_ref[:, cols])
            yield
        h2 = _rms_norm(x1_ref[...], g2_ref[...]) * (1.0 + mod_ref[0, 4:5, :]) + mod_ref[0, 3:4, :]
        h2_ref[...] = h2.astype(BF16)
        yield

    _interleave(merge(), attention())


def _attn_merge(yr, p, sinks, x2, mod, wbr, wba, wout, gain2, rows_per_batch, offs, tm):
    m, d = x2.shape
    c = yr.shape[1]
    n_heads = sinks.shape[0]
    qc = n_heads * HEAD
    kvc = 2 * (n_heads // ATTN_GROUP) * HEAD
    n_tiles = m // tm
    tpb = rows_per_batch // tm
    wpt = tm // WINDOW
    att = lambda s: jnp.minimum(s, n_tiles - 1)
    mrg = lambda s: jnp.maximum(s - 1, 0)
    const = lambda shape: pl.BlockSpec(shape, lambda s: (0, 0), pipeline_mode=pl.Buffered(1))
    return pl.pallas_call(
        functools.partial(_attn_merge_kernel, blocks_per_seq=rows_per_batch // WINDOW, n_heads=n_heads,
                          n_tiles=n_tiles),
        grid=(n_tiles + 1,),
        in_specs=[pl.BlockSpec(memory_space=pltpu.SMEM),
                  pl.BlockSpec((tm, qc), lambda s: (att(s), offs["q"] // qc)),
                  pl.BlockSpec((tm, kvc), lambda s: (att(s), offs["kv"] // kvc)),
                  pl.BlockSpec((WINDOW, kvc), lambda s: (jnp.maximum(att(s) * wpt - 1, 0), offs["kv"] // kvc)),
                  pl.BlockSpec((tm, c), lambda s: (mrg(s), 0)),
                  pl.BlockSpec((tm, 2 * d), lambda s: (mrg(s), 0)),
                  pl.BlockSpec((tm, d), lambda s: (mrg(s), 0)),
                  pl.BlockSpec((1, 6, d), lambda s: (mrg(s) // tpb, 0, 0)),
                  const((c, d)), const((c, d)), const((d, d)),
                  pl.BlockSpec((1, d), lambda s: (0, 0))],
        out_specs=[pl.BlockSpec((tm, d), lambda s: (mrg(s), 0)),
                   pl.BlockSpec((tm, d), lambda s: (mrg(s), 0))],
        out_shape=[jax.ShapeDtypeStruct((m, d), F32), jax.ShapeDtypeStruct((m, d), BF16)],
        scratch_shapes=[pltpu.VMEM((2 * n_heads, WINDOW, 2 * WINDOW), F32),
                        pltpu.VMEM((2, tm, qc), BF16)],
        compiler_params=_cparams(("arbitrary",)),
        name="attn_merge",
    )(sinks, p, p, p, yr, p, x2, mod, wbr, wba, wout, gain2)


def _mlp_kernel(h_ref, wu_ref, wd_ref, x1_ref, mod_ref, fg_ref, o_ref, acc_ref):
    f = pl.program_id(1)

    @pl.when(f == 0)
    def _():
        acc_ref[...] = jnp.zeros_like(acc_ref)

    u = _dot(h_ref[...], wu_ref[...])
    act = jnp.square(jnp.maximum(u, 0.0)).astype(BF16)
    acc_ref[...] += _dot(act, wd_ref[...])

    @pl.when(f == pl.num_programs(1) - 1)
    def _():
        x2 = x1_ref[...] + mod_ref[0, 5:6, :] * acc_ref[...]
        o_ref[...] = _rms_norm(x2, fg_ref[...])


def _mlp(h2, wu, wd, x1, mod, final_gain, rows_per_batch, tm, tf):
    m, d = h2.shape
    f = wu.shape[1]
    tpb = rows_per_batch // tm
    return pl.pallas_call(
        _mlp_kernel,
        grid=(m // tm, f // tf),
        in_specs=[pl.BlockSpec((tm, d), lambda i, j: (i, 0)),
                  pl.BlockSpec((d, tf), lambda i, j: (0, j)),
                  pl.BlockSpec((tf, d), lambda i, j: (j, 0)),
                  pl.BlockSpec((tm, d), lambda i, j: (i, 0)),
                  pl.BlockSpec((1, 6, d), lambda i, j: (i // tpb, 0, 0)),
                  pl.BlockSpec((1, d), lambda i, j: (0, 0))],
        out_specs=pl.BlockSpec((tm, d), lambda i, j: (i, 0)),
        out_shape=jax.ShapeDtypeStruct((m, d), F32),
        scratch_shapes=[pltpu.VMEM((tm, d), F32)],
        compiler_params=_cparams(("arbitrary", "arbitrary")),
        name="mlp",
    )(h2, wu, wd, x1, mod, final_gain)


def _forward(x, c, w_ada, b_ada, norm1_gain, w_in, b_in, rwkv_mix, rwkv_w0, rwkv_w2, rwkv_a0,
             rwkv_a2, rwkv_g2, rwkv_k_k, rwkv_k_a, rwkv_r_k, rwkv_ln_w, rwkv_ln_b, attn_sinks,
             w_branch_rwkv, w_branch_attn, w_out, norm2_gain, w_up, w_down, final_gain):
    batch, seq, d = x.shape
    depth = w_ada.shape[0]
    c_rwkv = rwkv_w0.shape[1]
    n_heads = attn_sinks.shape[1]
    qc = n_heads * HEAD
    kvc = 2 * (n_heads // ATTN_GROUP) * HEAD
    lora_w = DECAY_LORA + ICLR_LORA + GATE_LORA
    rwkv_cols = 3 * c_rwkv + lora_w
    attn_cols = qc + kvc
    offs = {"gate": 0, "r": 2 * d, "k": 2 * d + c_rwkv, "v": 2 * d + 2 * c_rwkv, "q": 2 * d + 3 * c_rwkv}
    offs["lora"] = offs["q"] + qc
    offs["kv"] = offs["lora"] + LORA_PAD
    n_pack = offs["kv"] + kvc

    segments = ((rwkv_cols + attn_cols, 2 * d, offs["gate"]), (0, 3 * c_rwkv, offs["r"]),
                (rwkv_cols, qc, offs["q"]), (3 * c_rwkv, lora_w, offs["lora"]),
                (rwkv_cols + qc, kvc, offs["kv"]))
    zero_rows = (offs["lora"] + lora_w, offs["kv"])

    def pack_cols(t):
        out = jnp.zeros(t.shape[:-1] + (n_pack,), t.dtype)
        for src, width, dst in segments:
            out = out.at[..., dst:dst + width].set(t[..., src:src + width])
        return out

    tm_in = min(1024, seq)
    tn_in = 1792 if n_pack % 1792 == 0 else LANES
    tb = min(1024, seq)
    tm_merge = min(256, seq)
    tm_mlp = min(512, seq)
    tf = min(1024, w_up.shape[2])

    assert depth == 1, "single-layer block"
    layer = lambda t: t.reshape(t.shape[1:])
    x2 = x.reshape(batch * seq, d)
    for l in range(depth):
        mod = _ada(c, layer(w_ada), b_ada[l]).reshape(batch, 6, d)
        w_pack = _pack_weight_t(layer(w_in).T, segments, zero_rows, n_pack)
        b_pack = pack_cols(b_in[l]).reshape(1, n_pack)
        p = _inproj(x2, mod, norm1_gain[l].reshape(1, d), w_pack, b_pack, seq, tm_in, tn_in)

        mix = rwkv_mix[l]
        chan = jnp.stack([mix[:c_rwkv], mix[c_rwkv:2 * c_rwkv], mix[2 * c_rwkv:3 * c_rwkv],
                          rwkv_w0[l], rwkv_a0[l], rwkv_k_k[l], rwkv_k_a[l], rwkv_r_k[l].reshape(c_rwkv),
                          rwkv_ln_w[l], rwkv_ln_b[l]])
        chan = jnp.pad(chan, ((0, CHAN_ROWS - chan.shape[0]), (0, 0)))
        mix_lora = jnp.pad(mix[3 * c_rwkv:], (0, LORA_PAD - lora_w)).reshape(1, LORA_PAD)
        g2_pad = jnp.pad(rwkv_g2[l], ((0, LORA_PAD - lora_w), (0, 0)))
        y_rwkv, (wbr, wba, wout, wup, wdown) = _rwkv(
            p, chan, mix_lora, rwkv_w2[l], rwkv_a2[l], g2_pad, batch, seq, offs, tb,
            cast=[layer(w_branch_rwkv), layer(w_branch_attn), layer(w_out), layer(w_up), layer(w_down)])
        x2, h2 = _attn_merge(y_rwkv, p, attn_sinks[l], x2, mod, wbr, wba, wout,
                             norm2_gain[l].reshape(1, d), seq, offs, tm_merge)
        x2 = _mlp(h2, wup, wdown, x2, mod, final_gain.reshape(1, d), seq, tm_mlp, tf)
    return x2.reshape(batch, seq, d)


def kernel(x, c, w_ada, b_ada, norm1_gain, w_in, b_in, rwkv_mix, rwkv_w0, rwkv_w2, rwkv_a0, rwkv_a2, rwkv_g2, rwkv_k_k, rwkv_k_a, rwkv_r_k, rwkv_ln_w, rwkv_ln_b, attn_sinks, w_branch_rwkv, w_branch_attn, w_out, norm2_gain, w_up, w_down, final_gain):
    return _forward(x, c, w_ada, b_ada, norm1_gain, w_in, b_in, rwkv_mix, rwkv_w0, rwkv_w2, rwkv_a0,
                    rwkv_a2, rwkv_g2, rwkv_k_k, rwkv_k_a, rwkv_r_k, rwkv_ln_w, rwkv_ln_b, attn_sinks,
                    w_branch_rwkv, w_branch_attn, w_out, norm2_gain, w_up, w_down, final_gain)
```

```python
import functools

import jax
import jax.numpy as jnp
from jax import lax
from jax.experimental import pallas as pl
from jax.experimental.pallas import tpu as pltpu

F32 = jnp.float32
BF16 = jnp.bfloat16

LANES = 128
SUBLANES = 8
BF16_ROWS = 16
MXU_N = 256
NORM_ROWS = 128
HEAD = 64
CHUNK = 128
WINDOW = 128
ATTN_GROUP = 8
ATTN_BATCH = 8
NORM_EPS = 1e-6
GN_EPS = 64e-5
DECAY_SCALE = 0.6065306597126334
KK_EPS = 1e-12
CHAN_ROWS = 16
DECAY_LORA = 64
ICLR_LORA = 64
GATE_LORA = 160
LORA_PAD = 512
VMEM_LIMIT = 56 * 1024 * 1024
MLP_VMEM_LIMIT = 62 * 1024 * 1024


def _cparams(sem, vmem_limit=VMEM_LIMIT):
    return pltpu.CompilerParams(dimension_semantics=sem, vmem_limit_bytes=vmem_limit)


_NN = (((1,), (0,)), ((), ()))
_NT = (((1,), (1,)), ((), ()))
_TN = (((0,), (0,)), ((), ()))


def _dot(a, b, dims=_NN):
    return lax.dot_general(a, b, dims, preferred_element_type=F32)


def _dot_nt(a, b):
    return _dot(a, b, _NT)


def _rms_norm(x, gain):
    ms = jnp.mean(x * x, axis=-1, keepdims=True)
    return x * lax.rsqrt(ms + NORM_EPS) * gain


def _ada_kernel(ct_ref, w_ref, b_ref, o_ref):
    ct = ct_ref[...]
    act = ct * jax.nn.sigmoid(ct)
    for m in range(o_ref.shape[0]):
        o_ref[m:m + 1, :] = jnp.sum(w_ref[...] * act[:, m:m + 1], axis=0, keepdims=True) + b_ref[...]


def _ada(c, w, b, tn=1024):
    nb, d = c.shape
    n = w.shape[1]
    return pl.pallas_call(
        _ada_kernel,
        grid=(n // tn,),
        in_specs=[pl.BlockSpec((d, nb), lambda j: (0, 0)),
                  pl.BlockSpec((d, tn), lambda j: (0, j)),
                  pl.BlockSpec((1, tn), lambda j: (0, j))],
        out_specs=pl.BlockSpec((nb, tn), lambda j: (0, j)),
        out_shape=jax.ShapeDtypeStruct((nb, n), F32),
        compiler_params=_cparams(("arbitrary",)),
        name="adaln",
    )(c.T, w, b.reshape(1, n))


def _pack_kernel(w_ref, o_ref, *, segments, zero_rows):
    for src, width, dst in segments:
        o_ref[dst:dst + width, :] = w_ref[src:src + width, :].astype(o_ref.dtype)
    lo, hi = zero_rows
    o_ref[lo:hi, :] = jnp.zeros((hi - lo, o_ref.shape[1]), o_ref.dtype)


def _pack_weight_t(w_t, segments, zero_rows, n_pack, tk=256):
    n, k = w_t.shape
    return pl.pallas_call(
        functools.partial(_pack_kernel, segments=segments, zero_rows=zero_rows),
        grid=(k // tk,),
        in_specs=[pl.BlockSpec((n, tk), lambda i: (0, i))],
        out_specs=pl.BlockSpec((n_pack, tk), lambda i: (0, i)),
        out_shape=jax.ShapeDtypeStruct((n_pack, k), BF16),
        compiler_params=_cparams(("arbitrary",)),
        name="pack_w_in",
    )(w_t)


def _inproj_kernel(x_ref, mod_ref, g_ref, w_ref, b_ref, o_ref, h_ref):
    @pl.when(pl.program_id(1) == 0)
    def _():
        scale = g_ref[...] * (1.0 + mod_ref[0, 1:2, :])
        shift = mod_ref[0, 0:1, :]

        def slab(c, carry):
            rows = pl.ds(pl.multiple_of(c * NORM_ROWS, NORM_ROWS), NORM_ROWS)
            x = x_ref[rows, :]
            ms = jnp.mean(x * x, axis=-1, keepdims=True)
            h_ref[rows, :] = (x * lax.rsqrt(ms + NORM_EPS) * scale + shift).astype(BF16)
            return carry

        lax.fori_loop(0, x_ref.shape[0] // NORM_ROWS, slab, 0)

    o_ref[...] = _dot_nt(h_ref[...], w_ref[...]) + b_ref[...]


def _inproj(x2, mod, gain, w_t, b, rows_per_batch, tm, tn):
    m, d = x2.shape
    n = w_t.shape[0]
    tpb = rows_per_batch // tm
    return pl.pallas_call(
        _inproj_kernel,
        grid=(m // tm, n // tn),
        in_specs=[pl.BlockSpec((tm, d), lambda i, j: (i, 0)),
                  pl.BlockSpec((1, 6, d), lambda i, j: (i // tpb, 0, 0)),
                  pl.BlockSpec((1, d), lambda i, j: (0, 0)),
                  pl.BlockSpec((tn, d), lambda i, j: (j, 0)),
                  pl.BlockSpec((1, tn), lambda i, j: (0, j))],
        out_specs=pl.BlockSpec((tm, tn), lambda i, j: (i, j)),
        out_shape=jax.ShapeDtypeStruct((m, n), F32),
        scratch_shapes=[pltpu.VMEM((tm, d), BF16)],
        compiler_params=_cparams(("arbitrary", "arbitrary")),
        name="inproj",
    )(x2, mod, gain, w_t, b)


def _head_sum(x):
    lo = lax.broadcasted_iota(jnp.int32, x.shape, 1) < HEAD
    s0 = jnp.sum(jnp.where(lo, x, 0.0), axis=-1, keepdims=True)
    s1 = jnp.sum(jnp.where(lo, 0.0, x), axis=-1, keepdims=True)
    return jnp.where(lo, s0, s1)


def _token_shift(x, carry_row, mix):
    rolled = pltpu.roll(x, 1, axis=0)
    head = rolled[:SUBLANES]
    first = lax.broadcasted_iota(jnp.int32, head.shape, 0) == 0
    prev = jnp.concatenate([jnp.where(first, carry_row, head), rolled[SUBLANES:]], axis=0)
    return x + (prev - x) * mix


class _Operands:
    def __init__(self):
        self._parts = {}

    def parts(self, x, n):
        ent = self._parts.setdefault(id(x), [x])
        while len(ent) - 1 < n:
            rest = x
            for piece in ent[1:]:
                rest = rest - piece.astype(F32)
            ent.append(rest.astype(BF16))
        return ent[1:n + 1]

    def mm(self, a, b, dims=_NN):
        return _dot(self.parts(a, 1)[0], self.parts(b, 1)[0], dims)


CHUNK_FIELDS = {"ra": F32, "aa": F32, "bi": BF16, "ki": BF16, "bh": BF16, "kh": BF16,
                "vm0": BF16, "vm1": BF16}


def _decay_chunk(ops, rc, lwc, kc, vsw, ac, bc):
    incl = (lax.broadcasted_iota(jnp.int32, (CHUNK, CHUNK), 0)
            >= lax.broadcasted_iota(jnp.int32, (CHUNK, CHUNK), 1))
    tri = jnp.where(incl, 1.0, 0.0).astype(BF16)
    lw = sum(jnp.dot(tri, piece, preferred_element_type=F32) for piece in ops.parts(lwc, 2))
    lw_end = lw[CHUNK - 1:CHUNK, :]
    lw_mid = lw[CHUNK // 2 - 1:CHUNK // 2, :]
    e_in = jnp.exp(lw)
    e_ex = jnp.exp(lw - lwc)
    e_inv = jnp.exp(lw_mid - lw)
    e_end = jnp.exp(lw_end - lw)
    lo_lane = lax.broadcasted_iota(jnp.int32, vsw.shape, 1) < HEAD
    return dict(vm0=jnp.where(lo_lane, 0.0, vsw), vm1=jnp.where(lo_lane, vsw, 0.0),
                g_end=jnp.exp(lw_end), e_mid=jnp.exp(-lw_mid), ra=rc * e_in, aa=ac * e_ex,
                bi=bc * e_inv, ki=kc * e_inv, bh=bc * e_end, kh=kc * e_end)


def _scan_matrices(ops, chunks):
    row = lax.broadcasted_iota(jnp.int32, (CHUNK, CHUNK), 0)
    col = lax.broadcasted_iota(jnp.int32, (CHUNK, CHUNK), 1)
    incl = row >= col
    strict = row > col
    eye = row == col
    lo_lane = col < HEAD
    same_half = lo_lane == (row < HEAD)
    lo_lane2 = lax.broadcasted_iota(jnp.int32, (2 * CHUNK, LANES), 1) < HEAD
    level_masks = []
    n = 1
    while n < CHUNK:
        level_masks.append((row // (2 * n) == col // (2 * n)) & ((row // n) % 2 == 1) & ((col // n) % 2 == 0))
        n *= 2

    probs = []
    for ch in chunks:
        lhs_all = jnp.concatenate([ch["aa"], ch["ra"]], axis=0) * ch["e_mid"]
        rhs_all = jnp.concatenate([ch["bi"], ch["ki"]], axis=0)
        for h in range(LANES // HEAD):
            own = lo_lane if h == 0 else ~lo_lane
            own2 = lo_lane2 if h == 0 else ~lo_lane2
            gram = ops.mm(jnp.where(own2, lhs_all, 0.0), rhs_all, _NT)
            m_ab = jnp.where(strict, gram[:CHUNK, :CHUNK], 0.0)
            probs.append(dict(ch=ch, own=own, vm=ch["vm0"] if h == 0 else ch["vm1"],
                              offs=[jnp.where(mask, m_ab, 0.0).astype(BF16) for mask in level_masks],
                              m_ak=jnp.where(strict, gram[:CHUNK, CHUNK:], 0.0),
                              m_rb=jnp.where(incl, gram[CHUNK:, :CHUNK], 0.0),
                              m_rk=jnp.where(incl, gram[CHUNK:, CHUNK:], 0.0)))
    yield

    unit = jnp.where(eye, 1.0, 0.0).astype(BF16)
    for p in probs:
        p["t"] = unit + p["offs"][0]
    n = 2
    for level in range(1, len(level_masks)):
        if n < BF16_ROWS:
            for p in probs:
                p["to"] = _dot(p["t"], p["offs"][level]).astype(BF16)
            for p in probs:
                p["t"] = p["t"] + _dot(p["to"], p["t"]).astype(BF16)
        else:
            lower = lambda t, n=n: jnp.concatenate(
                [t[i:i + n] for i in range(n, CHUNK, 2 * n)], axis=0)
            for p in probs:
                p["tl"] = lower(p["t"])
                p["to"] = _dot(p["tl"], p["offs"][level]).astype(BF16)
            for p in probs:
                tl = p["tl"] + _dot(p["to"], p["t"]).astype(BF16)
                slabs = []
                for j, i in enumerate(range(0, CHUNK, 2 * n)):
                    slabs += [p["t"][i:i + n], tl[j * n:(j + 1) * n]]
                p["t"] = jnp.concatenate(slabs, axis=0)
        n *= 2
        yield

    for p in probs:
        p["x"] = (jnp.where(p["own"], p["ch"]["aa"], 0.0) + ops.mm(p["m_ak"], p["vm"])).astype(BF16)
    yield
    for p in probs:
        p["tx"] = _dot(p["t"], p["x"]).astype(BF16)
    yield
    for p in probs:
        p["y"] = jnp.where(p["own"], p["ch"]["ra"], 0.0) + ops.mm(
            jnp.concatenate([p["m_rb"], p["m_rk"]], axis=1),
            jnp.concatenate([p["tx"], p["vm"]], axis=0))
    yield

    for i, ch in enumerate(chunks):
        p0, p1 = probs[2 * i], probs[2 * i + 1]
        pq = ops.mm(jnp.concatenate([ch["bh"], ch["kh"]], axis=0),
                    jnp.concatenate([jnp.concatenate([p0["tx"], p1["tx"]], axis=1),
                                     jnp.concatenate([p0["vm"], p1["vm"]], axis=1)], axis=0),
                    _TN)
        pq = jnp.where(row < HEAD, pq[:, :LANES], pq[:, LANES:])
        ch["p"] = jnp.where(same_half, pq, 0.0) + jnp.where(eye, ch["g_end"], 0.0)
        ch["q"] = jnp.where(same_half, 0.0, pq)
        ch["r"] = jnp.where(lo_lane, p0["y"], p1["y"])
        ch["y0"] = jnp.where(lo_lane, p1["y"], p0["y"])


def _interleave(main, *side):
    for _ in main:
        for gen in side:
            next(gen, None)
    for gen in side:
        for _ in gen:
            pass


def _rwkv_kernel(*refs, n_pairs, n_tblocks, n_tiles, cast_steps):
    n_cast = len(cast_steps)
    (r_ref, k_ref, v_ref, lora_ref, chan_ref, chanf_ref, mixl_ref, w2_ref, a2_ref, g2_ref) = refs[:10]
    cast_in, o_ref, cast_out = refs[10:10 + n_cast], refs[10 + n_cast], refs[11 + n_cast:11 + 2 * n_cast]
    (crkv_ref, clora_ref, z_ref, twd_s, ad_s, sgd_s, r3_s, k3_s, v3_s, g3_s,
     y_s, gend_s, p_s, q_s, rr_s, y0_s, *decay_s) = refs[11 + 2 * n_cast:]

    tb = r_ref.shape[0]
    n_chunks = tb // CHUNK
    s = pl.program_id(0)
    sa = jnp.minimum(s, n_tiles - 1)
    sf = jnp.maximum(s - 2, 0)
    hp, t_a = sa % n_pairs, (sa // n_pairs) % n_tblocks
    hpf, t_f = sf % n_pairs, (sf // n_pairs) % n_tblocks
    slot_w, slot_r = s % 2, (s + 1) % 2
    keep_w, keep_r = s % 3, (s + 1) % 3

    @pl.when(s == 0)
    def _():
        for ref in (crkv_ref, clora_ref, z_ref, r3_s, k3_s, v3_s, g3_s, gend_s, p_s, q_s, rr_s, y0_s,
                    *decay_s):
            ref[...] = jnp.zeros_like(ref)

    @pl.when(hp == 0)
    def _():
        lora_in = lora_ref[...]
        carry = jnp.where(t_a == 0, 0.0, clora_ref[...])
        lora = _token_shift(lora_in, carry, mixl_ref[...])
        clora_ref[...] = lora_in[tb - 1:tb, :]
        twd = jnp.tanh(lora[:, 0:DECAY_LORA])
        twd_hi = twd.astype(BF16)
        twd_s[0] = twd_hi
        twd_s[1] = (twd - twd_hi.astype(F32)).astype(BF16)
        ad_s[...] = lora[:, DECAY_LORA:DECAY_LORA + ICLR_LORA].astype(BF16)
        sgd_s[...] = jax.nn.sigmoid(lora[:, DECAY_LORA + ICLR_LORA:]).astype(BF16)

    ops = _Operands()

    def finish():
        chanf = chanf_ref[...]
        z = jnp.where(t_f == 0, 0.0, z_ref[hpf])
        for c in range(n_chunks):
            i = slot_r * n_chunks + c
            rz = ops.mm(jnp.concatenate([rr_s[i], p_s[i]], axis=0), z)
            y_s[c * CHUNK:(c + 1) * CHUNK, :] = rz[:CHUNK] + y0_s[i]
            z = rz[CHUNK:] + q_s[i]
            yield
        z_ref[hpf] = z
        r_k, ln_w, ln_b = chanf[7:8], chanf[8:9], chanf[9:10]
        y = pltpu.roll(y_s[...], HEAD, axis=1)
        mu = _head_sum(y) * (1.0 / HEAD)
        yc = y - mu
        var = _head_sum(yc * yc) * (1.0 / HEAD)
        yn = yc * lax.rsqrt(var + GN_EPS) * ln_w + ln_b
        bonus = _head_sum(r3_s[keep_r] * k3_s[keep_r] * r_k) * v3_s[keep_r]
        o_ref[...] = ((yn + bonus) * g3_s[keep_r]).astype(o_ref.dtype)
        yield

    def matrices():
        chunks = []
        for c in range(n_chunks):
            i = slot_r * n_chunks + c
            ch = {name: ref[i] for name, ref in zip(CHUNK_FIELDS, decay_s)}
            ch["g_end"], ch["e_mid"] = gend_s[i, 0:1, :], gend_s[i, 1:2, :]
            chunks.append(ch)
        yield from _scan_matrices(ops, chunks)
        for c, ch in enumerate(chunks):
            i = slot_w * n_chunks + c
            p_s[i], q_s[i] = ch["p"].astype(p_s.dtype), ch["q"]
            rr_s[i], y0_s[i] = ch["r"].astype(rr_s.dtype), ch["y0"]
        yield

    def prepare():
        chan = chan_ref[...]
        mix_r, mix_k, mix_v = chan[0:1], chan[1:2], chan[2:3]
        w0, a0, k_k, k_a = chan[3:4], chan[4:5], chan[5:6], chan[6:7]
        r_in, k_in, v_in = r_ref[...], k_ref[...], v_ref[...]
        carry = jnp.where(t_a == 0, 0.0, crkv_ref[hp])
        r = _token_shift(r_in, carry[0:1], mix_r)
        k = _token_shift(k_in, carry[1:2], mix_k)
        v = _token_shift(v_in, carry[2:3], mix_v)
        crkv_ref[hp, 0:1, :] = r_in[tb - 1:tb, :]
        crkv_ref[hp, 1:2, :] = k_in[tb - 1:tb, :]
        crkv_ref[hp, 2:3, :] = v_in[tb - 1:tb, :]
        yield
        w2_hi = w2_ref[...].astype(BF16)
        w2_lo = (w2_ref[...] - w2_hi.astype(F32)).astype(BF16)
        twd_hi, twd_lo = twd_s[0], twd_s[1]
        dw = w0 + (_dot(twd_hi, w2_hi) + (_dot(twd_hi, w2_lo) + _dot(twd_lo, w2_hi)))
        logw = -DECAY_SCALE * jax.nn.sigmoid(dw)
        alr = jax.nn.sigmoid(a0 + _dot(ad_s[...], a2_ref[...].astype(BF16)))
        g3_s[keep_w] = _dot(sgd_s[...], g2_ref[...].astype(BF16))
        yield
        kk = k * k_k
        kk = kk * lax.rsqrt(jnp.maximum(_head_sum(kk * kk), KK_EPS * KK_EPS))
        k_mod = k * (1.0 + (alr - 1.0) * k_a)
        vsw = pltpu.roll(v, HEAD, axis=1)
        a, b = -kk, kk * alr
        r3_s[keep_w], k3_s[keep_w], v3_s[keep_w] = r, k_mod, v
        yield
        for c in range(n_chunks):
            sl = slice(c * CHUNK, (c + 1) * CHUNK)
            ch = _decay_chunk(ops, r[sl], logw[sl], k_mod[sl], vsw[sl], a[sl], b[sl])
            i = slot_w * n_chunks + c
            for name, ref in zip(CHUNK_FIELDS, decay_s):
                ref[i] = ch[name].astype(ref.dtype)
            gend_s[i, 0:1, :] = ch["g_end"]
            gend_s[i, 1:2, :] = ch["e_mid"]
            yield

    def casts():
        for w_ref, wb_ref in zip(cast_in, cast_out):
            wb_ref[...] = w_ref[...].astype(wb_ref.dtype)
            yield

    _interleave(prepare(), matrices(), finish(), casts())


def _rwkv(p, chan, mix_lora, w2, a2, g2, batch, seq, offs, tb, cast):
    m = p.shape[0]
    c = w2.shape[1]
    n_pairs = c // LANES
    n_tblocks = seq // tb
    n_tiles = batch * n_tblocks * n_pairs
    n_chunks = tb // CHUNK
    gate_w = LORA_PAD - DECAY_LORA - ICLR_LORA
    prep = lambda s: jnp.minimum(s, n_tiles - 1)
    fin = lambda s: jnp.maximum(s - 2, 0)
    col = lambda name: lambda s: (prep(s) // n_pairs, offs[name] // LANES + prep(s) % n_pairs)
    weight = lambda s: (0, prep(s) % n_pairs)
    tile = lambda: pltpu.VMEM((tb, LANES), F32)
    tile3 = lambda: pltpu.VMEM((3, tb, LANES), F32)
    mats = lambda dtype=F32: pltpu.VMEM((2 * n_chunks, CHUNK, LANES), dtype)
    cast_steps, cast_specs = [], []
    for w in cast:
        steps = min(n_tiles, w.shape[0] // BF16_ROWS)
        assert w.shape[0] % (steps * BF16_ROWS) == 0, w.shape
        cast_steps.append(steps)
        cast_specs.append(pl.BlockSpec((w.shape[0] // steps, w.shape[1]),
                                       lambda s, steps=steps: (jnp.minimum(s, steps - 1), 0)))
    outs = pl.pallas_call(
        functools.partial(_rwkv_kernel, n_pairs=n_pairs, n_tblocks=n_tblocks, n_tiles=n_tiles,
                          cast_steps=tuple(cast_steps)),
        grid=(n_tiles + 2,),
        in_specs=[pl.BlockSpec((tb, LANES), col("r")),
                  pl.BlockSpec((tb, LANES), col("k")),
                  pl.BlockSpec((tb, LANES), col("v")),
                  pl.BlockSpec((tb, LORA_PAD), lambda s: (prep(s) // n_pairs, offs["lora"] // LORA_PAD)),
                  pl.BlockSpec((CHAN_ROWS, LANES), weight),
                  pl.BlockSpec((CHAN_ROWS, LANES), lambda s: (0, fin(s) % n_pairs)),
                  pl.BlockSpec((1, LORA_PAD), lambda s: (0, 0)),
                  pl.BlockSpec((DECAY_LORA, LANES), weight),
                  pl.BlockSpec((ICLR_LORA, LANES), weight),
                  pl.BlockSpec((gate_w, LANES), weight)] + cast_specs,
        out_specs=[pl.BlockSpec((tb, LANES), lambda s: (fin(s) // n_pairs, fin(s) % n_pairs))] + cast_specs,
        out_shape=[jax.ShapeDtypeStruct((m, c), BF16)]
                  + [jax.ShapeDtypeStruct(w.shape, BF16) for w in cast],
        scratch_shapes=[pltpu.VMEM((n_pairs, SUBLANES, LANES), F32),
                        pltpu.VMEM((1, LORA_PAD), F32),
                        pltpu.VMEM((n_pairs, CHUNK, LANES), F32),
                        pltpu.VMEM((2, tb, DECAY_LORA), BF16),
                        pltpu.VMEM((tb, ICLR_LORA), BF16),
                        pltpu.VMEM((tb, gate_w), BF16),
                        tile3(), tile3(), tile3(), tile3(),
                        tile(),
                        pltpu.VMEM((2 * n_chunks, SUBLANES, LANES), F32),
                        mats(BF16), mats(), mats(BF16), mats()]
                       + [mats(dtype) for dtype in CHUNK_FIELDS.values()],
        compiler_params=_cparams(("arbitrary",)),
        name="rwkv7",
    )(p, p, p, p, chan, chan, mix_lora, w2, a2, g2, *cast)
    return outs[0], outs[1:]


def _attn_stages(sink_ref, q, kv, bias, store, n_heads):
    n_kv = n_heads // ATTN_GROUP
    lo_lane = lax.broadcasted_iota(jnp.int32, (WINDOW, LANES), 1) < HEAD
    keys = [kv[:, g * HEAD:(g + 1) * HEAD] for g in range(n_kv)]
    values = [kv[:, (n_kv + g) * HEAD:(n_kv + g + 1) * HEAD] for g in range(n_kv)]
    for g0 in range(0, n_heads, ATTN_BATCH):
        heads = range(g0, min(g0 + ATTN_BATCH, n_heads))
        s = {h: _dot_nt(q[:, h * HEAD:(h + 1) * HEAD], keys[h // ATTN_GROUP]) + bias(h) for h in heads}
        yield
        mx = {h: jnp.maximum(jnp.max(s[h], axis=-1, keepdims=True), sink_ref[h]) for h in heads}
        e = {h: jnp.exp(s[h] - mx[h]) for h in heads}
        yield
        denom = {h: jnp.sum(e[h], axis=-1, keepdims=True) + jnp.exp(sink_ref[h] - mx[h]) for h in heads}
        pv = {h: _dot(e[h].astype(BF16), values[h // ATTN_GROUP]) for h in heads}
        yield
        for h0 in heads[::2]:
            pair = (jnp.concatenate([pv[h0], pv[h0 + 1]], axis=1)
                    * jnp.where(lo_lane, 1.0 / denom[h0], 1.0 / denom[h0 + 1]))
            store(h0, pair)
        yield


def _attn_merge_kernel(sink_ref, q_ref, kvc_ref, kvp_ref,
                       yr_ref, gate_ref, x_ref, mod_ref, wbr_ref, wba_ref, wout_ref, g2_ref,
                       x1_ref, h2_ref, bias_s, ya_s, *, blocks_per_seq, n_heads, n_tiles):
    tm, d = x_ref.shape
    s = pl.program_id(0)
    first_block = (jnp.minimum(s, n_tiles - 1) * (tm // WINDOW)) % blocks_per_seq == 0

    @pl.when(s == 0)
    def _():
        ya_s[...] = jnp.zeros_like(ya_s)
        qi = lax.broadcasted_iota(jnp.int32, (WINDOW, 2 * WINDOW), 0)
        kj = lax.broadcasted_iota(jnp.int32, (WINDOW, 2 * WINDOW), 1)
        dist = qi + WINDOW - kj
        valid = (dist >= 0) & (dist < WINDOW)
        for first in (0, 1):
            neg_dist = jnp.where(valid & (kj >= WINDOW) if first == 0 else valid, -dist.astype(F32), -jnp.inf)
            for h in range(n_heads):
                bias_s[first * n_heads + h] = (2.0 ** (-8.0 * (h + 1) / n_heads)) * neg_dist

    def attention():
        kv = jnp.concatenate([kvp_ref[...], kvc_ref[...]], axis=0).astype(BF16)
        q = (q_ref[...] * (HEAD ** -0.5)).astype(BF16)
        for blk in range(tm // WINDOW):
            rows = slice(blk * WINDOW, (blk + 1) * WINDOW)
            table = jnp.where(first_block, 0, 1) * n_heads if blk == 0 else n_heads

            def store(h0, pair, rows=rows):
                ya_s[s % 2, rows, h0 * HEAD:(h0 + 2) * HEAD] = pair.astype(ya_s.dtype)

            yield from _attn_stages(sink_ref, q[rows], kv[blk * WINDOW:(blk + 2) * WINDOW],
                                    lambda h, table=table: bias_s[table + h], store, n_heads)

    def merge():
        yr, ya = yr_ref[...], ya_s[(s + 1) % 2]
        merged = []
        for c0 in range(0, d, MXU_N):
            cols, cols_a = slice(c0, c0 + MXU_N), slice(d + c0, d + c0 + MXU_N)
            merged.append((jax.nn.sigmoid(gate_ref[:, cols]) * _dot(yr, wbr_ref[:, cols])
                           + jax.nn.sigmoid(gate_ref[:, cols_a]) * _dot(ya, wba_ref[:, cols])).astype(BF16))
            yield
        merged = jnp.concatenate(merged, axis=1)
        for c0 in range(0, d, MXU_N):
            cols = slice(c0, c0 + MXU_N)
            x1_ref[:, cols] = x_ref[:, cols] + mod_ref[0, 2:3, cols] * _dot(merged, wout_ref[:, cols])
            yield
        h2 = _rms_norm(x1_ref[...], g2_ref[...]) * (1.0 + mod_ref[0, 4:5, :]) + mod_ref[0, 3:4, :]
        h2_ref[...] = h2.astype(BF16)
        yield

    _interleave(merge(), attention())


def _attn_merge(yr, p, sinks, x2, mod, wbr, wba, wout, gain2, rows_per_batch, offs, tm):
    m, d = x2.shape
    c = yr.shape[1]
    n_heads = sinks.shape[0]
    qc = n_heads * HEAD
    kvc = 2 * (n_heads // ATTN_GROUP) * HEAD
    n_tiles = m // tm
    tpb = rows_per_batch // tm
    wpt = tm // WINDOW
    att = lambda s: jnp.minimum(s, n_tiles - 1)
    mrg = lambda s: jnp.maximum(s - 1, 0)
    const = lambda shape: pl.BlockSpec(shape, lambda s: (0, 0), pipeline_mode=pl.Buffered(1))
    return pl.pallas_call(
        functools.partial(_attn_merge_kernel, blocks_per_seq=rows_per_batch // WINDOW, n_heads=n_heads,
                          n_tiles=n_tiles),
        grid=(n_tiles + 1,),
        in_specs=[pl.BlockSpec(memory_space=pltpu.SMEM),
                  pl.BlockSpec((tm, qc), lambda s: (att(s), offs["q"] // qc)),
                  pl.BlockSpec((tm, kvc), lambda s: (att(s), offs["kv"] // kvc)),
                  pl.BlockSpec((WINDOW, kvc), lambda s: (jnp.maximum(att(s) * wpt - 1, 0), offs["kv"] // kvc)),
                  pl.BlockSpec((tm, c), lambda s: (mrg(s), 0)),
                  pl.BlockSpec((tm, 2 * d), lambda s: (mrg(s), 0)),
                  pl.BlockSpec((tm, d), lambda s: (mrg(s), 0)),
                  pl.BlockSpec((1, 6, d), lambda s: (mrg(s) // tpb, 0, 0)),
                  const((c, d)), const((c, d)), const((d, d)),
                  pl.BlockSpec((1, d), lambda s: (0, 0))],
        out_specs=[pl.BlockSpec((tm, d), lambda s: (mrg(s), 0)),
                   pl.BlockSpec((tm, d), lambda s: (mrg(s), 0))],
        out_shape=[jax.ShapeDtypeStruct((m, d), F32), jax.ShapeDtypeStruct((m, d), BF16)],
        scratch_shapes=[pltpu.VMEM((2 * n_heads, WINDOW, 2 * WINDOW), F32),
                        pltpu.VMEM((2, tm, qc), BF16)],
        compiler_params=_cparams(("arbitrary",)),
        name="attn_merge",
    )(sinks, p, p, p, yr, p, x2, mod, wbr, wba, wout, gain2)


def _mlp_kernel(h_ref, wu_ref, wd_ref, x1_ref, mod_ref, fg_ref, o_ref):
    f = pl.program_id(1)

    @pl.when(f == 0)
    def _():
        o_ref[...] = jnp.zeros_like(o_ref)

    u = _dot(h_ref[...], wu_ref[...])
    act = jnp.square(jnp.maximum(u, 0.0)).astype(BF16)
    o_ref[...] += _dot(act, wd_ref[...])

    @pl.when(f == pl.num_programs(1) - 1)
    def _():
        x2 = x1_ref[...] + mod_ref[0, 5:6, :] * o_ref[...]
        o_ref[...] = _rms_norm(x2, fg_ref[...])


def _mlp(h2, wu, wd, x1, mod, final_gain, rows_per_batch, tm, tf):
    m, d = h2.shape
    f = wu.shape[1]
    tpb = rows_per_batch // tm
    return pl.pallas_call(
        _mlp_kernel,
        grid=(m // tm, f // tf),
        in_specs=[pl.BlockSpec((tm, d), lambda i, j: (i, 0)),
                  pl.BlockSpec((d, tf), lambda i, j: (0, j)),
                  pl.BlockSpec((tf, d), lambda i, j: (j, 0)),
                  pl.BlockSpec((tm, d), lambda i, j: (i, 0)),
                  pl.BlockSpec((1, 6, d), lambda i, j: (i // tpb, 0, 0)),
                  pl.BlockSpec((1, d), lambda i, j: (0, 0))],
        out_specs=pl.BlockSpec((tm, d), lambda i, j: (i, 0)),
        out_shape=jax.ShapeDtypeStruct((m, d), F32),
        compiler_params=_cparams(("arbitrary", "arbitrary"), MLP_VMEM_LIMIT),
        name="mlp",
    )(h2, wu, wd, x1, mod, final_gain)


def _forward(x, c, w_ada, b_ada, norm1_gain, w_in, b_in, rwkv_mix, rwkv_w0, rwkv_w2, rwkv_a0,
             rwkv_a2, rwkv_g2, rwkv_k_k, rwkv_k_a, rwkv_r_k, rwkv_ln_w, rwkv_ln_b, attn_sinks,
             w_branch_rwkv, w_branch_attn, w_out, norm2_gain, w_up, w_down, final_gain):
    batch, seq, d = x.shape
    depth = w_ada.shape[0]
    c_rwkv = rwkv_w0.shape[1]
    n_heads = attn_sinks.shape[1]
    qc = n_heads * HEAD
    kvc = 2 * (n_heads // ATTN_GROUP) * HEAD
    lora_w = DECAY_LORA + ICLR_LORA + GATE_LORA
    rwkv_cols = 3 * c_rwkv + lora_w
    attn_cols = qc + kvc
    offs = {"gate": 0, "r": 2 * d, "k": 2 * d + c_rwkv, "v": 2 * d + 2 * c_rwkv, "q": 2 * d + 3 * c_rwkv}
    offs["lora"] = offs["q"] + qc
    offs["kv"] = offs["lora"] + LORA_PAD
    n_pack = offs["kv"] + kvc

    segments = ((rwkv_cols + attn_cols, 2 * d, offs["gate"]), (0, 3 * c_rwkv, offs["r"]),
                (rwkv_cols, qc, offs["q"]), (3 * c_rwkv, lora_w, offs["lora"]),
                (rwkv_cols + qc, kvc, offs["kv"]))
    zero_rows = (offs["lora"] + lora_w, offs["kv"])

    def pack_cols(t):
        out = jnp.zeros(t.shape[:-1] + (n_pack,), t.dtype)
        for src, width, dst in segments:
            out = out.at[..., dst:dst + width].set(t[..., src:src + width])
        return out

    tm_in = min(1024, seq)
    tn_in = 1792 if n_pack % 1792 == 0 else LANES
    tb = min(1024, seq)
    tm_merge = min(256, seq)
    tm_mlp = min(512, seq)
    tf = min(2048, w_up.shape[2])

    assert depth == 1, "single-layer block"
    layer = lambda t: t.reshape(t.shape[1:])
    x2 = x.reshape(batch * seq, d)
    for l in range(depth):
        mod = _ada(c, layer(w_ada), b_ada[l]).reshape(batch, 6, d)
        w_pack = _pack_weight_t(layer(w_in).T, segments, zero_rows, n_pack)
        b_pack = pack_cols(b_in[l]).reshape(1, n_pack)
        p = _inproj(x2, mod, norm1_gain[l].reshape(1, d), w_pack, b_pack, seq, tm_in, tn_in)

        mix = rwkv_mix[l]
        chan = jnp.stack([mix[:c_rwkv], mix[c_rwkv:2 * c_rwkv], mix[2 * c_rwkv:3 * c_rwkv],
                          rwkv_w0[l], rwkv_a0[l], rwkv_k_k[l], rwkv_k_a[l], rwkv_r_k[l].reshape(c_rwkv),
                          rwkv_ln_w[l], rwkv_ln_b[l]])
        chan = jnp.pad(chan, ((0, CHAN_ROWS - chan.shape[0]), (0, 0)))
        mix_lora = jnp.pad(mix[3 * c_rwkv:], (0, LORA_PAD - lora_w)).reshape(1, LORA_PAD)
        g2_pad = jnp.pad(rwkv_g2[l], ((0, LORA_PAD - lora_w), (0, 0)))
        y_rwkv, (wbr, wba, wout, wup, wdown) = _rwkv(
            p, chan, mix_lora, rwkv_w2[l], rwkv_a2[l], g2_pad, batch, seq, offs, tb,
            cast=[layer(w_branch_rwkv), layer(w_branch_attn), layer(w_out), layer(w_up), layer(w_down)])
        x2, h2 = _attn_merge(y_rwkv, p, attn_sinks[l], x2, mod, wbr, wba, wout,
                             norm2_gain[l].reshape(1, d), seq, offs, tm_merge)
        x2 = _mlp(h2, wup, wdown, x2, mod, final_gain.reshape(1, d), seq, tm_mlp, tf)
    return x2.reshape(batch, seq, d)


def kernel(x, c, w_ada, b_ada, norm1_gain, w_in, b_in, rwkv_mix, rwkv_w0, rwkv_w2, rwkv_a0, rwkv_a2, rwkv_g2, rwkv_k_k, rwkv_k_a, rwkv_r_k, rwkv_ln_w, rwkv_ln_b, attn_sinks, w_branch_rwkv, w_branch_attn, w_out, norm2_gain, w_up, w_down, final_gain):
    return _forward(x, c, w_ada, b_ada, norm1_gain, w_in, b_in, rwkv_mix, rwkv_w0, rwkv_w2, rwkv_a0,
                    rwkv_a2, rwkv_g2, rwkv_k_k, rwkv_k_a, rwkv_r_k, rwkv_ln_w, rwkv_ln_b, attn_sinks,
                    w_branch_rwkv, w_branch_attn, w_out, norm2_gain, w_up, w_down, final_gain)
```

```python
import functools

import jax
import jax.numpy as jnp
from jax import lax
from jax.experimental import pallas as pl
from jax.experimental.pallas import tpu as pltpu

F32 = jnp.float32
BF16 = jnp.bfloat16

LANES = 128
SUBLANES = 8
BF16_ROWS = 16
MXU_N = 256
NORM_ROWS = 128
HEAD = 64
CHUNK = 128
WINDOW = 128
ATTN_GROUP = 8
ATTN_BATCH = 8
NORM_EPS = 1e-6
GN_EPS = 64e-5
DECAY_SCALE = 0.6065306597126334
KK_EPS = 1e-12
CHAN_ROWS = 16
DECAY_LORA = 64
ICLR_LORA = 64
GATE_LORA = 160
LORA_PAD = 512
VMEM_LIMIT = 56 * 1024 * 1024
MLP_VMEM_LIMIT = 62 * 1024 * 1024


def _cparams(sem, vmem_limit=VMEM_LIMIT):
    return pltpu.CompilerParams(dimension_semantics=sem, vmem_limit_bytes=vmem_limit)


_NN = (((1,), (0,)), ((), ()))
_NT = (((1,), (1,)), ((), ()))
_TN = (((0,), (0,)), ((), ()))


def _dot(a, b, dims=_NN):
    return lax.dot_general(a, b, dims, preferred_element_type=F32)


def _dot_nt(a, b):
    return _dot(a, b, _NT)


def _rms_norm(x, gain):
    ms = jnp.mean(x * x, axis=-1, keepdims=True)
    return x * lax.rsqrt(ms + NORM_EPS) * gain


def _ada_kernel(ct_ref, w_ref, b_ref, o_ref):
    ct = ct_ref[...]
    act = ct * jax.nn.sigmoid(ct)
    for m in range(o_ref.shape[0]):
        o_ref[m:m + 1, :] = jnp.sum(w_ref[...] * act[:, m:m + 1], axis=0, keepdims=True) + b_ref[...]


def _ada(c, w, b, tn=1024):
    nb, d = c.shape
    n = w.shape[1]
    return pl.pallas_call(
        _ada_kernel,
        grid=(n // tn,),
        in_specs=[pl.BlockSpec((d, nb), lambda j: (0, 0)),
                  pl.BlockSpec((d, tn), lambda j: (0, j)),
                  pl.BlockSpec((1, tn), lambda j: (0, j))],
        out_specs=pl.BlockSpec((nb, tn), lambda j: (0, j)),
        out_shape=jax.ShapeDtypeStruct((nb, n), F32),
        compiler_params=_cparams(("arbitrary",)),
        name="adaln",
    )(c.T, w, b.reshape(1, n))


def _pack_kernel(w_ref, o_ref, *, segments, zero_rows):
    for src, width, dst in segments:
        o_ref[dst:dst + width, :] = w_ref[src:src + width, :].astype(o_ref.dtype)
    lo, hi = zero_rows
    o_ref[lo:hi, :] = jnp.zeros((hi - lo, o_ref.shape[1]), o_ref.dtype)


def _pack_weight_t(w_t, segments, zero_rows, n_pack, tk=256):
    n, k = w_t.shape
    return pl.pallas_call(
        functools.partial(_pack_kernel, segments=segments, zero_rows=zero_rows),
        grid=(k // tk,),
        in_specs=[pl.BlockSpec((n, tk), lambda i: (0, i))],
        out_specs=pl.BlockSpec((n_pack, tk), lambda i: (0, i)),
        out_shape=jax.ShapeDtypeStruct((n_pack, k), BF16),
        compiler_params=_cparams(("arbitrary",)),
        name="pack_w_in",
    )(w_t)


def _inproj_kernel(x_ref, mod_ref, g_ref, w_ref, b_ref, o_ref, h_ref):
    @pl.when(pl.program_id(1) == 0)
    def _():
        scale = g_ref[...] * (1.0 + mod_ref[0, 1:2, :])
        shift = mod_ref[0, 0:1, :]

        def slab(c, carry):
            rows = pl.ds(pl.multiple_of(c * NORM_ROWS, NORM_ROWS), NORM_ROWS)
            x = x_ref[rows, :]
            ms = jnp.mean(x * x, axis=-1, keepdims=True)
            h_ref[rows, :] = (x * lax.rsqrt(ms + NORM_EPS) * scale + shift).astype(BF16)
            return carry

        lax.fori_loop(0, x_ref.shape[0] // NORM_ROWS, slab, 0)

    o_ref[...] = _dot_nt(h_ref[...], w_ref[...]) + b_ref[...]


def _inproj(x2, mod, gain, w_t, b, rows_per_batch, tm, tn):
    m, d = x2.shape
    n = w_t.shape[0]
    tpb = rows_per_batch // tm
    return pl.pallas_call(
        _inproj_kernel,
        grid=(m // tm, n // tn),
        in_specs=[pl.BlockSpec((tm, d), lambda i, j: (i, 0)),
                  pl.BlockSpec((1, 6, d), lambda i, j: (i // tpb, 0, 0)),
                  pl.BlockSpec((1, d), lambda i, j: (0, 0)),
                  pl.BlockSpec((tn, d), lambda i, j: (j, 0)),
                  pl.BlockSpec((1, tn), lambda i, j: (0, j))],
        out_specs=pl.BlockSpec((tm, tn), lambda i, j: (i, j)),
        out_shape=jax.ShapeDtypeStruct((m, n), F32),
        scratch_shapes=[pltpu.VMEM((tm, d), BF16)],
        compiler_params=_cparams(("arbitrary", "arbitrary")),
        name="inproj",
    )(x2, mod, gain, w_t, b)


def _head_sum(x):
    lo = lax.broadcasted_iota(jnp.int32, x.shape, 1) < HEAD
    s0 = jnp.sum(jnp.where(lo, x, 0.0), axis=-1, keepdims=True)
    s1 = jnp.sum(jnp.where(lo, 0.0, x), axis=-1, keepdims=True)
    return jnp.where(lo, s0, s1)


def _token_shift(x, carry_row, mix):
    rolled = pltpu.roll(x, 1, axis=0)
    head = rolled[:SUBLANES]
    first = lax.broadcasted_iota(jnp.int32, head.shape, 0) == 0
    prev = jnp.concatenate([jnp.where(first, carry_row, head), rolled[SUBLANES:]], axis=0)
    return x + (prev - x) * mix


class _Operands:
    def __init__(self):
        self._parts = {}

    def parts(self, x, n):
        ent = self._parts.setdefault(id(x), [x])
        while len(ent) - 1 < n:
            rest = x
            for piece in ent[1:]:
                rest = rest - piece.astype(F32)
            ent.append(rest.astype(BF16))
        return ent[1:n + 1]

    def mm(self, a, b, dims=_NN):
        return _dot(self.parts(a, 1)[0], self.parts(b, 1)[0], dims)


CHUNK_FIELDS = {"ra": F32, "aa": F32, "bi": BF16, "ki": BF16, "bh": BF16, "kh": BF16,
                "vm0": BF16, "vm1": BF16}


def _decay_chunk(ops, rc, lwc, kc, vsw, ac, bc):
    incl = (lax.broadcasted_iota(jnp.int32, (CHUNK, CHUNK), 0)
            >= lax.broadcasted_iota(jnp.int32, (CHUNK, CHUNK), 1))
    tri = jnp.where(incl, 1.0, 0.0).astype(BF16)
    lw = sum(jnp.dot(tri, piece, preferred_element_type=F32) for piece in ops.parts(lwc, 2))
    lw_end = lw[CHUNK - 1:CHUNK, :]
    lw_mid = lw[CHUNK // 2 - 1:CHUNK // 2, :]
    e_in = jnp.exp(lw)
    e_ex = jnp.exp(lw - lwc)
    e_inv = jnp.exp(lw_mid - lw)
    e_end = jnp.exp(lw_end - lw)
    lo_lane = lax.broadcasted_iota(jnp.int32, vsw.shape, 1) < HEAD
    return dict(vm0=jnp.where(lo_lane, 0.0, vsw), vm1=jnp.where(lo_lane, vsw, 0.0),
                g_end=jnp.exp(lw_end), e_mid=jnp.exp(-lw_mid), ra=rc * e_in, aa=ac * e_ex,
                bi=bc * e_inv, ki=kc * e_inv, bh=bc * e_end, kh=kc * e_end)


def _scan_matrices(ops, chunks):
    row = lax.broadcasted_iota(jnp.int32, (CHUNK, CHUNK), 0)
    col = lax.broadcasted_iota(jnp.int32, (CHUNK, CHUNK), 1)
    incl = row >= col
    strict = row > col
    eye = row == col
    lo_lane = col < HEAD
    same_half = lo_lane == (row < HEAD)
    lo_lane2 = lax.broadcasted_iota(jnp.int32, (2 * CHUNK, LANES), 1) < HEAD
    level_masks = []
    n = 1
    while n < CHUNK:
        level_masks.append((row // (2 * n) == col // (2 * n)) & ((row // n) % 2 == 1) & ((col // n) % 2 == 0))
        n *= 2

    probs = []
    for ch in chunks:
        lhs_all = jnp.concatenate([ch["aa"], ch["ra"]], axis=0) * ch["e_mid"]
        rhs_all = jnp.concatenate([ch["bi"], ch["ki"]], axis=0)
        for h in range(LANES // HEAD):
            own = lo_lane if h == 0 else ~lo_lane
            own2 = lo_lane2 if h == 0 else ~lo_lane2
            gram = ops.mm(jnp.where(own2, lhs_all, 0.0), rhs_all, _NT)
            m_ab = jnp.where(strict, gram[:CHUNK, :CHUNK], 0.0)
            probs.append(dict(ch=ch, own=own, vm=ch["vm0"] if h == 0 else ch["vm1"],
                              offs=[jnp.where(mask, m_ab, 0.0).astype(BF16) for mask in level_masks],
                              m_ak=jnp.where(strict, gram[:CHUNK, CHUNK:], 0.0),
                              m_rb=jnp.where(incl, gram[CHUNK:, :CHUNK], 0.0),
                              m_rk=jnp.where(incl, gram[CHUNK:, CHUNK:], 0.0)))
    yield

    unit = jnp.where(eye, 1.0, 0.0).astype(BF16)
    for p in probs:
        p["t"] = unit + p["offs"][0]
    n = 2
    for level in range(1, len(level_masks)):
        if n < BF16_ROWS:
            for p in probs:
                p["to"] = _dot(p["t"], p["offs"][level]).astype(BF16)
            for p in probs:
                p["t"] = p["t"] + _dot(p["to"], p["t"]).astype(BF16)
        else:
            lower = lambda t, n=n: jnp.concatenate(
                [t[i:i + n] for i in range(n, CHUNK, 2 * n)], axis=0)
            for p in probs:
                p["tl"] = lower(p["t"])
                p["to"] = _dot(p["tl"], p["offs"][level]).astype(BF16)
            for p in probs:
                tl = p["tl"] + _dot(p["to"], p["t"]).astype(BF16)
                slabs = []
                for j, i in enumerate(range(0, CHUNK, 2 * n)):
                    slabs += [p["t"][i:i + n], tl[j * n:(j + 1) * n]]
                p["t"] = jnp.concatenate(slabs, axis=0)
        n *= 2
        yield

    for p in probs:
        p["x"] = (jnp.where(p["own"], p["ch"]["aa"], 0.0) + ops.mm(p["m_ak"], p["vm"])).astype(BF16)
    yield
    for p in probs:
        p["tx"] = _dot(p["t"], p["x"]).astype(BF16)
    yield
    for p in probs:
        p["y"] = jnp.where(p["own"], p["ch"]["ra"], 0.0) + ops.mm(
            jnp.concatenate([p["m_rb"], p["m_rk"]], axis=1),
            jnp.concatenate([p["tx"], p["vm"]], axis=0))
    yield

    for i, ch in enumerate(chunks):
        p0, p1 = probs[2 * i], probs[2 * i + 1]
        pq = ops.mm(jnp.concatenate([ch["bh"], ch["kh"]], axis=0),
                    jnp.concatenate([jnp.concatenate([p0["tx"], p1["tx"]], axis=1),
                                     jnp.concatenate([p0["vm"], p1["vm"]], axis=1)], axis=0),
                    _TN)
        pq = jnp.where(row < HEAD, pq[:, :LANES], pq[:, LANES:])
        ch["p"] = jnp.where(same_half, pq, 0.0) + jnp.where(eye, ch["g_end"], 0.0)
        ch["q"] = jnp.where(same_half, 0.0, pq)
        ch["r"] = jnp.where(lo_lane, p0["y"], p1["y"])
        ch["y0"] = jnp.where(lo_lane, p1["y"], p0["y"])


def _interleave(main, *side):
    for _ in main:
        for gen in side:
            next(gen, None)
    for gen in side:
        for _ in gen:
            pass


def _rwkv_kernel(*refs, n_pairs, n_tblocks, n_tiles, cast_steps):
    n_cast = len(cast_steps)
    (r_ref, k_ref, v_ref, lora_ref, chan_ref, chanf_ref, mixl_ref, w2_ref, a2_ref, g2_ref) = refs[:10]
    cast_in, o_ref, cast_out = refs[10:10 + n_cast], refs[10 + n_cast], refs[11 + n_cast:11 + 2 * n_cast]
    (crkv_ref, clora_ref, z_ref, twd_s, ad_s, sgd_s, r3_s, k3_s, v3_s, g3_s,
     y_s, gend_s, p_s, q_s, rr_s, y0_s, *decay_s) = refs[11 + 2 * n_cast:]

    tb = r_ref.shape[0]
    n_chunks = tb // CHUNK
    s = pl.program_id(0)
    sa = jnp.minimum(s, n_tiles - 1)
    sf = jnp.maximum(s - 2, 0)
    hp, t_a = sa % n_pairs, (sa // n_pairs) % n_tblocks
    hpf, t_f = sf % n_pairs, (sf // n_pairs) % n_tblocks
    slot_w, slot_r = s % 2, (s + 1) % 2
    keep_w, keep_r = s % 3, (s + 1) % 3

    @pl.when(s == 0)
    def _():
        for ref in (crkv_ref, clora_ref, z_ref, r3_s, k3_s, v3_s, g3_s, gend_s, p_s, q_s, rr_s, y0_s,
                    *decay_s):
            ref[...] = jnp.zeros_like(ref)

    @pl.when(hp == 0)
    def _():
        lora_in = lora_ref[:, :clora_ref.shape[1]]
        carry = jnp.where(t_a == 0, 0.0, clora_ref[...])
        lora = _token_shift(lora_in, carry, mixl_ref[...])
        clora_ref[...] = lora_in[tb - 1:tb, :]
        twd = jnp.tanh(lora[:, 0:DECAY_LORA])
        twd_hi = twd.astype(BF16)
        twd_s[0] = twd_hi
        twd_s[1] = (twd - twd_hi.astype(F32)).astype(BF16)
        ad_s[...] = lora[:, DECAY_LORA:DECAY_LORA + ICLR_LORA].astype(BF16)
        sgd_s[...] = jax.nn.sigmoid(lora[:, DECAY_LORA + ICLR_LORA:]).astype(BF16)

    ops = _Operands()

    def finish():
        chanf = chanf_ref[...]
        z = jnp.where(t_f == 0, 0.0, z_ref[hpf])
        for c in range(n_chunks):
            i = slot_r * n_chunks + c
            rz = ops.mm(jnp.concatenate([rr_s[i], p_s[i]], axis=0), z)
            y_s[c * CHUNK:(c + 1) * CHUNK, :] = rz[:CHUNK] + y0_s[i]
            z = rz[CHUNK:] + q_s[i]
            yield
        z_ref[hpf] = z
        r_k, ln_w, ln_b = chanf[7:8], chanf[8:9], chanf[9:10]
        y = pltpu.roll(y_s[...], HEAD, axis=1)
        mu = _head_sum(y) * (1.0 / HEAD)
        yc = y - mu
        var = _head_sum(yc * yc) * (1.0 / HEAD)
        yn = yc * lax.rsqrt(var + GN_EPS) * ln_w + ln_b
        bonus = _head_sum(r3_s[keep_r] * k3_s[keep_r] * r_k) * v3_s[keep_r]
        o_ref[...] = ((yn + bonus) * g3_s[keep_r]).astype(o_ref.dtype)
        yield

    def matrices():
        chunks = []
        for c in range(n_chunks):
            i = slot_r * n_chunks + c
            ch = {name: ref[i] for name, ref in zip(CHUNK_FIELDS, decay_s)}
            ch["g_end"], ch["e_mid"] = gend_s[i, 0:1, :], gend_s[i, 1:2, :]
            chunks.append(ch)
        yield from _scan_matrices(ops, chunks)
        for c, ch in enumerate(chunks):
            i = slot_w * n_chunks + c
            p_s[i], q_s[i] = ch["p"].astype(p_s.dtype), ch["q"]
            rr_s[i], y0_s[i] = ch["r"].astype(rr_s.dtype), ch["y0"]
        yield

    def prepare():
        chan = chan_ref[...]
        mix_r, mix_k, mix_v = chan[0:1], chan[1:2], chan[2:3]
        w0, a0, k_k, k_a = chan[3:4], chan[4:5], chan[5:6], chan[6:7]
        r_in, k_in, v_in = r_ref[...], k_ref[...], v_ref[...]
        carry = jnp.where(t_a == 0, 0.0, crkv_ref[hp])
        r = _token_shift(r_in, carry[0:1], mix_r)
        k = _token_shift(k_in, carry[1:2], mix_k)
        v = _token_shift(v_in, carry[2:3], mix_v)
        crkv_ref[hp, 0:1, :] = r_in[tb - 1:tb, :]
        crkv_ref[hp, 1:2, :] = k_in[tb - 1:tb, :]
        crkv_ref[hp, 2:3, :] = v_in[tb - 1:tb, :]
        yield
        w2_hi = w2_ref[...].astype(BF16)
        w2_lo = (w2_ref[...] - w2_hi.astype(F32)).astype(BF16)
        twd_hi, twd_lo = twd_s[0], twd_s[1]
        dw = w0 + (_dot(twd_hi, w2_hi) + (_dot(twd_hi, w2_lo) + _dot(twd_lo, w2_hi)))
        logw = -DECAY_SCALE * jax.nn.sigmoid(dw)
        alr = jax.nn.sigmoid(a0 + _dot(ad_s[...], a2_ref[...].astype(BF16)))
        g3_s[keep_w] = _dot(sgd_s[...], g2_ref[...].astype(BF16))
        yield
        kk = k * k_k
        kk = kk * lax.rsqrt(jnp.maximum(_head_sum(kk * kk), KK_EPS * KK_EPS))
        k_mod = k * (1.0 + (alr - 1.0) * k_a)
        vsw = pltpu.roll(v, HEAD, axis=1)
        a, b = -kk, kk * alr
        r3_s[keep_w], k3_s[keep_w], v3_s[keep_w] = r, k_mod, v
        yield
        for c in range(n_chunks):
            sl = slice(c * CHUNK, (c + 1) * CHUNK)
            ch = _decay_chunk(ops, r[sl], logw[sl], k_mod[sl], vsw[sl], a[sl], b[sl])
            i = slot_w * n_chunks + c
            for name, ref in zip(CHUNK_FIELDS, decay_s):
                ref[i] = ch[name].astype(ref.dtype)
            gend_s[i, 0:1, :] = ch["g_end"]
            gend_s[i, 1:2, :] = ch["e_mid"]
            yield

    def casts():
        for w_ref, wb_ref in zip(cast_in, cast_out):
            wb_ref[...] = w_ref[...].astype(wb_ref.dtype)
            yield

    _interleave(prepare(), matrices(), finish(), casts())


def _rwkv(p, chan, mix_lora, w2, a2, g2, batch, seq, offs, tb, cast):
    m = p.shape[0]
    c = w2.shape[1]
    n_pairs = c // LANES
    n_tblocks = seq // tb
    n_tiles = batch * n_tblocks * n_pairs
    n_chunks = tb // CHUNK
    lora_used = mix_lora.shape[1]
    gate_w = lora_used - DECAY_LORA - ICLR_LORA
    prep = lambda s: jnp.minimum(s, n_tiles - 1)
    fin = lambda s: jnp.maximum(s - 2, 0)
    col = lambda name: lambda s: (prep(s) // n_pairs, offs[name] // LANES + prep(s) % n_pairs)
    weight = lambda s: (0, prep(s) % n_pairs)
    tile = lambda: pltpu.VMEM((tb, LANES), F32)
    tile3 = lambda: pltpu.VMEM((3, tb, LANES), F32)
    mats = lambda dtype=F32: pltpu.VMEM((2 * n_chunks, CHUNK, LANES), dtype)
    cast_steps, cast_specs = [], []
    for w in cast:
        steps = min(n_tiles, w.shape[0] // BF16_ROWS)
        assert w.shape[0] % (steps * BF16_ROWS) == 0, w.shape
        cast_steps.append(steps)
        cast_specs.append(pl.BlockSpec((w.shape[0] // steps, w.shape[1]),
                                       lambda s, steps=steps: (jnp.minimum(s, steps - 1), 0)))
    outs = pl.pallas_call(
        functools.partial(_rwkv_kernel, n_pairs=n_pairs, n_tblocks=n_tblocks, n_tiles=n_tiles,
                          cast_steps=tuple(cast_steps)),
        grid=(n_tiles + 2,),
        in_specs=[pl.BlockSpec((tb, LANES), col("r")),
                  pl.BlockSpec((tb, LANES), col("k")),
                  pl.BlockSpec((tb, LANES), col("v")),
                  pl.BlockSpec((tb, LORA_PAD), lambda s: (prep(s) // n_pairs, offs["lora"] // LORA_PAD)),
                  pl.BlockSpec((CHAN_ROWS, LANES), weight),
                  pl.BlockSpec((CHAN_ROWS, LANES), lambda s: (0, fin(s) % n_pairs)),
                  pl.BlockSpec((1, lora_used), lambda s: (0, 0)),
                  pl.BlockSpec((DECAY_LORA, LANES), weight),
                  pl.BlockSpec((ICLR_LORA, LANES), weight),
                  pl.BlockSpec((gate_w, LANES), weight)] + cast_specs,
        out_specs=[pl.BlockSpec((tb, LANES), lambda s: (fin(s) // n_pairs, fin(s) % n_pairs))] + cast_specs,
        out_shape=[jax.ShapeDtypeStruct((m, c), BF16)]
                  + [jax.ShapeDtypeStruct(w.shape, BF16) for w in cast],
        scratch_shapes=[pltpu.VMEM((n_pairs, SUBLANES, LANES), F32),
                        pltpu.VMEM((1, lora_used), F32),
                        pltpu.VMEM((n_pairs, CHUNK, LANES), F32),
                        pltpu.VMEM((2, tb, DECAY_LORA), BF16),
                        pltpu.VMEM((tb, ICLR_LORA), BF16),
                        pltpu.VMEM((tb, gate_w), BF16),
                        tile3(), tile3(), tile3(), tile3(),
                        tile(),
                        pltpu.VMEM((2 * n_chunks, SUBLANES, LANES), F32),
                        mats(BF16), mats(), mats(BF16), mats()]
                       + [mats(dtype) for dtype in CHUNK_FIELDS.values()],
        compiler_params=_cparams(("arbitrary",)),
        name="rwkv7",
    )(p, p, p, p, chan, chan, mix_lora, w2, a2, g2, *cast)
    return outs[0], outs[1:]


def _attn_stages(sink_ref, q, kv, bias, store, n_heads):
    n_kv = n_heads // ATTN_GROUP
    lo_lane = lax.broadcasted_iota(jnp.int32, (WINDOW, LANES), 1) < HEAD
    keys = [kv[:, g * HEAD:(g + 1) * HEAD] for g in range(n_kv)]
    values = [kv[:, (n_kv + g) * HEAD:(n_kv + g + 1) * HEAD] for g in range(n_kv)]
    for g0 in range(0, n_heads, ATTN_BATCH):
        heads = range(g0, min(g0 + ATTN_BATCH, n_heads))
        s = {h: _dot_nt(q[:, h * HEAD:(h + 1) * HEAD], keys[h // ATTN_GROUP]) + bias(h) for h in heads}
        yield
        mx = {h: jnp.maximum(jnp.max(s[h], axis=-1, keepdims=True), sink_ref[h]) for h in heads}
        e = {h: jnp.exp(s[h] - mx[h]) for h in heads}
        yield
        denom = {h: jnp.sum(e[h], axis=-1, keepdims=True) + jnp.exp(sink_ref[h] - mx[h]) for h in heads}
        pv = {h: _dot(e[h].astype(BF16), values[h // ATTN_GROUP]) for h in heads}
        yield
        for h0 in heads[::2]:
            pair = (jnp.concatenate([pv[h0], pv[h0 + 1]], axis=1)
                    * jnp.where(lo_lane, 1.0 / denom[h0], 1.0 / denom[h0 + 1]))
            store(h0, pair)
        yield


def _attn_merge_kernel(sink_ref, q_ref, kvc_ref, kvp_ref,
                       yr_ref, gate_ref, x_ref, mod_ref, wbr_ref, wba_ref, wout_ref, g2_ref,
                       x1_ref, h2_ref, bias_s, ya_s, *, blocks_per_seq, n_heads, n_tiles):
    tm, d = x_ref.shape
    s = pl.program_id(0)
    first_block = (jnp.minimum(s, n_tiles - 1) * (tm // WINDOW)) % blocks_per_seq == 0

    @pl.when(s == 0)
    def _():
        ya_s[...] = jnp.zeros_like(ya_s)
        qi = lax.broadcasted_iota(jnp.int32, (WINDOW, 2 * WINDOW), 0)
        kj = lax.broadcasted_iota(jnp.int32, (WINDOW, 2 * WINDOW), 1)
        dist = qi + WINDOW - kj
        valid = (dist >= 0) & (dist < WINDOW)
        for first in (0, 1):
            neg_dist = jnp.where(valid & (kj >= WINDOW) if first == 0 else valid, -dist.astype(F32), -jnp.inf)
            for h in range(n_heads):
                bias_s[first * n_heads + h] = (2.0 ** (-8.0 * (h + 1) / n_heads)) * neg_dist

    def attention():
        kv = jnp.concatenate([kvp_ref[...], kvc_ref[...]], axis=0).astype(BF16)
        q = (q_ref[...] * (HEAD ** -0.5)).astype(BF16)
        for blk in range(tm // WINDOW):
            rows = slice(blk * WINDOW, (blk + 1) * WINDOW)
            table = jnp.where(first_block, 0, 1) * n_heads if blk == 0 else n_heads

            def store(h0, pair, rows=rows):
                ya_s[s % 2, rows, h0 * HEAD:(h0 + 2) * HEAD] = pair.astype(ya_s.dtype)

            yield from _attn_stages(sink_ref, q[rows], kv[blk * WINDOW:(blk + 2) * WINDOW],
                                    lambda h, table=table: bias_s[table + h], store, n_heads)

    def merge():
        yr, ya = yr_ref[...], ya_s[(s + 1) % 2]
        merged = []
        for c0 in range(0, d, MXU_N):
            cols, cols_a = slice(c0, c0 + MXU_N), slice(d + c0, d + c0 + MXU_N)
            merged.append((jax.nn.sigmoid(gate_ref[:, cols]) * _dot(yr, wbr_ref[:, cols])
                           + jax.nn.sigmoid(gate_ref[:, cols_a]) * _dot(ya, wba_ref[:, cols])).astype(BF16))
            yield
        merged = jnp.concatenate(merged, axis=1)
        for c0 in range(0, d, MXU_N):
            cols = slice(c0, c0 + MXU_N)
            x1_ref[:, cols] = x_ref[:, cols] + mod_ref[0, 2:3, cols] * _dot(merged, wout_ref[:, cols])
            yield
        h2 = _rms_norm(x1_ref[...], g2_ref[...]) * (1.0 + mod_ref[0, 4:5, :]) + mod_ref[0, 3:4, :]
        h2_ref[...] = h2.astype(BF16)
        yield

    _interleave(merge(), attention())


def _attn_merge(yr, p, sinks, x2, mod, wbr, wba, wout, gain2, rows_per_batch, offs, tm):
    m, d = x2.shape
    c = yr.shape[1]
    n_heads = sinks.shape[0]
    qc = n_heads * HEAD
    kvc = 2 * (n_heads // ATTN_GROUP) * HEAD
    n_tiles = m // tm
    tpb = rows_per_batch // tm
    wpt = tm // WINDOW
    att = lambda s: jnp.minimum(s, n_tiles - 1)
    mrg = lambda s: jnp.maximum(s - 1, 0)
    const = lambda shape: pl.BlockSpec(shape, lambda s: (0, 0), pipeline_mode=pl.Buffered(1))
    return pl.pallas_call(
        functools.partial(_attn_merge_kernel, blocks_per_seq=rows_per_batch // WINDOW, n_heads=n_heads,
                          n_tiles=n_tiles),
        grid=(n_tiles + 1,),
        in_specs=[pl.BlockSpec(memory_space=pltpu.SMEM),
                  pl.BlockSpec((tm, qc), lambda s: (att(s), offs["q"] // qc)),
                  pl.BlockSpec((tm, kvc), lambda s: (att(s), offs["kv"] // kvc)),
                  pl.BlockSpec((WINDOW, kvc), lambda s: (jnp.maximum(att(s) * wpt - 1, 0), offs["kv"] // kvc)),
                  pl.BlockSpec((tm, c), lambda s: (mrg(s), 0)),
                  pl.BlockSpec((tm, 2 * d), lambda s: (mrg(s), 0)),
                  pl.BlockSpec((tm, d), lambda s: (mrg(s), 0)),
                  pl.BlockSpec((1, 6, d), lambda s: (mrg(s) // tpb, 0, 0)),
                  const((c, d)), const((c, d)), const((d, d)),
                  pl.BlockSpec((1, d), lambda s: (0, 0))],
        out_specs=[pl.BlockSpec((tm, d), lambda s: (mrg(s), 0)),
                   pl.BlockSpec((tm, d), lambda s: (mrg(s), 0))],
        out_shape=[jax.ShapeDtypeStruct((m, d), F32), jax.ShapeDtypeStruct((m, d), BF16)],
        scratch_shapes=[pltpu.VMEM((2 * n_heads, WINDOW, 2 * WINDOW), F32),
                        pltpu.VMEM((2, tm, qc), BF16)],
        compiler_params=_cparams(("arbitrary",)),
        name="attn_merge",
    )(sinks, p, p, p, yr, p, x2, mod, wbr, wba, wout, gain2)


def _mlp_kernel(h_ref, wu_ref, wd_ref, x1_ref, mod_ref, fg_ref, o_ref):
    f = pl.program_id(1)

    @pl.when(f == 0)
    def _():
        o_ref[...] = jnp.zeros_like(o_ref)

    u = _dot(h_ref[...], wu_ref[...])
    act = jnp.square(jnp.maximum(u, 0.0)).astype(BF16)
    o_ref[...] += _dot(act, wd_ref[...])

    @pl.when(f == pl.num_programs(1) - 1)
    def _():
        x2 = x1_ref[...] + mod_ref[0, 5:6, :] * o_ref[...]
        o_ref[...] = _rms_norm(x2, fg_ref[...])


def _mlp(h2, wu, wd, x1, mod, final_gain, rows_per_batch, tm, tf):
    m, d = h2.shape
    f = wu.shape[1]
    tpb = rows_per_batch // tm
    return pl.pallas_call(
        _mlp_kernel,
        grid=(m // tm, f // tf),
        in_specs=[pl.BlockSpec((tm, d), lambda i, j: (i, 0)),
                  pl.BlockSpec((d, tf), lambda i, j: (0, j)),
                  pl.BlockSpec((tf, d), lambda i, j: (j, 0)),
                  pl.BlockSpec((tm, d), lambda i, j: (i, 0)),
                  pl.BlockSpec((1, 6, d), lambda i, j: (i // tpb, 0, 0)),
                  pl.BlockSpec((1, d), lambda i, j: (0, 0))],
        out_specs=pl.BlockSpec((tm, d), lambda i, j: (i, 0)),
        out_shape=jax.ShapeDtypeStruct((m, d), F32),
        compiler_params=_cparams(("arbitrary", "arbitrary"), MLP_VMEM_LIMIT),
        name="mlp",
    )(h2, wu, wd, x1, mod, final_gain)


def _forward(x, c, w_ada, b_ada, norm1_gain, w_in, b_in, rwkv_mix, rwkv_w0, rwkv_w2, rwkv_a0,
             rwkv_a2, rwkv_g2, rwkv_k_k, rwkv_k_a, rwkv_r_k, rwkv_ln_w, rwkv_ln_b, attn_sinks,
             w_branch_rwkv, w_branch_attn, w_out, norm2_gain, w_up, w_down, final_gain):
    batch, seq, d = x.shape
    depth = w_ada.shape[0]
    c_rwkv = rwkv_w0.shape[1]
    n_heads = attn_sinks.shape[1]
    qc = n_heads * HEAD
    kvc = 2 * (n_heads // ATTN_GROUP) * HEAD
    lora_w = DECAY_LORA + ICLR_LORA + GATE_LORA
    rwkv_cols = 3 * c_rwkv + lora_w
    attn_cols = qc + kvc
    offs = {"gate": 0, "r": 2 * d, "k": 2 * d + c_rwkv, "v": 2 * d + 2 * c_rwkv, "q": 2 * d + 3 * c_rwkv}
    offs["lora"] = offs["q"] + qc
    offs["kv"] = offs["lora"] + LORA_PAD
    n_pack = offs["kv"] + kvc

    segments = ((rwkv_cols + attn_cols, 2 * d, offs["gate"]), (0, 3 * c_rwkv, offs["r"]),
                (rwkv_cols, qc, offs["q"]), (3 * c_rwkv, lora_w, offs["lora"]),
                (rwkv_cols + qc, kvc, offs["kv"]))
    zero_rows = (offs["lora"] + lora_w, offs["kv"])

    def pack_cols(t):
        out = jnp.zeros(t.shape[:-1] + (n_pack,), t.dtype)
        for src, width, dst in segments:
            out = out.at[..., dst:dst + width].set(t[..., src:src + width])
        return out

    tm_in = min(1024, seq)
    tn_in = 1792 if n_pack % 1792 == 0 else LANES
    tb = min(1024, seq)
    tm_merge = min(256, seq)
    tm_mlp = min(512, seq)
    tf = min(2048, w_up.shape[2])

    assert depth == 1, "single-layer block"
    layer = lambda t: t.reshape(t.shape[1:])
    x2 = x.reshape(batch * seq, d)
    for l in range(depth):
        mod = _ada(c, layer(w_ada), b_ada[l]).reshape(batch, 6, d)
        w_pack = _pack_weight_t(layer(w_in).T, segments, zero_rows, n_pack)
        b_pack = pack_cols(b_in[l]).reshape(1, n_pack)
        p = _inproj(x2, mod, norm1_gain[l].reshape(1, d), w_pack, b_pack, seq, tm_in, tn_in)

        mix = rwkv_mix[l]
        chan = jnp.stack([mix[:c_rwkv], mix[c_rwkv:2 * c_rwkv], mix[2 * c_rwkv:3 * c_rwkv],
                          rwkv_w0[l], rwkv_a0[l], rwkv_k_k[l], rwkv_k_a[l], rwkv_r_k[l].reshape(c_rwkv),
                          rwkv_ln_w[l], rwkv_ln_b[l]])
        chan = jnp.pad(chan, ((0, CHAN_ROWS - chan.shape[0]), (0, 0)))
        lora_used = -(-lora_w // LANES) * LANES
        mix_lora = jnp.pad(mix[3 * c_rwkv:], (0, lora_used - lora_w)).reshape(1, lora_used)
        g2_pad = jnp.pad(rwkv_g2[l], ((0, lora_used - lora_w), (0, 0)))
        y_rwkv, (wbr, wba, wout, wup, wdown) = _rwkv(
            p, chan, mix_lora, rwkv_w2[l], rwkv_a2[l], g2_pad, batch, seq, offs, tb,
            cast=[layer(w_branch_rwkv), layer(w_branch_attn), layer(w_out), layer(w_up), layer(w_down)])
        x2, h2 = _attn_merge(y_rwkv, p, attn_sinks[l], x2, mod, wbr, wba, wout,
                             norm2_gain[l].reshape(1, d), seq, offs, tm_merge)
        x2 = _mlp(h2, wup, wdown, x2, mod, final_gain.reshape(1, d), seq, tm_mlp, tf)
    return x2.reshape(batch, seq, d)


def kernel(x, c, w_ada, b_ada, norm1_gain, w_in, b_in, rwkv_mix, rwkv_w0, rwkv_w2, rwkv_a0, rwkv_a2, rwkv_g2, rwkv_k_k, rwkv_k_a, rwkv_r_k, rwkv_ln_w, rwkv_ln_b, attn_sinks, w_branch_rwkv, w_branch_attn, w_out, norm2_gain, w_up, w_down, final_gain):
    return _forward(x, c, w_ada, b_ada, norm1_gain, w_in, b_in, rwkv_mix, rwkv_w0, rwkv_w2, rwkv_a0,
                    rwkv_a2, rwkv_g2, rwkv_k_k, rwkv_k_a, rwkv_r_k, rwkv_ln_w, rwkv_ln_b, attn_sinks,
                    w_branch_rwkv, w_branch_attn, w_out, norm2_gain, w_up, w_down, final_gain)
```

```python
import functools

import jax
import jax.numpy as jnp
from jax import lax
from jax.experimental import pallas as pl
from jax.experimental.pallas import tpu as pltpu

F32 = jnp.float32
BF16 = jnp.bfloat16

LANES = 128
SUBLANES = 8
BF16_ROWS = 16
MXU_N = 256
NORM_ROWS = 128
HEAD = 64
CHUNK = 128
WINDOW = 128
ATTN_GROUP = 8
ATTN_BATCH = 8
NORM_EPS = 1e-6
GN_EPS = 64e-5
DECAY_SCALE = 0.6065306597126334
KK_EPS = 1e-12
CHAN_ROWS = 16
DECAY_LORA = 64
ICLR_LORA = 64
GATE_LORA = 160
LORA_PAD = 512
VMEM_LIMIT = 56 * 1024 * 1024
MLP_VMEM_LIMIT = 62 * 1024 * 1024


def _cparams(sem, vmem_limit=VMEM_LIMIT):
    return pltpu.CompilerParams(dimension_semantics=sem, vmem_limit_bytes=vmem_limit)


_NN = (((1,), (0,)), ((), ()))
_NT = (((1,), (1,)), ((), ()))
_TN = (((0,), (0,)), ((), ()))


def _dot(a, b, dims=_NN):
    return lax.dot_general(a, b, dims, preferred_element_type=F32)


def _dot_nt(a, b):
    return _dot(a, b, _NT)


def _rms_norm(x, gain):
    ms = jnp.mean(x * x, axis=-1, keepdims=True)
    return x * lax.rsqrt(ms + NORM_EPS) * gain


def _ada_kernel(ct_ref, w_ref, b_ref, o_ref):
    ct = ct_ref[...]
    act = ct * jax.nn.sigmoid(ct)
    for m in range(o_ref.shape[0]):
        o_ref[m:m + 1, :] = jnp.sum(w_ref[...] * act[:, m:m + 1], axis=0, keepdims=True) + b_ref[...]


def _ada(c, w, b, tn=1024):
    nb, d = c.shape
    n = w.shape[1]
    return pl.pallas_call(
        _ada_kernel,
        grid=(n // tn,),
        in_specs=[pl.BlockSpec((d, nb), lambda j: (0, 0)),
                  pl.BlockSpec((d, tn), lambda j: (0, j)),
                  pl.BlockSpec((1, tn), lambda j: (0, j))],
        out_specs=pl.BlockSpec((nb, tn), lambda j: (0, j)),
        out_shape=jax.ShapeDtypeStruct((nb, n), F32),
        compiler_params=_cparams(("arbitrary",)),
        name="adaln",
    )(c.T, w, b.reshape(1, n))


def _pack_kernel(w_ref, o_ref, *, segments, zero_rows):
    for src, width, dst in segments:
        o_ref[dst:dst + width, :] = w_ref[src:src + width, :].astype(o_ref.dtype)
    lo, hi = zero_rows
    o_ref[lo:hi, :] = jnp.zeros((hi - lo, o_ref.shape[1]), o_ref.dtype)


def _pack_weight_t(w_t, segments, zero_rows, n_pack, tk=256):
    n, k = w_t.shape
    return pl.pallas_call(
        functools.partial(_pack_kernel, segments=segments, zero_rows=zero_rows),
        grid=(k // tk,),
        in_specs=[pl.BlockSpec((n, tk), lambda i: (0, i))],
        out_specs=pl.BlockSpec((n_pack, tk), lambda i: (0, i)),
        out_shape=jax.ShapeDtypeStruct((n_pack, k), BF16),
        compiler_params=_cparams(("arbitrary",)),
        name="pack_w_in",
    )(w_t)


def _inproj_kernel(x_ref, mod_ref, g_ref, w_ref, b_ref, o_ref, h_ref):
    @pl.when(pl.program_id(1) == 0)
    def _():
        scale = g_ref[...] * (1.0 + mod_ref[0, 1:2, :])
        shift = mod_ref[0, 0:1, :]

        def slab(c, carry):
            rows = pl.ds(pl.multiple_of(c * NORM_ROWS, NORM_ROWS), NORM_ROWS)
            x = x_ref[rows, :]
            ms = jnp.mean(x * x, axis=-1, keepdims=True)
            h_ref[rows, :] = (x * lax.rsqrt(ms + NORM_EPS) * scale + shift).astype(BF16)
            return carry

        lax.fori_loop(0, x_ref.shape[0] // NORM_ROWS, slab, 0)

    o_ref[...] = _dot_nt(h_ref[...], w_ref[...]) + b_ref[...]


def _inproj(x2, mod, gain, w_t, b, rows_per_batch, tm, tn):
    m, d = x2.shape
    n = w_t.shape[0]
    tpb = rows_per_batch // tm
    return pl.pallas_call(
        _inproj_kernel,
        grid=(m // tm, n // tn),
        in_specs=[pl.BlockSpec((tm, d), lambda i, j: (i, 0)),
                  pl.BlockSpec((1, 6, d), lambda i, j: (i // tpb, 0, 0)),
                  pl.BlockSpec((1, d), lambda i, j: (0, 0)),
                  pl.BlockSpec((tn, d), lambda i, j: (j, 0)),
                  pl.BlockSpec((1, tn), lambda i, j: (0, j))],
        out_specs=pl.BlockSpec((tm, tn), lambda i, j: (i, j)),
        out_shape=jax.ShapeDtypeStruct((m, n), F32),
        scratch_shapes=[pltpu.VMEM((tm, d), BF16)],
        compiler_params=_cparams(("arbitrary", "arbitrary")),
        name="inproj",
    )(x2, mod, gain, w_t, b)


def _head_sum(x):
    lo = lax.broadcasted_iota(jnp.int32, x.shape, 1) < HEAD
    s0 = jnp.sum(jnp.where(lo, x, 0.0), axis=-1, keepdims=True)
    s1 = jnp.sum(jnp.where(lo, 0.0, x), axis=-1, keepdims=True)
    return jnp.where(lo, s0, s1)


def _token_shift(x, carry_row, mix):
    rolled = pltpu.roll(x, 1, axis=0)
    head = rolled[:SUBLANES]
    first = lax.broadcasted_iota(jnp.int32, head.shape, 0) == 0
    prev = jnp.concatenate([jnp.where(first, carry_row, head), rolled[SUBLANES:]], axis=0)
    return x + (prev - x) * mix


class _Operands:
    def __init__(self):
        self._parts = {}

    def parts(self, x, n):
        ent = self._parts.setdefault(id(x), [x])
        while len(ent) - 1 < n:
            rest = x
            for piece in ent[1:]:
                rest = rest - piece.astype(F32)
            ent.append(rest.astype(BF16))
        return ent[1:n + 1]

    def mm(self, a, b, dims=_NN):
        return _dot(self.parts(a, 1)[0], self.parts(b, 1)[0], dims)


CHUNK_FIELDS = {"ra": F32, "aa": F32, "bi": BF16, "ki": BF16, "bh": BF16, "kh": BF16,
                "vm0": BF16, "vm1": BF16}


def _decay_chunk(ops, rc, lwc, kc, vsw, ac, bc):
    incl = (lax.broadcasted_iota(jnp.int32, (CHUNK, CHUNK), 0)
            >= lax.broadcasted_iota(jnp.int32, (CHUNK, CHUNK), 1))
    tri = jnp.where(incl, 1.0, 0.0).astype(BF16)
    lw = _dot(tri, jnp.concatenate(ops.parts(lwc, 2), axis=1))
    lw = lw[:, :LANES] + lw[:, LANES:]
    lw_end = lw[CHUNK - 1:CHUNK, :]
    lw_mid = lw[CHUNK // 2 - 1:CHUNK // 2, :]
    e_in = jnp.exp(lw)
    e_ex = jnp.exp(lw - lwc)
    e_inv = jnp.exp(lw_mid - lw)
    e_end = jnp.exp(lw_end - lw)
    lo_lane = lax.broadcasted_iota(jnp.int32, vsw.shape, 1) < HEAD
    return dict(vm0=jnp.where(lo_lane, 0.0, vsw), vm1=jnp.where(lo_lane, vsw, 0.0),
                g_end=jnp.exp(lw_end), e_mid=jnp.exp(-lw_mid), ra=rc * e_in, aa=ac * e_ex,
                bi=bc * e_inv, ki=kc * e_inv, bh=bc * e_end, kh=kc * e_end)


def _scan_matrices(ops, chunks):
    row = lax.broadcasted_iota(jnp.int32, (CHUNK, CHUNK), 0)
    col = lax.broadcasted_iota(jnp.int32, (CHUNK, CHUNK), 1)
    incl = row >= col
    strict = row > col
    eye = row == col
    lo_lane = col < HEAD
    same_half = lo_lane == (row < HEAD)
    lo_lane2 = lax.broadcasted_iota(jnp.int32, (2 * CHUNK, LANES), 1) < HEAD
    level_masks = []
    n = 1
    while n < CHUNK:
        level_masks.append((row // (2 * n) == col // (2 * n)) & ((row // n) % 2 == 1) & ((col // n) % 2 == 0))
        n *= 2

    def level_piece(m_ab, level):
        n = 2 ** level
        if n < SUBLANES:
            return jnp.where(level_masks[level], m_ab, 0.0).astype(BF16)
        lane = lax.broadcasted_iota(jnp.int32, (n, CHUNK), 1)
        slabs = []
        for i in range(0, CHUNK, 2 * n):
            slabs.append(jnp.zeros((n, CHUNK), F32))
            slabs.append(jnp.where((lane >= i) & (lane < i + n), m_ab[i + n:i + 2 * n], 0.0))
        return jnp.concatenate(slabs, axis=0).astype(BF16)

    probs = []
    for ch in chunks:
        lhs_all = jnp.concatenate([ch["aa"], ch["ra"]], axis=0) * ch["e_mid"]
        rhs_all = jnp.concatenate([ch["bi"], ch["ki"]], axis=0)
        for h in range(LANES // HEAD):
            own = lo_lane if h == 0 else ~lo_lane
            own2 = lo_lane2 if h == 0 else ~lo_lane2
            gram = ops.mm(jnp.where(own2, lhs_all, 0.0), rhs_all, _NT)
            m_ab = jnp.where(strict, gram[:CHUNK, :CHUNK], 0.0)
            probs.append(dict(ch=ch, own=own, vm=ch["vm0"] if h == 0 else ch["vm1"],
                              offs=[level_piece(m_ab, level) for level in range(len(level_masks))],
                              m_ak=jnp.where(strict, gram[:CHUNK, CHUNK:], 0.0),
                              m_rb=jnp.where(incl, gram[CHUNK:, :CHUNK], 0.0),
                              m_rk=jnp.where(incl, gram[CHUNK:, CHUNK:], 0.0)))
    yield

    unit = jnp.where(eye, 1.0, 0.0).astype(BF16)
    for p in probs:
        p["t"] = unit + p["offs"][0]
    n = 2
    for level in range(1, len(level_masks)):
        if n < BF16_ROWS:
            for p in probs:
                p["to"] = _dot(p["t"], p["offs"][level]).astype(BF16)
            for p in probs:
                p["t"] = p["t"] + _dot(p["to"], p["t"]).astype(BF16)
        else:
            lower = lambda t, n=n: jnp.concatenate(
                [t[i:i + n] for i in range(n, CHUNK, 2 * n)], axis=0)
            for p in probs:
                p["tl"] = lower(p["t"])
                p["to"] = _dot(p["tl"], p["offs"][level]).astype(BF16)
            for p in probs:
                tl = p["tl"] + _dot(p["to"], p["t"]).astype(BF16)
                slabs = []
                for j, i in enumerate(range(0, CHUNK, 2 * n)):
                    slabs += [p["t"][i:i + n], tl[j * n:(j + 1) * n]]
                p["t"] = jnp.concatenate(slabs, axis=0)
        n *= 2
        yield

    for p in probs:
        p["x"] = (jnp.where(p["own"], p["ch"]["aa"], 0.0) + ops.mm(p["m_ak"], p["vm"])).astype(BF16)
    yield
    for p in probs:
        p["tx"] = _dot(p["t"], p["x"]).astype(BF16)
    yield
    for p in probs:
        p["y"] = jnp.where(p["own"], p["ch"]["ra"], 0.0) + ops.mm(
            jnp.concatenate([p["m_rb"], p["m_rk"]], axis=1),
            jnp.concatenate([p["tx"], p["vm"]], axis=0))
    yield

    for i, ch in enumerate(chunks):
        p0, p1 = probs[2 * i], probs[2 * i + 1]
        pq = ops.mm(jnp.concatenate([ch["bh"], ch["kh"]], axis=0),
                    jnp.concatenate([jnp.concatenate([p0["tx"], p1["tx"]], axis=1),
                                     jnp.concatenate([p0["vm"], p1["vm"]], axis=1)], axis=0),
                    _TN)
        pq = jnp.where(row < HEAD, pq[:, :LANES], pq[:, LANES:])
        ch["p"] = jnp.where(same_half, pq, 0.0) + jnp.where(eye, ch["g_end"], 0.0)
        ch["q"] = jnp.where(same_half, 0.0, pq)
        ch["r"] = jnp.where(lo_lane, p0["y"], p1["y"])
        ch["y0"] = jnp.where(lo_lane, p1["y"], p0["y"])


def _interleave(main, *side):
    for _ in main:
        for gen in side:
            next(gen, None)
    for gen in side:
        for _ in gen:
            pass


def _rwkv_kernel(*refs, n_pairs, n_tblocks, n_tiles, cast_steps):
    n_cast = len(cast_steps)
    (r_ref, k_ref, v_ref, lora_ref, chan_ref, chanf_ref, mixl_ref, w2_ref, a2_ref, g2_ref) = refs[:10]
    cast_in, o_ref, cast_out = refs[10:10 + n_cast], refs[10 + n_cast], refs[11 + n_cast:11 + 2 * n_cast]
    (crkv_ref, clora_ref, z_ref, wa_s, sgd_s, r3_s, k3_s, v3_s, g3_s,
     y_s, gend_s, p_s, q_s, rr_s, y0_s, *decay_s) = refs[11 + 2 * n_cast:]

    tb = r_ref.shape[0]
    n_chunks = tb // CHUNK
    s = pl.program_id(0)
    sa = jnp.minimum(s, n_tiles - 1)
    sf = jnp.maximum(s - 2, 0)
    hp, t_a = sa % n_pairs, (sa // n_pairs) % n_tblocks
    hpf, t_f = sf % n_pairs, (sf // n_pairs) % n_tblocks
    slot_w, slot_r = s % 2, (s + 1) % 2
    keep_w, keep_r = s % 3, (s + 1) % 3

    @pl.when(s == 0)
    def _():
        for ref in (crkv_ref, clora_ref, z_ref, r3_s, k3_s, v3_s, g3_s, gend_s, p_s, q_s, rr_s, y0_s,
                    *decay_s):
            ref[...] = jnp.zeros_like(ref)

    @pl.when(hp == 0)
    def _():
        lora_in = lora_ref[:, :clora_ref.shape[1]]
        carry = jnp.where(t_a == 0, 0.0, clora_ref[...])
        lora = _token_shift(lora_in, carry, mixl_ref[...])
        clora_ref[...] = lora_in[tb - 1:tb, :]
        twd = jnp.tanh(lora[:, 0:DECAY_LORA])
        twd_hi = twd.astype(BF16).astype(F32)
        wa_s[...] = jnp.concatenate([twd_hi, twd_hi, twd - twd_hi,
                                     lora[:, DECAY_LORA:DECAY_LORA + ICLR_LORA]], axis=1).astype(BF16)
        sgd_s[...] = jax.nn.sigmoid(lora[:, DECAY_LORA + ICLR_LORA:]).astype(BF16)

    ops = _Operands()

    def finish():
        chanf = chanf_ref[...]
        z = jnp.where(t_f == 0, 0.0, z_ref[hpf])
        for c in range(n_chunks):
            i = slot_r * n_chunks + c
            rz = ops.mm(jnp.concatenate([rr_s[i], p_s[i]], axis=0), z)
            y_s[c * CHUNK:(c + 1) * CHUNK, :] = rz[:CHUNK] + y0_s[i]
            z = rz[CHUNK:] + q_s[i]
            yield
        z_ref[hpf] = z
        r_k, ln_w, ln_b = chanf[7:8], chanf[8:9], chanf[9:10]
        y = pltpu.roll(y_s[...], HEAD, axis=1)
        mu = _head_sum(y) * (1.0 / HEAD)
        yc = y - mu
        var = _head_sum(yc * yc) * (1.0 / HEAD)
        yn = yc * lax.rsqrt(var + GN_EPS) * ln_w + ln_b
        bonus = _head_sum(r3_s[keep_r] * k3_s[keep_r] * r_k) * v3_s[keep_r]
        o_ref[...] = ((yn + bonus) * g3_s[keep_r]).astype(o_ref.dtype)
        yield

    def matrices():
        chunks = []
        for c in range(n_chunks):
            i = slot_r * n_chunks + c
            ch = {name: ref[i] for name, ref in zip(CHUNK_FIELDS, decay_s)}
            ch["g_end"], ch["e_mid"] = gend_s[i, 0:1, :], gend_s[i, 1:2, :]
            chunks.append(ch)
        yield from _scan_matrices(ops, chunks)
        for c, ch in enumerate(chunks):
            i = slot_w * n_chunks + c
            p_s[i], q_s[i] = ch["p"].astype(p_s.dtype), ch["q"]
            rr_s[i], y0_s[i] = ch["r"].astype(rr_s.dtype), ch["y0"]
        yield

    def prepare():
        chan = chan_ref[...]
        mix_r, mix_k, mix_v = chan[0:1], chan[1:2], chan[2:3]
        w0, a0, k_k, k_a = chan[3:4], chan[4:5], chan[5:6], chan[6:7]
        r_in, k_in, v_in = r_ref[...], k_ref[...], v_ref[...]
        carry = jnp.where(t_a == 0, 0.0, crkv_ref[hp])
        r = _token_shift(r_in, carry[0:1], mix_r)
        k = _token_shift(k_in, carry[1:2], mix_k)
        v = _token_shift(v_in, carry[2:3], mix_v)
        crkv_ref[hp, 0:1, :] = r_in[tb - 1:tb, :]
        crkv_ref[hp, 1:2, :] = k_in[tb - 1:tb, :]
        crkv_ref[hp, 2:3, :] = v_in[tb - 1:tb, :]
        yield
        w2_hi = w2_ref[...].astype(BF16)
        w2_lo = (w2_ref[...] - w2_hi.astype(F32)).astype(BF16)
        zero = jnp.zeros_like(w2_hi)
        rhs = jnp.concatenate([jnp.concatenate([w2_hi, w2_lo, w2_hi, zero], axis=0),
                               jnp.concatenate([zero, zero, zero, a2_ref[...].astype(BF16)], axis=0)], axis=1)
        wa = _dot(wa_s[...], rhs)
        dw = w0 + wa[:, :LANES]
        logw = -DECAY_SCALE * jax.nn.sigmoid(dw)
        alr = jax.nn.sigmoid(a0 + wa[:, LANES:])
        g3_s[keep_w] = _dot(sgd_s[...], g2_ref[...].astype(BF16))
        yield
        kk = k * k_k
        kk = kk * lax.rsqrt(jnp.maximum(_head_sum(kk * kk), KK_EPS * KK_EPS))
        k_mod = k * (1.0 + (alr - 1.0) * k_a)
        vsw = pltpu.roll(v, HEAD, axis=1)
        a, b = -kk, kk * alr
        r3_s[keep_w], k3_s[keep_w], v3_s[keep_w] = r, k_mod, v
        yield
        for c in range(n_chunks):
            sl = slice(c * CHUNK, (c + 1) * CHUNK)
            ch = _decay_chunk(ops, r[sl], logw[sl], k_mod[sl], vsw[sl], a[sl], b[sl])
            i = slot_w * n_chunks + c
            for name, ref in zip(CHUNK_FIELDS, decay_s):
                ref[i] = ch[name].astype(ref.dtype)
            gend_s[i, 0:1, :] = ch["g_end"]
            gend_s[i, 1:2, :] = ch["e_mid"]
            yield

    def casts():
        for w_ref, wb_ref in zip(cast_in, cast_out):
            wb_ref[...] = w_ref[...].astype(wb_ref.dtype)
            yield

    _interleave(prepare(), matrices(), finish(), casts())


def _rwkv(p, chan, mix_lora, w2, a2, g2, batch, seq, offs, tb, cast):
    m = p.shape[0]
    c = w2.shape[1]
    n_pairs = c // LANES
    n_tblocks = seq // tb
    n_tiles = batch * n_tblocks * n_pairs
    n_chunks = tb // CHUNK
    lora_used = mix_lora.shape[1]
    gate_w = lora_used - DECAY_LORA - ICLR_LORA
    prep = lambda s: jnp.minimum(s, n_tiles - 1)
    fin = lambda s: jnp.maximum(s - 2, 0)
    col = lambda name: lambda s: (prep(s) // n_pairs, offs[name] // LANES + prep(s) % n_pairs)
    weight = lambda s: (0, prep(s) % n_pairs)
    tile = lambda: pltpu.VMEM((tb, LANES), F32)
    tile3 = lambda: pltpu.VMEM((3, tb, LANES), F32)
    mats = lambda dtype=F32: pltpu.VMEM((2 * n_chunks, CHUNK, LANES), dtype)
    cast_steps, cast_specs = [], []
    for w in cast:
        steps = min(n_tiles, w.shape[0] // BF16_ROWS)
        assert w.shape[0] % (steps * BF16_ROWS) == 0, w.shape
        cast_steps.append(steps)
        cast_specs.append(pl.BlockSpec((w.shape[0] // steps, w.shape[1]),
                                       lambda s, steps=steps: (jnp.minimum(s, steps - 1), 0)))
    outs = pl.pallas_call(
        functools.partial(_rwkv_kernel, n_pairs=n_pairs, n_tblocks=n_tblocks, n_tiles=n_tiles,
                          cast_steps=tuple(cast_steps)),
        grid=(n_tiles + 2,),
        in_specs=[pl.BlockSpec((tb, LANES), col("r")),
                  pl.BlockSpec((tb, LANES), col("k")),
                  pl.BlockSpec((tb, LANES), col("v")),
                  pl.BlockSpec((tb, LORA_PAD), lambda s: (prep(s) // n_pairs, offs["lora"] // LORA_PAD)),
                  pl.BlockSpec((CHAN_ROWS, LANES), weight),
                  pl.BlockSpec((CHAN_ROWS, LANES), lambda s: (0, fin(s) % n_pairs)),
                  pl.BlockSpec((1, lora_used), lambda s: (0, 0)),
                  pl.BlockSpec((DECAY_LORA, LANES), weight),
                  pl.BlockSpec((ICLR_LORA, LANES), weight),
                  pl.BlockSpec((gate_w, LANES), weight)] + cast_specs,
        out_specs=[pl.BlockSpec((tb, LANES), lambda s: (fin(s) // n_pairs, fin(s) % n_pairs))] + cast_specs,
        out_shape=[jax.ShapeDtypeStruct((m, c), BF16)]
                  + [jax.ShapeDtypeStruct(w.shape, BF16) for w in cast],
        scratch_shapes=[pltpu.VMEM((n_pairs, SUBLANES, LANES), F32),
                        pltpu.VMEM((1, lora_used), F32),
                        pltpu.VMEM((n_pairs, CHUNK, LANES), F32),
                        pltpu.VMEM((tb, 3 * DECAY_LORA + ICLR_LORA), BF16),
                        pltpu.VMEM((tb, gate_w), BF16),
                        tile3(), tile3(), tile3(), tile3(),
                        tile(),
                        pltpu.VMEM((2 * n_chunks, SUBLANES, LANES), F32),
                        mats(BF16), mats(), mats(BF16), mats()]
                       + [mats(dtype) for dtype in CHUNK_FIELDS.values()],
        compiler_params=_cparams(("arbitrary",)),
        name="rwkv7",
    )(p, p, p, p, chan, chan, mix_lora, w2, a2, g2, *cast)
    return outs[0], outs[1:]


def _attn_stages(sink_ref, q, kv, bias, store, n_heads):
    n_kv = n_heads // ATTN_GROUP
    lo_lane = lax.broadcasted_iota(jnp.int32, (WINDOW, LANES), 1) < HEAD
    keys = [kv[:, g * HEAD:(g + 1) * HEAD] for g in range(n_kv)]
    values = [kv[:, (n_kv + g) * HEAD:(n_kv + g + 1) * HEAD] for g in range(n_kv)]
    for g0 in range(0, n_heads, ATTN_BATCH):
        heads = range(g0, min(g0 + ATTN_BATCH, n_heads))
        s = {h: _dot_nt(q[:, h * HEAD:(h + 1) * HEAD], keys[h // ATTN_GROUP]) + bias(h) for h in heads}
        yield
        mx = {h: jnp.maximum(jnp.max(s[h], axis=-1, keepdims=True), sink_ref[h]) for h in heads}
        e = {h: jnp.exp(s[h] - mx[h]) for h in heads}
        yield
        denom = {h: jnp.sum(e[h], axis=-1, keepdims=True) + jnp.exp(sink_ref[h] - mx[h]) for h in heads}
        pv = {h: _dot(e[h].astype(BF16), values[h // ATTN_GROUP]) for h in heads}
        yield
        for h0 in heads[::2]:
            pair = (jnp.concatenate([pv[h0], pv[h0 + 1]], axis=1)
                    * jnp.where(lo_lane, 1.0 / denom[h0], 1.0 / denom[h0 + 1]))
            store(h0, pair)
        yield


def _attn_merge_kernel(sink_ref, q_ref, kvc_ref, kvp_ref,
                       yr_ref, gate_ref, x_ref, mod_ref, wbr_ref, wba_ref, wout_ref, g2_ref,
                       x1_ref, h2_ref, bias_s, ya_s, *, blocks_per_seq, n_heads, n_tiles):
    tm, d = x_ref.shape
    s = pl.program_id(0)
    first_block = (jnp.minimum(s, n_tiles - 1) * (tm // WINDOW)) % blocks_per_seq == 0

    @pl.when(s == 0)
    def _():
        ya_s[...] = jnp.zeros_like(ya_s)
        qi = lax.broadcasted_iota(jnp.int32, (WINDOW, 2 * WINDOW), 0)
        kj = lax.broadcasted_iota(jnp.int32, (WINDOW, 2 * WINDOW), 1)
        dist = qi + WINDOW - kj
        valid = (dist >= 0) & (dist < WINDOW)
        for first in (0, 1):
            neg_dist = jnp.where(valid & (kj >= WINDOW) if first == 0 else valid, -dist.astype(F32), -jnp.inf)
            for h in range(n_heads):
                bias_s[first * n_heads + h] = (2.0 ** (-8.0 * (h + 1) / n_heads)) * neg_dist

    def attention():
        kv = jnp.concatenate([kvp_ref[...], kvc_ref[...]], axis=0).astype(BF16)
        q = (q_ref[...] * (HEAD ** -0.5)).astype(BF16)
        for blk in range(tm // WINDOW):
            rows = slice(blk * WINDOW, (blk + 1) * WINDOW)
            table = jnp.where(first_block, 0, 1) * n_heads if blk == 0 else n_heads

            def store(h0, pair, rows=rows):
                ya_s[s % 2, rows, h0 * HEAD:(h0 + 2) * HEAD] = pair.astype(ya_s.dtype)

            yield from _attn_stages(sink_ref, q[rows], kv[blk * WINDOW:(blk + 2) * WINDOW],
                                    lambda h, table=table: bias_s[table + h], store, n_heads)

    def merge():
        yr, ya = yr_ref[...], ya_s[(s + 1) % 2]
        merged = []
        for c0 in range(0, d, MXU_N):
            cols, cols_a = slice(c0, c0 + MXU_N), slice(d + c0, d + c0 + MXU_N)
            merged.append((jax.nn.sigmoid(gate_ref[:, cols]) * _dot(yr, wbr_ref[:, cols])
                           + jax.nn.sigmoid(gate_ref[:, cols_a]) * _dot(ya, wba_ref[:, cols])).astype(BF16))
            yield
        merged = jnp.concatenate(merged, axis=1)
        for c0 in range(0, d, MXU_N):
            cols = slice(c0, c0 + MXU_N)
            x1_ref[:, cols] = x_ref[:, cols] + mod_ref[0, 2:3, cols] * _dot(merged, wout_ref[:, cols])
            yield
        h2 = _rms_norm(x1_ref[...], g2_ref[...]) * (1.0 + mod_ref[0, 4:5, :]) + mod_ref[0, 3:4, :]
        h2_ref[...] = h2.astype(BF16)
        yield

    _interleave(merge(), attention())


def _attn_merge(yr, p, sinks, x2, mod, wbr, wba, wout, gain2, rows_per_batch, offs, tm):
    m, d = x2.shape
    c = yr.shape[1]
    n_heads = sinks.shape[0]
    qc = n_heads * HEAD
    kvc = 2 * (n_heads // ATTN_GROUP) * HEAD
    n_tiles = m // tm
    tpb = rows_per_batch // tm
    wpt = tm // WINDOW
    att = lambda s: jnp.minimum(s, n_tiles - 1)
    mrg = lambda s: jnp.maximum(s - 1, 0)
    const = lambda shape: pl.BlockSpec(shape, lambda s: (0, 0), pipeline_mode=pl.Buffered(1))
    return pl.pallas_call(
        functools.partial(_attn_merge_kernel, blocks_per_seq=rows_per_batch // WINDOW, n_heads=n_heads,
                          n_tiles=n_tiles),
        grid=(n_tiles + 1,),
        in_specs=[pl.BlockSpec(memory_space=pltpu.SMEM),
                  pl.BlockSpec((tm, qc), lambda s: (att(s), offs["q"] // qc)),
                  pl.BlockSpec((tm, kvc), lambda s: (att(s), offs["kv"] // kvc)),
                  pl.BlockSpec((WINDOW, kvc), lambda s: (jnp.maximum(att(s) * wpt - 1, 0), offs["kv"] // kvc)),
                  pl.BlockSpec((tm, c), lambda s: (mrg(s), 0)),
                  pl.BlockSpec((tm, 2 * d), lambda s: (mrg(s), 0)),
                  pl.BlockSpec((tm, d), lambda s: (mrg(s), 0)),
                  pl.BlockSpec((1, 6, d), lambda s: (mrg(s) // tpb, 0, 0)),
                  const((c, d)), const((c, d)), const((d, d)),
                  pl.BlockSpec((1, d), lambda s: (0, 0))],
        out_specs=[pl.BlockSpec((tm, d), lambda s: (mrg(s), 0)),
                   pl.BlockSpec((tm, d), lambda s: (mrg(s), 0))],
        out_shape=[jax.ShapeDtypeStruct((m, d), F32), jax.ShapeDtypeStruct((m, d), BF16)],
        scratch_shapes=[pltpu.VMEM((2 * n_heads, WINDOW, 2 * WINDOW), F32),
                        pltpu.VMEM((2, tm, qc), BF16)],
        compiler_params=_cparams(("arbitrary",)),
        name="attn_merge",
    )(sinks, p, p, p, yr, p, x2, mod, wbr, wba, wout, gain2)


def _mlp_kernel(h_ref, wu_ref, wd_ref, x1_ref, mod_ref, fg_ref, o_ref):
    f = pl.program_id(1)

    @pl.when(f == 0)
    def _():
        o_ref[...] = jnp.zeros_like(o_ref)

    u = _dot(h_ref[...], wu_ref[...])
    act = jnp.square(jnp.maximum(u, 0.0)).astype(BF16)
    o_ref[...] += _dot(act, wd_ref[...])

    @pl.when(f == pl.num_programs(1) - 1)
    def _():
        x2 = x1_ref[...] + mod_ref[0, 5:6, :] * o_ref[...]
        o_ref[...] = _rms_norm(x2, fg_ref[...])


def _mlp(h2, wu, wd, x1, mod, final_gain, rows_per_batch, tm, tf):
    m, d = h2.shape
    f = wu.shape[1]
    tpb = rows_per_batch // tm
    return pl.pallas_call(
        _mlp_kernel,
        grid=(m // tm, f // tf),
        in_specs=[pl.BlockSpec((tm, d), lambda i, j: (i, 0)),
                  pl.BlockSpec((d, tf), lambda i, j: (0, j)),
                  pl.BlockSpec((tf, d), lambda i, j: (j, 0)),
                  pl.BlockSpec((tm, d), lambda i, j: (i, 0)),
                  pl.BlockSpec((1, 6, d), lambda i, j: (i // tpb, 0, 0)),
                  pl.BlockSpec((1, d), lambda i, j: (0, 0))],
        out_specs=pl.BlockSpec((tm, d), lambda i, j: (i, 0)),
        out_shape=jax.ShapeDtypeStruct((m, d), F32),
        compiler_params=_cparams(("arbitrary", "arbitrary"), MLP_VMEM_LIMIT),
        name="mlp",
    )(h2, wu, wd, x1, mod, final_gain)


def _forward(x, c, w_ada, b_ada, norm1_gain, w_in, b_in, rwkv_mix, rwkv_w0, rwkv_w2, rwkv_a0,
             rwkv_a2, rwkv_g2, rwkv_k_k, rwkv_k_a, rwkv_r_k, rwkv_ln_w, rwkv_ln_b, attn_sinks,
             w_branch_rwkv, w_branch_attn, w_out, norm2_gain, w_up, w_down, final_gain):
    batch, seq, d = x.shape
    depth = w_ada.shape[0]
    c_rwkv = rwkv_w0.shape[1]
    n_heads = attn_sinks.shape[1]
    qc = n_heads * HEAD
    kvc = 2 * (n_heads // ATTN_GROUP) * HEAD
    lora_w = DECAY_LORA + ICLR_LORA + GATE_LORA
    rwkv_cols = 3 * c_rwkv + lora_w
    attn_cols = qc + kvc
    offs = {"gate": 0, "r": 2 * d, "k": 2 * d + c_rwkv, "v": 2 * d + 2 * c_rwkv, "q": 2 * d + 3 * c_rwkv}
    offs["lora"] = offs["q"] + qc
    offs["kv"] = offs["lora"] + LORA_PAD
    n_pack = offs["kv"] + kvc

    segments = ((rwkv_cols + attn_cols, 2 * d, offs["gate"]), (0, 3 * c_rwkv, offs["r"]),
                (rwkv_cols, qc, offs["q"]), (3 * c_rwkv, lora_w, offs["lora"]),
                (rwkv_cols + qc, kvc, offs["kv"]))
    zero_rows = (offs["lora"] + lora_w, offs["kv"])

    def pack_cols(t):
        out = jnp.zeros(t.shape[:-1] + (n_pack,), t.dtype)
        for src, width, dst in segments:
            out = out.at[..., dst:dst + width].set(t[..., src:src + width])
        return out

    tm_in = min(1024, seq)
    tn_in = 1792 if n_pack % 1792 == 0 else LANES
    tb = min(1024, seq)
    tm_merge = min(256, seq)
    tm_mlp = min(512, seq)
    tf = min(2048, w_up.shape[2])

    assert depth == 1, "single-layer block"
    layer = lambda t: t.reshape(t.shape[1:])
    x2 = x.reshape(batch * seq, d)
    for l in range(depth):
        mod = _ada(c, layer(w_ada), b_ada[l]).reshape(batch, 6, d)
        w_pack = _pack_weight_t(layer(w_in).T, segments, zero_rows, n_pack)
        b_pack = pack_cols(b_in[l]).reshape(1, n_pack)
        p = _inproj(x2, mod, norm1_gain[l].reshape(1, d), w_pack, b_pack, seq, tm_in, tn_in)

        mix = rwkv_mix[l]
        chan = jnp.stack([mix[:c_rwkv], mix[c_rwkv:2 * c_rwkv], mix[2 * c_rwkv:3 * c_rwkv],
                          rwkv_w0[l], rwkv_a0[l], rwkv_k_k[l], rwkv_k_a[l], rwkv_r_k[l].reshape(c_rwkv),
                          rwkv_ln_w[l], rwkv_ln_b[l]])
        chan = jnp.pad(chan, ((0, CHAN_ROWS - chan.shape[0]), (0, 0)))
        lora_used = -(-lora_w // LANES) * LANES
        mix_lora = jnp.pad(mix[3 * c_rwkv:], (0, lora_used - lora_w)).reshape(1, lora_used)
        g2_pad = jnp.pad(rwkv_g2[l], ((0, lora_used - lora_w), (0, 0)))
        y_rwkv, (wbr, wba, wout, wup, wdown) = _rwkv(
            p, chan, mix_lora, rwkv_w2[l], rwkv_a2[l], g2_pad, batch, seq, offs, tb,
            cast=[layer(w_branch_rwkv), layer(w_branch_attn), layer(w_out), layer(w_up), layer(w_down)])
        x2, h2 = _attn_merge(y_rwkv, p, attn_sinks[l], x2, mod, wbr, wba, wout,
                             norm2_gain[l].reshape(1, d), seq, offs, tm_merge)
        x2 = _mlp(h2, wup, wdown, x2, mod, final_gain.reshape(1, d), seq, tm_mlp, tf)
    return x2.reshape(batch, seq, d)


def kernel(x, c, w_ada, b_ada, norm1_gain, w_in, b_in, rwkv_mix, rwkv_w0, rwkv_w2, rwkv_a0, rwkv_a2, rwkv_g2, rwkv_k_k, rwkv_k_a, rwkv_r_k, rwkv_ln_w, rwkv_ln_b, attn_sinks, w_branch_rwkv, w_branch_attn, w_out, norm2_gain, w_up, w_down, final_gain):
    return _forward(x, c, w_ada, b_ada, norm1_gain, w_in, b_in, rwkv_mix, rwkv_w0, rwkv_w2, rwkv_a0,
                    rwkv_a2, rwkv_g2, rwkv_k_k, rwkv_k_a, rwkv_r_k, rwkv_ln_w, rwkv_ln_b, attn_sinks,
                    w_branch_rwkv, w_branch_attn, w_out, norm2_gain, w_up, w_down, final_gain)
```

```python
import functools

import jax
import jax.numpy as jnp
from jax import lax
from jax.experimental import pallas as pl
from jax.experimental.pallas import tpu as pltpu

F32 = jnp.float32
BF16 = jnp.bfloat16

LANES = 128
SUBLANES = 8
BF16_ROWS = 16
MXU_N = 256
NORM_ROWS = 128
HEAD = 64
CHUNK = 128
WINDOW = 128
ATTN_GROUP = 8
ATTN_BATCH = 8
NORM_EPS = 1e-6
GN_EPS = 64e-5
DECAY_SCALE = 0.6065306597126334
KK_EPS = 1e-12
CHAN_ROWS = 16
DECAY_LORA = 64
ICLR_LORA = 64
GATE_LORA = 160
LORA_PAD = 512
VMEM_LIMIT = 56 * 1024 * 1024
MLP_VMEM_LIMIT = 62 * 1024 * 1024


def _cparams(sem, vmem_limit=VMEM_LIMIT):
    return pltpu.CompilerParams(dimension_semantics=sem, vmem_limit_bytes=vmem_limit)


_NN = (((1,), (0,)), ((), ()))
_NT = (((1,), (1,)), ((), ()))
_TN = (((0,), (0,)), ((), ()))


def _dot(a, b, dims=_NN):
    return lax.dot_general(a, b, dims, preferred_element_type=F32)


def _dot_nt(a, b):
    return _dot(a, b, _NT)


def _rms_norm(x, gain):
    ms = jnp.mean(x * x, axis=-1, keepdims=True)
    return x * lax.rsqrt(ms + NORM_EPS) * gain


def _ada_kernel(ct_ref, w_ref, b_ref, o_ref):
    ct = ct_ref[...]
    act = ct * jax.nn.sigmoid(ct)
    for m in range(o_ref.shape[0]):
        o_ref[m:m + 1, :] = jnp.sum(w_ref[...] * act[:, m:m + 1], axis=0, keepdims=True) + b_ref[...]


def _ada(c, w, b, tn=1024):
    nb, d = c.shape
    n = w.shape[1]
    return pl.pallas_call(
        _ada_kernel,
        grid=(n // tn,),
        in_specs=[pl.BlockSpec((d, nb), lambda j: (0, 0)),
                  pl.BlockSpec((d, tn), lambda j: (0, j)),
                  pl.BlockSpec((1, tn), lambda j: (0, j))],
        out_specs=pl.BlockSpec((nb, tn), lambda j: (0, j)),
        out_shape=jax.ShapeDtypeStruct((nb, n), F32),
        compiler_params=_cparams(("arbitrary",)),
        name="adaln",
    )(c.T, w, b.reshape(1, n))


def _pack_kernel(w_ref, o_ref, *, segments, zero_rows):
    for src, width, dst in segments:
        o_ref[dst:dst + width, :] = w_ref[src:src + width, :].astype(o_ref.dtype)
    lo, hi = zero_rows
    o_ref[lo:hi, :] = jnp.zeros((hi - lo, o_ref.shape[1]), o_ref.dtype)


def _pack_weight_t(w_t, segments, zero_rows, n_pack, tk=256):
    n, k = w_t.shape
    return pl.pallas_call(
        functools.partial(_pack_kernel, segments=segments, zero_rows=zero_rows),
        grid=(k // tk,),
        in_specs=[pl.BlockSpec((n, tk), lambda i: (0, i))],
        out_specs=pl.BlockSpec((n_pack, tk), lambda i: (0, i)),
        out_shape=jax.ShapeDtypeStruct((n_pack, k), BF16),
        compiler_params=_cparams(("arbitrary",)),
        name="pack_w_in",
    )(w_t)


def _inproj_kernel(x_ref, mod_ref, g_ref, w_ref, b_ref, o_ref, h_ref):
    @pl.when(pl.program_id(1) == 0)
    def _():
        scale = g_ref[...] * (1.0 + mod_ref[0, 1:2, :])
        shift = mod_ref[0, 0:1, :]

        def slab(c, carry):
            rows = pl.ds(pl.multiple_of(c * NORM_ROWS, NORM_ROWS), NORM_ROWS)
            x = x_ref[rows, :]
            ms = jnp.mean(x * x, axis=-1, keepdims=True)
            h_ref[rows, :] = (x * lax.rsqrt(ms + NORM_EPS) * scale + shift).astype(BF16)
            return carry

        lax.fori_loop(0, x_ref.shape[0] // NORM_ROWS, slab, 0)

    o_ref[...] = _dot_nt(h_ref[...], w_ref[...]) + b_ref[...]


def _inproj(x2, mod, gain, w_t, b, rows_per_batch, tm, tn):
    m, d = x2.shape
    n = w_t.shape[0]
    tpb = rows_per_batch // tm
    return pl.pallas_call(
        _inproj_kernel,
        grid=(m // tm, n // tn),
        in_specs=[pl.BlockSpec((tm, d), lambda i, j: (i, 0)),
                  pl.BlockSpec((1, 6, d), lambda i, j: (i // tpb, 0, 0)),
                  pl.BlockSpec((1, d), lambda i, j: (0, 0)),
                  pl.BlockSpec((tn, d), lambda i, j: (j, 0)),
                  pl.BlockSpec((1, tn), lambda i, j: (0, j))],
        out_specs=pl.BlockSpec((tm, tn), lambda i, j: (i, j)),
        out_shape=jax.ShapeDtypeStruct((m, n), F32),
        scratch_shapes=[pltpu.VMEM((tm, d), BF16)],
        compiler_params=_cparams(("arbitrary", "arbitrary")),
        name="inproj",
    )(x2, mod, gain, w_t, b)


def _head_sum(x):
    lo = lax.broadcasted_iota(jnp.int32, x.shape, 1) < HEAD
    s0 = jnp.sum(jnp.where(lo, x, 0.0), axis=-1, keepdims=True)
    s1 = jnp.sum(jnp.where(lo, 0.0, x), axis=-1, keepdims=True)
    return jnp.where(lo, s0, s1)


def _token_shift(x, carry_row, mix):
    rolled = pltpu.roll(x, 1, axis=0)
    head = rolled[:SUBLANES]
    first = lax.broadcasted_iota(jnp.int32, head.shape, 0) == 0
    prev = jnp.concatenate([jnp.where(first, carry_row, head), rolled[SUBLANES:]], axis=0)
    return x + (prev - x) * mix


class _Operands:
    def __init__(self):
        self._parts = {}

    def parts(self, x, n):
        ent = self._parts.setdefault(id(x), [x])
        while len(ent) - 1 < n:
            rest = x
            for piece in ent[1:]:
                rest = rest - piece.astype(F32)
            ent.append(rest.astype(BF16))
        return ent[1:n + 1]

    def mm(self, a, b, dims=_NN):
        return _dot(self.parts(a, 1)[0], self.parts(b, 1)[0], dims)


CHUNK_FIELDS = {"ra": F32, "aa": F32, "bi": BF16, "ki": BF16, "bh": BF16, "kh": BF16,
                "vm0": BF16, "vm1": BF16}


def _decay_chunk(ops, rc, lwc, kc, vsw, ac, bc):
    incl = (lax.broadcasted_iota(jnp.int32, (CHUNK, CHUNK), 0)
            >= lax.broadcasted_iota(jnp.int32, (CHUNK, CHUNK), 1))
    tri = jnp.where(incl, 1.0, 0.0).astype(BF16)
    lw = _dot(tri, jnp.concatenate(ops.parts(lwc, 2), axis=1))
    lw = lw[:, :LANES] + lw[:, LANES:]
    lw_end = lw[CHUNK - 1:CHUNK, :]
    lw_mid = lw[CHUNK // 2 - 1:CHUNK // 2, :]
    e_in = jnp.exp(lw)
    e_ex = jnp.exp(lw - lwc)
    e_inv = jnp.exp(lw_mid - lw)
    e_end = jnp.exp(lw_end - lw)
    lo_lane = lax.broadcasted_iota(jnp.int32, vsw.shape, 1) < HEAD
    return dict(vm0=jnp.where(lo_lane, 0.0, vsw), vm1=jnp.where(lo_lane, vsw, 0.0),
                g_end=jnp.exp(lw_end), e_mid=jnp.exp(-lw_mid), ra=rc * e_in, aa=ac * e_ex,
                bi=bc * e_inv, ki=kc * e_inv, bh=bc * e_end, kh=kc * e_end)


def _scan_matrices(ops, chunks):
    row = lax.broadcasted_iota(jnp.int32, (CHUNK, CHUNK), 0)
    col = lax.broadcasted_iota(jnp.int32, (CHUNK, CHUNK), 1)
    incl = row >= col
    strict = row > col
    eye = row == col
    lo_lane = col < HEAD
    same_half = lo_lane == (row < HEAD)
    lo_lane2 = lax.broadcasted_iota(jnp.int32, (2 * CHUNK, LANES), 1) < HEAD
    level_masks = []
    n = 1
    while n < CHUNK:
        level_masks.append((row // (2 * n) == col // (2 * n)) & ((row // n) % 2 == 1) & ((col // n) % 2 == 0))
        n *= 2

    probs = []
    for ch in chunks:
        lhs_all = jnp.concatenate([ch["aa"], ch["ra"]], axis=0) * ch["e_mid"]
        rhs_all = jnp.concatenate([ch["bi"], ch["ki"]], axis=0)
        for h in range(LANES // HEAD):
            own = lo_lane if h == 0 else ~lo_lane
            own2 = lo_lane2 if h == 0 else ~lo_lane2
            gram = ops.mm(jnp.where(own2, lhs_all, 0.0), rhs_all, _NT)
            m_ab = jnp.where(strict, gram[:CHUNK, :CHUNK], 0.0)
            probs.append(dict(ch=ch, own=own, vm=ch["vm0"] if h == 0 else ch["vm1"],
                              offs=[jnp.where(mask, m_ab, 0.0).astype(BF16) for mask in level_masks],
                              m_ak=jnp.where(strict, gram[:CHUNK, CHUNK:], 0.0),
                              m_rb=jnp.where(incl, gram[CHUNK:, :CHUNK], 0.0),
                              m_rk=jnp.where(incl, gram[CHUNK:, CHUNK:], 0.0)))
    yield

    unit = jnp.where(eye, 1.0, 0.0).astype(BF16)
    for p in probs:
        p["t"] = unit + p["offs"][0]
    n = 2
    for level in range(1, len(level_masks)):
        if n < BF16_ROWS:
            for p in probs:
                p["to"] = _dot(p["t"], p["offs"][level]).astype(BF16)
            for p in probs:
                p["t"] = p["t"] + _dot(p["to"], p["t"]).astype(BF16)
        else:
            lower = lambda t, n=n: jnp.concatenate(
                [t[i:i + n] for i in range(n, CHUNK, 2 * n)], axis=0)
            for p in probs:
                p["tl"] = lower(p["t"])
                p["to"] = _dot(p["tl"], p["offs"][level]).astype(BF16)
            for p in probs:
                tl = p["tl"] + _dot(p["to"], p["t"]).astype(BF16)
                slabs = []
                for j, i in enumerate(range(0, CHUNK, 2 * n)):
                    slabs += [p["t"][i:i + n], tl[j * n:(j + 1) * n]]
                p["t"] = jnp.concatenate(slabs, axis=0)
        n *= 2
        yield

    for p in probs:
        p["x"] = (jnp.where(p["own"], p["ch"]["aa"], 0.0) + ops.mm(p["m_ak"], p["vm"])).astype(BF16)
    yield
    for p in probs:
        p["tx"] = _dot(p["t"], p["x"]).astype(BF16)
    yield
    for p in probs:
        p["y"] = jnp.where(p["own"], p["ch"]["ra"], 0.0) + ops.mm(
            jnp.concatenate([p["m_rb"], p["m_rk"]], axis=1),
            jnp.concatenate([p["tx"], p["vm"]], axis=0))
    yield

    for i, ch in enumerate(chunks):
        p0, p1 = probs[2 * i], probs[2 * i + 1]
        pq = ops.mm(jnp.concatenate([ch["bh"], ch["kh"]], axis=0),
                    jnp.concatenate([jnp.concatenate([p0["tx"], p1["tx"]], axis=1),
                                     jnp.concatenate([p0["vm"], p1["vm"]], axis=1)], axis=0),
                    _TN)
        pq = jnp.where(row < HEAD, pq[:, :LANES], pq[:, LANES:])
        ch["p"] = jnp.where(same_half, pq, 0.0) + jnp.where(eye, ch["g_end"], 0.0)
        ch["q"] = jnp.where(same_half, 0.0, pq)
        ch["r"] = jnp.where(lo_lane, p0["y"], p1["y"])
        ch["y0"] = jnp.where(lo_lane, p1["y"], p0["y"])


def _interleave(main, *side):
    for _ in main:
        for gen in side:
            next(gen, None)
    for gen in side:
        for _ in gen:
            pass


def _rwkv_kernel(*refs, n_pairs, n_tblocks, n_tiles, cast_steps):
    n_cast = len(cast_steps)
    (r_ref, k_ref, v_ref, lora_ref, chan_ref, chanf_ref, mixl_ref, w2_ref, a2_ref, g2_ref) = refs[:10]
    cast_in, o_ref, cast_out = refs[10:10 + n_cast], refs[10 + n_cast], refs[11 + n_cast:11 + 2 * n_cast]
    (crkv_ref, clora_ref, z_ref, wa_s, sgd_s, r3_s, k3_s, v3_s, g3_s,
     y_s, gend_s, p_s, q_s, rr_s, y0_s, *decay_s) = refs[11 + 2 * n_cast:]

    tb = r_ref.shape[0]
    n_chunks = tb // CHUNK
    s = pl.program_id(0)
    sa = jnp.minimum(s, n_tiles - 1)
    sf = jnp.maximum(s - 2, 0)
    hp, t_a = sa % n_pairs, (sa // n_pairs) % n_tblocks
    hpf, t_f = sf % n_pairs, (sf // n_pairs) % n_tblocks
    slot_w, slot_r = s % 2, (s + 1) % 2
    keep_w, keep_r = s % 3, (s + 1) % 3

    @pl.when(s == 0)
    def _():
        for ref in (crkv_ref, clora_ref, z_ref, r3_s, k3_s, v3_s, g3_s, gend_s, p_s, q_s, rr_s, y0_s,
                    *decay_s):
            ref[...] = jnp.zeros_like(ref)

    @pl.when(hp == 0)
    def _():
        lora_in = lora_ref[:, :clora_ref.shape[1]]
        carry = jnp.where(t_a == 0, 0.0, clora_ref[...])
        lora = _token_shift(lora_in, carry, mixl_ref[...])
        clora_ref[...] = lora_in[tb - 1:tb, :]
        twd = jnp.tanh(lora[:, 0:DECAY_LORA])
        twd_hi = twd.astype(BF16).astype(F32)
        wa_s[...] = jnp.concatenate([twd_hi, twd_hi, twd - twd_hi,
                                     lora[:, DECAY_LORA:DECAY_LORA + ICLR_LORA]], axis=1).astype(BF16)
        sgd_s[...] = jax.nn.sigmoid(lora[:, DECAY_LORA + ICLR_LORA:]).astype(BF16)

    def finish(ops):
        chanf = chanf_ref[...]
        z = jnp.where(t_f == 0, 0.0, z_ref[hpf])
        for c in range(n_chunks):
            i = slot_r * n_chunks + c
            rz = ops.mm(jnp.concatenate([rr_s[i], p_s[i]], axis=0), z)
            y_s[c * CHUNK:(c + 1) * CHUNK, :] = rz[:CHUNK] + y0_s[i]
            z = rz[CHUNK:] + q_s[i]
            yield
        z_ref[hpf] = z
        r_k, ln_w, ln_b = chanf[7:8], chanf[8:9], chanf[9:10]
        y = pltpu.roll(y_s[...], HEAD, axis=1)
        mu = _head_sum(y) * (1.0 / HEAD)
        yc = y - mu
        var = _head_sum(yc * yc) * (1.0 / HEAD)
        yn = yc * lax.rsqrt(var + GN_EPS) * ln_w + ln_b
        bonus = _head_sum(r3_s[keep_r] * k3_s[keep_r] * r_k) * v3_s[keep_r]
        o_ref[...] = ((yn + bonus) * g3_s[keep_r]).astype(o_ref.dtype)
        yield

    def matrices(ops):
        chunks = []
        for c in range(n_chunks):
            i = slot_r * n_chunks + c
            ch = {name: ref[i] for name, ref in zip(CHUNK_FIELDS, decay_s)}
            ch["g_end"], ch["e_mid"] = gend_s[i, 0:1, :], gend_s[i, 1:2, :]
            chunks.append(ch)
        yield from _scan_matrices(ops, chunks)
        for c, ch in enumerate(chunks):
            i = slot_w * n_chunks + c
            p_s[i], q_s[i] = ch["p"].astype(p_s.dtype), ch["q"]
            rr_s[i], y0_s[i] = ch["r"].astype(rr_s.dtype), ch["y0"]
        yield

    def prepare(ops):
        chan = chan_ref[...]
        mix_r, mix_k, mix_v = chan[0:1], chan[1:2], chan[2:3]
        w0, a0, k_k, k_a = chan[3:4], chan[4:5], chan[5:6], chan[6:7]
        r_in, k_in, v_in = r_ref[...], k_ref[...], v_ref[...]
        carry = jnp.where(t_a == 0, 0.0, crkv_ref[hp])
        r = _token_shift(r_in, carry[0:1], mix_r)
        k = _token_shift(k_in, carry[1:2], mix_k)
        v = _token_shift(v_in, carry[2:3], mix_v)
        crkv_ref[hp, 0:1, :] = r_in[tb - 1:tb, :]
        crkv_ref[hp, 1:2, :] = k_in[tb - 1:tb, :]
        crkv_ref[hp, 2:3, :] = v_in[tb - 1:tb, :]
        yield
        w2_hi = w2_ref[...].astype(BF16)
        w2_lo = (w2_ref[...] - w2_hi.astype(F32)).astype(BF16)
        zero = jnp.zeros_like(w2_hi)
        rhs = jnp.concatenate([jnp.concatenate([w2_hi, w2_lo, w2_hi, zero], axis=0),
                               jnp.concatenate([zero, zero, zero, a2_ref[...].astype(BF16)], axis=0)], axis=1)
        wa = _dot(wa_s[...], rhs)
        dw = w0 + wa[:, :LANES]
        logw = -DECAY_SCALE * jax.nn.sigmoid(dw)
        alr = jax.nn.sigmoid(a0 + wa[:, LANES:])
        g3_s[keep_w] = _dot(sgd_s[...], g2_ref[...].astype(BF16))
        yield
        kk = k * k_k
        kk = kk * lax.rsqrt(jnp.maximum(_head_sum(kk * kk), KK_EPS * KK_EPS))
        k_mod = k * (1.0 + (alr - 1.0) * k_a)
        vsw = pltpu.roll(v, HEAD, axis=1)
        a, b = -kk, kk * alr
        r3_s[keep_w], k3_s[keep_w], v3_s[keep_w] = r, k_mod, v
        yield
        for c in range(n_chunks):
            sl = slice(c * CHUNK, (c + 1) * CHUNK)
            ch = _decay_chunk(ops, r[sl], logw[sl], k_mod[sl], vsw[sl], a[sl], b[sl])
            i = slot_w * n_chunks + c
            for name, ref in zip(CHUNK_FIELDS, decay_s):
                ref[i] = ch[name].astype(ref.dtype)
            gend_s[i, 0:1, :] = ch["g_end"]
            gend_s[i, 1:2, :] = ch["e_mid"]
            yield

    def casts(ops):
        for w_ref, wb_ref in zip(cast_in, cast_out):
            wb_ref[...] = w_ref[...].astype(wb_ref.dtype)
            yield

    def run(*stages):
        ops = _Operands()
        _interleave(*[stage(ops) for stage in stages])

    pl.when(s == 0)(lambda: run(prepare, casts))
    pl.when(s == 1)(lambda: run(prepare, matrices, casts))
    pl.when((s >= 2) & (s < n_tiles))(lambda: run(prepare, matrices, finish, casts))
    pl.when(s == n_tiles)(lambda: run(matrices, finish))
    pl.when(s == n_tiles + 1)(lambda: run(finish))


def _rwkv(p, chan, mix_lora, w2, a2, g2, batch, seq, offs, tb, cast):
    m = p.shape[0]
    c = w2.shape[1]
    n_pairs = c // LANES
    n_tblocks = seq // tb
    n_tiles = batch * n_tblocks * n_pairs
    n_chunks = tb // CHUNK
    lora_used = mix_lora.shape[1]
    gate_w = lora_used - DECAY_LORA - ICLR_LORA
    prep = lambda s: jnp.minimum(s, n_tiles - 1)
    fin = lambda s: jnp.maximum(s - 2, 0)
    col = lambda name: lambda s: (prep(s) // n_pairs, offs[name] // LANES + prep(s) % n_pairs)
    weight = lambda s: (0, prep(s) % n_pairs)
    tile = lambda: pltpu.VMEM((tb, LANES), F32)
    tile3 = lambda: pltpu.VMEM((3, tb, LANES), F32)
    mats = lambda dtype=F32: pltpu.VMEM((2 * n_chunks, CHUNK, LANES), dtype)
    cast_steps, cast_specs = [], []
    for w in cast:
        steps = min(n_tiles, w.shape[0] // BF16_ROWS)
        assert w.shape[0] % (steps * BF16_ROWS) == 0, w.shape
        cast_steps.append(steps)
        cast_specs.append(pl.BlockSpec((w.shape[0] // steps, w.shape[1]),
                                       lambda s, steps=steps: (jnp.minimum(s, steps - 1), 0)))
    outs = pl.pallas_call(
        functools.partial(_rwkv_kernel, n_pairs=n_pairs, n_tblocks=n_tblocks, n_tiles=n_tiles,
                          cast_steps=tuple(cast_steps)),
        grid=(n_tiles + 2,),
        in_specs=[pl.BlockSpec((tb, LANES), col("r")),
                  pl.BlockSpec((tb, LANES), col("k")),
                  pl.BlockSpec((tb, LANES), col("v")),
                  pl.BlockSpec((tb, LORA_PAD), lambda s: (prep(s) // n_pairs, offs["lora"] // LORA_PAD)),
                  pl.BlockSpec((CHAN_ROWS, LANES), weight),
                  pl.BlockSpec((CHAN_ROWS, LANES), lambda s: (0, fin(s) % n_pairs)),
                  pl.BlockSpec((1, lora_used), lambda s: (0, 0)),
                  pl.BlockSpec((DECAY_LORA, LANES), weight),
                  pl.BlockSpec((ICLR_LORA, LANES), weight),
                  pl.BlockSpec((gate_w, LANES), weight)] + cast_specs,
        out_specs=[pl.BlockSpec((tb, LANES), lambda s: (fin(s) // n_pairs, fin(s) % n_pairs))] + cast_specs,
        out_shape=[jax.ShapeDtypeStruct((m, c), BF16)]
                  + [jax.ShapeDtypeStruct(w.shape, BF16) for w in cast],
        scratch_shapes=[pltpu.VMEM((n_pairs, SUBLANES, LANES), F32),
                        pltpu.VMEM((1, lora_used), F32),
                        pltpu.VMEM((n_pairs, CHUNK, LANES), F32),
                        pltpu.VMEM((tb, 3 * DECAY_LORA + ICLR_LORA), BF16),
                        pltpu.VMEM((tb, gate_w), BF16),
                        tile3(), tile3(), tile3(), tile3(),
                        tile(),
                        pltpu.VMEM((2 * n_chunks, SUBLANES, LANES), F32),
                        mats(BF16), mats(), mats(BF16), mats()]
                       + [mats(dtype) for dtype in CHUNK_FIELDS.values()],
        compiler_params=_cparams(("arbitrary",)),
        name="rwkv7",
    )(p, p, p, p, chan, chan, mix_lora, w2, a2, g2, *cast)
    return outs[0], outs[1:]


def _attn_stages(sink_ref, q, kv, bias, store, n_heads):
    n_kv = n_heads // ATTN_GROUP
    lo_lane = lax.broadcasted_iota(jnp.int32, (WINDOW, LANES), 1) < HEAD
    keys = [kv[:, g * HEAD:(g + 1) * HEAD] for g in range(n_kv)]
    values = [kv[:, (n_kv + g) * HEAD:(n_kv + g + 1) * HEAD] for g in range(n_kv)]
    for g0 in range(0, n_heads, ATTN_BATCH):
        heads = range(g0, min(g0 + ATTN_BATCH, n_heads))
        s = {h: _dot_nt(q[:, h * HEAD:(h + 1) * HEAD], keys[h // ATTN_GROUP]) + bias(h) for h in heads}
        yield
        mx = {h: jnp.maximum(jnp.max(s[h], axis=-1, keepdims=True), sink_ref[h]) for h in heads}
        e = {h: jnp.exp(s[h] - mx[h]) for h in heads}
        yield
        denom = {h: jnp.sum(e[h], axis=-1, keepdims=True) + jnp.exp(sink_ref[h] - mx[h]) for h in heads}
        pv = {h: _dot(e[h].astype(BF16), values[h // ATTN_GROUP]) for h in heads}
        yield
        for h0 in heads[::2]:
            pair = (jnp.concatenate([pv[h0], pv[h0 + 1]], axis=1)
                    * jnp.where(lo_lane, 1.0 / denom[h0], 1.0 / denom[h0 + 1]))
            store(h0, pair)
        yield


def _attn_merge_kernel(sink_ref, q_ref, kvc_ref, kvp_ref,
                       yr_ref, gate_ref, x_ref, mod_ref, wbr_ref, wba_ref, wout_ref, g2_ref,
                       x1_ref, h2_ref, bias_s, ya_s, *, blocks_per_seq, n_heads, n_tiles):
    tm, d = x_ref.shape
    s = pl.program_id(0)
    first_block = (jnp.minimum(s, n_tiles - 1) * (tm // WINDOW)) % blocks_per_seq == 0

    @pl.when(s == 0)
    def _():
        qi = lax.broadcasted_iota(jnp.int32, (WINDOW, 2 * WINDOW), 0)
        kj = lax.broadcasted_iota(jnp.int32, (WINDOW, 2 * WINDOW), 1)
        dist = qi + WINDOW - kj
        valid = (dist >= 0) & (dist < WINDOW)
        for first in (0, 1):
            neg_dist = jnp.where(valid & (kj >= WINDOW) if first == 0 else valid, -dist.astype(F32), -jnp.inf)
            for h in range(n_heads):
                bias_s[first * n_heads + h] = (2.0 ** (-8.0 * (h + 1) / n_heads)) * neg_dist

    def attention():
        kv = jnp.concatenate([kvp_ref[...], kvc_ref[...]], axis=0).astype(BF16)
        q = (q_ref[...] * (HEAD ** -0.5)).astype(BF16)
        for blk in range(tm // WINDOW):
            rows = slice(blk * WINDOW, (blk + 1) * WINDOW)
            table = jnp.where(first_block, 0, 1) * n_heads if blk == 0 else n_heads

            def store(h0, pair, rows=rows):
                ya_s[s % 2, rows, h0 * HEAD:(h0 + 2) * HEAD] = pair.astype(ya_s.dtype)

            yield from _attn_stages(sink_ref, q[rows], kv[blk * WINDOW:(blk + 2) * WINDOW],
                                    lambda h, table=table: bias_s[table + h], store, n_heads)

    def merge():
        yr, ya = yr_ref[...], ya_s[(s + 1) % 2]
        merged = []
        for c0 in range(0, d, MXU_N):
            cols, cols_a = slice(c0, c0 + MXU_N), slice(d + c0, d + c0 + MXU_N)
            merged.append((jax.nn.sigmoid(gate_ref[:, cols]) * _dot(yr, wbr_ref[:, cols])
                           + jax.nn.sigmoid(gate_ref[:, cols_a]) * _dot(ya, wba_ref[:, cols])).astype(BF16))
            yield
        merged = jnp.concatenate(merged, axis=1)
        for c0 in range(0, d, MXU_N):
            cols = slice(c0, c0 + MXU_N)
            x1_ref[:, cols] = x_ref[:, cols] + mod_ref[0, 2:3, cols] * _dot(merged, wout_ref[:, cols])
            yield
        h2 = _rms_norm(x1_ref[...], g2_ref[...]) * (1.0 + mod_ref[0, 4:5, :]) + mod_ref[0, 3:4, :]
        h2_ref[...] = h2.astype(BF16)
        yield

    @pl.when(s == 0)
    def _():
        _interleave(attention())

    @pl.when((s > 0) & (s < n_tiles))
    def _():
        _interleave(merge(), attention())

    @pl.when(s == n_tiles)
    def _():
        _interleave(merge())


def _attn_merge(yr, p, sinks, x2, mod, wbr, wba, wout, gain2, rows_per_batch, offs, tm):
    m, d = x2.shape
    c = yr.shape[1]
    n_heads = sinks.shape[0]
    qc = n_heads * HEAD
    kvc = 2 * (n_heads // ATTN_GROUP) * HEAD
    n_tiles = m // tm
    tpb = rows_per_batch // tm
    wpt = tm // WINDOW
    att = lambda s: jnp.minimum(s, n_tiles - 1)
    mrg = lambda s: jnp.maximum(s - 1, 0)
    const = lambda shape: pl.BlockSpec(shape, lambda s: (0, 0), pipeline_mode=pl.Buffered(1))
    return pl.pallas_call(
        functools.partial(_attn_merge_kernel, blocks_per_seq=rows_per_batch // WINDOW, n_heads=n_heads,
                          n_tiles=n_tiles),
        grid=(n_tiles + 1,),
        in_specs=[pl.BlockSpec(memory_space=pltpu.SMEM),
                  pl.BlockSpec((tm, qc), lambda s: (att(s), offs["q"] // qc)),
                  pl.BlockSpec((tm, kvc), lambda s: (att(s), offs["kv"] // kvc)),
                  pl.BlockSpec((WINDOW, kvc), lambda s: (jnp.maximum(att(s) * wpt - 1, 0), offs["kv"] // kvc)),
                  pl.BlockSpec((tm, c), lambda s: (mrg(s), 0)),
                  pl.BlockSpec((tm, 2 * d), lambda s: (mrg(s), 0)),
                  pl.BlockSpec((tm, d), lambda s: (mrg(s), 0)),
                  pl.BlockSpec((1, 6, d), lambda s: (mrg(s) // tpb, 0, 0)),
                  const((c, d)), const((c, d)), const((d, d)),
                  pl.BlockSpec((1, d), lambda s: (0, 0))],
        out_specs=[pl.BlockSpec((tm, d), lambda s: (mrg(s), 0)),
                   pl.BlockSpec((tm, d), lambda s: (mrg(s), 0))],
        out_shape=[jax.ShapeDtypeStruct((m, d), F32), jax.ShapeDtypeStruct((m, d), BF16)],
        scratch_shapes=[pltpu.VMEM((2 * n_heads, WINDOW, 2 * WINDOW), F32),
                        pltpu.VMEM((2, tm, qc), BF16)],
        compiler_params=_cparams(("arbitrary",)),
        name="attn_merge",
    )(sinks, p, p, p, yr, p, x2, mod, wbr, wba, wout, gain2)


def _mlp_kernel(h_ref, wu_ref, wd_ref, x1_ref, mod_ref, fg_ref, o_ref):
    f = pl.program_id(1)

    @pl.when(f == 0)
    def _():
        o_ref[...] = jnp.zeros_like(o_ref)

    u = _dot(h_ref[...], wu_ref[...])
    act = jnp.square(jnp.maximum(u, 0.0)).astype(BF16)
    o_ref[...] += _dot(act, wd_ref[...])

    @pl.when(f == pl.num_programs(1) - 1)
    def _():
        x2 = x1_ref[...] + mod_ref[0, 5:6, :] * o_ref[...]
        o_ref[...] = _rms_norm(x2, fg_ref[...])


def _mlp(h2, wu, wd, x1, mod, final_gain, rows_per_batch, tm, tf):
    m, d = h2.shape
    f = wu.shape[1]
    tpb = rows_per_batch // tm
    return pl.pallas_call(
        _mlp_kernel,
        grid=(m // tm, f // tf),
        in_specs=[pl.BlockSpec((tm, d), lambda i, j: (i, 0)),
                  pl.BlockSpec((d, tf), lambda i, j: (0, j)),
                  pl.BlockSpec((tf, d), lambda i, j: (j, 0)),
                  pl.BlockSpec((tm, d), lambda i, j: (i, 0)),
                  pl.BlockSpec((1, 6, d), lambda i, j: (i // tpb, 0, 0)),
                  pl.BlockSpec((1, d), lambda i, j: (0, 0))],
        out_specs=pl.BlockSpec((tm, d), lambda i, j: (i, 0)),
        out_shape=jax.ShapeDtypeStruct((m, d), F32),
        compiler_params=_cparams(("arbitrary", "arbitrary"), MLP_VMEM_LIMIT),
        name="mlp",
    )(h2, wu, wd, x1, mod, final_gain)


def _forward(x, c, w_ada, b_ada, norm1_gain, w_in, b_in, rwkv_mix, rwkv_w0, rwkv_w2, rwkv_a0,
             rwkv_a2, rwkv_g2, rwkv_k_k, rwkv_k_a, rwkv_r_k, rwkv_ln_w, rwkv_ln_b, attn_sinks,
             w_branch_rwkv, w_branch_attn, w_out, norm2_gain, w_up, w_down, final_gain):
    batch, seq, d = x.shape
    depth = w_ada.shape[0]
    c_rwkv = rwkv_w0.shape[1]
    n_heads = attn_sinks.shape[1]
    qc = n_heads * HEAD
    kvc = 2 * (n_heads // ATTN_GROUP) * HEAD
    lora_w = DECAY_LORA + ICLR_LORA + GATE_LORA
    rwkv_cols = 3 * c_rwkv + lora_w
    attn_cols = qc + kvc
    offs = {"gate": 0, "r": 2 * d, "k": 2 * d + c_rwkv, "v": 2 * d + 2 * c_rwkv, "q": 2 * d + 3 * c_rwkv}
    offs["lora"] = offs["q"] + qc
    offs["kv"] = offs["lora"] + LORA_PAD
    n_pack = offs["kv"] + kvc

    segments = ((rwkv_cols + attn_cols, 2 * d, offs["gate"]), (0, 3 * c_rwkv, offs["r"]),
                (rwkv_cols, qc, offs["q"]), (3 * c_rwkv, lora_w, offs["lora"]),
                (rwkv_cols + qc, kvc, offs["kv"]))
    zero_rows = (offs["lora"] + lora_w, offs["kv"])

    def pack_cols(t):
        out = jnp.zeros(t.shape[:-1] + (n_pack,), t.dtype)
        for src, width, dst in segments:
            out = out.at[..., dst:dst + width].set(t[..., src:src + width])
        return out

    tm_in = min(1024, seq)
    tn_in = 1792 if n_pack % 1792 == 0 else LANES
    tb = min(1024, seq)
    tm_merge = min(256, seq)
    tm_mlp = min(512, seq)
    tf = min(2048, w_up.shape[2])

    assert depth == 1, "single-layer block"
    layer = lambda t: t.reshape(t.shape[1:])
    x2 = x.reshape(batch * seq, d)
    for l in range(depth):
        mod = _ada(c, layer(w_ada), b_ada[l]).reshape(batch, 6, d)
        w_pack = _pack_weight_t(layer(w_in).T, segments, zero_rows, n_pack)
        b_pack = pack_cols(b_in[l]).reshape(1, n_pack)
        p = _inproj(x2, mod, norm1_gain[l].reshape(1, d), w_pack, b_pack, seq, tm_in, tn_in)

        mix = rwkv_mix[l]
        chan = jnp.stack([mix[:c_rwkv], mix[c_rwkv:2 * c_rwkv], mix[2 * c_rwkv:3 * c_rwkv],
                          rwkv_w0[l], rwkv_a0[l], rwkv_k_k[l], rwkv_k_a[l], rwkv_r_k[l].reshape(c_rwkv),
                          rwkv_ln_w[l], rwkv_ln_b[l]])
        chan = jnp.pad(chan, ((0, CHAN_ROWS - chan.shape[0]), (0, 0)))
        lora_used = -(-lora_w // LANES) * LANES
        mix_lora = jnp.pad(mix[3 * c_rwkv:], (0, lora_used - lora_w)).reshape(1, lora_used)
        g2_pad = jnp.pad(rwkv_g2[l], ((0, lora_used - lora_w), (0, 0)))
        y_rwkv, (wbr, wba, wout, wup, wdown) = _rwkv(
            p, chan, mix_lora, rwkv_w2[l], rwkv_a2[l], g2_pad, batch, seq, offs, tb,
            cast=[layer(w_branch_rwkv), layer(w_branch_attn), layer(w_out), layer(w_up), layer(w_down)])
        x2, h2 = _attn_merge(y_rwkv, p, attn_sinks[l], x2, mod, wbr, wba, wout,
                             norm2_gain[l].reshape(1, d), seq, offs, tm_merge)
        x2 = _mlp(h2, wup, wdown, x2, mod, final_gain.reshape(1, d), seq, tm_mlp, tf)
    return x2.reshape(batch, seq, d)


def kernel(x, c, w_ada, b_ada, norm1_gain, w_in, b_in, rwkv_mix, rwkv_w0, rwkv_w2, rwkv_a0, rwkv_a2, rwkv_g2, rwkv_k_k, rwkv_k_a, rwkv_r_k, rwkv_ln_w, rwkv_ln_b, attn_sinks, w_branch_rwkv, w_branch_attn, w_out, norm2_gain, w_up, w_down, final_gain):
    return _forward(x, c, w_ada, b_ada, norm1_gain, w_in, b_in, rwkv_mix, rwkv_w0, rwkv_w2, rwkv_a0,
                    rwkv_a2, rwkv_g2, rwkv_k_k, rwkv_k_a, rwkv_r_k, rwkv_ln_w, rwkv_ln_b, attn_sinks,
                    w_branch_rwkv, w_branch_attn, w_out, norm2_gain, w_up, w_down, final_gain)
```
